```python
import jax, jax.numpy as jnp
from jax import lax
import numpy as np

D_MODEL = 1024
BATCH = 8
SEQ = 2048
DEPTH = 4

CHUNK = 64
N_MEM = 256
N_A_LAYERS = DEPTH // 2
N_B_LAYERS = DEPTH - N_A_LAYERS
MAIN_WIDTH = 3 * D_MODEL // 4
MEM_WIDTH = D_MODEL // 4
MIX_WIDTH = MAIN_WIDTH + MEM_WIDTH
HG_HEAD_DIM = 128
HG_HEADS = MAIN_WIDTH // HG_HEAD_DIM
FOX_HEAD_DIM = 64
FOX_HEADS = MAIN_WIDTH // FOX_HEAD_DIM
MEM_HEADS = 4
MEM_HEAD_DIM = MEM_WIDTH // MEM_HEADS
D_FF = 2816
Q_BLOCK = 128
EPS = 1e-6
A_IN_WIDTH = 4 * MAIN_WIDTH + MEM_WIDTH
B_IN_WIDTH = 2 * MAIN_WIDTH + MEM_WIDTH
KV_WIDTH = 2 * MAIN_WIDTH + FOX_HEADS

kernel_name = 'hybrid_hgrn2_fox_yoco_macaron'


def rms_norm(x, gain):
    x32 = x.astype(jnp.float32)
    y = x32 * lax.rsqrt(jnp.mean(x32 * x32, axis=-1, keepdims=True) + EPS)
    return (y * gain.astype(jnp.float32)).astype(x.dtype)


def swiglu(h, w_gate, w_up, w_down):
    return (jax.nn.silu(h @ w_gate) * (h @ w_up)) @ w_down


def split_heads(t, n_heads):
    b, s, _ = t.shape
    return t.reshape(b, s, n_heads, -1).transpose(0, 2, 1, 3)


def merge_heads(t):
    b, h, s, d = t.shape
    return t.transpose(0, 2, 1, 3).reshape(b, s, h * d)


def hgrn2_recurrence(q, k, v, log_f):
    b, h, s, dk = q.shape
    dv = v.shape[-1]
    n = s // CHUNK

    def to_chunks(t):
        return t.reshape(b, h, n, CHUNK, t.shape[-1]).transpose(2, 0, 1, 3, 4)

    qc, kc, vc = to_chunks(q), to_chunks(k), to_chunks(v)
    cum = jnp.cumsum(to_chunks(log_f), axis=-2)
    tri = jnp.tril(jnp.ones((CHUNK, CHUNK), dtype=bool))

    def step(state, inp):
        q_, k_, v_, c_ = inp
        diff = c_[:, :, :, None, :] - c_[:, :, None, :, :]
        decay = jnp.exp(jnp.where(tri[:, :, None], diff, -jnp.inf))
        scores = jnp.einsum('bhtd,bhsd,bhtsd->bhts', q_, k_, decay)
        o = (jnp.einsum('bhts,bhsv->bhtv', scores, v_)
             + jnp.einsum('bhtd,bhdv->bhtv', q_ * jnp.exp(c_), state))
        c_end = c_[:, :, -1, :]
        state = (jnp.exp(c_end)[..., None] * state
                 + jnp.einsum('bhsd,bhsv->bhdv', k_ * jnp.exp(c_end[:, :, None, :] - c_), v_))
        return state, o

    state0 = jnp.zeros((b, h, dk, dv), jnp.float32)
    _, o = lax.scan(step, state0, (qc, kc, vc, cum))
    return o.transpose(1, 2, 0, 3, 4).reshape(b, h, s, dv)


def forgetting_attention(q, k, v, cum_log_f):
    s = q.shape[2]
    scale = FOX_HEAD_DIM ** -0.5
    outs = []
    for blk in range(s // Q_BLOCK):
        start = blk * Q_BLOCK
        end = start + Q_BLOCK
        logits = jnp.einsum('bhqd,bhkd->bhqk', q[:, :, start:end], k[:, :, :end]).astype(jnp.float32) * scale
        logits = logits + cum_log_f[:, :, start:end, None] - cum_log_f[:, :, None, :end]
        causal = jnp.arange(end)[None, :] <= jnp.arange(start, end)[:, None]
        p = jax.nn.softmax(jnp.where(causal, logits, -jnp.inf), axis=-1)
        outs.append(jnp.einsum('bhqk,bhkd->bhqd', p.astype(v.dtype), v[:, :, :end]))
    return jnp.concatenate(outs, axis=2)


def memory_attention(qm_raw, mem_n, w_mem_kv, q_gain, k_gain):
    kv = mem_n @ w_mem_kv
    km = rms_norm(split_heads(kv[..., :MEM_WIDTH], MEM_HEADS), k_gain)
    vm = split_heads(kv[..., MEM_WIDTH:], MEM_HEADS)
    qm = rms_norm(split_heads(qm_raw, MEM_HEADS), q_gain)
    logits = jnp.einsum('bhqd,bhkd->bhqk', qm, km).astype(jnp.float32) * (MEM_HEAD_DIM ** -0.5)
    p = jax.nn.softmax(logits, axis=-1)
    return merge_heads(jnp.einsum('bhqk,bhkd->bhqd', p.astype(vm.dtype), vm))


def _fwd_setup_inputs(seed: int = 0) -> dict:
    key = jax.random.key(seed)
    ks = jax.random.split(key, 32)

    def w(k, shape, fan_in):
        return jax.random.normal(k, shape, jnp.float32) * (fan_in ** -0.5)

    def g(k, shape):
        return 1.0 + 0.02 * jax.random.normal(k, shape, jnp.float32)

    return {
        'x': jax.random.normal(ks[0], (BATCH, SEQ, D_MODEL), jnp.float32),
        'mem': jax.random.normal(ks[1], (BATCH, N_MEM, D_MODEL), jnp.float32),
        'ffn1_norm': g(ks[2], (DEPTH, D_MODEL)),
        'ffn1_w_gate': w(ks[3], (DEPTH, D_MODEL, D_FF), D_MODEL),
        'ffn1_w_up': w(ks[4], (DEPTH, D_MODEL, D_FF), D_MODEL),
        'ffn1_w_down': w(ks[5], (DEPTH, D_FF, D_MODEL), D_FF),
        'mix_norm': g(ks[6], (DEPTH, D_MODEL)),
        'mem_norm': g(ks[7], (DEPTH, D_MODEL)),
        'w_mem_kv': w(ks[8], (DEPTH, D_MODEL, 2 * MEM_WIDTH), D_MODEL),
        'mem_q_gain': g(ks[9], (DEPTH, MEM_HEAD_DIM)),
        'mem_k_gain': g(ks[10], (DEPTH, MEM_HEAD_DIM)),
        'w_in_a': w(ks[11], (N_A_LAYERS, D_MODEL, A_IN_WIDTH), D_MODEL),
        'hgrn_lb_logits': jax.random.normal(ks[12], (N_A_LAYERS, MAIN_WIDTH), jnp.float32),
        'hgrn_o_gain': g(ks[13], (N_A_LAYERS, HG_HEAD_DIM)),
        'w_in_b': w(ks[14], (N_B_LAYERS, D_MODEL, B_IN_WIDTH), D_MODEL),
        'fox_q_gain': g(ks[15], (N_B_LAYERS, FOX_HEAD_DIM)),
        'kv_norm': g(ks[16], (D_MODEL,)),
        'w_kv': w(ks[17], (D_MODEL, KV_WIDTH), D_MODEL),
        'fox_f_bias': 0.1 * jax.random.normal(ks[18], (FOX_HEADS,), jnp.float32),
        'fox_k_gain': g(ks[19], (FOX_HEAD_DIM,)),
        'w_out': w(ks[20], (DEPTH, MIX_WIDTH, D_MODEL), MIX_WIDTH),
        'ffn2_norm': g(ks[21], (DEPTH, D_MODEL)),
        'ffn2_w_gate': w(ks[22], (DEPTH, D_MODEL, D_FF), D_MODEL),
        'ffn2_w_up': w(ks[23], (DEPTH, D_MODEL, D_FF), D_MODEL),
        'ffn2_w_down': w(ks[24], (DEPTH, D_FF, D_MODEL), D_FF),
    }


def _fwd_reference(x, mem, ffn1_norm, ffn1_w_gate, ffn1_w_up, ffn1_w_down, mix_norm, mem_norm,
              w_mem_kv, mem_q_gain, mem_k_gain, w_in_a, hgrn_lb_logits, hgrn_o_gain,
              w_in_b, fox_q_gain, kv_norm, w_kv, fox_f_bias, fox_k_gain, w_out,
              ffn2_norm, ffn2_w_gate, ffn2_w_up, ffn2_w_down):
    lb = jnp.cumsum(jax.nn.softmax(hgrn_lb_logits.astype(jnp.float32), axis=0), axis=0)
    lb = lb - lb[0:1]
    k_sh = v_sh = cum_log_f = None
    for l in range(DEPTH):
        x = x + 0.5 * swiglu(rms_norm(x, ffn1_norm[l]), ffn1_w_gate[l], ffn1_w_up[l], ffn1_w_down[l])
        h = rms_norm(x, mix_norm[l])
        mem_n = rms_norm(mem, mem_norm[l])
        if l < N_A_LAYERS:
            proj = h @ w_in_a[l]
            q_raw = proj[..., :MAIN_WIDTH]
            f_raw = proj[..., MAIN_WIDTH:2 * MAIN_WIDTH]
            i_raw = proj[..., 2 * MAIN_WIDTH:3 * MAIN_WIDTH]
            g_raw = proj[..., 3 * MAIN_WIDTH:4 * MAIN_WIDTH]
            qm_raw = proj[..., 4 * MAIN_WIDTH:]
            f = lb[l] + (1.0 - lb[l]) * jax.nn.sigmoid(f_raw.astype(jnp.float32))
            q = jax.nn.silu(q_raw.astype(jnp.float32))
            o = hgrn2_recurrence(split_heads(q, HG_HEADS), split_heads(1.0 - f, HG_HEADS),
                                 split_heads(i_raw.astype(jnp.float32), HG_HEADS),
                                 split_heads(jnp.log(f), HG_HEADS))
            main = merge_heads(rms_norm(o, hgrn_o_gain[l])) * jax.nn.silu(g_raw.astype(jnp.float32))
        else:
            j = l - N_A_LAYERS
            proj = h @ w_in_b[j]
            q = rms_norm(split_heads(proj[..., :MAIN_WIDTH], FOX_HEADS), fox_q_gain[j])
            gate = proj[..., MAIN_WIDTH:2 * MAIN_WIDTH]
            qm_raw = proj[..., 2 * MAIN_WIDTH:]
            o = forgetting_attention(q, k_sh, v_sh, cum_log_f)
            main = merge_heads(o) * jax.nn.sigmoid(gate)
        mem_o = memory_attention(qm_raw, mem_n, w_mem_kv[l], mem_q_gain[l], mem_k_gain[l])
        mixed = jnp.concatenate([main.astype(x.dtype), mem_o.astype(x.dtype)], axis=-1)
        x = x + mixed @ w_out[l]
        x = x + 0.5 * swiglu(rms_norm(x, ffn2_norm[l]), ffn2_w_gate[l], ffn2_w_up[l], ffn2_w_down[l])
        if l == N_A_LAYERS - 1:
            kvf = rms_norm(x, kv_norm) @ w_kv
            k_sh = rms_norm(split_heads(kvf[..., :MAIN_WIDTH], FOX_HEADS), fox_k_gain)
            v_sh = split_heads(kvf[..., MAIN_WIDTH:2 * MAIN_WIDTH], FOX_HEADS)
            log_f = jax.nn.log_sigmoid(kvf[..., 2 * MAIN_WIDTH:].astype(jnp.float32) + fox_f_bias.astype(jnp.float32))
            cum_log_f = jnp.cumsum(log_f.transpose(0, 2, 1), axis=-1)
    return x


import jax as _jax
import jax.numpy as _jnp

TWIN_FORMAT = 'train_step'
FWD_PARAMS = ['x', 'mem', 'ffn1_norm', 'ffn1_w_gate', 'ffn1_w_up', 'ffn1_w_down', 'mix_norm', 'mem_norm', 'w_mem_kv', 'mem_q_gain', 'mem_k_gain', 'w_in_a', 'hgrn_lb_logits', 'hgrn_o_gain', 'w_in_b', 'fox_q_gain', 'kv_norm', 'w_kv', 'fox_f_bias', 'fox_k_gain', 'w_out', 'ffn2_norm', 'ffn2_w_gate', 'ffn2_w_up', 'ffn2_w_down']
TWIN_WEIGHTS = ['ffn1_norm', 'ffn1_w_gate', 'ffn1_w_up', 'ffn1_w_down', 'mix_norm', 'mem_norm', 'w_mem_kv', 'mem_q_gain', 'mem_k_gain', 'w_in_a', 'hgrn_lb_logits', 'hgrn_o_gain', 'w_in_b', 'fox_q_gain', 'kv_norm', 'w_kv', 'fox_f_bias', 'fox_k_gain', 'w_out', 'ffn2_norm', 'ffn2_w_gate', 'ffn2_w_up', 'ffn2_w_down']
TWIN_DIFF_INPUT = 'x'
TWIN_INPUTS = ['x', 'mem', 'ffn1_norm', 'ffn1_w_gate', 'ffn1_w_up', 'ffn1_w_down', 'mix_norm', 'mem_norm', 'w_mem_kv', 'mem_q_gain', 'mem_k_gain', 'w_in_a', 'hgrn_lb_logits', 'hgrn_o_gain', 'w_in_b', 'fox_q_gain', 'kv_norm', 'w_kv', 'fox_f_bias', 'fox_k_gain', 'w_out', 'ffn2_norm', 'ffn2_w_gate', 'ffn2_w_up', 'ffn2_w_down', 'loss_target', 'm_ffn1_norm', 'm_ffn1_w_gate', 'm_ffn1_w_up', 'm_ffn1_w_down', 'm_mix_norm', 'm_mem_norm', 'm_w_mem_kv', 'm_mem_q_gain', 'm_mem_k_gain', 'm_w_in_a', 'm_hgrn_lb_logits', 'm_hgrn_o_gain', 'm_w_in_b', 'm_fox_q_gain', 'm_kv_norm', 'm_w_kv', 'm_fox_f_bias', 'm_fox_k_gain', 'm_w_out', 'm_ffn2_norm', 'm_ffn2_w_gate', 'm_ffn2_w_up', 'm_ffn2_w_down', 'v_ffn1_norm', 'v_ffn1_w_gate', 'v_ffn1_w_up', 'v_ffn1_w_down', 'v_mix_norm', 'v_mem_norm', 'v_w_mem_kv', 'v_mem_q_gain', 'v_mem_k_gain', 'v_w_in_a', 'v_hgrn_lb_logits', 'v_hgrn_o_gain', 'v_w_in_b', 'v_fox_q_gain', 'v_kv_norm', 'v_w_kv', 'v_fox_f_bias', 'v_fox_k_gain', 'v_w_out', 'v_ffn2_norm', 'v_ffn2_w_gate', 'v_ffn2_w_up', 'v_ffn2_w_down']
TWIN_OUTPUTS = ['loss', 'grad_x', 'grad_ffn1_norm', 'grad_ffn1_w_gate', 'grad_ffn1_w_up', 'grad_ffn1_w_down', 'grad_mix_norm', 'grad_mem_norm', 'grad_w_mem_kv', 'grad_mem_q_gain', 'grad_mem_k_gain', 'grad_w_in_a', 'grad_hgrn_lb_logits', 'grad_hgrn_o_gain', 'grad_w_in_b', 'grad_fox_q_gain', 'grad_kv_norm', 'grad_w_kv', 'grad_fox_f_bias', 'grad_fox_k_gain', 'grad_w_out', 'grad_ffn2_norm', 'grad_ffn2_w_gate', 'grad_ffn2_w_up', 'grad_ffn2_w_down', 'delta_ffn1_norm', 'delta_ffn1_w_gate', 'delta_ffn1_w_up', 'delta_ffn1_w_down', 'delta_mix_norm', 'delta_mem_norm', 'delta_w_mem_kv', 'delta_mem_q_gain', 'delta_mem_k_gain', 'delta_w_in_a', 'delta_hgrn_lb_logits', 'delta_hgrn_o_gain', 'delta_w_in_b', 'delta_fox_q_gain', 'delta_kv_norm', 'delta_w_kv', 'delta_fox_f_bias', 'delta_fox_k_gain', 'delta_w_out', 'delta_ffn2_norm', 'delta_ffn2_w_gate', 'delta_ffn2_w_up', 'delta_ffn2_w_down', 'new_m_ffn1_norm', 'new_m_ffn1_w_gate', 'new_m_ffn1_w_up', 'new_m_ffn1_w_down', 'new_m_mix_norm', 'new_m_mem_norm', 'new_m_w_mem_kv', 'new_m_mem_q_gain', 'new_m_mem_k_gain', 'new_m_w_in_a', 'new_m_hgrn_lb_logits', 'new_m_hgrn_o_gain', 'new_m_w_in_b', 'new_m_fox_q_gain', 'new_m_kv_norm', 'new_m_w_kv', 'new_m_fox_f_bias', 'new_m_fox_k_gain', 'new_m_w_out', 'new_m_ffn2_norm', 'new_m_ffn2_w_gate', 'new_m_ffn2_w_up', 'new_m_ffn2_w_down', 'new_v_ffn1_norm', 'new_v_ffn1_w_gate', 'new_v_ffn1_w_up', 'new_v_ffn1_w_down', 'new_v_mix_norm', 'new_v_mem_norm', 'new_v_w_mem_kv', 'new_v_mem_q_gain', 'new_v_mem_k_gain', 'new_v_w_in_a', 'new_v_hgrn_lb_logits', 'new_v_hgrn_o_gain', 'new_v_w_in_b', 'new_v_fox_q_gain', 'new_v_kv_norm', 'new_v_w_kv', 'new_v_fox_f_bias', 'new_v_fox_k_gain', 'new_v_w_out', 'new_v_ffn2_norm', 'new_v_ffn2_w_gate', 'new_v_ffn2_w_up', 'new_v_ffn2_w_down']
TWIN_LEAF_KINDS = {'loss': 'loss', 'grad_x': 'grad_x', 'grad_ffn1_norm': 'grad_w', 'grad_ffn1_w_gate': 'grad_w', 'grad_ffn1_w_up': 'grad_w', 'grad_ffn1_w_down': 'grad_w', 'grad_mix_norm': 'grad_w', 'grad_mem_norm': 'grad_w', 'grad_w_mem_kv': 'grad_w', 'grad_mem_q_gain': 'grad_w', 'grad_mem_k_gain': 'grad_w', 'grad_w_in_a': 'grad_w', 'grad_hgrn_lb_logits': 'grad_w', 'grad_hgrn_o_gain': 'grad_w', 'grad_w_in_b': 'grad_w', 'grad_fox_q_gain': 'grad_w', 'grad_kv_norm': 'grad_w', 'grad_w_kv': 'grad_w', 'grad_fox_f_bias': 'grad_w', 'grad_fox_k_gain': 'grad_w', 'grad_w_out': 'grad_w', 'grad_ffn2_norm': 'grad_w', 'grad_ffn2_w_gate': 'grad_w', 'grad_ffn2_w_up': 'grad_w', 'grad_ffn2_w_down': 'grad_w', 'delta_ffn1_norm': 'delta_w', 'delta_ffn1_w_gate': 'delta_w', 'delta_ffn1_w_up': 'delta_w', 'delta_ffn1_w_down': 'delta_w', 'delta_mix_norm': 'delta_w', 'delta_mem_norm': 'delta_w', 'delta_w_mem_kv': 'delta_w', 'delta_mem_q_gain': 'delta_w', 'delta_mem_k_gain': 'delta_w', 'delta_w_in_a': 'delta_w', 'delta_hgrn_lb_logits': 'delta_w', 'delta_hgrn_o_gain': 'delta_w', 'delta_w_in_b': 'delta_w', 'delta_fox_q_gain': 'delta_w', 'delta_kv_norm': 'delta_w', 'delta_w_kv': 'delta_w', 'delta_fox_f_bias': 'delta_w', 'delta_fox_k_gain': 'delta_w', 'delta_w_out': 'delta_w', 'delta_ffn2_norm': 'delta_w', 'delta_ffn2_w_gate': 'delta_w', 'delta_ffn2_w_up': 'delta_w', 'delta_ffn2_w_down': 'delta_w', 'new_m_ffn1_norm': 'new_m', 'new_m_ffn1_w_gate': 'new_m', 'new_m_ffn1_w_up': 'new_m', 'new_m_ffn1_w_down': 'new_m', 'new_m_mix_norm': 'new_m', 'new_m_mem_norm': 'new_m', 'new_m_w_mem_kv': 'new_m', 'new_m_mem_q_gain': 'new_m', 'new_m_mem_k_gain': 'new_m', 'new_m_w_in_a': 'new_m', 'new_m_hgrn_lb_logits': 'new_m', 'new_m_hgrn_o_gain': 'new_m', 'new_m_w_in_b': 'new_m', 'new_m_fox_q_gain': 'new_m', 'new_m_kv_norm': 'new_m', 'new_m_w_kv': 'new_m', 'new_m_fox_f_bias': 'new_m', 'new_m_fox_k_gain': 'new_m', 'new_m_w_out': 'new_m', 'new_m_ffn2_norm': 'new_m', 'new_m_ffn2_w_gate': 'new_m', 'new_m_ffn2_w_up': 'new_m', 'new_m_ffn2_w_down': 'new_m', 'new_v_ffn1_norm': 'new_v', 'new_v_ffn1_w_gate': 'new_v', 'new_v_ffn1_w_up': 'new_v', 'new_v_ffn1_w_down': 'new_v', 'new_v_mix_norm': 'new_v', 'new_v_mem_norm': 'new_v', 'new_v_w_mem_kv': 'new_v', 'new_v_mem_q_gain': 'new_v', 'new_v_mem_k_gain': 'new_v', 'new_v_w_in_a': 'new_v', 'new_v_hgrn_lb_logits': 'new_v', 'new_v_hgrn_o_gain': 'new_v', 'new_v_w_in_b': 'new_v', 'new_v_fox_q_gain': 'new_v', 'new_v_kv_norm': 'new_v', 'new_v_w_kv': 'new_v', 'new_v_fox_f_bias': 'new_v', 'new_v_fox_k_gain': 'new_v', 'new_v_w_out': 'new_v', 'new_v_ffn2_norm': 'new_v', 'new_v_ffn2_w_gate': 'new_v', 'new_v_ffn2_w_up': 'new_v', 'new_v_ffn2_w_down': 'new_v'}


def _forward(args):
    return _fwd_reference(*[args[k] for k in FWD_PARAMS])


def _output_shape():
    out = _jax.eval_shape(lambda: _forward(_fwd_setup_inputs(0)))
    return out.shape, out.dtype

N_MICROBATCH = 1
ADAM_LR = 0.001
ADAM_B1 = 0.9
ADAM_B2 = 0.999
ADAM_EPS = 1e-08
ADAM_WD = 0.01
ADAM_STEP = 10
PER_EXAMPLE_BATCH_AXIS = {'x': 0, 'mem': 0, 'loss_target': 0}
SHARED_INPUTS = []
_WEIGHT_DTYPES = {'ffn1_norm': _jnp.float32, 'ffn1_w_gate': _jnp.float32, 'ffn1_w_up': _jnp.float32, 'ffn1_w_down': _jnp.float32, 'mix_norm': _jnp.float32, 'mem_norm': _jnp.float32, 'w_mem_kv': _jnp.float32, 'mem_q_gain': _jnp.float32, 'mem_k_gain': _jnp.float32, 'w_in_a': _jnp.float32, 'hgrn_lb_logits': _jnp.float32, 'hgrn_o_gain': _jnp.float32, 'w_in_b': _jnp.float32, 'fox_q_gain': _jnp.float32, 'kv_norm': _jnp.float32, 'w_kv': _jnp.float32, 'fox_f_bias': _jnp.float32, 'fox_k_gain': _jnp.float32, 'w_out': _jnp.float32, 'ffn2_norm': _jnp.float32, 'ffn2_w_gate': _jnp.float32, 'ffn2_w_up': _jnp.float32, 'ffn2_w_down': _jnp.float32}
MOMENT_SCALE = {'ffn1_norm': 3.072565e+00, 'ffn1_w_gate': 7.686406e-02, 'ffn1_w_up': 8.027084e-02, 'ffn1_w_down': 1.320436e-01, 'mix_norm': 3.535932e+00, 'mem_norm': 7.253184e-02, 'w_mem_kv': 7.818314e-02, 'mem_q_gain': 6.068612e-01, 'mem_k_gain': 6.052018e-01, 'w_in_a': 2.050354e-01, 'hgrn_lb_logits': 1.970490e-02, 'hgrn_o_gain': 3.446105e+01, 'w_in_b': 5.419978e-02, 'fox_q_gain': 2.609983e+00, 'kv_norm': 3.345729e+00, 'w_kv': 2.206750e-01, 'fox_f_bias': 4.667672e+01, 'fox_k_gain': 5.269037e+00, 'w_out': 2.119618e-01, 'ffn2_norm': 3.073402e+00, 'ffn2_w_gate': 6.244247e-02, 'ffn2_w_up': 6.856869e-02, 'ffn2_w_down': 1.123751e-01}


def _to_microbatches(a, axis):
    t = _jnp.moveaxis(a, axis, 0)
    t = t.reshape((N_MICROBATCH, t.shape[0] // N_MICROBATCH) + t.shape[1:])
    return _jnp.moveaxis(t, 1, axis + 1)


def setup_inputs(seed: int = 0) -> dict:
    inp = _fwd_setup_inputs(seed)
    key = _jax.random.fold_in(_jax.random.key(seed), 7919)
    shape, _ = _output_shape()
    out = dict(inp)
    out["loss_target"] = _jax.random.normal(_jax.random.fold_in(key, 0), shape, _jnp.float32)
    for i, name in enumerate(TWIN_WEIGHTS):
        w = inp[name].astype(_jnp.float32)
        if MOMENT_SCALE is None:
            s = _jnp.sqrt(_jnp.mean(_jnp.square(w)) + 1e-30)
        else:
            s = MOMENT_SCALE[name]
        km, kv = _jax.random.split(_jax.random.fold_in(key, i + 1))
        out[name] = w
        out["m_" + name] = s * _jax.random.normal(km, w.shape, _jnp.float32)
        out["v_" + name] = (s * s) * _jax.random.uniform(kv, w.shape, _jnp.float32, 0.5, 1.5)
    if N_MICROBATCH > 1:
        for name, axis in PER_EXAMPLE_BATCH_AXIS.items():
            out[name] = _to_microbatches(out[name], axis)
    return {'x': out['x'], 'mem': out['mem'], 'ffn1_norm': out['ffn1_norm'], 'ffn1_w_gate': out['ffn1_w_gate'], 'ffn1_w_up': out['ffn1_w_up'], 'ffn1_w_down': out['ffn1_w_down'], 'mix_norm': out['mix_norm'], 'mem_norm': out['mem_norm'], 'w_mem_kv': out['w_mem_kv'], 'mem_q_gain': out['mem_q_gain'], 'mem_k_gain': out['mem_k_gain'], 'w_in_a': out['w_in_a'], 'hgrn_lb_logits': out['hgrn_lb_logits'], 'hgrn_o_gain': out['hgrn_o_gain'], 'w_in_b': out['w_in_b'], 'fox_q_gain': out['fox_q_gain'], 'kv_norm': out['kv_norm'], 'w_kv': out['w_kv'], 'fox_f_bias': out['fox_f_bias'], 'fox_k_gain': out['fox_k_gain'], 'w_out': out['w_out'], 'ffn2_norm': out['ffn2_norm'], 'ffn2_w_gate': out['ffn2_w_gate'], 'ffn2_w_up': out['ffn2_w_up'], 'ffn2_w_down': out['ffn2_w_down'], 'loss_target': out['loss_target'], 'm_ffn1_norm': out['m_ffn1_norm'], 'm_ffn1_w_gate': out['m_ffn1_w_gate'], 'm_ffn1_w_up': out['m_ffn1_w_up'], 'm_ffn1_w_down': out['m_ffn1_w_down'], 'm_mix_norm': out['m_mix_norm'], 'm_mem_norm': out['m_mem_norm'], 'm_w_mem_kv': out['m_w_mem_kv'], 'm_mem_q_gain': out['m_mem_q_gain'], 'm_mem_k_gain': out['m_mem_k_gain'], 'm_w_in_a': out['m_w_in_a'], 'm_hgrn_lb_logits': out['m_hgrn_lb_logits'], 'm_hgrn_o_gain': out['m_hgrn_o_gain'], 'm_w_in_b': out['m_w_in_b'], 'm_fox_q_gain': out['m_fox_q_gain'], 'm_kv_norm': out['m_kv_norm'], 'm_w_kv': out['m_w_kv'], 'm_fox_f_bias': out['m_fox_f_bias'], 'm_fox_k_gain': out['m_fox_k_gain'], 'm_w_out': out['m_w_out'], 'm_ffn2_norm': out['m_ffn2_norm'], 'm_ffn2_w_gate': out['m_ffn2_w_gate'], 'm_ffn2_w_up': out['m_ffn2_w_up'], 'm_ffn2_w_down': out['m_ffn2_w_down'], 'v_ffn1_norm': out['v_ffn1_norm'], 'v_ffn1_w_gate': out['v_ffn1_w_gate'], 'v_ffn1_w_up': out['v_ffn1_w_up'], 'v_ffn1_w_down': out['v_ffn1_w_down'], 'v_mix_norm': out['v_mix_norm'], 'v_mem_norm': out['v_mem_norm'], 'v_w_mem_kv': out['v_w_mem_kv'], 'v_mem_q_gain': out['v_mem_q_gain'], 'v_mem_k_gain': out['v_mem_k_gain'], 'v_w_in_a': out['v_w_in_a'], 'v_hgrn_lb_logits': out['v_hgrn_lb_logits'], 'v_hgrn_o_gain': out['v_hgrn_o_gain'], 'v_w_in_b': out['v_w_in_b'], 'v_fox_q_gain': out['v_fox_q_gain'], 'v_kv_norm': out['v_kv_norm'], 'v_w_kv': out['v_w_kv'], 'v_fox_f_bias': out['v_fox_f_bias'], 'v_fox_k_gain': out['v_fox_k_gain'], 'v_w_out': out['v_w_out'], 'v_ffn2_norm': out['v_ffn2_norm'], 'v_ffn2_w_gate': out['v_ffn2_w_gate'], 'v_ffn2_w_up': out['v_ffn2_w_up'], 'v_ffn2_w_down': out['v_ffn2_w_down']}


def _loss(weights, diff, rest, loss_target):
    with _jax.named_scope("forward"):
        args = {**rest, TWIN_DIFF_INPUT: diff, **{k: w.astype(_WEIGHT_DTYPES[k]) for k, w in weights.items()}}
        y = _forward(args)
    with _jax.named_scope("loss_head"):
        err = _jnp.square(y.astype(_jnp.float32) - loss_target)
        return 0.5 * _jnp.sum(_jnp.mean(err, axis=-1)) if err.ndim else 0.5 * err


def _adamw(w, g, m, v):
    m = ADAM_B1 * m + (1.0 - ADAM_B1) * g
    v = ADAM_B2 * v + (1.0 - ADAM_B2) * _jnp.square(g)
    m_hat = m / (1.0 - ADAM_B1 ** ADAM_STEP)
    v_hat = v / (1.0 - ADAM_B2 ** ADAM_STEP)
    delta = -ADAM_LR * (m_hat / (_jnp.sqrt(v_hat) + ADAM_EPS) + ADAM_WD * w)
    return delta, m, v


def reference(x, mem, ffn1_norm, ffn1_w_gate, ffn1_w_up, ffn1_w_down, mix_norm, mem_norm, w_mem_kv, mem_q_gain, mem_k_gain, w_in_a, hgrn_lb_logits, hgrn_o_gain, w_in_b, fox_q_gain, kv_norm, w_kv, fox_f_bias, fox_k_gain, w_out, ffn2_norm, ffn2_w_gate, ffn2_w_up, ffn2_w_down, loss_target, m_ffn1_norm, m_ffn1_w_gate, m_ffn1_w_up, m_ffn1_w_down, m_mix_norm, m_mem_norm, m_w_mem_kv, m_mem_q_gain, m_mem_k_gain, m_w_in_a, m_hgrn_lb_logits, m_hgrn_o_gain, m_w_in_b, m_fox_q_gain, m_kv_norm, m_w_kv, m_fox_f_bias, m_fox_k_gain, m_w_out, m_ffn2_norm, m_ffn2_w_gate, m_ffn2_w_up, m_ffn2_w_down, v_ffn1_norm, v_ffn1_w_gate, v_ffn1_w_up, v_ffn1_w_down, v_mix_norm, v_mem_norm, v_w_mem_kv, v_mem_q_gain, v_mem_k_gain, v_w_in_a, v_hgrn_lb_logits, v_hgrn_o_gain, v_w_in_b, v_fox_q_gain, v_kv_norm, v_w_kv, v_fox_f_bias, v_fox_k_gain, v_w_out, v_ffn2_norm, v_ffn2_w_gate, v_ffn2_w_up, v_ffn2_w_down):
    given = dict(x=x, mem=mem, ffn1_norm=ffn1_norm, ffn1_w_gate=ffn1_w_gate, ffn1_w_up=ffn1_w_up, ffn1_w_down=ffn1_w_down, mix_norm=mix_norm, mem_norm=mem_norm, w_mem_kv=w_mem_kv, mem_q_gain=mem_q_gain, mem_k_gain=mem_k_gain, w_in_a=w_in_a, hgrn_lb_logits=hgrn_lb_logits, hgrn_o_gain=hgrn_o_gain, w_in_b=w_in_b, fox_q_gain=fox_q_gain, kv_norm=kv_norm, w_kv=w_kv, fox_f_bias=fox_f_bias, fox_k_gain=fox_k_gain, w_out=w_out, ffn2_norm=ffn2_norm, ffn2_w_gate=ffn2_w_gate, ffn2_w_up=ffn2_w_up, ffn2_w_down=ffn2_w_down, loss_target=loss_target, m_ffn1_norm=m_ffn1_norm, m_ffn1_w_gate=m_ffn1_w_gate, m_ffn1_w_up=m_ffn1_w_up, m_ffn1_w_down=m_ffn1_w_down, m_mix_norm=m_mix_norm, m_mem_norm=m_mem_norm, m_w_mem_kv=m_w_mem_kv, m_mem_q_gain=m_mem_q_gain, m_mem_k_gain=m_mem_k_gain, m_w_in_a=m_w_in_a, m_hgrn_lb_logits=m_hgrn_lb_logits, m_hgrn_o_gain=m_hgrn_o_gain, m_w_in_b=m_w_in_b, m_fox_q_gain=m_fox_q_gain, m_kv_norm=m_kv_norm, m_w_kv=m_w_kv, m_fox_f_bias=m_fox_f_bias, m_fox_k_gain=m_fox_k_gain, m_w_out=m_w_out, m_ffn2_norm=m_ffn2_norm, m_ffn2_w_gate=m_ffn2_w_gate, m_ffn2_w_up=m_ffn2_w_up, m_ffn2_w_down=m_ffn2_w_down, v_ffn1_norm=v_ffn1_norm, v_ffn1_w_gate=v_ffn1_w_gate, v_ffn1_w_up=v_ffn1_w_up, v_ffn1_w_down=v_ffn1_w_down, v_mix_norm=v_mix_norm, v_mem_norm=v_mem_norm, v_w_mem_kv=v_w_mem_kv, v_mem_q_gain=v_mem_q_gain, v_mem_k_gain=v_mem_k_gain, v_w_in_a=v_w_in_a, v_hgrn_lb_logits=v_hgrn_lb_logits, v_hgrn_o_gain=v_hgrn_o_gain, v_w_in_b=v_w_in_b, v_fox_q_gain=v_fox_q_gain, v_kv_norm=v_kv_norm, v_w_kv=v_w_kv, v_fox_f_bias=v_fox_f_bias, v_fox_k_gain=v_fox_k_gain, v_w_out=v_w_out, v_ffn2_norm=v_ffn2_norm, v_ffn2_w_gate=v_ffn2_w_gate, v_ffn2_w_up=v_ffn2_w_up, v_ffn2_w_down=v_ffn2_w_down)
    weights = {n: given[n] for n in TWIN_WEIGHTS}
    shared = {n: given[n] for n in SHARED_INPUTS}
    per_example = {n: given[n] for n in ['x', 'mem']}
    grad_fn = _jax.value_and_grad(_loss, argnums=(0, 1))

    def one_microbatch(ex, loss_target):
        ex = dict(ex)
        diff = ex.pop(TWIN_DIFF_INPUT)
        return grad_fn(weights, diff, {**shared, **ex}, loss_target)

    if N_MICROBATCH == 1:
        loss, (grad_w, grad_x) = one_microbatch(per_example, given["loss_target"])
    else:
        def body(carry, xs):
            loss_sum, grad_sum = carry
            l_k, (gw_k, gx_k) = one_microbatch(xs[0], xs[1])
            with _jax.named_scope("update"):
                return (loss_sum + l_k, _jax.tree.map(_jnp.add, grad_sum, gw_k)), gx_k

        init = (_jnp.zeros((), _jnp.float32), _jax.tree.map(_jnp.zeros_like, weights))
        (loss, grad_w), grad_x = _jax.lax.scan(body, init, (per_example, given["loss_target"]))
    with _jax.named_scope("update"):
        delta_w, new_m, new_v = {}, {}, {}
        for n in TWIN_WEIGHTS:
            delta_w[n], new_m[n], new_v[n] = _adamw(weights[n], grad_w[n], given["m_" + n], given["v_" + n])
    return (loss, grad_x, *[grad_w[n] for n in TWIN_WEIGHTS], *[delta_w[n] for n in TWIN_WEIGHTS],
            *[new_m[n] for n in TWIN_WEIGHTS], *[new_v[n] for n in TWIN_WEIGHTS])
```

```python
import functools

import jax
import jax.numpy as jnp
from jax import lax
from jax.experimental import pallas as pl
from jax.experimental.pallas import tpu as pltpu

F32, BF16 = jnp.float32, jnp.bfloat16
EPS = 1e-6
V7X_VMEM_LIMIT = 56 * 1024 * 1024
LANES = 128
N_DEV = 8

D_MODEL = 1024
MAIN_WIDTH = 768
MEM_WIDTH = 256
HG_HEAD_DIM = 128
HG_HEADS = 6
FOX_HEAD_DIM = 64
FOX_HEADS = 12
MEM_HEADS = 4
MEM_HEAD_DIM = 64
HG_BLOCK = 16

ADAM_LR, ADAM_B1, ADAM_B2, ADAM_EPS, ADAM_WD, ADAM_STEP = 0.001, 0.9, 0.999, 1e-08, 0.01, 10

NN = ((1,), (0,))
NT = ((1,), (1,))
TN = ((0,), (0,))


def _dot(a, b, dims, precision=None):
    return lax.dot_general(a, b, (dims, ((), ())), preferred_element_type=F32, precision=precision)


def _bdot(a, b, dims):
    return _dot(a.astype(BF16), b.astype(BF16), dims)


def _fdot(a, b, dims):
    return _dot(a, b, dims, precision=lax.Precision.HIGHEST)


def _params(n_grid):
    return pltpu.CompilerParams(dimension_semantics=("arbitrary",) * n_grid, vmem_limit_bytes=V7X_VMEM_LIMIT)


def _rms(x, g):
    return x * lax.rsqrt(jnp.mean(x * x, axis=-1, keepdims=True) + EPS) * g


def _sigmoid(x):
    return jax.nn.sigmoid(x)


def _silu(x):
    return x * jax.nn.sigmoid(x)


def _rowwise(name, fn, rows, consts, out_rows, out_reds=(), tm=256):
    R = rows[0].shape[0]
    tm = min(tm, R)
    assert R % tm == 0
    n_in, n_o = len(rows) + len(consts), len(out_rows)

    def body(*refs):
        outs = fn(*[r[...] for r in refs[:n_in]])
        if not isinstance(outs, (tuple, list)):
            outs = (outs,)
        for r, o in zip(refs[n_in:n_in + n_o], outs[:n_o]):
            r[...] = o.astype(r.dtype)
        red_refs = refs[n_in + n_o:]
        if red_refs:
            @pl.when(pl.program_id(0) == 0)
            def _():
                for r in red_refs:
                    r[...] = jnp.zeros(r.shape, r.dtype)
            for r, o in zip(red_refs, outs[n_o:]):
                r[...] += o

    zero = lambda n: (lambda i: (0,) * n)
    in_specs = [pl.BlockSpec((tm, a.shape[1]), lambda i: (i, 0)) for a in rows]
    in_specs += [pl.BlockSpec(c.shape, zero(c.ndim)) for c in consts]
    out_specs = [pl.BlockSpec((tm, c), lambda i: (i, 0)) for c, _ in out_rows]
    out_specs += [pl.BlockSpec(s, zero(len(s))) for s, _ in out_reds]
    out_shape = [jax.ShapeDtypeStruct((R, c), dt) for c, dt in out_rows]
    out_shape += [jax.ShapeDtypeStruct(s, dt) for s, dt in out_reds]
    return pl.pallas_call(body, grid=(R // tm,), in_specs=in_specs, out_specs=out_specs, out_shape=out_shape,
                          name=name, compiler_params=_params(1))(*rows, *consts)


def _tile(n, cap):
    best = None
    for t in range(LANES, min(n, cap) + 1, LANES):
        if n % t == 0:
            best = t
    return best or n


def _mm(name, pairs, out_dtypes, M, N, epi=None, extras=(), tm=512, tn=512):
    tm, tn = _tile(M, tm), _tile(N, tn)
    n_p, n_e = len(pairs), len(extras)
    modes = [m for _, _, m in pairs]

    def body(*refs):
        accs = [_bdot(refs[2 * k][...], refs[2 * k + 1][...], modes[k]) for k in range(n_p)]
        ex = [r[...] for r in refs[2 * n_p:2 * n_p + n_e]]
        outs = epi(accs, ex) if epi is not None else accs
        for r, o in zip(refs[2 * n_p + n_e:], outs):
            r[...] = o.astype(r.dtype)

    in_specs = []
    ops = []
    for a, b, mode in pairs:
        if mode == NN:
            K = a.shape[1]
            assert a.shape == (M, K) and b.shape == (K, N), (name, a.shape, b.shape)
            in_specs += [pl.BlockSpec((tm, K), lambda i, j: (i, 0)), pl.BlockSpec((K, tn), lambda i, j: (0, j))]
        elif mode == NT:
            K = a.shape[1]
            assert a.shape == (M, K) and b.shape == (N, K), (name, a.shape, b.shape)
            in_specs += [pl.BlockSpec((tm, K), lambda i, j: (i, 0)), pl.BlockSpec((tn, K), lambda i, j: (j, 0))]
        else:
            K = a.shape[0]
            assert a.shape == (K, M) and b.shape == (K, N), (name, a.shape, b.shape)
            in_specs += [pl.BlockSpec((K, tm), lambda i, j: (0, i)), pl.BlockSpec((K, tn), lambda i, j: (0, j))]
        ops += [a, b]
    in_specs += [pl.BlockSpec((tm, tn), lambda i, j: (i, j)) for _ in extras]
    out_specs = [pl.BlockSpec((tm, tn), lambda i, j: (i, j)) for _ in out_dtypes]
    out_shape = [jax.ShapeDtypeStruct((M, N), dt) for dt in out_dtypes]
    res = pl.pallas_call(body, grid=(M // tm, N // tn), in_specs=in_specs, out_specs=out_specs, out_shape=out_shape,
                         name=name, compiler_params=_params(2))(*ops, *extras)
    return res[0] if len(res) == 1 else res


def _rms_fwd(name, x, gain, dtype=BF16):
    return _rowwise(name, _rms, [x], [gain], [(x.shape[1], dtype)])[0]


def _rms_bwd(name, x, gain, dh, dres=None):
    def fn(x, dh, *rest):
        g = rest[-1]
        _, vjp = jax.vjp(_rms, x, g)
        dx, dg = vjp(dh)
        if dres is not None:
            dx = dx + rest[0]
        return dx, dg
    rows = [x, dh] + ([dres] if dres is not None else [])
    d = x.shape[1]
    return _rowwise(name, fn, rows, [gain], [(d, F32)], [((1, d), F32)])


def _ffn_fwd(tag, x, gain, wg, wu, wd):
    T, D = x.shape
    F = wg.shape[1]
    h = _rms_fwd(tag + "_rms", x, gain)
    z = _mm(tag + "_gateup", [(h, wg, NN), (h, wu, NN)], [BF16], T, F, epi=lambda a, e: (_silu(a[0]) * a[1],), tm=1024, tn=256)
    return _mm(tag + "_down", [(z, wd, NN)], [F32], T, D, epi=lambda a, e: (e[0] + 0.5 * a[0],), extras=[x], tm=1024, tn=512)


def _ffn_bwd(tag, x, gain, wg, wu, wd, dy):
    T, D = x.shape
    F = wg.shape[1]
    h = _rms_fwd(tag + "_rms", x, gain)

    def epi(a, e):
        s = _sigmoid(a[0])
        si = a[0] * s
        dz = 0.5 * a[2]
        return si * a[1], dz * a[1] * (s + si * (1.0 - s)), dz * si

    z, da, db = _mm(tag + "_act", [(h, wg, NN), (h, wu, NN), (dy, wd, NT)], [BF16, BF16, BF16], T, F, epi=epi, tm=1024, tn=256)
    dwg = _mm(tag + "_dwg", [(h, da, TN)], [F32], D, F)
    dwu = _mm(tag + "_dwu", [(h, db, TN)], [F32], D, F)
    dwd = _mm(tag + "_dwd", [(z, dy, TN)], [F32], F, D, epi=lambda a, e: (0.5 * a[0],))
    dh = _mm(tag + "_dh", [(da, wg, NT), (db, wu, NT)], [F32], T, D, epi=lambda a, e: (a[0] + a[1],), tm=1024, tn=512)
    dx, dgain = _rms_bwd(tag + "_rmsb", x, gain, dh, dres=dy)
    return dx, dgain, dwg, dwu, dwd


def _block_tri(n, reverse=False):
    r = lax.broadcasted_iota(jnp.int32, (n, n), 0)
    c = lax.broadcasted_iota(jnp.int32, (n, n), 1)
    same = (r // HG_BLOCK) == (c // HG_BLOCK)
    return (same & ((c >= r) if reverse else (c <= r))).astype(F32)


def _hgrn_prep(q_ref, f_ref, lbv, qs, ks, cs, T):
    pt = min(T, 256)
    tri = _block_tri(pt)
    for p in range(T // pt):
        rows = pl.ds(p * pt, pt)
        f = lbv + (1.0 - lbv) * _sigmoid(f_ref[rows, :])
        qs[rows, :] = _silu(q_ref[rows, :])
        ks[rows, :] = 1.0 - f
        cs[rows, :] = _dot(tri, jnp.log(f), NN, precision=lax.Precision.HIGHEST)


def _strided(t, nb):
    return pl.ds(t, nb, stride=HG_BLOCK)


def _gate_out(o, og, g):
    return _rms(o, og) * _silu(g)


def _hgrn_fwd(name, proj, lb, og):
    T = proj.shape[0]
    nb = T // HG_BLOCK
    hd = HG_HEAD_DIM

    def body(q_ref, f_ref, i_ref, g_ref, lb_ref, og_ref, main_ref, o_ref, qs, ks, cs):
        _hgrn_prep(q_ref, f_ref, lb_ref[...], qs, ks, cs, T)
        for t in range(HG_BLOCK):
            qt, ct = qs[_strided(t, nb), :], cs[_strided(t, nb), :]
            acc = jnp.zeros((nb, hd), F32)
            for s in range(t + 1):
                w = qt * ks[_strided(s, nb), :] * jnp.exp(ct - cs[_strided(s, nb), :])
                acc = acc + jnp.sum(w, axis=-1, keepdims=True) * i_ref[_strided(s, nb), :]
            o_ref[_strided(t, nb), :] = acc

        def step(n, st):
            rows = pl.ds(pl.multiple_of(n * HG_BLOCK, HG_BLOCK), HG_BLOCK)
            c = cs[rows, :]
            ce = c[HG_BLOCK - 1:HG_BLOCK, :]
            o_ref[rows, :] += _fdot(qs[rows, :] * jnp.exp(c), st, NT)
            return jnp.exp(ce) * st + _fdot(i_ref[rows, :], ks[rows, :] * jnp.exp(ce - c), TN)

        lax.fori_loop(0, nb, step, jnp.zeros((hd, hd), F32))
        pt = min(T, 256)
        for p in range(T // pt):
            rows = pl.ds(p * pt, pt)
            main_ref[rows, :] = _gate_out(o_ref[rows, :], og_ref[...], g_ref[rows, :])

    nh = HG_HEADS
    col = lambda off: pl.BlockSpec((T, hd), lambda h, off=off: (0, off + h))
    return pl.pallas_call(
        body, grid=(nh,),
        in_specs=[col(0), col(nh), col(2 * nh), col(3 * nh), pl.BlockSpec((1, hd), lambda h: (0, h)),
                  pl.BlockSpec((1, hd), lambda h: (0, 0))],
        out_specs=[col(0), col(0)],
        out_shape=[jax.ShapeDtypeStruct((T, MAIN_WIDTH), F32)] * 2,
        scratch_shapes=[pltpu.VMEM((T, hd), F32)] * 3,
        name=name, compiler_params=_params(1))(proj, proj, proj, proj, lb, og)


def _hgrn_bwd(name, proj, o, dmix, lb, og):
    T = proj.shape[0]
    nb = T // HG_BLOCK
    hd = HG_HEAD_DIM
    pt = min(T, 256)

    def body(q_ref, f_ref, i_ref, g_ref, o_ref, dm_ref, lb_ref, og_ref,
             dq_ref, df_ref, di_ref, dg_ref, dlb_ref, dog_ref, qs, ks, cs, dos, dqs, dks, dvs, states):
        lbv = lb_ref[...]
        _hgrn_prep(q_ref, f_ref, lbv, qs, ks, cs, T)
        dog = jnp.zeros((1, hd), F32)
        for p in range(T // pt):
            rows = pl.ds(p * pt, pt)
            _, vjp = jax.vjp(_gate_out, o_ref[rows, :], og_ref[...], g_ref[rows, :])
            do, dog_p, dg = vjp(dm_ref[rows, :])
            dos[rows, :] = do
            dg_ref[rows, :] = dg.astype(dg_ref.dtype)
            dog = dog + dog_p

        @pl.when(pl.program_id(0) == 0)
        def _():
            dog_ref[...] = jnp.zeros(dog_ref.shape, F32)
        dog_ref[...] += dog

        for t in range(HG_BLOCK):
            dqs[_strided(t, nb), :] = jnp.zeros((nb, hd), F32)
        for s in range(HG_BLOCK):
            k_s, c_s, v_s = ks[_strided(s, nb), :], cs[_strided(s, nb), :], i_ref[_strided(s, nb), :]
            dk = jnp.zeros((nb, hd), F32)
            dv = jnp.zeros((nb, hd), F32)
            for t in range(s, HG_BLOCK):
                q_t, do_t = qs[_strided(t, nb), :], dos[_strided(t, nb), :]
                e = jnp.exp(cs[_strided(t, nb), :] - c_s)
                a = jnp.sum(q_t * k_s * e, axis=-1, keepdims=True)
                g = jnp.sum(do_t * v_s, axis=-1, keepdims=True)
                dqs[_strided(t, nb), :] += g * k_s * e
                dk = dk + g * q_t * e
                dv = dv + a * do_t
            dks[_strided(s, nb), :] = dk
            dvs[_strided(s, nb), :] = dv

        def fwd_step(n, st):
            rows = pl.ds(pl.multiple_of(n * HG_BLOCK, HG_BLOCK), HG_BLOCK)
            states[n] = st
            c = cs[rows, :]
            ce = c[HG_BLOCK - 1:HG_BLOCK, :]
            return jnp.exp(ce) * st + _fdot(i_ref[rows, :], ks[rows, :] * jnp.exp(ce - c), TN)

        lax.fori_loop(0, nb, fwd_step, jnp.zeros((hd, hd), F32))

        def bwd_step(m, dst):
            n = nb - 1 - m
            rows = pl.ds(pl.multiple_of(n * HG_BLOCK, HG_BLOCK), HG_BLOCK)
            c = cs[rows, :]
            ce = c[HG_BLOCK - 1:HG_BLOCK, :]
            ec, ek = jnp.exp(c), jnp.exp(ce - c)
            do, v = dos[rows, :], i_ref[rows, :]
            dqs[rows, :] += _fdot(do, states[n], NN) * ec
            dks[rows, :] += _fdot(v, dst, NN) * ek
            dvs[rows, :] += _fdot(ks[rows, :] * ek, dst, NT)
            return jnp.exp(ce) * dst + _fdot(do, qs[rows, :] * ec, TN)

        lax.fori_loop(0, nb, bwd_step, jnp.zeros((hd, hd), F32))

        full = (lax.broadcasted_iota(jnp.int32, (pt, pt), 1) >= lax.broadcasted_iota(jnp.int32, (pt, pt), 0)).astype(F32)
        carry = jnp.zeros((1, hd), F32)
        dlb = jnp.zeros((1, hd), F32)
        for p in reversed(range(T // pt)):
            rows = pl.ds(p * pt, pt)
            q, k, dq, dk = qs[rows, :], ks[rows, :], dqs[rows, :], dks[rows, :]
            db = q * dq - k * dk
            dlf = _dot(full, db, NN, precision=lax.Precision.HIGHEST) + carry
            carry = carry + jnp.sum(db, axis=0, keepdims=True)
            sg = _sigmoid(f_ref[rows, :])
            df = dlf / (1.0 - k) - dk
            df_ref[rows, :] = (df * (1.0 - lbv) * sg * (1.0 - sg)).astype(df_ref.dtype)
            dlb = dlb + jnp.sum(df * (1.0 - sg), axis=0, keepdims=True)
            qr = q_ref[rows, :]
            sq = _sigmoid(qr)
            dq_ref[rows, :] = (dq * (sq + qr * sq * (1.0 - sq))).astype(dq_ref.dtype)
            di_ref[rows, :] = dvs[rows, :].astype(di_ref.dtype)
        dlb_ref[...] = dlb

    nh = HG_HEADS
    col = lambda off: pl.BlockSpec((T, hd), lambda h, off=off: (0, off + h))
    vec = pl.BlockSpec((1, hd), lambda h: (0, h))
    one = pl.BlockSpec((1, hd), lambda h: (0, 0))
    return pl.pallas_call(
        body, grid=(nh,),
        in_specs=[col(0), col(nh), col(2 * nh), col(3 * nh), col(0), col(0), vec, one],
        out_specs=[col(0), col(0), col(0), col(0), vec, one],
        out_shape=[jax.ShapeDtypeStruct((T, MAIN_WIDTH), BF16)] * 4
        + [jax.ShapeDtypeStruct((1, MAIN_WIDTH), F32), jax.ShapeDtypeStruct((1, hd), F32)],
        scratch_shapes=[pltpu.VMEM((T, hd), F32)] * 7 + [pltpu.VMEM((nb, hd, hd), F32)],
        name=name, compiler_params=_params(1))(proj, proj, proj, proj, o, dmix, lb, og)


def _softmax_rows(s):
    p = jnp.exp(s - jnp.max(s, axis=-1, keepdims=True))
    return p, jnp.sum(p, axis=-1, keepdims=True)


def _fox_logits(q, k, cc, cr, q0):
    s = _dot(q, k, NT) * (FOX_HEAD_DIM ** -0.5) + cc - cr
    row = lax.broadcasted_iota(jnp.int32, s.shape, 0) + q0
    col = lax.broadcasted_iota(jnp.int32, s.shape, 1)
    return jnp.where(col <= row, s, -jnp.inf)


def _fox_specs(T):
    w = 2 * FOX_HEAD_DIM
    n = MAIN_WIDTH // w
    col = lambda off: pl.BlockSpec((T, w), lambda p, off=off: (0, off + p))
    cc = pl.BlockSpec((2, T, 1), lambda p: (p, 0, 0))
    cr = pl.BlockSpec((2, 1, T), lambda p: (p, 0, 0))
    gain = pl.BlockSpec((1, FOX_HEAD_DIM), lambda p: (0, 0))
    return n, col, cc, cr, gain


def _fox_fwd(name, proj, kvf, cc, cr, gq, gk):
    T = proj.shape[0]
    tq = min(T, 256)
    hd = FOX_HEAD_DIM

    def body(q_ref, g_ref, k_ref, v_ref, cc_ref, cr_ref, gq_ref, gk_ref, main_ref, o_ref):
        for hh in range(2):
            lanes = pl.ds(hh * hd, hd)
            k = _rms(k_ref[:, lanes], gk_ref[...]).astype(BF16)
            v = v_ref[:, lanes].astype(BF16)
            for qi in range(T // tq):
                rows, kl = pl.ds(qi * tq, tq), (qi + 1) * tq
                q = _rms(q_ref[rows, lanes], gq_ref[...]).astype(BF16)
                s = _fox_logits(q, k[:kl], cc_ref[hh, rows, :], cr_ref[hh, :, pl.ds(0, kl)], qi * tq)
                p, l = _softmax_rows(s)
                o = _dot(p.astype(BF16), v[:kl], NN) / l
                o_ref[rows, lanes] = o
                main_ref[rows, lanes] = o * _sigmoid(g_ref[rows, lanes])

    n, col, ccs, crs, gain = _fox_specs(T)
    return pl.pallas_call(
        body, grid=(n,), in_specs=[col(0), col(n), col(0), col(n), ccs, crs, gain, gain], out_specs=[col(0), col(0)],
        out_shape=[jax.ShapeDtypeStruct((T, MAIN_WIDTH), F32)] * 2,
        name=name, compiler_params=_params(1))(proj, proj, kvf, kvf, cc, cr, gq, gk)


def _fox_bwd(name, proj, kvf, cc, cr, gq, gk, o, dmix, pdk, pdv, pdc):
    T = proj.shape[0]
    tq = min(T, 256)
    hd = FOX_HEAD_DIM
    scale = hd ** -0.5

    def body(q_ref, g_ref, k_ref, v_ref, cc_ref, cr_ref, gq_ref, gk_ref, o_ref, dm_ref, pdk_ref, pdv_ref, pdc_ref,
             dq_ref, dg_ref, dk_ref, dv_ref, dc_ref, dgq_ref, dgk_ref, dka, dva, dca):
        dgq = jnp.zeros((1, hd), F32)
        dgk = jnp.zeros((1, hd), F32)
        for hh in range(2):
            lanes = pl.ds(hh * hd, hd)
            k32, vjp_k = jax.vjp(_rms, k_ref[:, lanes], gk_ref[...])
            k = k32.astype(BF16)
            v = v_ref[:, lanes].astype(BF16)
            dka[...] = jnp.zeros(dka.shape, F32)
            dva[...] = jnp.zeros(dva.shape, F32)
            dca[...] = jnp.zeros(dca.shape, F32)
            for qi in range(T // tq):
                rows, kl = pl.ds(qi * tq, tq), (qi + 1) * tq
                q32, vjp_q = jax.vjp(_rms, q_ref[rows, lanes], gq_ref[...])
                q = q32.astype(BF16)
                s = _fox_logits(q, k[:kl], cc_ref[hh, rows, :], cr_ref[hh, :, pl.ds(0, kl)], qi * tq)
                p, l = _softmax_rows(s)
                p = p / l
                sg = _sigmoid(g_ref[rows, lanes])
                dm = dm_ref[rows, lanes]
                do = (dm * sg).astype(BF16)
                dg_ref[rows, lanes] = (dm * o_ref[rows, lanes] * sg * (1.0 - sg)).astype(dg_ref.dtype)
                dp = _dot(do, v[:kl], NT)
                ds = p * (dp - jnp.sum(p * dp, axis=-1, keepdims=True))
                dsb = ds.astype(BF16)
                dqr, dgq_p = vjp_q(_dot(dsb, k[:kl], NN) * scale)
                dq_ref[rows, lanes] = dqr.astype(dq_ref.dtype)
                dgq = dgq + dgq_p
                dka[pl.ds(0, kl), :] += _dot(dsb, q, TN) * scale
                dva[pl.ds(0, kl), :] += _dot(p.astype(BF16), do, TN)
                dca[:, pl.ds(0, kl)] -= jnp.sum(ds, axis=0, keepdims=True)
            dkr, dgk_p = vjp_k(dka[...])
            dgk = dgk + dgk_p
            dk_ref[:, lanes] = dkr + pdk_ref[:, lanes]
            dv_ref[:, lanes] = dva[...] + pdv_ref[:, lanes]
            dc_ref[hh] = dca[...] + pdc_ref[hh]

        @pl.when(pl.program_id(0) == 0)
        def _():
            dgq_ref[...] = jnp.zeros(dgq_ref.shape, F32)
            dgk_ref[...] = jnp.zeros(dgk_ref.shape, F32)
        dgq_ref[...] += dgq
        dgk_ref[...] += dgk

    n, col, ccs, crs, gain = _fox_specs(T)
    wide = jax.ShapeDtypeStruct((T, MAIN_WIDTH), F32)
    half = jax.ShapeDtypeStruct((T, MAIN_WIDTH), BF16)
    return pl.pallas_call(
        body, grid=(n,),
        in_specs=[col(0), col(n), col(0), col(n), ccs, crs, gain, gain, col(0), col(0), col(0), col(0), crs],
        out_specs=[col(0), col(0), col(0), col(0), crs, gain, gain],
        out_shape=[half, half, wide, wide, jax.ShapeDtypeStruct((FOX_HEADS, 1, T), F32),
                   jax.ShapeDtypeStruct((1, hd), F32), jax.ShapeDtypeStruct((1, hd), F32)],
        scratch_shapes=[pltpu.VMEM((T, hd), F32), pltpu.VMEM((T, hd), F32), pltpu.VMEM((1, T), F32)],
        name=name, compiler_params=_params(1))(proj, proj, kvf, kvf, cc, cr, gq, gk, o, dmix, pdk, pdv, pdc)


def _mem_specs(T, width):
    tq = min(T, 512)
    q = pl.BlockSpec((tq, MEM_WIDTH), lambda i, c=(width - MEM_WIDTH) // MEM_WIDTH: (i, c))
    gain = pl.BlockSpec((1, MEM_HEAD_DIM), lambda i: (0, 0))
    return tq, q, gain


def _mem_fwd(name, proj, kv, gq, gk):
    T, W = proj.shape
    hd = MEM_HEAD_DIM
    tq, qspec, gain = _mem_specs(T, W)

    def body(q_ref, kv_ref, gq_ref, gk_ref, o_ref):
        for h in range(MEM_HEADS):
            lanes = pl.ds(h * hd, hd)
            q = _rms(q_ref[:, lanes], gq_ref[...]).astype(BF16)
            k = _rms(kv_ref[:, lanes], gk_ref[...]).astype(BF16)
            v = kv_ref[:, pl.ds(MEM_WIDTH + h * hd, hd)].astype(BF16)
            p, l = _softmax_rows(_dot(q, k, NT) * (hd ** -0.5))
            o_ref[:, lanes] = _dot(p.astype(BF16), v, NN) / l

    return pl.pallas_call(
        body, grid=(T // tq,),
        in_specs=[qspec, pl.BlockSpec(kv.shape, lambda i: (0, 0)), gain, gain],
        out_specs=pl.BlockSpec((tq, MEM_WIDTH), lambda i: (i, 0)),
        out_shape=jax.ShapeDtypeStruct((T, MEM_WIDTH), F32),
        name=name, compiler_params=_params(1))(proj, kv, gq, gk)


def _mem_bwd(name, proj, kv, gq, gk, dmix):
    T, W = proj.shape
    hd = MEM_HEAD_DIM
    scale = hd ** -0.5
    tq, qspec, gain = _mem_specs(T, W)

    def body(q_ref, kv_ref, gq_ref, gk_ref, dm_ref, dq_ref, dkv_ref, dgq_ref, dgk_ref):
        @pl.when(pl.program_id(0) == 0)
        def _():
            dkv_ref[...] = jnp.zeros(dkv_ref.shape, F32)
            dgq_ref[...] = jnp.zeros(dgq_ref.shape, F32)
            dgk_ref[...] = jnp.zeros(dgk_ref.shape, F32)
        for h in range(MEM_HEADS):
            lanes = pl.ds(h * hd, hd)
            vl = pl.ds(MEM_WIDTH + h * hd, hd)
            q32, vjp_q = jax.vjp(_rms, q_ref[:, lanes], gq_ref[...])
            k32, vjp_k = jax.vjp(_rms, kv_ref[:, lanes], gk_ref[...])
            q, k, v = q32.astype(BF16), k32.astype(BF16), kv_ref[:, vl].astype(BF16)
            p, l = _softmax_rows(_dot(q, k, NT) * scale)
            p = p / l
            do = dm_ref[:, lanes].astype(BF16)
            dp = _dot(do, v, NT)
            dsb = (p * (dp - jnp.sum(p * dp, axis=-1, keepdims=True))).astype(BF16)
            dqr, dgq_p = vjp_q(_dot(dsb, k, NN) * scale)
            dkr, dgk_p = vjp_k(_dot(dsb, q, TN) * scale)
            dq_ref[:, lanes] = dqr.astype(dq_ref.dtype)
            dkv_ref[:, lanes] += dkr
            dkv_ref[:, vl] += _dot(p.astype(BF16), do, TN)
            dgq_ref[...] += dgq_p
            dgk_ref[...] += dgk_p

    return pl.pallas_call(
        body, grid=(T // tq,),
        in_specs=[qspec, pl.BlockSpec(kv.shape, lambda i: (0, 0)), gain, gain,
                  pl.BlockSpec((tq, MEM_WIDTH), lambda i: (i, MAIN_WIDTH // MEM_WIDTH))],
        out_specs=[pl.BlockSpec((tq, MEM_WIDTH), lambda i: (i, 0)), pl.BlockSpec(kv.shape, lambda i: (0, 0)), gain, gain],
        out_shape=[jax.ShapeDtypeStruct((T, MEM_WIDTH), BF16), jax.ShapeDtypeStruct(kv.shape, F32),
                   jax.ShapeDtypeStruct((1, hd), F32), jax.ShapeDtypeStruct((1, hd), F32)],
        name=name, compiler_params=_params(1))(proj, kv, gq, gk, dmix)


def _cumsum_rows(name, x, reverse=False):
    T, C = x.shape
    pt = min(T, 256)

    def body(x_ref, o_ref):
        r = lax.broadcasted_iota(jnp.int32, (pt, pt), 0)
        c = lax.broadcasted_iota(jnp.int32, (pt, pt), 1)
        tri = ((c >= r) if reverse else (c <= r)).astype(F32)
        carry = jnp.zeros((1, C), F32)
        order = range(T // pt)
        for p in (reversed(order) if reverse else order):
            rows = pl.ds(p * pt, pt)
            blk = x_ref[rows, :]
            o_ref[rows, :] = _dot(tri, blk, NN, precision=lax.Precision.HIGHEST) + carry
            carry = carry + jnp.sum(blk, axis=0, keepdims=True)

    return pl.pallas_call(body, out_shape=jax.ShapeDtypeStruct((T, C), F32), name=name,
                          compiler_params=pltpu.CompilerParams(vmem_limit_bytes=V7X_VMEM_LIMIT))(x)


MESH = pl.DeviceIdType.MESH
ANY = pl.BlockSpec(memory_space=pl.ANY)


def _mesh_pos():
    return lax.axis_index("x"), lax.axis_index("y"), lax.axis_index("c")


def _all_gather(name, x):
    R, C = x.shape

    def body(x_ref, out_ref, send_sems, recv_sems, local_sem):
        mx, my, mc = _mesh_pos()
        me, sibling = (mx, my, mc), (mx, my, 1 - mc)
        chips = [(1 - mx, my), (mx, 1 - my), (1 - mx, 1 - my)]

        def slot(px, py, pc):
            return out_ref.at[4 * px + 2 * py + pc]

        def copy(k, block, to, src=None):
            return pltpu.make_async_remote_copy(
                src_ref=slot(*block) if src is None else src, dst_ref=slot(*block),
                send_sem=send_sems.at[k], recv_sem=recv_sems.at[k], device_id=to, device_id_type=MESH)

        mine = pltpu.make_async_copy(x_ref, slot(*me), local_sem)
        mine.start()
        first = [copy(0, me, sibling, src=x_ref)]
        first += [copy(1 + j, me, (*chip, mc), src=x_ref) for j, chip in enumerate(chips)]
        for cp in first:
            cp.start()
        passed = [copy(4 + j, (*chip, mc), sibling) for j, chip in enumerate(chips)]
        for j, chip in enumerate(chips):
            copy(1 + j, (*chip, mc), me).wait_recv()
            passed[j].start()
        copy(0, sibling, me).wait_recv()
        for j, chip in enumerate(chips):
            copy(4 + j, (*chip, 1 - mc), me).wait_recv()
        for cp in first + passed:
            cp.wait_send()
        mine.wait()

    return pl.pallas_call(
        body, out_shape=jax.ShapeDtypeStruct((N_DEV, R, C), x.dtype), in_specs=[ANY], out_specs=ANY,
        scratch_shapes=[pltpu.SemaphoreType.DMA((7,)), pltpu.SemaphoreType.DMA((7,)), pltpu.SemaphoreType.DMA],
        name=name)(x)


def _exchange_cores(name, g):
    _, R, C = g.shape

    def body(g_ref, keep_ref, recv_ref, send_sems, recv_sems, local_sems):
        mx, my, mc = _mesh_pos()
        local = [pltpu.make_async_copy(g_ref.at[2 * q + mc], keep_ref.at[q], local_sems.at[q]) for q in range(4)]
        for cp in local:
            cp.start()
        swap = [pltpu.make_async_remote_copy(
            src_ref=g_ref.at[2 * q + (1 - mc)], dst_ref=recv_ref.at[q], send_sem=send_sems.at[q], recv_sem=recv_sems.at[q],
            device_id=(mx, my, 1 - mc), device_id_type=MESH) for q in range(4)]
        for cp in swap:
            cp.start()
        for cp in swap:
            cp.wait()
        for cp in local:
            cp.wait()

    out = jax.ShapeDtypeStruct((4, R, C), g.dtype)
    return pl.pallas_call(
        body, out_shape=[out, out], in_specs=[ANY], out_specs=[ANY, ANY],
        scratch_shapes=[pltpu.SemaphoreType.DMA((4,)), pltpu.SemaphoreType.DMA((4,)), pltpu.SemaphoreType.DMA((4,))],
        name=name)(g)


def _exchange_chips(name, s):
    _, R, C = s.shape

    def body(s_ref, recv_ref, send_sems, recv_sems, local_sem):
        mx, my, mc = _mesh_pos()
        myq = 2 * mx + my
        chips = [(1 - mx, my), (mx, 1 - my), (1 - mx, 1 - my)]
        mine = pltpu.make_async_copy(s_ref.at[myq], recv_ref.at[myq], local_sem)
        mine.start()
        swap = [pltpu.make_async_remote_copy(
            src_ref=s_ref.at[2 * px + py], dst_ref=recv_ref.at[myq], send_sem=send_sems.at[k], recv_sem=recv_sems.at[k],
            device_id=(px, py, mc), device_id_type=MESH) for k, (px, py) in enumerate(chips)]
        for cp in swap:
            cp.start()
        for k, (px, py) in enumerate(chips):
            pltpu.make_async_remote_copy(
                src_ref=s_ref.at[myq], dst_ref=recv_ref.at[2 * px + py], send_sem=send_sems.at[k], recv_sem=recv_sems.at[k],
                device_id=(px, py, mc), device_id_type=MESH).wait_recv()
        for cp in swap:
            cp.wait_send()
        mine.wait()

    return pl.pallas_call(
        body, out_shape=jax.ShapeDtypeStruct((4, R, C), s.dtype), in_specs=[ANY], out_specs=ANY,
        scratch_shapes=[pltpu.SemaphoreType.DMA((3,)), pltpu.SemaphoreType.DMA((3,)), pltpu.SemaphoreType.DMA],
        name=name)(s)


def _row_tile(R, cap):
    best = None
    for t in range(8, min(R, cap) + 1, 8):
        if R % t == 0:
            best = t
    return best or R


def _sum_slabs(name, a, out_dtype):
    n, R, C = a.shape
    tm = _row_tile(R, 512)

    def body(*refs):
        acc = refs[0][...].astype(F32)
        for r in refs[1:n]:
            acc = acc + r[...].astype(F32)
        refs[n][...] = acc.astype(out_dtype)

    return pl.pallas_call(
        body, grid=(R // tm,),
        in_specs=[pl.BlockSpec((None, tm, C), lambda i, q=q: (q, i, 0)) for q in range(n)],
        out_specs=pl.BlockSpec((tm, C), lambda i: (i, 0)), out_shape=jax.ShapeDtypeStruct((R, C), out_dtype),
        name=name, compiler_params=_params(1))(*([a] * n))


def _reduce_scatter(g):
    _, R, C = g.shape
    keep, recv = _exchange_cores("rs_cores", g)
    pair = _rowwise("rs_pair_sum", lambda a, b: a.astype(F32) + b.astype(F32),
                    [keep.reshape(4 * R, C), recv.reshape(4 * R, C)], [], [(C, BF16)], tm=512)[0]
    return _sum_slabs("rs_chip_sum", _exchange_chips("rs_chips", pair.reshape(4, R, C)), F32)


PACK_COLS = 1024
BIG = [("ffn1_w_gate", "col"), ("ffn1_w_up", "col"), ("ffn1_w_down", "row"), ("w_mem_kv", "row"), ("w_in_a", "col"),
       ("w_in_b", "col"), ("w_kv", "row0"), ("w_out", "row"), ("ffn2_w_gate", "col"), ("ffn2_w_up", "col"),
       ("ffn2_w_down", "row")]
SMALL = ["ffn1_norm", "mix_norm", "mem_norm", "mem_q_gain", "mem_k_gain", "hgrn_o_gain", "fox_q_gain", "kv_norm",
         "fox_f_bias", "fox_k_gain", "ffn2_norm"]


def _pack_rows(n_elems, row_multiple):
    rows = -(-n_elems // PACK_COLS)
    return -(-rows // row_multiple) * row_multiple


def _flat_pack(parts, lead, cols, row_multiple):
    flat = jnp.concatenate([p.reshape(lead + (-1,)) for p in parts], axis=-1)
    rows = -(-flat.shape[-1] // cols)
    rows = -(-rows // row_multiple) * row_multiple
    pad = rows * cols - flat.shape[-1]
    flat = jnp.pad(flat, [(0, 0)] * len(lead) + [(0, pad)])
    return flat.reshape(lead + (rows, cols))


def _flat_unpack(buf, lead, shapes):
    flat = buf.reshape(lead + (-1,))
    out, off = [], 0
    for s in shapes:
        n = 1
        for d in s:
            n *= d
        out.append(lax.slice_in_dim(flat, off, off + n, axis=len(lead)).reshape(lead + tuple(s)))
        off += n
    return out


def _to_natural(gathered, kind):
    if kind == "col":
        n, L, K, w = gathered.shape
        return gathered.transpose(1, 2, 0, 3).reshape(L, K, n * w)
    if kind == "row":
        n, L, r, N = gathered.shape
        return gathered.transpose(1, 0, 2, 3).reshape(L, n * r, N)
    n, r, N = gathered.shape
    return gathered.reshape(n * r, N)


def _to_blocks(full, kind):
    if kind == "col":
        L, K, W = full.shape
        return full.reshape(L, K, N_DEV, W // N_DEV).transpose(2, 0, 1, 3)
    if kind == "row":
        L, Rr, N = full.shape
        return full.reshape(L, N_DEV, Rr // N_DEV, N).transpose(1, 0, 2, 3)
    Rr, N = full.shape
    return full.reshape(N_DEV, Rr // N_DEV, N)


def _pad128(a):
    flat = a.reshape(-1)
    return jnp.pad(flat, (0, -flat.shape[0] % LANES))


def _small_pack(parts):
    flat = jnp.concatenate([_pad128(p) for p in parts])
    rows = -(-flat.shape[0] // LANES)
    flat = jnp.pad(flat, (0, (-rows % 8) * LANES))
    return flat.reshape(-1, LANES)


def _small_unpack(buf, shapes):
    flat = buf.reshape(-1)
    out, off = [], 0
    for s in shapes:
        n = 1
        for d in s:
            n *= d
        out.append(flat[off:off + n].reshape(s))
        off += n + (-n % LANES)
    return out


def _lb_fn(l0, l1):
    m = lax.stop_gradient(jnp.maximum(l0, l1))
    e0, e1 = jnp.exp(l0 - m), jnp.exp(l1 - m)
    p0, p1 = e0 / (e0 + e1), e1 / (e0 + e1)
    return p0 - p0, (p0 + p1) - p0


def _lb_fwd(logits):
    return _rowwise("lb", _lb_fn, [logits[0:1], logits[1:2]], [], [(MAIN_WIDTH, F32)] * 2)


def _lb_bwd(logits, dlb0, dlb1):
    def fn(l0, l1, d0, d1):
        _, vjp = jax.vjp(_lb_fn, l0, l1)
        return vjp((d0, d1))
    return _rowwise("lb_bwd", fn, [logits[0:1], logits[1:2], dlb0, dlb1], [], [(MAIN_WIDTH, F32)] * 2)


def _adamw_fn(w, g, m, v):
    m = ADAM_B1 * m + (1.0 - ADAM_B1) * g
    v = ADAM_B2 * v + (1.0 - ADAM_B2) * jnp.square(g)
    m_hat = m / (1.0 - ADAM_B1 ** ADAM_STEP)
    v_hat = v / (1.0 - ADAM_B2 ** ADAM_STEP)
    return -ADAM_LR * (m_hat / (jnp.sqrt(v_hat) + ADAM_EPS) + ADAM_WD * w), m, v


def _adamw(name, w, g, m, v):
    shape = w.shape
    C = shape[-1]
    two = lambda a: a.reshape(-1, C)
    R = two(w).shape[0]
    outs = _rowwise(name, _adamw_fn, [two(w), two(g), two(m), two(v)], [], [(C, F32)] * 3, tm=_row_tile(R, 512))
    return [o.reshape(shape) for o in outs]


def _mixer_fwd(l, x1, mem, W, lbs, shared):
    T = x1.shape[0]
    tag = f"l{l}"
    h = _rms_fwd(tag + "_mixrms", x1, W["mix_norm"][l:l + 1])
    mem_n = _rms_fwd(tag + "_memrms", mem, W["mem_norm"][l:l + 1])
    kv = _mm(tag + "_memkv", [(mem_n, W["w_mem_kv"][l], NN)], [F32], mem.shape[0], 2 * MEM_WIDTH)
    if l < 2:
        w_in = W["w_in_a"][l]
        proj = _mm(tag + "_in", [(h, w_in, NN)], [F32], T, w_in.shape[1])
        main, o = _hgrn_fwd(tag + "_hgrn", proj, lbs[l], W["hgrn_o_gain"][l:l + 1])
    else:
        w_in = W["w_in_b"][l - 2]
        proj = _mm(tag + "_in", [(h, w_in, NN)], [F32], T, w_in.shape[1])
        main, o = _fox_fwd(tag + "_fox", proj, shared["kvf"], shared["cc"], shared["cr"], W["fox_q_gain"][l - 2:l - 1],
                           W["fox_k_gain"])
    mem_o = _mem_fwd(tag + "_mem", proj, kv, W["mem_q_gain"][l:l + 1], W["mem_k_gain"][l:l + 1])
    w_out = W["w_out"][l]
    x2 = _mm(tag + "_out", [(main, w_out[:MAIN_WIDTH], NN), (mem_o, w_out[MAIN_WIDTH:], NN)], [F32], T, D_MODEL,
             epi=lambda a, e: (e[0] + a[0] + a[1],), extras=[x1])
    return x2, dict(h=h, mem_n=mem_n, kv=kv, proj=proj, main=main, o=o, mem_o=mem_o, w_in=w_in)


def _mixer_bwd(l, x1, mem, W, lbs, shared, sv, dx2, acc):
    T = x1.shape[0]
    tag = f"l{l}b"
    w_out = W["w_out"][l]
    g = {}
    dmix = _mm(tag + "_dmix", [(dx2, w_out, NT)], [F32], T, D_MODEL)
    g["w_out"] = jnp.concatenate([
        _mm(tag + "_dwout_a", [(sv["main"], dx2, TN)], [F32], MAIN_WIDTH, D_MODEL),
        _mm(tag + "_dwout_b", [(sv["mem_o"], dx2, TN)], [F32], MEM_WIDTH, D_MODEL)], axis=0)
    dqm, dkv, g["mem_q_gain"], g["mem_k_gain"] = _mem_bwd(tag + "_mem", sv["proj"], sv["kv"], W["mem_q_gain"][l:l + 1],
                                                           W["mem_k_gain"][l:l + 1], dmix)
    if l < 2:
        dq, df, di, dg, g["lb"], g["hgrn_o_gain"] = _hgrn_bwd(tag + "_hgrn", sv["proj"], sv["o"], dmix, lbs[l],
                                                               W["hgrn_o_gain"][l:l + 1])
        dproj = jnp.concatenate([dq, df, di, dg, dqm], axis=1)
    else:
        dq, dgate, acc["dk"], acc["dv"], acc["dc"], g["fox_q_gain"], g["fox_k_gain"] = _fox_bwd(
            tag + "_fox", sv["proj"], shared["kvf"], shared["cc"], shared["cr"], W["fox_q_gain"][l - 2:l - 1], W["fox_k_gain"],
            sv["o"], dmix, acc["dk"], acc["dv"], acc["dc"])
        dproj = jnp.concatenate([dq, dgate, dqm], axis=1)
    w_in = sv["w_in"]
    g["w_in"] = _mm(tag + "_dwin", [(sv["h"], dproj, TN)], [F32], D_MODEL, w_in.shape[1])
    dh = _mm(tag + "_dh", [(dproj, w_in, NT)], [F32], T, D_MODEL)
    dx1, g["mix_norm"] = _rms_bwd(tag + "_mixrms", x1, W["mix_norm"][l:l + 1], dh, dres=dx2)
    g["w_mem_kv"] = _mm(tag + "_dwmemkv", [(sv["mem_n"], dkv, TN)], [F32], D_MODEL, 2 * MEM_WIDTH)
    dmem_n = _mm(tag + "_dmemn", [(dkv, W["w_mem_kv"][l], NT)], [F32], mem.shape[0], D_MODEL)
    _, g["mem_norm"] = _rms_bwd(tag + "_memrms", mem, W["mem_norm"][l:l + 1], dmem_n)
    return dx1, g


def _log_forget(kvf, bias):
    f = kvf[:, 2 * MAIN_WIDTH:]
    return _rowwise("kv_logf", lambda f, b: jax.nn.log_sigmoid(f + b), [f], [bias], [(LANES, F32)])[0]


def _step(x, mem, target, W, lb_logits):
    T = x.shape[0]
    W = dict(W, fox_k_gain=W["fox_k_gain"].reshape(1, -1))
    lbs = _lb_fwd(lb_logits)
    fox_bias = jnp.pad(W["fox_f_bias"], (0, LANES - FOX_HEADS)).reshape(1, LANES)
    w_kv = W["w_kv"]
    w_kv = jnp.concatenate([w_kv[:, :2 * MAIN_WIDTH], jnp.pad(w_kv[:, 2 * MAIN_WIDTH:], ((0, 0), (0, LANES - FOX_HEADS)))], axis=1)
    ffn = lambda which, l: (W[which + "_norm"][l:l + 1], W[which + "_w_gate"][l], W[which + "_w_up"][l], W[which + "_w_down"][l])

    saved, shared = [], {}
    for l in range(4):
        x1 = _ffn_fwd(f"l{l}_ffn1", x, *ffn("ffn1", l))
        x2, sv = _mixer_fwd(l, x1, mem, W, lbs, shared)
        x3 = _ffn_fwd(f"l{l}_ffn2", x2, *ffn("ffn2", l))
        sv.update(x=x, x1=x1, x2=x2)
        saved.append(sv)
        x = x3
        if l == 1:
            hk = _rms_fwd("kv_rms", x, W["kv_norm"].reshape(1, -1))
            kvf = _mm("kv_proj", [(hk, w_kv, NN)], [F32], T, w_kv.shape[1])
            cum = _cumsum_rows("kv_cum", _log_forget(kvf, fox_bias))[:, :FOX_HEADS].T
            shared = dict(kvf=kvf, cc=cum[:, :, None], cr=cum[:, None, :], hk=hk, x=x)

    def loss_fn(y, t):
        err = y - t
        return err * (1.0 / D_MODEL), jnp.sum(0.5 / D_MODEL * err * err, axis=0, keepdims=True)
    dx, loss = _rowwise("loss", loss_fn, [x, target], [], [(D_MODEL, F32)], [((1, D_MODEL), F32)])

    grads = [None] * 4
    acc = dict(dk=jnp.zeros((T, MAIN_WIDTH), F32), dv=jnp.zeros((T, MAIN_WIDTH), F32), dc=jnp.zeros((FOX_HEADS, 1, T), F32))
    gkv = {}
    for l in reversed(range(4)):
        sv = saved[l]
        if l == 1:
            dcum = jnp.pad(acc["dc"][:, 0, :].T, ((0, 0), (0, LANES - FOX_HEADS)))
            dlf = _cumsum_rows("kv_dcum", dcum, reverse=True)
            def dlogf_fn(d, f, b):
                p = d * _sigmoid(-(f + b))
                return p, jnp.sum(p, axis=0, keepdims=True)
            dfl, gkv["fox_f_bias"] = _rowwise("kv_dlogf", dlogf_fn, [dlf, shared["kvf"][:, 2 * MAIN_WIDTH:]], [fox_bias],
                                              [(LANES, BF16)], [((1, LANES), F32)])
            dkvf = jnp.concatenate([acc["dk"].astype(BF16), acc["dv"].astype(BF16), dfl], axis=1)
            gkv["w_kv"] = _mm("kv_dw", [(shared["hk"], dkvf, TN)], [F32], D_MODEL, dkvf.shape[1])[:, :2 * MAIN_WIDTH + FOX_HEADS]
            dhk = _mm("kv_dh", [(dkvf, w_kv, NT)], [F32], T, D_MODEL)
            dx, gkv["kv_norm"] = _rms_bwd("kv_rmsb", shared["x"], W["kv_norm"].reshape(1, -1), dhk, dres=dx)
        g = {}
        dx2, g["ffn2_norm"], g["ffn2_w_gate"], g["ffn2_w_up"], g["ffn2_w_down"] = _ffn_bwd(f"l{l}b_ffn2", sv["x2"], *ffn("ffn2", l), dx)
        dx1, gm = _mixer_bwd(l, sv["x1"], mem, W, lbs, shared, sv, dx2, acc)
        g.update(gm)
        dx, g["ffn1_norm"], g["ffn1_w_gate"], g["ffn1_w_up"], g["ffn1_w_down"] = _ffn_bwd(f"l{l}b_ffn1", sv["x"], *ffn("ffn1", l), dx1)
        grads[l] = g

    stack = lambda name, ls=range(4): jnp.stack([grads[l][name] for l in ls])
    out = {n: stack(n) for n in ["ffn1_w_gate", "ffn1_w_up", "ffn1_w_down", "ffn2_w_gate", "ffn2_w_up", "ffn2_w_down", "w_out", "w_mem_kv"]}
    out["w_in_a"] = jnp.stack([grads[l]["w_in"] for l in (0, 1)])
    out["w_in_b"] = jnp.stack([grads[l]["w_in"] for l in (2, 3)])
    out["w_kv"] = gkv["w_kv"]
    for n in ["ffn1_norm", "mix_norm", "mem_norm", "mem_q_gain", "mem_k_gain", "ffn2_norm"]:
        out[n] = jnp.concatenate([grads[l][n] for l in range(4)], axis=0)
    out["hgrn_o_gain"] = jnp.concatenate([grads[l]["hgrn_o_gain"] for l in (0, 1)], axis=0)
    out["fox_q_gain"] = jnp.concatenate([grads[l]["fox_q_gain"] for l in (2, 3)], axis=0)
    out["fox_k_gain"] = (grads[2]["fox_k_gain"] + grads[3]["fox_k_gain"]).reshape(-1)
    out["kv_norm"] = gkv["kv_norm"].reshape(-1)
    out["fox_f_bias"] = gkv["fox_f_bias"][0, :FOX_HEADS]
    dl0, dl1 = _lb_bwd(lb_logits, grads[0]["lb"], grads[1]["lb"])
    out["hgrn_lb_logits"] = jnp.concatenate([dl0, dl1], axis=0)
    return loss, dx, out


WEIGHTS = ["ffn1_norm", "ffn1_w_gate", "ffn1_w_up", "ffn1_w_down", "mix_norm", "mem_norm", "w_mem_kv", "mem_q_gain",
           "mem_k_gain", "w_in_a", "hgrn_lb_logits", "hgrn_o_gain", "w_in_b", "fox_q_gain", "kv_norm", "w_kv", "fox_f_bias",
           "fox_k_gain", "w_out", "ffn2_norm", "ffn2_w_gate", "ffn2_w_up", "ffn2_w_down"]
PACK_ROW_MULTIPLE = 512


def _train_step(a):
    big_shapes = [a[n].shape for n, _ in BIG]
    packed = _flat_pack([a[n].astype(BF16) for n, _ in BIG], (), PACK_COLS, PACK_ROW_MULTIPLE)
    blocks = _flat_unpack(_all_gather("ag_weights", packed), (N_DEV,), big_shapes)
    W = {n: _to_natural(b, kind) for (n, kind), b in zip(BIG, blocks)}
    W.update({n: a[n] for n in SMALL})
    lb_shape = a["hgrn_lb_logits"].shape
    lb_all = _all_gather("ag_lb", _small_pack([a["hgrn_lb_logits"]])).reshape(N_DEV, -1)[:, :lb_shape[0] * lb_shape[1]]
    lb_logits = lb_all.reshape((N_DEV,) + lb_shape).transpose(1, 0, 2).reshape(lb_shape[0], -1)

    loss_part, dx, g = _step(a["x"][0], a["mem"][0], a["loss_target"][0], W, lb_logits)

    grad_blocks = _flat_pack([_to_blocks(g[n], kind).astype(BF16) for n, kind in BIG], (N_DEV,), PACK_COLS, PACK_ROW_MULTIPLE)
    grad = dict(zip([n for n, _ in BIG], _flat_unpack(_reduce_scatter(grad_blocks), (), big_shapes)))

    zeros = [jnp.zeros(lb_logits.shape, F32), jnp.zeros(loss_part.shape, F32)]
    small_shapes = [a[n].shape for n in SMALL] + [lb_logits.shape, loss_part.shape]
    small_sum = _sum_slabs("small_sum", _all_gather("ag_small", _small_pack([g[n] for n in SMALL] + [g["hgrn_lb_logits"], loss_part])), F32)
    small = _small_unpack(small_sum, small_shapes)
    grad.update(dict(zip(SMALL, small)))
    loss = jnp.sum(small[-1])
    me = 4 * lax.axis_index("x") + 2 * lax.axis_index("y") + lax.axis_index("c")
    grad["hgrn_lb_logits"] = lax.dynamic_slice_in_dim(small[-2], me * lb_shape[1], lb_shape[1], axis=1)

    delta, new_m, new_v = {}, {}, {}
    for n in [n for n, _ in BIG] + ["hgrn_lb_logits"]:
        delta[n], new_m[n], new_v[n] = _adamw("adam_" + n, a[n], grad[n], a["m_" + n], a["v_" + n])
    packs = [_small_pack([a[p + n] for n in SMALL] + zeros) for p in ("", "m_", "v_")]
    upd = _rowwise("adam_small", _adamw_fn, [packs[0], small_sum, packs[1], packs[2]], [], [(LANES, F32)] * 3, tm=packs[0].shape[0])
    for d, u in zip((delta, new_m, new_v), upd):
        d.update(dict(zip(SMALL, _small_unpack(u, small_shapes))))
    return (loss, dx[None], *[grad[n] for n in WEIGHTS], *[delta[n] for n in WEIGHTS], *[new_m[n] for n in WEIGHTS],
            *[new_v[n] for n in WEIGHTS])


def kernel(x, mem, ffn1_norm, ffn1_w_gate, ffn1_w_up, ffn1_w_down, mix_norm, mem_norm, w_mem_kv, mem_q_gain, mem_k_gain, w_in_a, hgrn_lb_logits, hgrn_o_gain, w_in_b, fox_q_gain, kv_norm, w_kv, fox_f_bias, fox_k_gain, w_out, ffn2_norm, ffn2_w_gate, ffn2_w_up, ffn2_w_down, loss_target, m_ffn1_norm, m_ffn1_w_gate, m_ffn1_w_up, m_ffn1_w_down, m_mix_norm, m_mem_norm, m_w_mem_kv, m_mem_q_gain, m_mem_k_gain, m_w_in_a, m_hgrn_lb_logits, m_hgrn_o_gain, m_w_in_b, m_fox_q_gain, m_kv_norm, m_w_kv, m_fox_f_bias, m_fox_k_gain, m_w_out, m_ffn2_norm, m_ffn2_w_gate, m_ffn2_w_up, m_ffn2_w_down, v_ffn1_norm, v_ffn1_w_gate, v_ffn1_w_up, v_ffn1_w_down, v_mix_norm, v_mem_norm, v_w_mem_kv, v_mem_q_gain, v_mem_k_gain, v_w_in_a, v_hgrn_lb_logits, v_hgrn_o_gain, v_w_in_b, v_fox_q_gain, v_kv_norm, v_w_kv, v_fox_f_bias, v_fox_k_gain, v_w_out, v_ffn2_norm, v_ffn2_w_gate, v_ffn2_w_up, v_ffn2_w_down):
    return _train_step(dict(locals()))
```

```python
import functools

import jax
import jax.numpy as jnp
from jax import lax
from jax.experimental import pallas as pl
from jax.experimental.pallas import tpu as pltpu

F32, BF16 = jnp.float32, jnp.bfloat16
EPS = 1e-6
V7X_VMEM_LIMIT = 56 * 1024 * 1024
LANES = 128
N_DEV = 8

D_MODEL = 1024
MAIN_WIDTH = 768
MEM_WIDTH = 256
HG_HEAD_DIM = 128
HG_HEADS = 6
FOX_HEAD_DIM = 64
FOX_HEADS = 12
MEM_HEADS = 4
MEM_HEAD_DIM = 64
HG_BLOCK = 16

ADAM_LR, ADAM_B1, ADAM_B2, ADAM_EPS, ADAM_WD, ADAM_STEP = 0.001, 0.9, 0.999, 1e-08, 0.01, 10

NN = ((1,), (0,))
NT = ((1,), (1,))
TN = ((0,), (0,))


def _dot(a, b, dims, precision=None):
    return lax.dot_general(a, b, (dims, ((), ())), preferred_element_type=F32, precision=precision)


def _bdot(a, b, dims):
    return _dot(a.astype(BF16), b.astype(BF16), dims)


def _fdot(a, b, dims):
    return _dot(a, b, dims, precision=lax.Precision.HIGHEST)


def _params(n_grid):
    return pltpu.CompilerParams(dimension_semantics=("arbitrary",) * n_grid, vmem_limit_bytes=V7X_VMEM_LIMIT)


def _rms(x, g):
    return x * lax.rsqrt(jnp.mean(x * x, axis=-1, keepdims=True) + EPS) * g


def _sigmoid(x):
    return jax.nn.sigmoid(x)


def _silu(x):
    return x * jax.nn.sigmoid(x)


def _rowwise(name, fn, rows, consts, out_rows, out_reds=(), tm=256):
    R = rows[0].shape[0]
    tm = min(tm, R)
    assert R % tm == 0
    n_in, n_o = len(rows) + len(consts), len(out_rows)

    def body(*refs):
        outs = fn(*[r[...] for r in refs[:n_in]])
        if not isinstance(outs, (tuple, list)):
            outs = (outs,)
        for r, o in zip(refs[n_in:n_in + n_o], outs[:n_o]):
            r[...] = o.astype(r.dtype)
        red_refs = refs[n_in + n_o:]
        if red_refs:
            @pl.when(pl.program_id(0) == 0)
            def _():
                for r in red_refs:
                    r[...] = jnp.zeros(r.shape, r.dtype)
            for r, o in zip(red_refs, outs[n_o:]):
                r[...] += o

    zero = lambda n: (lambda i: (0,) * n)
    in_specs = [pl.BlockSpec((tm, a.shape[1]), lambda i: (i, 0)) for a in rows]
    in_specs += [pl.BlockSpec(c.shape, zero(c.ndim)) for c in consts]
    out_specs = [pl.BlockSpec((tm, c), lambda i: (i, 0)) for c, _ in out_rows]
    out_specs += [pl.BlockSpec(s, zero(len(s))) for s, _ in out_reds]
    out_shape = [jax.ShapeDtypeStruct((R, c), dt) for c, dt in out_rows]
    out_shape += [jax.ShapeDtypeStruct(s, dt) for s, dt in out_reds]
    return pl.pallas_call(body, grid=(R // tm,), in_specs=in_specs, out_specs=out_specs, out_shape=out_shape,
                          name=name, compiler_params=_params(1))(*rows, *consts)


def _tile(n, cap):
    best = None
    for t in range(LANES, min(n, cap) + 1, LANES):
        if n % t == 0:
            best = t
    return best or n


def _mm(name, pairs, out_dtypes, M, N, epi=None, extras=(), tm=512, tn=512):
    tm, tn = _tile(M, tm), _tile(N, tn)
    n_p, n_e = len(pairs), len(extras)
    modes = [m for _, _, m in pairs]

    def body(*refs):
        accs = [_bdot(refs[2 * k][...], refs[2 * k + 1][...], modes[k]) for k in range(n_p)]
        ex = [r[...] for r in refs[2 * n_p:2 * n_p + n_e]]
        outs = epi(accs, ex) if epi is not None else accs
        for r, o in zip(refs[2 * n_p + n_e:], outs):
            r[...] = o.astype(r.dtype)

    in_specs = []
    ops = []
    for a, b, mode in pairs:
        if mode == NN:
            K = a.shape[1]
            assert a.shape == (M, K) and b.shape == (K, N), (name, a.shape, b.shape)
            in_specs += [pl.BlockSpec((tm, K), lambda i, j: (i, 0)), pl.BlockSpec((K, tn), lambda i, j: (0, j))]
        elif mode == NT:
            K = a.shape[1]
            assert a.shape == (M, K) and b.shape == (N, K), (name, a.shape, b.shape)
            in_specs += [pl.BlockSpec((tm, K), lambda i, j: (i, 0)), pl.BlockSpec((tn, K), lambda i, j: (j, 0))]
        else:
            K = a.shape[0]
            assert a.shape == (K, M) and b.shape == (K, N), (name, a.shape, b.shape)
            in_specs += [pl.BlockSpec((K, tm), lambda i, j: (0, i)), pl.BlockSpec((K, tn), lambda i, j: (0, j))]
        ops += [a, b]
    in_specs += [pl.BlockSpec((tm, tn), lambda i, j: (i, j)) for _ in extras]
    out_specs = [pl.BlockSpec((tm, tn), lambda i, j: (i, j)) for _ in out_dtypes]
    out_shape = [jax.ShapeDtypeStruct((M, N), dt) for dt in out_dtypes]
    res = pl.pallas_call(body, grid=(M // tm, N // tn), in_specs=in_specs, out_specs=out_specs, out_shape=out_shape,
                         name=name, compiler_params=_params(2))(*ops, *extras)
    return res[0] if len(res) == 1 else res


def _rms_fwd(name, x, gain, dtype=BF16):
    return _rowwise(name, _rms, [x], [gain], [(x.shape[1], dtype)])[0]


def _rms_bwd(name, x, gain, dh, dres=None):
    def fn(x, dh, *rest):
        g = rest[-1]
        _, vjp = jax.vjp(_rms, x, g)
        dx, dg = vjp(dh)
        if dres is not None:
            dx = dx + rest[0]
        return dx, dg
    rows = [x, dh] + ([dres] if dres is not None else [])
    d = x.shape[1]
    return _rowwise(name, fn, rows, [gain], [(d, F32)], [((1, d), F32)])


def _ffn_specs(gcols, grows, ig, iu, idn):
    n = gcols.shape[2]
    D = grows.shape[2]
    wg = pl.BlockSpec((None, D, n), lambda i, j: (j, ig, 0))
    wu = pl.BlockSpec((None, D, n), lambda i, j: (j, iu, 0))
    wd = pl.BlockSpec((None, n, D), lambda i, j: (j, idn, 0))
    return n, wg, wu, wd


def _ffn_fwd(name, x, gain, gcols, grows, ig, iu, idn, tm=1024):
    T, D = x.shape
    tm = min(T, tm)
    n, wg_s, wu_s, wd_s = _ffn_specs(gcols, grows, ig, iu, idn)
    last = N_DEV - 1

    def body(x_ref, g_ref, wg_ref, wu_ref, wd_ref, y_ref, h_s, acc):
        j = pl.program_id(1)

        @pl.when(j == 0)
        def _():
            h_s[...] = _rms(x_ref[...], g_ref[...]).astype(BF16)
            acc[...] = jnp.zeros(acc.shape, F32)
        h = h_s[...]
        z = _silu(_dot(h, wg_ref[...], NN)) * _dot(h, wu_ref[...], NN)
        acc[...] += _dot(z.astype(BF16), wd_ref[...], NN)

        @pl.when(j == last)
        def _():
            y_ref[...] = x_ref[...] + 0.5 * acc[...]

    row = pl.BlockSpec((tm, D), lambda i, j: (i, 0))
    return pl.pallas_call(
        body, grid=(T // tm, N_DEV), in_specs=[row, pl.BlockSpec((1, D), lambda i, j: (0, 0)), wg_s, wu_s, wd_s],
        out_specs=row, out_shape=jax.ShapeDtypeStruct((T, D), F32),
        scratch_shapes=[pltpu.VMEM((tm, D), BF16), pltpu.VMEM((tm, D), F32)],
        name=name, compiler_params=_params(2))(x, gain, gcols, gcols, grows)


def _ffn_bwd(tag, x, gain, gcols, grows, ig, iu, idn, dy, tm=512):
    T, D = x.shape
    tm = min(T, tm)
    n, wg_s, wu_s, wd_s = _ffn_specs(gcols, grows, ig, iu, idn)
    last = N_DEV - 1

    def body(x_ref, dy_ref, g_ref, wg_ref, wu_ref, wd_ref, dx_ref, dg_ref, h_ref, z_ref, da_ref, db_ref, dh_acc):
        i, j = pl.program_id(0), pl.program_id(1)

        @pl.when(j == 0)
        def _():
            h_ref[...] = _rms(x_ref[...], g_ref[...]).astype(BF16)
            dh_acc[...] = jnp.zeros(dh_acc.shape, F32)

        @pl.when((i == 0) & (j == 0))
        def _():
            dg_ref[...] = jnp.zeros(dg_ref.shape, F32)
        h = h_ref[...]
        a, b = _dot(h, wg_ref[...], NN), _dot(h, wu_ref[...], NN)
        dz = 0.5 * _dot(dy_ref[...].astype(BF16), wd_ref[...], NT)
        s = _sigmoid(a)
        si = a * s
        da = (dz * b * (s + si * (1.0 - s))).astype(BF16)
        db = (dz * si).astype(BF16)
        z_ref[...] = (si * b).astype(BF16)
        da_ref[...] = da
        db_ref[...] = db
        dh_acc[...] += _dot(da, wg_ref[...], NT) + _dot(db, wu_ref[...], NT)

        @pl.when(j == last)
        def _():
            _, vjp = jax.vjp(_rms, x_ref[...], g_ref[...])
            dx, dg = vjp(dh_acc[...])
            dx_ref[...] = dx + dy_ref[...]
            dg_ref[...] += dg

    row = pl.BlockSpec((tm, D), lambda i, j: (i, 0))
    vec = pl.BlockSpec((1, D), lambda i, j: (0, 0))
    hid = pl.BlockSpec((None, tm, n), lambda i, j: (j, i, 0))
    hidden = jax.ShapeDtypeStruct((N_DEV, T, n), BF16)
    dx, dgain, h, z, da, db = pl.pallas_call(
        body, grid=(T // tm, N_DEV), in_specs=[row, row, vec, wg_s, wu_s, wd_s],
        out_specs=[row, vec, row, hid, hid, hid],
        out_shape=[jax.ShapeDtypeStruct((T, D), F32), jax.ShapeDtypeStruct((1, D), F32), jax.ShapeDtypeStruct((T, D), BF16),
                   hidden, hidden, hidden],
        scratch_shapes=[pltpu.VMEM((tm, D), F32)],
        name=tag + "_a", compiler_params=_params(2))(x, dy, gain, gcols, gcols, grows)

    def wbody(h_ref, dy_ref, z_ref, da_ref, db_ref, dwg_ref, dwu_ref, dwd_ref):
        h = h_ref[...]
        dwg_ref[...] = _dot(h, da_ref[...], TN).astype(BF16)
        dwu_ref[...] = _dot(h, db_ref[...], TN).astype(BF16)
        dwd_ref[...] = (0.5 * _dot(z_ref[...], dy_ref[...].astype(BF16), TN)).astype(BF16)

    full = pl.BlockSpec((T, D), lambda j: (0, 0))
    hid_all = pl.BlockSpec((None, T, n), lambda j: (j, 0, 0))
    dwg, dwu, dwd = pl.pallas_call(
        wbody, grid=(N_DEV,), in_specs=[full, full, hid_all, hid_all, hid_all],
        out_specs=[pl.BlockSpec((None, D, n), lambda j: (j, 0, 0))] * 2 + [pl.BlockSpec((None, n, D), lambda j: (j, 0, 0))],
        out_shape=[jax.ShapeDtypeStruct((N_DEV, D, n), BF16)] * 2 + [jax.ShapeDtypeStruct((N_DEV, n, D), BF16)],
        name=tag + "_w", compiler_params=_params(1))(h, dy, z, da, db)
    return dx, dgain, dwg, dwu, dwd


def _wcols_spec(gw, l, grid_rank):
    _, _, n = gw.shape
    K = D_MODEL
    zero = (lambda i: (0, l, 0)) if grid_rank == 1 else (lambda i, j: (0, l, 0))
    return n, K, pl.BlockSpec((N_DEV, K, n), zero)


def _proj_cols(name, h, gw, l, tm=512):
    T = h.shape[0]
    tm = min(T, tm)
    n, K, wspec = _wcols_spec(gw, l, 1)

    def body(h_ref, w_ref, o_ref):
        h = h_ref[...]
        for j in range(N_DEV):
            o_ref[:, pl.ds(j * n, n)] = _dot(h, w_ref[j], NN)

    return pl.pallas_call(
        body, grid=(T // tm,), in_specs=[pl.BlockSpec((tm, K), lambda i: (i, 0)), wspec],
        out_specs=pl.BlockSpec((tm, N_DEV * n), lambda i: (i, 0)), out_shape=jax.ShapeDtypeStruct((T, N_DEV * n), F32),
        name=name, compiler_params=_params(1))(h, gw)


def _proj_cols_bwd(tag, h, dproj, gw, l, tm=512, tk=512):
    T = h.shape[0]
    tm = min(T, tm)
    n, K, wspec = _wcols_spec(gw, l, 1)

    def dh_body(dp_ref, w_ref, o_ref):
        acc = jnp.zeros(o_ref.shape, F32)
        for j in range(N_DEV):
            acc = acc + _dot(dp_ref[:, pl.ds(j * n, n)], w_ref[j], NT)
        o_ref[...] = acc

    dh = pl.pallas_call(
        dh_body, grid=(T // tm,), in_specs=[pl.BlockSpec((tm, N_DEV * n), lambda i: (i, 0)), wspec],
        out_specs=pl.BlockSpec((tm, K), lambda i: (i, 0)), out_shape=jax.ShapeDtypeStruct((T, K), F32),
        name=tag + "_dh", compiler_params=_params(1))(dproj, gw)

    def dw_body(h_ref, dp_ref, o_ref):
        h = h_ref[...]
        for j in range(N_DEV):
            o_ref[j] = _dot(h, dp_ref[:, pl.ds(j * n, n)], TN).astype(BF16)

    dw = pl.pallas_call(
        dw_body, grid=(K // tk,), in_specs=[pl.BlockSpec((T, tk), lambda i: (0, i)), pl.BlockSpec((T, N_DEV * n), lambda i: (0, 0))],
        out_specs=pl.BlockSpec((N_DEV, tk, n), lambda i: (0, i, 0)), out_shape=jax.ShapeDtypeStruct((N_DEV, K, n), BF16),
        name=tag + "_dw", compiler_params=_params(1))(h, dproj)
    return dh, dw


def _block_tri(n, reverse=False):
    r = lax.broadcasted_iota(jnp.int32, (n, n), 0)
    c = lax.broadcasted_iota(jnp.int32, (n, n), 1)
    same = (r // HG_BLOCK) == (c // HG_BLOCK)
    return (same & ((c >= r) if reverse else (c <= r))).astype(F32)


def _hgrn_prep(q_ref, f_ref, lbv, qs, ks, cs, T):
    pt = min(T, 256)
    tri = _block_tri(pt)
    for p in range(T // pt):
        rows = pl.ds(p * pt, pt)
        f = lbv + (1.0 - lbv) * _sigmoid(f_ref[rows, :])
        qs[rows, :] = _silu(q_ref[rows, :])
        ks[rows, :] = 1.0 - f
        cs[rows, :] = _dot(tri, jnp.log(f), NN, precision=lax.Precision.HIGHEST)


def _strided(t, nb):
    return pl.ds(t, nb, stride=HG_BLOCK)


def _gate_out(o, og, g):
    return _rms(o, og) * _silu(g)


def _hgrn_fwd(name, proj, lb, og):
    T = proj.shape[0]
    nb = T // HG_BLOCK
    hd = HG_HEAD_DIM

    def body(q_ref, f_ref, i_ref, g_ref, lb_ref, og_ref, main_ref, o_ref, qs, ks, cs):
        _hgrn_prep(q_ref, f_ref, lb_ref[...], qs, ks, cs, T)
        for t in range(HG_BLOCK):
            qt, ct = qs[_strided(t, nb), :], cs[_strided(t, nb), :]
            acc = jnp.zeros((nb, hd), F32)
            for s in range(t + 1):
                w = qt * ks[_strided(s, nb), :] * jnp.exp(ct - cs[_strided(s, nb), :])
                acc = acc + jnp.sum(w, axis=-1, keepdims=True) * i_ref[_strided(s, nb), :]
            o_ref[_strided(t, nb), :] = acc

        def step(n, st):
            rows = pl.ds(pl.multiple_of(n * HG_BLOCK, HG_BLOCK), HG_BLOCK)
            c = cs[rows, :]
            ce = c[HG_BLOCK - 1:HG_BLOCK, :]
            o_ref[rows, :] += _fdot(qs[rows, :] * jnp.exp(c), st, NT)
            return jnp.exp(ce) * st + _fdot(i_ref[rows, :], ks[rows, :] * jnp.exp(ce - c), TN)

        lax.fori_loop(0, nb, step, jnp.zeros((hd, hd), F32))
        pt = min(T, 256)
        for p in range(T // pt):
            rows = pl.ds(p * pt, pt)
            main_ref[rows, :] = _gate_out(o_ref[rows, :], og_ref[...], g_ref[rows, :])

    nh = HG_HEADS
    col = lambda off: pl.BlockSpec((T, hd), lambda h, off=off: (0, off + h))
    return pl.pallas_call(
        body, grid=(nh,),
        in_specs=[col(0), col(nh), col(2 * nh), col(3 * nh), pl.BlockSpec((1, hd), lambda h: (0, h)),
                  pl.BlockSpec((1, hd), lambda h: (0, 0))],
        out_specs=[col(0), col(0)],
        out_shape=[jax.ShapeDtypeStruct((T, MAIN_WIDTH), F32)] * 2,
        scratch_shapes=[pltpu.VMEM((T, hd), F32)] * 3,
        name=name, compiler_params=_params(1))(proj, proj, proj, proj, lb, og)


def _hgrn_bwd(name, proj, o, dmix, lb, og):
    T = proj.shape[0]
    nb = T // HG_BLOCK
    hd = HG_HEAD_DIM
    pt = min(T, 256)

    def body(q_ref, f_ref, i_ref, g_ref, o_ref, dm_ref, lb_ref, og_ref,
             dq_ref, df_ref, di_ref, dg_ref, dlb_ref, dog_ref, qs, ks, cs, dos, dqs, dks, dvs, states):
        lbv = lb_ref[...]
        _hgrn_prep(q_ref, f_ref, lbv, qs, ks, cs, T)
        dog = jnp.zeros((1, hd), F32)
        for p in range(T // pt):
            rows = pl.ds(p * pt, pt)
            _, vjp = jax.vjp(_gate_out, o_ref[rows, :], og_ref[...], g_ref[rows, :])
            do, dog_p, dg = vjp(dm_ref[rows, :])
            dos[rows, :] = do
            dg_ref[rows, :] = dg.astype(dg_ref.dtype)
            dog = dog + dog_p

        @pl.when(pl.program_id(0) == 0)
        def _():
            dog_ref[...] = jnp.zeros(dog_ref.shape, F32)
        dog_ref[...] += dog

        for t in range(HG_BLOCK):
            dqs[_strided(t, nb), :] = jnp.zeros((nb, hd), F32)
        for s in range(HG_BLOCK):
            k_s, c_s, v_s = ks[_strided(s, nb), :], cs[_strided(s, nb), :], i_ref[_strided(s, nb), :]
            dk = jnp.zeros((nb, hd), F32)
            dv = jnp.zeros((nb, hd), F32)
            for t in range(s, HG_BLOCK):
                q_t, do_t = qs[_strided(t, nb), :], dos[_strided(t, nb), :]
                e = jnp.exp(cs[_strided(t, nb), :] - c_s)
                a = jnp.sum(q_t * k_s * e, axis=-1, keepdims=True)
                g = jnp.sum(do_t * v_s, axis=-1, keepdims=True)
                dqs[_strided(t, nb), :] += g * k_s * e
                dk = dk + g * q_t * e
                dv = dv + a * do_t
            dks[_strided(s, nb), :] = dk
            dvs[_strided(s, nb), :] = dv

        def fwd_step(n, st):
            rows = pl.ds(pl.multiple_of(n * HG_BLOCK, HG_BLOCK), HG_BLOCK)
            states[n] = st
            c = cs[rows, :]
            ce = c[HG_BLOCK - 1:HG_BLOCK, :]
            return jnp.exp(ce) * st + _fdot(i_ref[rows, :], ks[rows, :] * jnp.exp(ce - c), TN)

        lax.fori_loop(0, nb, fwd_step, jnp.zeros((hd, hd), F32))

        def bwd_step(m, dst):
            n = nb - 1 - m
            rows = pl.ds(pl.multiple_of(n * HG_BLOCK, HG_BLOCK), HG_BLOCK)
            c = cs[rows, :]
            ce = c[HG_BLOCK - 1:HG_BLOCK, :]
            ec, ek = jnp.exp(c), jnp.exp(ce - c)
            do, v = dos[rows, :], i_ref[rows, :]
            dqs[rows, :] += _fdot(do, states[n], NN) * ec
            dks[rows, :] += _fdot(v, dst, NN) * ek
            dvs[rows, :] += _fdot(ks[rows, :] * ek, dst, NT)
            return jnp.exp(ce) * dst + _fdot(do, qs[rows, :] * ec, TN)

        lax.fori_loop(0, nb, bwd_step, jnp.zeros((hd, hd), F32))

        full = (lax.broadcasted_iota(jnp.int32, (pt, pt), 1) >= lax.broadcasted_iota(jnp.int32, (pt, pt), 0)).astype(F32)
        carry = jnp.zeros((1, hd), F32)
        dlb = jnp.zeros((1, hd), F32)
        for p in reversed(range(T // pt)):
            rows = pl.ds(p * pt, pt)
            q, k, dq, dk = qs[rows, :], ks[rows, :], dqs[rows, :], dks[rows, :]
            db = q * dq - k * dk
            dlf = _dot(full, db, NN, precision=lax.Precision.HIGHEST) + carry
            carry = carry + jnp.sum(db, axis=0, keepdims=True)
            sg = _sigmoid(f_ref[rows, :])
            df = dlf / (1.0 - k) - dk
            df_ref[rows, :] = (df * (1.0 - lbv) * sg * (1.0 - sg)).astype(df_ref.dtype)
            dlb = dlb + jnp.sum(df * (1.0 - sg), axis=0, keepdims=True)
            qr = q_ref[rows, :]
            sq = _sigmoid(qr)
            dq_ref[rows, :] = (dq * (sq + qr * sq * (1.0 - sq))).astype(dq_ref.dtype)
            di_ref[rows, :] = dvs[rows, :].astype(di_ref.dtype)
        dlb_ref[...] = dlb

    nh = HG_HEADS
    col = lambda off: pl.BlockSpec((T, hd), lambda h, off=off: (0, off + h))
    vec = pl.BlockSpec((1, hd), lambda h: (0, h))
    one = pl.BlockSpec((1, hd), lambda h: (0, 0))
    return pl.pallas_call(
        body, grid=(nh,),
        in_specs=[col(0), col(nh), col(2 * nh), col(3 * nh), col(0), col(0), vec, one],
        out_specs=[col(0), col(0), col(0), col(0), vec, one],
        out_shape=[jax.ShapeDtypeStruct((T, MAIN_WIDTH), BF16)] * 4
        + [jax.ShapeDtypeStruct((1, MAIN_WIDTH), F32), jax.ShapeDtypeStruct((1, hd), F32)],
        scratch_shapes=[pltpu.VMEM((T, hd), F32)] * 7 + [pltpu.VMEM((nb, hd, hd), F32)],
        name=name, compiler_params=_params(1))(proj, proj, proj, proj, o, dmix, lb, og)


def _softmax_rows(s):
    p = jnp.exp(s - jnp.max(s, axis=-1, keepdims=True))
    return p, jnp.sum(p, axis=-1, keepdims=True)


def _fox_logits(q, k, cc, cr, q0):
    s = _dot(q, k, NT) * (FOX_HEAD_DIM ** -0.5) + cc - cr
    row = lax.broadcasted_iota(jnp.int32, s.shape, 0) + q0
    col = lax.broadcasted_iota(jnp.int32, s.shape, 1)
    return jnp.where(col <= row, s, -jnp.inf)


def _fox_specs(T):
    w = 2 * FOX_HEAD_DIM
    n = MAIN_WIDTH // w
    col = lambda off: pl.BlockSpec((T, w), lambda p, off=off: (0, off + p))
    cc = pl.BlockSpec((2, T, 1), lambda p: (p, 0, 0))
    cr = pl.BlockSpec((2, 1, T), lambda p: (p, 0, 0))
    gain = pl.BlockSpec((1, FOX_HEAD_DIM), lambda p: (0, 0))
    return n, col, cc, cr, gain


def _fox_fwd(name, proj, kvf, cc, cr, gq, gk):
    T = proj.shape[0]
    tq = min(T, 256)
    hd = FOX_HEAD_DIM

    def body(q_ref, g_ref, k_ref, v_ref, cc_ref, cr_ref, gq_ref, gk_ref, main_ref, o_ref):
        for hh in range(2):
            lanes = pl.ds(hh * hd, hd)
            k = _rms(k_ref[:, lanes], gk_ref[...]).astype(BF16)
            v = v_ref[:, lanes].astype(BF16)
            for qi in range(T // tq):
                rows, kl = pl.ds(qi * tq, tq), (qi + 1) * tq
                q = _rms(q_ref[rows, lanes], gq_ref[...]).astype(BF16)
                s = _fox_logits(q, k[:kl], cc_ref[hh, rows, :], cr_ref[hh, :, pl.ds(0, kl)], qi * tq)
                p, l = _softmax_rows(s)
                o = _dot(p.astype(BF16), v[:kl], NN) / l
                o_ref[rows, lanes] = o
                main_ref[rows, lanes] = o * _sigmoid(g_ref[rows, lanes])

    n, col, ccs, crs, gain = _fox_specs(T)
    return pl.pallas_call(
        body, grid=(n,), in_specs=[col(0), col(n), col(0), col(n), ccs, crs, gain, gain], out_specs=[col(0), col(0)],
        out_shape=[jax.ShapeDtypeStruct((T, MAIN_WIDTH), F32)] * 2,
        name=name, compiler_params=_params(1))(proj, proj, kvf, kvf, cc, cr, gq, gk)


def _fox_bwd(name, proj, kvf, cc, cr, gq, gk, o, dmix, pdk, pdv, pdc):
    T = proj.shape[0]
    tq = min(T, 256)
    hd = FOX_HEAD_DIM
    scale = hd ** -0.5

    def body(q_ref, g_ref, k_ref, v_ref, cc_ref, cr_ref, gq_ref, gk_ref, o_ref, dm_ref, pdk_ref, pdv_ref, pdc_ref,
             dq_ref, dg_ref, dk_ref, dv_ref, dc_ref, dgq_ref, dgk_ref, dka, dva, dca):
        dgq = jnp.zeros((1, hd), F32)
        dgk = jnp.zeros((1, hd), F32)
        for hh in range(2):
            lanes = pl.ds(hh * hd, hd)
            k32, vjp_k = jax.vjp(_rms, k_ref[:, lanes], gk_ref[...])
            k = k32.astype(BF16)
            v = v_ref[:, lanes].astype(BF16)
            dka[...] = jnp.zeros(dka.shape, F32)
            dva[...] = jnp.zeros(dva.shape, F32)
            dca[...] = jnp.zeros(dca.shape, F32)
            for qi in range(T // tq):
                rows, kl = pl.ds(qi * tq, tq), (qi + 1) * tq
                q32, vjp_q = jax.vjp(_rms, q_ref[rows, lanes], gq_ref[...])
                q = q32.astype(BF16)
                s = _fox_logits(q, k[:kl], cc_ref[hh, rows, :], cr_ref[hh, :, pl.ds(0, kl)], qi * tq)
                p, l = _softmax_rows(s)
                p = p / l
                sg = _sigmoid(g_ref[rows, lanes])
                dm = dm_ref[rows, lanes]
                do = (dm * sg).astype(BF16)
                dg_ref[rows, lanes] = (dm * o_ref[rows, lanes] * sg * (1.0 - sg)).astype(dg_ref.dtype)
                dp = _dot(do, v[:kl], NT)
                ds = p * (dp - jnp.sum(p * dp, axis=-1, keepdims=True))
                dsb = ds.astype(BF16)
                dqr, dgq_p = vjp_q(_dot(dsb, k[:kl], NN) * scale)
                dq_ref[rows, lanes] = dqr.astype(dq_ref.dtype)
                dgq = dgq + dgq_p
                dka[pl.ds(0, kl), :] += _dot(dsb, q, TN) * scale
                dva[pl.ds(0, kl), :] += _dot(p.astype(BF16), do, TN)
                dca[:, pl.ds(0, kl)] -= jnp.sum(ds, axis=0, keepdims=True)
            dkr, dgk_p = vjp_k(dka[...])
            dgk = dgk + dgk_p
            dk_ref[:, lanes] = dkr + pdk_ref[:, lanes]
            dv_ref[:, lanes] = dva[...] + pdv_ref[:, lanes]
            dc_ref[hh] = dca[...] + pdc_ref[hh]

        @pl.when(pl.program_id(0) == 0)
        def _():
            dgq_ref[...] = jnp.zeros(dgq_ref.shape, F32)
            dgk_ref[...] = jnp.zeros(dgk_ref.shape, F32)
        dgq_ref[...] += dgq
        dgk_ref[...] += dgk

    n, col, ccs, crs, gain = _fox_specs(T)
    wide = jax.ShapeDtypeStruct((T, MAIN_WIDTH), F32)
    half = jax.ShapeDtypeStruct((T, MAIN_WIDTH), BF16)
    return pl.pallas_call(
        body, grid=(n,),
        in_specs=[col(0), col(n), col(0), col(n), ccs, crs, gain, gain, col(0), col(0), col(0), col(0), crs],
        out_specs=[col(0), col(0), col(0), col(0), crs, gain, gain],
        out_shape=[half, half, wide, wide, jax.ShapeDtypeStruct((FOX_HEADS, 1, T), F32),
                   jax.ShapeDtypeStruct((1, hd), F32), jax.ShapeDtypeStruct((1, hd), F32)],
        scratch_shapes=[pltpu.VMEM((T, hd), F32), pltpu.VMEM((T, hd), F32), pltpu.VMEM((1, T), F32)],
        name=name, compiler_params=_params(1))(proj, proj, kvf, kvf, cc, cr, gq, gk, o, dmix, pdk, pdv, pdc)


def _mem_specs(T, width):
    tq = min(T, 512)
    q = pl.BlockSpec((tq, MEM_WIDTH), lambda i, c=(width - MEM_WIDTH) // MEM_WIDTH: (i, c))
    gain = pl.BlockSpec((1, MEM_HEAD_DIM), lambda i: (0, 0))
    return tq, q, gain


def _mem_fwd(name, proj, kv, gq, gk):
    T, W = proj.shape
    hd = MEM_HEAD_DIM
    tq, qspec, gain = _mem_specs(T, W)

    def body(q_ref, kv_ref, gq_ref, gk_ref, o_ref):
        for h in range(MEM_HEADS):
            lanes = pl.ds(h * hd, hd)
            q = _rms(q_ref[:, lanes], gq_ref[...]).astype(BF16)
            k = _rms(kv_ref[:, lanes], gk_ref[...]).astype(BF16)
            v = kv_ref[:, pl.ds(MEM_WIDTH + h * hd, hd)].astype(BF16)
            p, l = _softmax_rows(_dot(q, k, NT) * (hd ** -0.5))
            o_ref[:, lanes] = _dot(p.astype(BF16), v, NN) / l

    return pl.pallas_call(
        body, grid=(T // tq,),
        in_specs=[qspec, pl.BlockSpec(kv.shape, lambda i: (0, 0)), gain, gain],
        out_specs=pl.BlockSpec((tq, MEM_WIDTH), lambda i: (i, 0)),
        out_shape=jax.ShapeDtypeStruct((T, MEM_WIDTH), F32),
        name=name, compiler_params=_params(1))(proj, kv, gq, gk)


def _mem_bwd(name, proj, kv, gq, gk, dmix):
    T, W = proj.shape
    hd = MEM_HEAD_DIM
    scale = hd ** -0.5
    tq, qspec, gain = _mem_specs(T, W)

    def body(q_ref, kv_ref, gq_ref, gk_ref, dm_ref, dq_ref, dkv_ref, dgq_ref, dgk_ref):
        @pl.when(pl.program_id(0) == 0)
        def _():
            dkv_ref[...] = jnp.zeros(dkv_ref.shape, F32)
            dgq_ref[...] = jnp.zeros(dgq_ref.shape, F32)
            dgk_ref[...] = jnp.zeros(dgk_ref.shape, F32)
        for h in range(MEM_HEADS):
            lanes = pl.ds(h * hd, hd)
            vl = pl.ds(MEM_WIDTH + h * hd, hd)
            q32, vjp_q = jax.vjp(_rms, q_ref[:, lanes], gq_ref[...])
            k32, vjp_k = jax.vjp(_rms, kv_ref[:, lanes], gk_ref[...])
            q, k, v = q32.astype(BF16), k32.astype(BF16), kv_ref[:, vl].astype(BF16)
            p, l = _softmax_rows(_dot(q, k, NT) * scale)
            p = p / l
            do = dm_ref[:, lanes].astype(BF16)
            dp = _dot(do, v, NT)
            dsb = (p * (dp - jnp.sum(p * dp, axis=-1, keepdims=True))).astype(BF16)
            dqr, dgq_p = vjp_q(_dot(dsb, k, NN) * scale)
            dkr, dgk_p = vjp_k(_dot(dsb, q, TN) * scale)
            dq_ref[:, lanes] = dqr.astype(dq_ref.dtype)
            dkv_ref[:, lanes] += dkr
            dkv_ref[:, vl] += _dot(p.astype(BF16), do, TN)
            dgq_ref[...] += dgq_p
            dgk_ref[...] += dgk_p

    return pl.pallas_call(
        body, grid=(T // tq,),
        in_specs=[qspec, pl.BlockSpec(kv.shape, lambda i: (0, 0)), gain, gain,
                  pl.BlockSpec((tq, MEM_WIDTH), lambda i: (i, MAIN_WIDTH // MEM_WIDTH))],
        out_specs=[pl.BlockSpec((tq, MEM_WIDTH), lambda i: (i, 0)), pl.BlockSpec(kv.shape, lambda i: (0, 0)), gain, gain],
        out_shape=[jax.ShapeDtypeStruct((T, MEM_WIDTH), BF16), jax.ShapeDtypeStruct(kv.shape, F32),
                   jax.ShapeDtypeStruct((1, hd), F32), jax.ShapeDtypeStruct((1, hd), F32)],
        name=name, compiler_params=_params(1))(proj, kv, gq, gk, dmix)


def _cumsum_rows(name, x, reverse=False):
    T, C = x.shape
    pt = min(T, 256)

    def body(x_ref, o_ref):
        r = lax.broadcasted_iota(jnp.int32, (pt, pt), 0)
        c = lax.broadcasted_iota(jnp.int32, (pt, pt), 1)
        tri = ((c >= r) if reverse else (c <= r)).astype(F32)
        carry = jnp.zeros((1, C), F32)
        order = range(T // pt)
        for p in (reversed(order) if reverse else order):
            rows = pl.ds(p * pt, pt)
            blk = x_ref[rows, :]
            o_ref[rows, :] = _dot(tri, blk, NN, precision=lax.Precision.HIGHEST) + carry
            carry = carry + jnp.sum(blk, axis=0, keepdims=True)

    return pl.pallas_call(body, out_shape=jax.ShapeDtypeStruct((T, C), F32), name=name,
                          compiler_params=pltpu.CompilerParams(vmem_limit_bytes=V7X_VMEM_LIMIT))(x)


MESH = pl.DeviceIdType.MESH
ANY = pl.BlockSpec(memory_space=pl.ANY)


def _mesh_pos():
    return lax.axis_index("x"), lax.axis_index("y"), lax.axis_index("c")


def _all_gather(name, xs):
    n = len(xs)

    def body(*refs):
        x_refs, out_refs = refs[:n], refs[n:2 * n]
        send_sems, recv_sems, local_sems = refs[2 * n:]
        mx, my, mc = _mesh_pos()
        me, sibling = (mx, my, mc), (mx, my, 1 - mc)
        chips = [(1 - mx, my), (mx, 1 - my), (1 - mx, 1 - my)]

        def slot(a, px, py, pc):
            return out_refs[a].at[4 * px + 2 * py + pc]

        def copy(a, k, block, to, src=None):
            return pltpu.make_async_remote_copy(
                src_ref=slot(a, *block) if src is None else src, dst_ref=slot(a, *block),
                send_sem=send_sems.at[7 * a + k], recv_sem=recv_sems.at[7 * a + k], device_id=to, device_id_type=MESH)

        mine = [pltpu.make_async_copy(x_refs[a], slot(a, *me), local_sems.at[a]) for a in range(n)]
        first = []
        for a in range(n):
            mine[a].start()
            first.append(copy(a, 0, me, sibling, src=x_refs[a]))
            first += [copy(a, 1 + j, me, (*chip, mc), src=x_refs[a]) for j, chip in enumerate(chips)]
        for cp in first:
            cp.start()
        passed = []
        for j, chip in enumerate(chips):
            for a in range(n):
                copy(a, 1 + j, (*chip, mc), me).wait_recv()
                passed.append(copy(a, 4 + j, (*chip, mc), sibling))
                passed[-1].start()
        for a in range(n):
            copy(a, 0, sibling, me).wait_recv()
            for j, chip in enumerate(chips):
                copy(a, 4 + j, (*chip, 1 - mc), me).wait_recv()
        for cp in first + passed:
            cp.wait_send()
        for cp in mine:
            cp.wait()

    return pl.pallas_call(
        body, out_shape=[jax.ShapeDtypeStruct((N_DEV,) + x.shape, x.dtype) for x in xs], in_specs=[ANY] * n, out_specs=[ANY] * n,
        scratch_shapes=[pltpu.SemaphoreType.DMA((7 * n,)), pltpu.SemaphoreType.DMA((7 * n,)), pltpu.SemaphoreType.DMA((n,))],
        name=name)(*xs)


def _exchange_cores(name, gs):
    n = len(gs)

    def body(*refs):
        g_refs, recv_refs = refs[:n], refs[n:2 * n]
        send_sems, recv_sems = refs[2 * n:]
        mx, my, mc = _mesh_pos()
        swap = [pltpu.make_async_remote_copy(
            src_ref=g_refs[a].at[2 * q + (1 - mc)], dst_ref=recv_refs[a].at[q], send_sem=send_sems.at[4 * a + q],
            recv_sem=recv_sems.at[4 * a + q], device_id=(mx, my, 1 - mc), device_id_type=MESH) for a in range(n) for q in range(4)]
        for cp in swap:
            cp.start()
        for cp in swap:
            cp.wait()

    return pl.pallas_call(
        body, out_shape=[jax.ShapeDtypeStruct((4,) + g.shape[1:], g.dtype) for g in gs], in_specs=[ANY] * n, out_specs=[ANY] * n,
        scratch_shapes=[pltpu.SemaphoreType.DMA((4 * n,)), pltpu.SemaphoreType.DMA((4 * n,))],
        name=name)(*gs)


def _exchange_chips(name, ss):
    n = len(ss)

    def body(*refs):
        s_refs, recv_refs = refs[:n], refs[n:2 * n]
        send_sems, recv_sems, local_sems = refs[2 * n:]
        mx, my, mc = _mesh_pos()
        myq = 2 * mx + my
        chips = [(1 - mx, my), (mx, 1 - my), (1 - mx, 1 - my)]
        mine = [pltpu.make_async_copy(s_refs[a].at[myq], recv_refs[a].at[myq], local_sems.at[a]) for a in range(n)]
        for cp in mine:
            cp.start()
        swap = [pltpu.make_async_remote_copy(
            src_ref=s_refs[a].at[2 * px + py], dst_ref=recv_refs[a].at[myq], send_sem=send_sems.at[3 * a + k],
            recv_sem=recv_sems.at[3 * a + k], device_id=(px, py, mc), device_id_type=MESH)
            for a in range(n) for k, (px, py) in enumerate(chips)]
        for cp in swap:
            cp.start()
        for a in range(n):
            for k, (px, py) in enumerate(chips):
                pltpu.make_async_remote_copy(
                    src_ref=s_refs[a].at[myq], dst_ref=recv_refs[a].at[2 * px + py], send_sem=send_sems.at[3 * a + k],
                    recv_sem=recv_sems.at[3 * a + k], device_id=(px, py, mc), device_id_type=MESH).wait_recv()
        for cp in swap:
            cp.wait_send()
        for cp in mine:
            cp.wait()

    return pl.pallas_call(
        body, out_shape=[jax.ShapeDtypeStruct(s.shape, s.dtype) for s in ss], in_specs=[ANY] * n, out_specs=[ANY] * n,
        scratch_shapes=[pltpu.SemaphoreType.DMA((3 * n,)), pltpu.SemaphoreType.DMA((3 * n,)), pltpu.SemaphoreType.DMA((n,))],
        name=name)(*ss)


def _pair_sum(name, g, recv, mc):
    _, R, C = g.shape
    tm = _row_tile(R, 512)

    def body(mc_ref, own_ref, recv_ref, o_ref):
        o_ref[...] = (own_ref[...].astype(F32) + recv_ref[...].astype(F32)).astype(o_ref.dtype)

    spec = pl.BlockSpec((None, tm, C), lambda q, i, mc_ref: (q, i, 0))
    grid_spec = pltpu.PrefetchScalarGridSpec(
        num_scalar_prefetch=1, grid=(4, R // tm),
        in_specs=[pl.BlockSpec((None, tm, C), lambda q, i, mc_ref: (2 * q + mc_ref[0], i, 0)), spec], out_specs=spec)
    return pl.pallas_call(body, grid_spec=grid_spec, out_shape=jax.ShapeDtypeStruct((4, R, C), BF16), name=name,
                          compiler_params=_params(2))(mc, g, recv)


def _row_tile(R, cap):
    best = None
    for t in range(8, min(R, cap) + 1, 8):
        if R % t == 0:
            best = t
    return best or R


def _sum_slabs(name, a, out_dtype):
    n, R, C = a.shape
    tm = _row_tile(R, 512)

    def body(*refs):
        acc = refs[0][...].astype(F32)
        for r in refs[1:n]:
            acc = acc + r[...].astype(F32)
        refs[n][...] = acc.astype(out_dtype)

    return pl.pallas_call(
        body, grid=(R // tm,),
        in_specs=[pl.BlockSpec((None, tm, C), lambda i, q=q: (q, i, 0)) for q in range(n)],
        out_specs=pl.BlockSpec((tm, C), lambda i: (i, 0)), out_shape=jax.ShapeDtypeStruct((R, C), out_dtype),
        name=name, compiler_params=_params(1))(*([a] * n))


def _reduce_scatter(gs):
    mc = lax.axis_index("c").astype(jnp.int32).reshape(1)
    recvs = _exchange_cores("rs_cores", gs)
    pairs = [_pair_sum(f"rs_pair_sum{a}", g, r, mc) for a, (g, r) in enumerate(zip(gs, recvs))]
    return [_sum_slabs(f"rs_chip_sum{a}", r, F32) for a, r in enumerate(_exchange_chips("rs_chips", pairs))]


SMALL = ["ffn1_norm", "mix_norm", "mem_norm", "mem_q_gain", "mem_k_gain", "hgrn_o_gain", "fox_q_gain", "kv_norm",
         "fox_f_bias", "fox_k_gain", "ffn2_norm"]
COLS352 = ["ffn1_w_gate", "ffn1_w_up", "ffn2_w_gate", "ffn2_w_up"]
KV_SPLIT = 1024
KV_WIDTH = 2 * MAIN_WIDTH + FOX_HEADS


def _rows2d(w):
    return w.reshape(-1, w.shape[-1])


def _pad_cols(w, width):
    return jnp.pad(w, [(0, 0)] * (w.ndim - 1) + [(0, width - w.shape[-1])])


def _pack_rows1024(down1, down2, w_out, w_mem_kv, w_kv):
    kv = jnp.concatenate([w_kv[:, :KV_SPLIT], _pad_cols(w_kv[:, KV_SPLIT:], D_MODEL)], axis=0)
    return jnp.concatenate([_rows2d(down1), _rows2d(down2), _rows2d(w_out), _rows2d(_pad_cols(w_mem_kv, D_MODEL)), kv], axis=0)


def _unpack_rows1024(buf, shapes):
    out, off = [], 0
    for name in ("ffn1_w_down", "ffn2_w_down", "w_out", "w_mem_kv"):
        L, r, c = shapes[name]
        out.append(buf[off:off + L * r].reshape(L, r, D_MODEL)[:, :, :c])
        off += L * r
    r, c = shapes["w_kv"]
    out.append(jnp.concatenate([buf[off:off + r], buf[off + r:off + 2 * r, :c - KV_SPLIT]], axis=1))
    return out


def _pad128(a):
    flat = a.reshape(-1)
    return jnp.pad(flat, (0, -flat.shape[0] % LANES))


def _small_pack(parts):
    flat = jnp.concatenate([_pad128(p) for p in parts])
    rows = -(-flat.shape[0] // LANES)
    flat = jnp.pad(flat, (0, (-rows % 8) * LANES))
    return flat.reshape(-1, LANES)


def _small_unpack(buf, shapes):
    flat = buf.reshape(-1)
    out, off = [], 0
    for s in shapes:
        n = 1
        for d in s:
            n *= d
        out.append(flat[off:off + n].reshape(s))
        off += n + (-n % LANES)
    return out


def _lb_fn(l0, l1):
    m = lax.stop_gradient(jnp.maximum(l0, l1))
    e0, e1 = jnp.exp(l0 - m), jnp.exp(l1 - m)
    p0, p1 = e0 / (e0 + e1), e1 / (e0 + e1)
    return p0 - p0, (p0 + p1) - p0


def _lb_fwd(logits):
    return _rowwise("lb", _lb_fn, [logits[0:1], logits[1:2]], [], [(MAIN_WIDTH, F32)] * 2)


def _lb_bwd(logits, dlb0, dlb1):
    def fn(l0, l1, d0, d1):
        _, vjp = jax.vjp(_lb_fn, l0, l1)
        return vjp((d0, d1))
    return _rowwise("lb_bwd", fn, [logits[0:1], logits[1:2], dlb0, dlb1], [], [(MAIN_WIDTH, F32)] * 2)


def _adamw_fn(w, g, m, v):
    m = ADAM_B1 * m + (1.0 - ADAM_B1) * g
    v = ADAM_B2 * v + (1.0 - ADAM_B2) * jnp.square(g)
    m_hat = m / (1.0 - ADAM_B1 ** ADAM_STEP)
    v_hat = v / (1.0 - ADAM_B2 ** ADAM_STEP)
    return -ADAM_LR * (m_hat / (jnp.sqrt(v_hat) + ADAM_EPS) + ADAM_WD * w), m, v


def _adamw(name, w, g, m, v):
    shape = w.shape
    C = shape[-1]
    two = lambda a: a.reshape(-1, C)
    R = two(w).shape[0]
    outs = _rowwise(name, _adamw_fn, [two(w), two(g), two(m), two(v)], [], [(C, F32)] * 3, tm=_row_tile(R, 512))
    return [o.reshape(shape) for o in outs]


def _mixer_fwd(l, x1, mem, W, lbs, shared):
    T = x1.shape[0]
    tag = f"l{l}"
    h = _rms_fwd(tag + "_mixrms", x1, W["mix_norm"][l:l + 1])
    mem_n = _rms_fwd(tag + "_memrms", mem, W["mem_norm"][l:l + 1])
    kv = _mm(tag + "_memkv", [(mem_n, W["w_mem_kv"][l], NN)], [F32], mem.shape[0], 2 * MEM_WIDTH)
    if l < 2:
        proj = _proj_cols(tag + "_in", h, W["w_in_a"], l)
        main, o = _hgrn_fwd(tag + "_hgrn", proj, lbs[l], W["hgrn_o_gain"][l:l + 1])
    else:
        proj = _proj_cols(tag + "_in", h, W["w_in_b"], l - 2)
        main, o = _fox_fwd(tag + "_fox", proj, shared["kvf"], shared["cc"], shared["cr"], W["fox_q_gain"][l - 2:l - 1],
                           W["fox_k_gain"])
    mem_o = _mem_fwd(tag + "_mem", proj, kv, W["mem_q_gain"][l:l + 1], W["mem_k_gain"][l:l + 1])
    w_out = W["w_out"][l]
    x2 = _mm(tag + "_out", [(main, w_out[:MAIN_WIDTH], NN), (mem_o, w_out[MAIN_WIDTH:], NN)], [F32], T, D_MODEL,
             epi=lambda a, e: (e[0] + a[0] + a[1],), extras=[x1])
    return x2, dict(h=h, mem_n=mem_n, kv=kv, proj=proj, main=main, o=o, mem_o=mem_o)


def _mixer_bwd(l, x1, mem, W, lbs, shared, sv, dx2, acc):
    T = x1.shape[0]
    tag = f"l{l}b"
    w_out = W["w_out"][l]
    g = {}
    dmix = _mm(tag + "_dmix", [(dx2, w_out, NT)], [F32], T, D_MODEL)
    g["w_out"] = jnp.concatenate([
        _mm(tag + "_dwout_a", [(sv["main"], dx2, TN)], [BF16], MAIN_WIDTH, D_MODEL),
        _mm(tag + "_dwout_b", [(sv["mem_o"], dx2, TN)], [BF16], MEM_WIDTH, D_MODEL)], axis=0).reshape(N_DEV, -1, D_MODEL)
    dqm, dkv, g["mem_q_gain"], g["mem_k_gain"] = _mem_bwd(tag + "_mem", sv["proj"], sv["kv"], W["mem_q_gain"][l:l + 1],
                                                           W["mem_k_gain"][l:l + 1], dmix)
    if l < 2:
        dq, df, di, dg, g["lb"], g["hgrn_o_gain"] = _hgrn_bwd(tag + "_hgrn", sv["proj"], sv["o"], dmix, lbs[l],
                                                               W["hgrn_o_gain"][l:l + 1])
        dproj = jnp.concatenate([dq, df, di, dg, dqm], axis=1)
    else:
        dq, dgate, acc["dk"], acc["dv"], acc["dc"], g["fox_q_gain"], g["fox_k_gain"] = _fox_bwd(
            tag + "_fox", sv["proj"], shared["kvf"], shared["cc"], shared["cr"], W["fox_q_gain"][l - 2:l - 1], W["fox_k_gain"],
            sv["o"], dmix, acc["dk"], acc["dv"], acc["dc"])
        dproj = jnp.concatenate([dq, dgate, dqm], axis=1)
    dh, g["w_in"] = _proj_cols_bwd(tag + "_in", sv["h"], dproj, W["w_in_a"] if l < 2 else W["w_in_b"], l % 2)
    dx1, g["mix_norm"] = _rms_bwd(tag + "_mixrms", x1, W["mix_norm"][l:l + 1], dh, dres=dx2)
    dw_mem_kv = _mm(tag + "_dwmemkv", [(sv["mem_n"], dkv, TN)], [BF16], D_MODEL, 2 * MEM_WIDTH)
    g["w_mem_kv"] = _pad_cols(dw_mem_kv, D_MODEL).reshape(N_DEV, -1, D_MODEL)
    dmem_n = _mm(tag + "_dmemn", [(dkv, W["w_mem_kv"][l], NT)], [F32], mem.shape[0], D_MODEL)
    _, g["mem_norm"] = _rms_bwd(tag + "_memrms", mem, W["mem_norm"][l:l + 1], dmem_n)
    return dx1, g


def _forget_cols(kvf):
    return kvf[:, 2 * MAIN_WIDTH:2 * MAIN_WIDTH + LANES]


def _log_forget(kvf, bias):
    return _rowwise("kv_logf", lambda f, b: jax.nn.log_sigmoid(f + b), [_forget_cols(kvf)], [bias], [(LANES, F32)])[0]


def _step(x, mem, target, W, lb_logits):
    T = x.shape[0]
    W = dict(W, fox_k_gain=W["fox_k_gain"].reshape(1, -1))
    lbs = _lb_fwd(lb_logits)
    fox_bias = jnp.pad(W["fox_f_bias"], (0, LANES - FOX_HEADS)).reshape(1, LANES)
    w_kv = W["w_kv"]
    n_l = W["ffn1_norm"].shape[0]
    ffn = lambda which, l: (W[which + "_norm"][l:l + 1], W["cols352"], W["rows1024"],
                            (0 if which == "ffn1" else 2 * n_l) + l, (n_l if which == "ffn1" else 3 * n_l) + l,
                            (0 if which == "ffn1" else n_l) + l)

    saved, shared = [], {}
    for l in range(4):
        x1 = _ffn_fwd(f"l{l}_ffn1", x, *ffn("ffn1", l))
        x2, sv = _mixer_fwd(l, x1, mem, W, lbs, shared)
        x3 = _ffn_fwd(f"l{l}_ffn2", x2, *ffn("ffn2", l))
        sv.update(x=x, x1=x1, x2=x2)
        saved.append(sv)
        x = x3
        if l == 1:
            hk = _rms_fwd("kv_rms", x, W["kv_norm"].reshape(1, -1))
            kvf = _mm("kv_proj", [(hk, w_kv, NN)], [F32], T, w_kv.shape[1])
            cum = _cumsum_rows("kv_cum", _log_forget(kvf, fox_bias))[:, :FOX_HEADS].T
            shared = dict(kvf=kvf, cc=cum[:, :, None], cr=cum[:, None, :], hk=hk, x=x)

    def loss_fn(y, t):
        err = y - t
        return err * (1.0 / D_MODEL), jnp.sum(0.5 / D_MODEL * err * err, axis=0, keepdims=True)
    dx, loss = _rowwise("loss", loss_fn, [x, target], [], [(D_MODEL, F32)], [((1, D_MODEL), F32)])

    grads = [None] * 4
    acc = dict(dk=jnp.zeros((T, MAIN_WIDTH), F32), dv=jnp.zeros((T, MAIN_WIDTH), F32), dc=jnp.zeros((FOX_HEADS, 1, T), F32))
    gkv = {}
    for l in reversed(range(4)):
        sv = saved[l]
        if l == 1:
            dcum = jnp.pad(acc["dc"][:, 0, :].T, ((0, 0), (0, LANES - FOX_HEADS)))
            dlf = _cumsum_rows("kv_dcum", dcum, reverse=True)
            def dlogf_fn(d, f, b):
                p = d * _sigmoid(-(f + b))
                return p, jnp.sum(p, axis=0, keepdims=True)
            dfl, gkv["fox_f_bias"] = _rowwise("kv_dlogf", dlogf_fn, [dlf, _forget_cols(shared["kvf"])], [fox_bias],
                                              [(LANES, BF16)], [((1, LANES), F32)])
            dkvf = _pad_cols(jnp.concatenate([acc["dk"].astype(BF16), acc["dv"].astype(BF16), dfl], axis=1), w_kv.shape[1])
            dw_kv = _mm("kv_dw", [(shared["hk"], dkvf, TN)], [BF16], D_MODEL, dkvf.shape[1])
            gkv["w_kv"] = jnp.concatenate([dw_kv[:, :KV_SPLIT].reshape(N_DEV, -1, KV_SPLIT),
                                           dw_kv[:, KV_SPLIT:].reshape(N_DEV, -1, KV_SPLIT)], axis=1)
            dhk = _mm("kv_dh", [(dkvf, w_kv, NT)], [F32], T, D_MODEL)
            dx, gkv["kv_norm"] = _rms_bwd("kv_rmsb", shared["x"], W["kv_norm"].reshape(1, -1), dhk, dres=dx)
        g = {}
        dx2, g["ffn2_norm"], g["ffn2_w_gate"], g["ffn2_w_up"], g["ffn2_w_down"] = _ffn_bwd(f"l{l}b_ffn2", sv["x2"], *ffn("ffn2", l), dx)
        dx1, gm = _mixer_bwd(l, sv["x1"], mem, W, lbs, shared, sv, dx2, acc)
        g.update(gm)
        dx, g["ffn1_norm"], g["ffn1_w_gate"], g["ffn1_w_up"], g["ffn1_w_down"] = _ffn_bwd(f"l{l}b_ffn1", sv["x"], *ffn("ffn1", l), dx1)
        grads[l] = g

    rows = lambda names, ls=range(4): [grads[l][n] for n in names for l in ls]
    out = dict(cols352=jnp.concatenate(rows(COLS352), axis=1),
               cols416=jnp.concatenate(rows(["w_in"], (0, 1)), axis=1), cols224=jnp.concatenate(rows(["w_in"], (2, 3)), axis=1),
               rows1024=jnp.concatenate(rows(["ffn1_w_down", "ffn2_w_down", "w_out", "w_mem_kv"]) + [gkv["w_kv"]], axis=1))
    for n in ["ffn1_norm", "mix_norm", "mem_norm", "mem_q_gain", "mem_k_gain", "ffn2_norm"]:
        out[n] = jnp.concatenate([grads[l][n] for l in range(4)], axis=0)
    out["hgrn_o_gain"] = jnp.concatenate([grads[l]["hgrn_o_gain"] for l in (0, 1)], axis=0)
    out["fox_q_gain"] = jnp.concatenate([grads[l]["fox_q_gain"] for l in (2, 3)], axis=0)
    out["fox_k_gain"] = (grads[2]["fox_k_gain"] + grads[3]["fox_k_gain"]).reshape(-1)
    out["kv_norm"] = gkv["kv_norm"].reshape(-1)
    out["fox_f_bias"] = gkv["fox_f_bias"][0, :FOX_HEADS]
    dl0, dl1 = _lb_bwd(lb_logits, grads[0]["lb"], grads[1]["lb"])
    out["hgrn_lb_logits"] = jnp.concatenate([dl0, dl1], axis=0)
    return loss, dx, out


WEIGHTS = ["ffn1_norm", "ffn1_w_gate", "ffn1_w_up", "ffn1_w_down", "mix_norm", "mem_norm", "w_mem_kv", "mem_q_gain",
           "mem_k_gain", "w_in_a", "hgrn_lb_logits", "hgrn_o_gain", "w_in_b", "fox_q_gain", "kv_norm", "w_kv", "fox_f_bias",
           "fox_k_gain", "w_out", "ffn2_norm", "ffn2_w_gate", "ffn2_w_up", "ffn2_w_down"]
BIG = COLS352 + ["w_in_a", "w_in_b", "ffn1_w_down", "ffn2_w_down", "w_out", "w_mem_kv", "w_kv"]


def _train_step(a):
    bf = lambda n: a[n].astype(BF16)
    local = [jnp.concatenate([_rows2d(bf(n)) for n in COLS352], axis=0), _rows2d(bf("w_in_a")), _rows2d(bf("w_in_b")),
             _pack_rows1024(bf("ffn1_w_down"), bf("ffn2_w_down"), bf("w_out"), bf("w_mem_kv"), bf("w_kv"))]
    cols352, w_in_a, w_in_b, rows1024 = _all_gather("ag_weights", local)
    W = dict(cols352=cols352, w_in_a=w_in_a, w_in_b=w_in_b, rows1024=rows1024)
    n_l, r_out = a["w_out"].shape[:2]
    off = a["ffn1_w_down"].shape[0] * a["ffn1_w_down"].shape[1] * 2


    def whole(o):
        blocks = rows1024[:, o:o + n_l * r_out].reshape(N_DEV, n_l, r_out, D_MODEL)
        return blocks.transpose(1, 0, 2, 3).reshape(n_l, N_DEV * r_out, D_MODEL)

    W["w_out"] = whole(off)
    W["w_mem_kv"] = whole(off + n_l * r_out)[:, :, :a["w_mem_kv"].shape[2]]
    kv0 = off + 2 * n_l * r_out
    r_kv = a["w_kv"].shape[0]
    W["w_kv"] = jnp.concatenate([rows1024[:, kv0:kv0 + r_kv].reshape(-1, D_MODEL),
                                 rows1024[:, kv0 + r_kv:kv0 + 2 * r_kv].reshape(-1, D_MODEL)], axis=1)
    W.update({n: a[n] for n in SMALL})
    lb_shape = a["hgrn_lb_logits"].shape
    lb_all = _all_gather("ag_lb", [_small_pack([a["hgrn_lb_logits"]])])[0].reshape(N_DEV, -1)[:, :lb_shape[0] * lb_shape[1]]
    lb_logits = lb_all.reshape((N_DEV,) + lb_shape).transpose(1, 0, 2).reshape(lb_shape[0], -1)

    loss_part, dx, g = _step(a["x"][0], a["mem"][0], a["loss_target"][0], W, lb_logits)

    s352, s416, s224, s1024 = _reduce_scatter([g["cols352"], g["cols416"], g["cols224"], g["rows1024"]])
    grad = {n: s352[i * n_l * D_MODEL:(i + 1) * n_l * D_MODEL].reshape(a[n].shape) for i, n in enumerate(COLS352)}
    grad["w_in_a"], grad["w_in_b"] = s416.reshape(a["w_in_a"].shape), s224.reshape(a["w_in_b"].shape)
    names1024 = ["ffn1_w_down", "ffn2_w_down", "w_out", "w_mem_kv", "w_kv"]
    grad.update(dict(zip(names1024, _unpack_rows1024(s1024, {n: a[n].shape for n in names1024}))))

    zeros = [jnp.zeros(lb_logits.shape, F32), jnp.zeros(loss_part.shape, F32)]
    small_shapes = [a[n].shape for n in SMALL] + [lb_logits.shape, loss_part.shape]
    small_part = _small_pack([g[n] for n in SMALL] + [g["hgrn_lb_logits"], loss_part])
    small_sum = _sum_slabs("small_sum", _all_gather("ag_small", [small_part])[0], F32)
    small = _small_unpack(small_sum, small_shapes)
    grad.update(dict(zip(SMALL, small)))
    loss = jnp.sum(small[-1])
    me = 4 * lax.axis_index("x") + 2 * lax.axis_index("y") + lax.axis_index("c")
    grad["hgrn_lb_logits"] = lax.dynamic_slice_in_dim(small[-2], me * lb_shape[1], lb_shape[1], axis=1)

    delta, new_m, new_v = {}, {}, {}
    for n in BIG + ["hgrn_lb_logits"]:
        delta[n], new_m[n], new_v[n] = _adamw("adam_" + n, a[n], grad[n], a["m_" + n], a["v_" + n])
    packs = [_small_pack([a[p + n] for n in SMALL] + zeros) for p in ("", "m_", "v_")]
    upd = _rowwise("adam_small", _adamw_fn, [packs[0], small_sum, packs[1], packs[2]], [], [(LANES, F32)] * 3, tm=packs[0].shape[0])
    for d, u in zip((delta, new_m, new_v), upd):
        d.update(dict(zip(SMALL, _small_unpack(u, small_shapes))))
    return (loss, dx[None], *[grad[n] for n in WEIGHTS], *[delta[n] for n in WEIGHTS], *[new_m[n] for n in WEIGHTS],
            *[new_v[n] for n in WEIGHTS])


def kernel(x, mem, ffn1_norm, ffn1_w_gate, ffn1_w_up, ffn1_w_down, mix_norm, mem_norm, w_mem_kv, mem_q_gain, mem_k_gain, w_in_a, hgrn_lb_logits, hgrn_o_gain, w_in_b, fox_q_gain, kv_norm, w_kv, fox_f_bias, fox_k_gain, w_out, ffn2_norm, ffn2_w_gate, ffn2_w_up, ffn2_w_down, loss_target, m_ffn1_norm, m_ffn1_w_gate, m_ffn1_w_up, m_ffn1_w_down, m_mix_norm, m_mem_norm, m_w_mem_kv, m_mem_q_gain, m_mem_k_gain, m_w_in_a, m_hgrn_lb_logits, m_hgrn_o_gain, m_w_in_b, m_fox_q_gain, m_kv_norm, m_w_kv, m_fox_f_bias, m_fox_k_gain, m_w_out, m_ffn2_norm, m_ffn2_w_gate, m_ffn2_w_up, m_ffn2_w_down, v_ffn1_norm, v_ffn1_w_gate, v_ffn1_w_up, v_ffn1_w_down, v_mix_norm, v_mem_norm, v_w_mem_kv, v_mem_q_gain, v_mem_k_gain, v_w_in_a, v_hgrn_lb_logits, v_hgrn_o_gain, v_w_in_b, v_fox_q_gain, v_kv_norm, v_w_kv, v_fox_f_bias, v_fox_k_gain, v_w_out, v_ffn2_norm, v_ffn2_w_gate, v_ffn2_w_up, v_ffn2_w_down):
    return _train_step(dict(locals()))
```

```python
import functools

import jax
import jax.numpy as jnp
from jax import lax
from jax.experimental import pallas as pl
from jax.experimental.pallas import tpu as pltpu

F32, BF16 = jnp.float32, jnp.bfloat16
EPS = 1e-6
V7X_VMEM_LIMIT = 56 * 1024 * 1024
LANES = 128
N_DEV = 8

D_MODEL = 1024
MAIN_WIDTH = 768
MEM_WIDTH = 256
HG_HEAD_DIM = 128
HG_HEADS = 6
FOX_HEAD_DIM = 64
FOX_HEADS = 12
MEM_HEADS = 4
MEM_HEAD_DIM = 64
HG_BLOCK = 16

ADAM_LR, ADAM_B1, ADAM_B2, ADAM_EPS, ADAM_WD, ADAM_STEP = 0.001, 0.9, 0.999, 1e-08, 0.01, 10

NN = ((1,), (0,))
NT = ((1,), (1,))
TN = ((0,), (0,))


def _dot(a, b, dims, precision=None):
    return lax.dot_general(a, b, (dims, ((), ())), preferred_element_type=F32, precision=precision)


def _bdot(a, b, dims):
    return _dot(a.astype(BF16), b.astype(BF16), dims)


def _split(a):
    hi = a.astype(BF16)
    return hi, (a - hi.astype(F32)).astype(BF16)


def _fdot(a, b, dims):
    ah, al = _split(a)
    bh, bl = _split(b)
    return _dot(ah, bh, dims) + (_dot(ah, bl, dims) + _dot(al, bh, dims))


def _params(n_grid):
    return pltpu.CompilerParams(dimension_semantics=("arbitrary",) * n_grid, vmem_limit_bytes=V7X_VMEM_LIMIT)


def _rms(x, g):
    return x * lax.rsqrt(jnp.mean(x * x, axis=-1, keepdims=True) + EPS) * g


def _sigmoid(x):
    return jax.nn.sigmoid(x)


def _silu(x):
    return x * jax.nn.sigmoid(x)


def _rowwise(name, fn, rows, consts, out_rows, out_reds=(), tm=256):
    R = rows[0].shape[0]
    tm = min(tm, R)
    assert R % tm == 0
    n_in, n_o = len(rows) + len(consts), len(out_rows)

    def body(*refs):
        outs = fn(*[r[...] for r in refs[:n_in]])
        if not isinstance(outs, (tuple, list)):
            outs = (outs,)
        for r, o in zip(refs[n_in:n_in + n_o], outs[:n_o]):
            r[...] = o.astype(r.dtype)
        red_refs = refs[n_in + n_o:]
        if red_refs:
            @pl.when(pl.program_id(0) == 0)
            def _():
                for r in red_refs:
                    r[...] = jnp.zeros(r.shape, r.dtype)
            for r, o in zip(red_refs, outs[n_o:]):
                r[...] += o

    zero = lambda n: (lambda i: (0,) * n)
    in_specs = [pl.BlockSpec((tm, a.shape[1]), lambda i: (i, 0)) for a in rows]
    in_specs += [pl.BlockSpec(c.shape, zero(c.ndim)) for c in consts]
    out_specs = [pl.BlockSpec((tm, c), lambda i: (i, 0)) for c, _ in out_rows]
    out_specs += [pl.BlockSpec(s, zero(len(s))) for s, _ in out_reds]
    out_shape = [jax.ShapeDtypeStruct((R, c), dt) for c, dt in out_rows]
    out_shape += [jax.ShapeDtypeStruct(s, dt) for s, dt in out_reds]
    return pl.pallas_call(body, grid=(R // tm,), in_specs=in_specs, out_specs=out_specs, out_shape=out_shape,
                          name=name, compiler_params=_params(1))(*rows, *consts)


def _tile(n, cap):
    best = None
    for t in range(LANES, min(n, cap) + 1, LANES):
        if n % t == 0:
            best = t
    return best or n


def _mm(name, pairs, out_dtypes, M, N, epi=None, extras=(), tm=512, tn=512):
    tm, tn = _tile(M, tm), _tile(N, tn)
    n_p, n_e = len(pairs), len(extras)
    modes = [m for _, _, m in pairs]

    def body(*refs):
        accs = [_bdot(refs[2 * k][...], refs[2 * k + 1][...], modes[k]) for k in range(n_p)]
        ex = [r[...] for r in refs[2 * n_p:2 * n_p + n_e]]
        outs = epi(accs, ex) if epi is not None else accs
        for r, o in zip(refs[2 * n_p + n_e:], outs):
            r[...] = o.astype(r.dtype)

    in_specs = []
    ops = []
    for a, b, mode in pairs:
        if mode == NN:
            K = a.shape[1]
            assert a.shape == (M, K) and b.shape == (K, N), (name, a.shape, b.shape)
            in_specs += [pl.BlockSpec((tm, K), lambda i, j: (i, 0)), pl.BlockSpec((K, tn), lambda i, j: (0, j))]
        elif mode == NT:
            K = a.shape[1]
            assert a.shape == (M, K) and b.shape == (N, K), (name, a.shape, b.shape)
            in_specs += [pl.BlockSpec((tm, K), lambda i, j: (i, 0)), pl.BlockSpec((tn, K), lambda i, j: (j, 0))]
        else:
            K = a.shape[0]
            assert a.shape == (K, M) and b.shape == (K, N), (name, a.shape, b.shape)
            in_specs += [pl.BlockSpec((K, tm), lambda i, j: (0, i)), pl.BlockSpec((K, tn), lambda i, j: (0, j))]
        ops += [a, b]
    in_specs += [pl.BlockSpec((tm, tn), lambda i, j: (i, j)) for _ in extras]
    out_specs = [pl.BlockSpec((tm, tn), lambda i, j: (i, j)) for _ in out_dtypes]
    out_shape = [jax.ShapeDtypeStruct((M, N), dt) for dt in out_dtypes]
    res = pl.pallas_call(body, grid=(M // tm, N // tn), in_specs=in_specs, out_specs=out_specs, out_shape=out_shape,
                         name=name, compiler_params=_params(2))(*ops, *extras)
    return res[0] if len(res) == 1 else res


def _rms_fwd(name, x, gain, dtype=BF16):
    return _rowwise(name, _rms, [x], [gain], [(x.shape[1], dtype)])[0]


def _rms_bwd(name, x, gain, dh, dres=None):
    def fn(x, dh, *rest):
        g = rest[-1]
        _, vjp = jax.vjp(_rms, x, g)
        dx, dg = vjp(dh)
        if dres is not None:
            dx = dx + rest[0]
        return dx, dg
    rows = [x, dh] + ([dres] if dres is not None else [])
    d = x.shape[1]
    return _rowwise(name, fn, rows, [gain], [(d, F32)], [((1, d), F32)])


def _ffn_specs(gcols, grows, ig, iu, idn):
    n = gcols.shape[2]
    D = grows.shape[2]
    wg = pl.BlockSpec((None, D, n), lambda i, j: (j, ig, 0))
    wu = pl.BlockSpec((None, D, n), lambda i, j: (j, iu, 0))
    wd = pl.BlockSpec((None, n, D), lambda i, j: (j, idn, 0))
    return n, wg, wu, wd


def _ffn_fwd(name, x, gain, gcols, grows, ig, iu, idn, tm=1024):
    T, D = x.shape
    tm = min(T, tm)
    n, wg_s, wu_s, wd_s = _ffn_specs(gcols, grows, ig, iu, idn)
    last = N_DEV - 1

    def body(x_ref, g_ref, wg_ref, wu_ref, wd_ref, y_ref, h_s, acc):
        j = pl.program_id(1)

        @pl.when(j == 0)
        def _():
            h_s[...] = _rms(x_ref[...], g_ref[...]).astype(BF16)
            acc[...] = jnp.zeros(acc.shape, F32)
        h = h_s[...]
        z = _silu(_dot(h, wg_ref[...], NN)) * _dot(h, wu_ref[...], NN)
        acc[...] += _dot(z.astype(BF16), wd_ref[...], NN)

        @pl.when(j == last)
        def _():
            y_ref[...] = x_ref[...] + 0.5 * acc[...]

    row = pl.BlockSpec((tm, D), lambda i, j: (i, 0))
    return pl.pallas_call(
        body, grid=(T // tm, N_DEV), in_specs=[row, pl.BlockSpec((1, D), lambda i, j: (0, 0)), wg_s, wu_s, wd_s],
        out_specs=row, out_shape=jax.ShapeDtypeStruct((T, D), F32),
        scratch_shapes=[pltpu.VMEM((tm, D), BF16), pltpu.VMEM((tm, D), F32)],
        name=name, compiler_params=_params(2))(x, gain, gcols, gcols, grows)


def _ffn_bwd(tag, x, gain, gcols, grows, ig, iu, idn, dy, tm=512):
    T, D = x.shape
    tm = min(T, tm)
    n, wg_s, wu_s, wd_s = _ffn_specs(gcols, grows, ig, iu, idn)
    last = N_DEV - 1

    def body(x_ref, dy_ref, g_ref, wg_ref, wu_ref, wd_ref, dx_ref, dg_ref, h_ref, z_ref, da_ref, db_ref, dh_acc):
        i, j = pl.program_id(0), pl.program_id(1)

        @pl.when(j == 0)
        def _():
            h_ref[...] = _rms(x_ref[...], g_ref[...]).astype(BF16)
            dh_acc[...] = jnp.zeros(dh_acc.shape, F32)

        @pl.when((i == 0) & (j == 0))
        def _():
            dg_ref[...] = jnp.zeros(dg_ref.shape, F32)
        h = h_ref[...]
        a, b = _dot(h, wg_ref[...], NN), _dot(h, wu_ref[...], NN)
        dz = 0.5 * _dot(dy_ref[...].astype(BF16), wd_ref[...], NT)
        s = _sigmoid(a)
        si = a * s
        da = (dz * b * (s + si * (1.0 - s))).astype(BF16)
        db = (dz * si).astype(BF16)
        z_ref[...] = (si * b).astype(BF16)
        da_ref[...] = da
        db_ref[...] = db
        dh_acc[...] += _dot(da, wg_ref[...], NT) + _dot(db, wu_ref[...], NT)

        @pl.when(j == last)
        def _():
            _, vjp = jax.vjp(_rms, x_ref[...], g_ref[...])
            dx, dg = vjp(dh_acc[...])
            dx_ref[...] = dx + dy_ref[...]
            dg_ref[...] += dg

    row = pl.BlockSpec((tm, D), lambda i, j: (i, 0))
    vec = pl.BlockSpec((1, D), lambda i, j: (0, 0))
    hid = pl.BlockSpec((None, tm, n), lambda i, j: (j, i, 0))
    hidden = jax.ShapeDtypeStruct((N_DEV, T, n), BF16)
    dx, dgain, h, z, da, db = pl.pallas_call(
        body, grid=(T // tm, N_DEV), in_specs=[row, row, vec, wg_s, wu_s, wd_s],
        out_specs=[row, vec, row, hid, hid, hid],
        out_shape=[jax.ShapeDtypeStruct((T, D), F32), jax.ShapeDtypeStruct((1, D), F32), jax.ShapeDtypeStruct((T, D), BF16),
                   hidden, hidden, hidden],
        scratch_shapes=[pltpu.VMEM((tm, D), F32)],
        name=tag + "_a", compiler_params=_params(2))(x, dy, gain, gcols, gcols, grows)

    def wbody(h_ref, dy_ref, z_ref, da_ref, db_ref, dwg_ref, dwu_ref, dwd_ref):
        h = h_ref[...]
        dwg_ref[...] = _dot(h, da_ref[...], TN).astype(BF16)
        dwu_ref[...] = _dot(h, db_ref[...], TN).astype(BF16)
        dwd_ref[...] = (0.5 * _dot(z_ref[...], dy_ref[...].astype(BF16), TN)).astype(BF16)

    full = pl.BlockSpec((T, D), lambda j: (0, 0))
    hid_all = pl.BlockSpec((None, T, n), lambda j: (j, 0, 0))
    dwg, dwu, dwd = pl.pallas_call(
        wbody, grid=(N_DEV,), in_specs=[full, full, hid_all, hid_all, hid_all],
        out_specs=[pl.BlockSpec((None, D, n), lambda j: (j, 0, 0))] * 2 + [pl.BlockSpec((None, n, D), lambda j: (j, 0, 0))],
        out_shape=[jax.ShapeDtypeStruct((N_DEV, D, n), BF16)] * 2 + [jax.ShapeDtypeStruct((N_DEV, n, D), BF16)],
        name=tag + "_w", compiler_params=_params(1))(h, dy, z, da, db)
    return dx, dgain, dwg, dwu, dwd


def _wcols_spec(gw, l, grid_rank):
    _, _, n = gw.shape
    K = D_MODEL
    zero = (lambda i: (0, l, 0)) if grid_rank == 1 else (lambda i, j: (0, l, 0))
    return n, K, pl.BlockSpec((N_DEV, K, n), zero)


def _proj_cols(name, h, gw, l, tm=512):
    T = h.shape[0]
    tm = min(T, tm)
    n, K, wspec = _wcols_spec(gw, l, 1)

    def body(h_ref, w_ref, o_ref):
        h = h_ref[...]
        for j in range(N_DEV):
            o_ref[:, pl.ds(j * n, n)] = _dot(h, w_ref[j], NN)

    return pl.pallas_call(
        body, grid=(T // tm,), in_specs=[pl.BlockSpec((tm, K), lambda i: (i, 0)), wspec],
        out_specs=pl.BlockSpec((tm, N_DEV * n), lambda i: (i, 0)), out_shape=jax.ShapeDtypeStruct((T, N_DEV * n), F32),
        name=name, compiler_params=_params(1))(h, gw)


def _proj_cols_bwd(tag, h, dproj, gw, l, tm=512, tk=512):
    T = h.shape[0]
    tm = min(T, tm)
    n, K, wspec = _wcols_spec(gw, l, 1)

    def dh_body(dp_ref, w_ref, o_ref):
        acc = jnp.zeros(o_ref.shape, F32)
        for j in range(N_DEV):
            acc = acc + _dot(dp_ref[:, pl.ds(j * n, n)], w_ref[j], NT)
        o_ref[...] = acc

    dh = pl.pallas_call(
        dh_body, grid=(T // tm,), in_specs=[pl.BlockSpec((tm, N_DEV * n), lambda i: (i, 0)), wspec],
        out_specs=pl.BlockSpec((tm, K), lambda i: (i, 0)), out_shape=jax.ShapeDtypeStruct((T, K), F32),
        name=tag + "_dh", compiler_params=_params(1))(dproj, gw)

    def dw_body(h_ref, dp_ref, o_ref):
        h = h_ref[...]
        for j in range(N_DEV):
            o_ref[j] = _dot(h, dp_ref[:, pl.ds(j * n, n)], TN).astype(BF16)

    dw = pl.pallas_call(
        dw_body, grid=(K // tk,), in_specs=[pl.BlockSpec((T, tk), lambda i: (0, i)), pl.BlockSpec((T, N_DEV * n), lambda i: (0, 0))],
        out_specs=pl.BlockSpec((N_DEV, tk, n), lambda i: (0, i, 0)), out_shape=jax.ShapeDtypeStruct((N_DEV, K, n), BF16),
        name=tag + "_dw", compiler_params=_params(1))(h, dproj)
    return dh, dw


def _block_tri(n, reverse=False):
    r = lax.broadcasted_iota(jnp.int32, (n, n), 0)
    c = lax.broadcasted_iota(jnp.int32, (n, n), 1)
    same = (r // HG_BLOCK) == (c // HG_BLOCK)
    return (same & ((c >= r) if reverse else (c <= r))).astype(F32)


def _hgrn_prep(q_ref, f_ref, lbv, qs, ks, cs, T):
    pt = min(T, 256)
    tri = _block_tri(pt)
    for p in range(T // pt):
        rows = pl.ds(p * pt, pt)
        f = lbv + (1.0 - lbv) * _sigmoid(f_ref[rows, :])
        qs[rows, :] = _silu(q_ref[rows, :])
        ks[rows, :] = 1.0 - f
        cs[rows, :] = _dot(tri, jnp.log(f), NN, precision=lax.Precision.HIGHEST)


def _strided(t, nb):
    return pl.ds(t, nb, stride=HG_BLOCK)


def _gate_out(o, og, g):
    return _rms(o, og) * _silu(g)


HG_UNROLL = 4


def _block_rows(n):
    return pl.ds(pl.multiple_of(n * HG_BLOCK, HG_BLOCK), HG_BLOCK)


def _blocks_loop(nb, fn):
    u = HG_UNROLL if nb % HG_UNROLL == 0 else 1

    def step(i, carry):
        for k in range(u):
            fn(i * u + k)
        return carry

    lax.fori_loop(0, nb // u, step, 0)


def _scan_states(buf, cs, nb, reverse=False):
    def step(m, st):
        n = nb - 1 - m if reverse else m
        own = buf[n]
        buf[n] = st
        rows = _block_rows(n)
        return jnp.exp(cs[rows, :][HG_BLOCK - 1:HG_BLOCK, :]) * st + own

    lax.fori_loop(0, nb, step, jnp.zeros(buf.shape[1:], F32))


def _hgrn_states(i_ref, ks, cs, states, nb):
    def own_step(n):
        rows = _block_rows(n)
        c = cs[rows, :]
        states[n] = _fdot(i_ref[rows, :], ks[rows, :] * jnp.exp(c[HG_BLOCK - 1:HG_BLOCK, :] - c), TN)

    _blocks_loop(nb, own_step)
    _scan_states(states, cs, nb)


def _hgrn_fwd(name, proj, lb, og):
    T = proj.shape[0]
    nb = T // HG_BLOCK
    hd = HG_HEAD_DIM

    def body(q_ref, f_ref, i_ref, g_ref, lb_ref, og_ref, main_ref, o_ref, qs, ks, cs, states):
        _hgrn_prep(q_ref, f_ref, lb_ref[...], qs, ks, cs, T)
        for t in range(HG_BLOCK):
            qt, ct = qs[_strided(t, nb), :], cs[_strided(t, nb), :]
            acc = jnp.zeros((nb, hd), F32)
            for s in range(t + 1):
                w = qt * ks[_strided(s, nb), :] * jnp.exp(ct - cs[_strided(s, nb), :])
                acc = acc + jnp.sum(w, axis=-1, keepdims=True) * i_ref[_strided(s, nb), :]
            o_ref[_strided(t, nb), :] = acc

        _hgrn_states(i_ref, ks, cs, states, nb)

        def out_step(n):
            rows = _block_rows(n)
            o_ref[rows, :] += _fdot(qs[rows, :] * jnp.exp(cs[rows, :]), states[n], NT)

        _blocks_loop(nb, out_step)
        pt = min(T, 256)
        for p in range(T // pt):
            rows = pl.ds(p * pt, pt)
            main_ref[rows, :] = _gate_out(o_ref[rows, :], og_ref[...], g_ref[rows, :])

    nh = HG_HEADS
    col = lambda off: pl.BlockSpec((T, hd), lambda h, off=off: (0, off + h))
    return pl.pallas_call(
        body, grid=(nh,),
        in_specs=[col(0), col(nh), col(2 * nh), col(3 * nh), pl.BlockSpec((1, hd), lambda h: (0, h)),
                  pl.BlockSpec((1, hd), lambda h: (0, 0))],
        out_specs=[col(0), col(0)],
        out_shape=[jax.ShapeDtypeStruct((T, MAIN_WIDTH), F32)] * 2,
        scratch_shapes=[pltpu.VMEM((T, hd), F32)] * 3 + [pltpu.VMEM((nb, hd, hd), F32)],
        name=name, compiler_params=_params(1))(proj, proj, proj, proj, lb, og)


def _hgrn_bwd(name, proj, o, dmix, lb, og):
    T = proj.shape[0]
    nb = T // HG_BLOCK
    hd = HG_HEAD_DIM
    pt = min(T, 256)

    def body(q_ref, f_ref, i_ref, g_ref, o_ref, dm_ref, lb_ref, og_ref,
             dq_ref, df_ref, di_ref, dg_ref, dlb_ref, dog_ref, qs, ks, cs, dos, dqs, dks, dvs, states, behind):
        lbv = lb_ref[...]
        _hgrn_prep(q_ref, f_ref, lbv, qs, ks, cs, T)
        dog = jnp.zeros((1, hd), F32)
        for p in range(T // pt):
            rows = pl.ds(p * pt, pt)
            _, vjp = jax.vjp(_gate_out, o_ref[rows, :], og_ref[...], g_ref[rows, :])
            do, dog_p, dg = vjp(dm_ref[rows, :])
            dos[rows, :] = do
            dg_ref[rows, :] = dg.astype(dg_ref.dtype)
            dog = dog + dog_p

        @pl.when(pl.program_id(0) == 0)
        def _():
            dog_ref[...] = jnp.zeros(dog_ref.shape, F32)
        dog_ref[...] += dog

        for t in range(HG_BLOCK):
            dqs[_strided(t, nb), :] = jnp.zeros((nb, hd), F32)
        for s in range(HG_BLOCK):
            k_s, c_s, v_s = ks[_strided(s, nb), :], cs[_strided(s, nb), :], i_ref[_strided(s, nb), :]
            dk = jnp.zeros((nb, hd), F32)
            dv = jnp.zeros((nb, hd), F32)
            for t in range(s, HG_BLOCK):
                q_t, do_t = qs[_strided(t, nb), :], dos[_strided(t, nb), :]
                e = jnp.exp(cs[_strided(t, nb), :] - c_s)
                a = jnp.sum(q_t * k_s * e, axis=-1, keepdims=True)
                g = jnp.sum(do_t * v_s, axis=-1, keepdims=True)
                dqs[_strided(t, nb), :] += g * k_s * e
                dk = dk + g * q_t * e
                dv = dv + a * do_t
            dks[_strided(s, nb), :] = dk
            dvs[_strided(s, nb), :] = dv

        _hgrn_states(i_ref, ks, cs, states, nb)

        def own_step(n):
            rows = _block_rows(n)
            behind[n] = _fdot(dos[rows, :], qs[rows, :] * jnp.exp(cs[rows, :]), TN)

        _blocks_loop(nb, own_step)
        _scan_states(behind, cs, nb, reverse=True)

        def grad_step(n):
            rows = _block_rows(n)
            c = cs[rows, :]
            ec, ek = jnp.exp(c), jnp.exp(c[HG_BLOCK - 1:HG_BLOCK, :] - c)
            dst = behind[n]
            dqs[rows, :] += _fdot(dos[rows, :], states[n], NN) * ec
            dks[rows, :] += _fdot(i_ref[rows, :], dst, NN) * ek
            dvs[rows, :] += _fdot(ks[rows, :] * ek, dst, NT)

        _blocks_loop(nb, grad_step)

        full = (lax.broadcasted_iota(jnp.int32, (pt, pt), 1) >= lax.broadcasted_iota(jnp.int32, (pt, pt), 0)).astype(F32)
        carry = jnp.zeros((1, hd), F32)
        dlb = jnp.zeros((1, hd), F32)
        for p in reversed(range(T // pt)):
            rows = pl.ds(p * pt, pt)
            q, k, dq, dk = qs[rows, :], ks[rows, :], dqs[rows, :], dks[rows, :]
            db = q * dq - k * dk
            dlf = _dot(full, db, NN, precision=lax.Precision.HIGHEST) + carry
            carry = carry + jnp.sum(db, axis=0, keepdims=True)
            sg = _sigmoid(f_ref[rows, :])
            df = dlf / (1.0 - k) - dk
            df_ref[rows, :] = (df * (1.0 - lbv) * sg * (1.0 - sg)).astype(df_ref.dtype)
            dlb = dlb + jnp.sum(df * (1.0 - sg), axis=0, keepdims=True)
            qr = q_ref[rows, :]
            sq = _sigmoid(qr)
            dq_ref[rows, :] = (dq * (sq + qr * sq * (1.0 - sq))).astype(dq_ref.dtype)
            di_ref[rows, :] = dvs[rows, :].astype(di_ref.dtype)
        dlb_ref[...] = dlb

    nh = HG_HEADS
    col = lambda off: pl.BlockSpec((T, hd), lambda h, off=off: (0, off + h))
    vec = pl.BlockSpec((1, hd), lambda h: (0, h))
    one = pl.BlockSpec((1, hd), lambda h: (0, 0))
    return pl.pallas_call(
        body, grid=(nh,),
        in_specs=[col(0), col(nh), col(2 * nh), col(3 * nh), col(0), col(0), vec, one],
        out_specs=[col(0), col(0), col(0), col(0), vec, one],
        out_shape=[jax.ShapeDtypeStruct((T, MAIN_WIDTH), BF16)] * 4
        + [jax.ShapeDtypeStruct((1, MAIN_WIDTH), F32), jax.ShapeDtypeStruct((1, hd), F32)],
        scratch_shapes=[pltpu.VMEM((T, hd), F32)] * 7 + [pltpu.VMEM((nb, hd, hd), F32)] * 2,
        name=name, compiler_params=_params(1))(proj, proj, proj, proj, o, dmix, lb, og)


def _softmax_rows(s):
    p = jnp.exp(s - jnp.max(s, axis=-1, keepdims=True))
    return p, jnp.sum(p, axis=-1, keepdims=True)


def _fox_logits(q, k, cc, cr, q0):
    s = _dot(q, k, NT) * (FOX_HEAD_DIM ** -0.5) + cc - cr
    row = lax.broadcasted_iota(jnp.int32, s.shape, 0) + q0
    col = lax.broadcasted_iota(jnp.int32, s.shape, 1)
    return jnp.where(col <= row, s, -jnp.inf)


def _fox_specs(T):
    w = 2 * FOX_HEAD_DIM
    n = MAIN_WIDTH // w
    col = lambda off: pl.BlockSpec((T, w), lambda p, off=off: (0, off + p))
    cc = pl.BlockSpec((2, T, 1), lambda p: (p, 0, 0))
    cr = pl.BlockSpec((2, 1, T), lambda p: (p, 0, 0))
    gain = pl.BlockSpec((1, FOX_HEAD_DIM), lambda p: (0, 0))
    return n, col, cc, cr, gain


def _fox_fwd(name, proj, kvf, cc, cr, gq, gk):
    T = proj.shape[0]
    tq = min(T, 256)
    hd = FOX_HEAD_DIM

    def body(q_ref, g_ref, k_ref, v_ref, cc_ref, cr_ref, gq_ref, gk_ref, main_ref, o_ref):
        for hh in range(2):
            lanes = pl.ds(hh * hd, hd)
            k = _rms(k_ref[:, lanes], gk_ref[...]).astype(BF16)
            v = v_ref[:, lanes].astype(BF16)
            for qi in range(T // tq):
                rows, kl = pl.ds(qi * tq, tq), (qi + 1) * tq
                q = _rms(q_ref[rows, lanes], gq_ref[...]).astype(BF16)
                s = _fox_logits(q, k[:kl], cc_ref[hh, rows, :], cr_ref[hh, :, pl.ds(0, kl)], qi * tq)
                p, l = _softmax_rows(s)
                o = _dot(p.astype(BF16), v[:kl], NN) / l
                o_ref[rows, lanes] = o
                main_ref[rows, lanes] = o * _sigmoid(g_ref[rows, lanes])

    n, col, ccs, crs, gain = _fox_specs(T)
    return pl.pallas_call(
        body, grid=(n,), in_specs=[col(0), col(n), col(0), col(n), ccs, crs, gain, gain], out_specs=[col(0), col(0)],
        out_shape=[jax.ShapeDtypeStruct((T, MAIN_WIDTH), F32)] * 2,
        name=name, compiler_params=_params(1))(proj, proj, kvf, kvf, cc, cr, gq, gk)


def _fox_bwd(name, proj, kvf, cc, cr, gq, gk, o, dmix, pdk, pdv, pdc):
    T = proj.shape[0]
    tq = min(T, 256)
    hd = FOX_HEAD_DIM
    scale = hd ** -0.5

    def body(q_ref, g_ref, k_ref, v_ref, cc_ref, cr_ref, gq_ref, gk_ref, o_ref, dm_ref, pdk_ref, pdv_ref, pdc_ref,
             dq_ref, dg_ref, dk_ref, dv_ref, dc_ref, dgq_ref, dgk_ref, dka, dva, dca):
        dgq = jnp.zeros((1, hd), F32)
        dgk = jnp.zeros((1, hd), F32)
        for hh in range(2):
            lanes = pl.ds(hh * hd, hd)
            k32, vjp_k = jax.vjp(_rms, k_ref[:, lanes], gk_ref[...])
            k = k32.astype(BF16)
            v = v_ref[:, lanes].astype(BF16)
            dka[...] = jnp.zeros(dka.shape, F32)
            dva[...] = jnp.zeros(dva.shape, F32)
            dca[...] = jnp.zeros(dca.shape, F32)
            for qi in range(T // tq):
                rows, kl = pl.ds(qi * tq, tq), (qi + 1) * tq
                q32, vjp_q = jax.vjp(_rms, q_ref[rows, lanes], gq_ref[...])
                q = q32.astype(BF16)
                s = _fox_logits(q, k[:kl], cc_ref[hh, rows, :], cr_ref[hh, :, pl.ds(0, kl)], qi * tq)
                p, l = _softmax_rows(s)
                p = p / l
                sg = _sigmoid(g_ref[rows, lanes])
                dm = dm_ref[rows, lanes]
                do = (dm * sg).astype(BF16)
                dg_ref[rows, lanes] = (dm * o_ref[rows, lanes] * sg * (1.0 - sg)).astype(dg_ref.dtype)
                dp = _dot(do, v[:kl], NT)
                ds = p * (dp - jnp.sum(p * dp, axis=-1, keepdims=True))
                dsb = ds.astype(BF16)
                dqr, dgq_p = vjp_q(_dot(dsb, k[:kl], NN) * scale)
                dq_ref[rows, lanes] = dqr.astype(dq_ref.dtype)
                dgq = dgq + dgq_p
                dka[pl.ds(0, kl), :] += _dot(dsb, q, TN) * scale
                dva[pl.ds(0, kl), :] += _dot(p.astype(BF16), do, TN)
                dca[:, pl.ds(0, kl)] -= jnp.sum(ds, axis=0, keepdims=True)
            dkr, dgk_p = vjp_k(dka[...])
            dgk = dgk + dgk_p
            dk_ref[:, lanes] = dkr + pdk_ref[:, lanes]
            dv_ref[:, lanes] = dva[...] + pdv_ref[:, lanes]
            dc_ref[hh] = dca[...] + pdc_ref[hh]

        @pl.when(pl.program_id(0) == 0)
        def _():
            dgq_ref[...] = jnp.zeros(dgq_ref.shape, F32)
            dgk_ref[...] = jnp.zeros(dgk_ref.shape, F32)
        dgq_ref[...] += dgq
        dgk_ref[...] += dgk

    n, col, ccs, crs, gain = _fox_specs(T)
    wide = jax.ShapeDtypeStruct((T, MAIN_WIDTH), F32)
    half = jax.ShapeDtypeStruct((T, MAIN_WIDTH), BF16)
    return pl.pallas_call(
        body, grid=(n,),
        in_specs=[col(0), col(n), col(0), col(n), ccs, crs, gain, gain, col(0), col(0), col(0), col(0), crs],
        out_specs=[col(0), col(0), col(0), col(0), crs, gain, gain],
        out_shape=[half, half, wide, wide, jax.ShapeDtypeStruct((FOX_HEADS, 1, T), F32),
                   jax.ShapeDtypeStruct((1, hd), F32), jax.ShapeDtypeStruct((1, hd), F32)],
        scratch_shapes=[pltpu.VMEM((T, hd), F32), pltpu.VMEM((T, hd), F32), pltpu.VMEM((1, T), F32)],
        name=name, compiler_params=_params(1))(proj, proj, kvf, kvf, cc, cr, gq, gk, o, dmix, pdk, pdv, pdc)


def _mem_specs(T, width):
    tq = min(T, 512)
    q = pl.BlockSpec((tq, MEM_WIDTH), lambda i, c=(width - MEM_WIDTH) // MEM_WIDTH: (i, c))
    gain = pl.BlockSpec((1, MEM_HEAD_DIM), lambda i: (0, 0))
    return tq, q, gain


def _mem_fwd(name, proj, kv, gq, gk):
    T, W = proj.shape
    hd = MEM_HEAD_DIM
    tq, qspec, gain = _mem_specs(T, W)

    def body(q_ref, kv_ref, gq_ref, gk_ref, o_ref):
        for h in range(MEM_HEADS):
            lanes = pl.ds(h * hd, hd)
            q = _rms(q_ref[:, lanes], gq_ref[...]).astype(BF16)
            k = _rms(kv_ref[:, lanes], gk_ref[...]).astype(BF16)
            v = kv_ref[:, pl.ds(MEM_WIDTH + h * hd, hd)].astype(BF16)
            p, l = _softmax_rows(_dot(q, k, NT) * (hd ** -0.5))
            o_ref[:, lanes] = _dot(p.astype(BF16), v, NN) / l

    return pl.pallas_call(
        body, grid=(T // tq,),
        in_specs=[qspec, pl.BlockSpec(kv.shape, lambda i: (0, 0)), gain, gain],
        out_specs=pl.BlockSpec((tq, MEM_WIDTH), lambda i: (i, 0)),
        out_shape=jax.ShapeDtypeStruct((T, MEM_WIDTH), F32),
        name=name, compiler_params=_params(1))(proj, kv, gq, gk)


def _mem_bwd(name, proj, kv, gq, gk, dmix):
    T, W = proj.shape
    hd = MEM_HEAD_DIM
    scale = hd ** -0.5
    tq, qspec, gain = _mem_specs(T, W)

    def body(q_ref, kv_ref, gq_ref, gk_ref, dm_ref, dq_ref, dkv_ref, dgq_ref, dgk_ref):
        @pl.when(pl.program_id(0) == 0)
        def _():
            dkv_ref[...] = jnp.zeros(dkv_ref.shape, F32)
            dgq_ref[...] = jnp.zeros(dgq_ref.shape, F32)
            dgk_ref[...] = jnp.zeros(dgk_ref.shape, F32)
        for h in range(MEM_HEADS):
            lanes = pl.ds(h * hd, hd)
            vl = pl.ds(MEM_WIDTH + h * hd, hd)
            q32, vjp_q = jax.vjp(_rms, q_ref[:, lanes], gq_ref[...])
            k32, vjp_k = jax.vjp(_rms, kv_ref[:, lanes], gk_ref[...])
            q, k, v = q32.astype(BF16), k32.astype(BF16), kv_ref[:, vl].astype(BF16)
            p, l = _softmax_rows(_dot(q, k, NT) * scale)
            p = p / l
            do = dm_ref[:, lanes].astype(BF16)
            dp = _dot(do, v, NT)
            dsb = (p * (dp - jnp.sum(p * dp, axis=-1, keepdims=True))).astype(BF16)
            dqr, dgq_p = vjp_q(_dot(dsb, k, NN) * scale)
            dkr, dgk_p = vjp_k(_dot(dsb, q, TN) * scale)
            dq_ref[:, lanes] = dqr.astype(dq_ref.dtype)
            dkv_ref[:, lanes] += dkr
            dkv_ref[:, vl] += _dot(p.astype(BF16), do, TN)
            dgq_ref[...] += dgq_p
            dgk_ref[...] += dgk_p

    return pl.pallas_call(
        body, grid=(T // tq,),
        in_specs=[qspec, pl.BlockSpec(kv.shape, lambda i: (0, 0)), gain, gain,
                  pl.BlockSpec((tq, MEM_WIDTH), lambda i: (i, MAIN_WIDTH // MEM_WIDTH))],
        out_specs=[pl.BlockSpec((tq, MEM_WIDTH), lambda i: (i, 0)), pl.BlockSpec(kv.shape, lambda i: (0, 0)), gain, gain],
        out_shape=[jax.ShapeDtypeStruct((T, MEM_WIDTH), BF16), jax.ShapeDtypeStruct(kv.shape, F32),
                   jax.ShapeDtypeStruct((1, hd), F32), jax.ShapeDtypeStruct((1, hd), F32)],
        name=name, compiler_params=_params(1))(proj, kv, gq, gk, dmix)


def _cumsum_rows(name, x, reverse=False):
    T, C = x.shape
    pt = min(T, 256)

    def body(x_ref, o_ref):
        r = lax.broadcasted_iota(jnp.int32, (pt, pt), 0)
        c = lax.broadcasted_iota(jnp.int32, (pt, pt), 1)
        tri = ((c >= r) if reverse else (c <= r)).astype(F32)
        carry = jnp.zeros((1, C), F32)
        order = range(T // pt)
        for p in (reversed(order) if reverse else order):
            rows = pl.ds(p * pt, pt)
            blk = x_ref[rows, :]
            o_ref[rows, :] = _dot(tri, blk, NN, precision=lax.Precision.HIGHEST) + carry
            carry = carry + jnp.sum(blk, axis=0, keepdims=True)

    return pl.pallas_call(body, out_shape=jax.ShapeDtypeStruct((T, C), F32), name=name,
                          compiler_params=pltpu.CompilerParams(vmem_limit_bytes=V7X_VMEM_LIMIT))(x)


MESH = pl.DeviceIdType.MESH
ANY = pl.BlockSpec(memory_space=pl.ANY)


def _mesh_pos():
    return lax.axis_index("x"), lax.axis_index("y"), lax.axis_index("c")


def _all_gather(name, xs):
    n = len(xs)

    def body(*refs):
        x_refs, out_refs = refs[:n], refs[n:2 * n]
        send_sems, recv_sems, local_sems = refs[2 * n:]
        mx, my, mc = _mesh_pos()
        me, sibling = (mx, my, mc), (mx, my, 1 - mc)
        chips = [(1 - mx, my), (mx, 1 - my), (1 - mx, 1 - my)]

        def slot(a, px, py, pc):
            return out_refs[a].at[4 * px + 2 * py + pc]

        def copy(a, k, block, to, src=None):
            return pltpu.make_async_remote_copy(
                src_ref=slot(a, *block) if src is None else src, dst_ref=slot(a, *block),
                send_sem=send_sems.at[7 * a + k], recv_sem=recv_sems.at[7 * a + k], device_id=to, device_id_type=MESH)

        mine = [pltpu.make_async_copy(x_refs[a], slot(a, *me), local_sems.at[a]) for a in range(n)]
        first = []
        for a in range(n):
            mine[a].start()
            first.append(copy(a, 0, me, sibling, src=x_refs[a]))
            first += [copy(a, 1 + j, me, (*chip, mc), src=x_refs[a]) for j, chip in enumerate(chips)]
        for cp in first:
            cp.start()
        passed = []
        for j, chip in enumerate(chips):
            for a in range(n):
                copy(a, 1 + j, (*chip, mc), me).wait_recv()
                passed.append(copy(a, 4 + j, (*chip, mc), sibling))
                passed[-1].start()
        for a in range(n):
            copy(a, 0, sibling, me).wait_recv()
            for j, chip in enumerate(chips):
                copy(a, 4 + j, (*chip, 1 - mc), me).wait_recv()
        for cp in first + passed:
            cp.wait_send()
        for cp in mine:
            cp.wait()

    return pl.pallas_call(
        body, out_shape=[jax.ShapeDtypeStruct((N_DEV,) + x.shape, x.dtype) for x in xs], in_specs=[ANY] * n, out_specs=[ANY] * n,
        scratch_shapes=[pltpu.SemaphoreType.DMA((7 * n,)), pltpu.SemaphoreType.DMA((7 * n,)), pltpu.SemaphoreType.DMA((n,))],
        name=name)(*xs)


def _exchange_cores(name, gs):
    n = len(gs)

    def body(*refs):
        g_refs, recv_refs = refs[:n], refs[n:2 * n]
        send_sems, recv_sems = refs[2 * n:]
        mx, my, mc = _mesh_pos()
        swap = [pltpu.make_async_remote_copy(
            src_ref=g_refs[a].at[2 * q + (1 - mc)], dst_ref=recv_refs[a].at[q], send_sem=send_sems.at[4 * a + q],
            recv_sem=recv_sems.at[4 * a + q], device_id=(mx, my, 1 - mc), device_id_type=MESH) for a in range(n) for q in range(4)]
        for cp in swap:
            cp.start()
        for cp in swap:
            cp.wait()

    return pl.pallas_call(
        body, out_shape=[jax.ShapeDtypeStruct((4,) + g.shape[1:], g.dtype) for g in gs], in_specs=[ANY] * n, out_specs=[ANY] * n,
        scratch_shapes=[pltpu.SemaphoreType.DMA((4 * n,)), pltpu.SemaphoreType.DMA((4 * n,))],
        name=name)(*gs)


def _exchange_chips(name, ss):
    n = len(ss)

    def body(*refs):
        s_refs, recv_refs = refs[:n], refs[n:2 * n]
        send_sems, recv_sems, local_sems = refs[2 * n:]
        mx, my, mc = _mesh_pos()
        myq = 2 * mx + my
        chips = [(1 - mx, my), (mx, 1 - my), (1 - mx, 1 - my)]
        mine = [pltpu.make_async_copy(s_refs[a].at[myq], recv_refs[a].at[myq], local_sems.at[a]) for a in range(n)]
        for cp in mine:
            cp.start()
        swap = [pltpu.make_async_remote_copy(
            src_ref=s_refs[a].at[2 * px + py], dst_ref=recv_refs[a].at[myq], send_sem=send_sems.at[3 * a + k],
            recv_sem=recv_sems.at[3 * a + k], device_id=(px, py, mc), device_id_type=MESH)
            for a in range(n) for k, (px, py) in enumerate(chips)]
        for cp in swap:
            cp.start()
        for a in range(n):
            for k, (px, py) in enumerate(chips):
                pltpu.make_async_remote_copy(
                    src_ref=s_refs[a].at[myq], dst_ref=recv_refs[a].at[2 * px + py], send_sem=send_sems.at[3 * a + k],
                    recv_sem=recv_sems.at[3 * a + k], device_id=(px, py, mc), device_id_type=MESH).wait_recv()
        for cp in swap:
            cp.wait_send()
        for cp in mine:
            cp.wait()

    return pl.pallas_call(
        body, out_shape=[jax.ShapeDtypeStruct(s.shape, s.dtype) for s in ss], in_specs=[ANY] * n, out_specs=[ANY] * n,
        scratch_shapes=[pltpu.SemaphoreType.DMA((3 * n,)), pltpu.SemaphoreType.DMA((3 * n,)), pltpu.SemaphoreType.DMA((n,))],
        name=name)(*ss)


def _pair_sum(name, g, recv, mc):
    _, R, C = g.shape
    tm = _row_tile(R, 512)

    def body(mc_ref, own_ref, recv_ref, o_ref):
        o_ref[...] = (own_ref[...].astype(F32) + recv_ref[...].astype(F32)).astype(o_ref.dtype)

    spec = pl.BlockSpec((None, tm, C), lambda q, i, mc_ref: (q, i, 0))
    grid_spec = pltpu.PrefetchScalarGridSpec(
        num_scalar_prefetch=1, grid=(4, R // tm),
        in_specs=[pl.BlockSpec((None, tm, C), lambda q, i, mc_ref: (2 * q + mc_ref[0], i, 0)), spec], out_specs=spec)
    return pl.pallas_call(body, grid_spec=grid_spec, out_shape=jax.ShapeDtypeStruct((4, R, C), BF16), name=name,
                          compiler_params=_params(2))(mc, g, recv)


def _row_tile(R, cap):
    best = None
    for t in range(8, min(R, cap) + 1, 8):
        if R % t == 0:
            best = t
    return best or R


def _sum_slabs(name, a, out_dtype):
    n, R, C = a.shape
    tm = _row_tile(R, 512)

    def body(*refs):
        acc = refs[0][...].astype(F32)
        for r in refs[1:n]:
            acc = acc + r[...].astype(F32)
        refs[n][...] = acc.astype(out_dtype)

    return pl.pallas_call(
        body, grid=(R // tm,),
        in_specs=[pl.BlockSpec((None, tm, C), lambda i, q=q: (q, i, 0)) for q in range(n)],
        out_specs=pl.BlockSpec((tm, C), lambda i: (i, 0)), out_shape=jax.ShapeDtypeStruct((R, C), out_dtype),
        name=name, compiler_params=_params(1))(*([a] * n))


def _reduce_scatter(gs):
    mc = lax.axis_index("c").astype(jnp.int32).reshape(1)
    recvs = _exchange_cores("rs_cores", gs)
    pairs = [_pair_sum(f"rs_pair_sum{a}", g, r, mc) for a, (g, r) in enumerate(zip(gs, recvs))]
    return [_sum_slabs(f"rs_chip_sum{a}", r, F32) for a, r in enumerate(_exchange_chips("rs_chips", pairs))]


SMALL = ["ffn1_norm", "mix_norm", "mem_norm", "mem_q_gain", "mem_k_gain", "hgrn_o_gain", "fox_q_gain", "kv_norm",
         "fox_f_bias", "fox_k_gain", "ffn2_norm"]
COLS352 = ["ffn1_w_gate", "ffn1_w_up", "ffn2_w_gate", "ffn2_w_up"]
KV_SPLIT = 1024
KV_WIDTH = 2 * MAIN_WIDTH + FOX_HEADS


def _rows2d(w):
    return w.reshape(-1, w.shape[-1])


def _pad_cols(w, width):
    return jnp.pad(w, [(0, 0)] * (w.ndim - 1) + [(0, width - w.shape[-1])])


def _pack_rows1024(down1, down2, w_out, w_mem_kv, w_kv):
    kv = jnp.concatenate([w_kv[:, :KV_SPLIT], _pad_cols(w_kv[:, KV_SPLIT:], D_MODEL)], axis=0)
    return jnp.concatenate([_rows2d(down1), _rows2d(down2), _rows2d(w_out), _rows2d(_pad_cols(w_mem_kv, D_MODEL)), kv], axis=0)


def _unpack_rows1024(buf, shapes):
    out, off = [], 0
    for name in ("ffn1_w_down", "ffn2_w_down", "w_out", "w_mem_kv"):
        L, r, c = shapes[name]
        out.append(buf[off:off + L * r].reshape(L, r, D_MODEL)[:, :, :c])
        off += L * r
    r, c = shapes["w_kv"]
    out.append(jnp.concatenate([buf[off:off + r], buf[off + r:off + 2 * r, :c - KV_SPLIT]], axis=1))
    return out


def _pad128(a):
    flat = a.reshape(-1)
    return jnp.pad(flat, (0, -flat.shape[0] % LANES))


def _small_pack(parts):
    flat = jnp.concatenate([_pad128(p) for p in parts])
    rows = -(-flat.shape[0] // LANES)
    flat = jnp.pad(flat, (0, (-rows % 8) * LANES))
    return flat.reshape(-1, LANES)


def _small_unpack(buf, shapes):
    flat = buf.reshape(-1)
    out, off = [], 0
    for s in shapes:
        n = 1
        for d in s:
            n *= d
        out.append(flat[off:off + n].reshape(s))
        off += n + (-n % LANES)
    return out


def _lb_fn(l0, l1):
    m = lax.stop_gradient(jnp.maximum(l0, l1))
    e0, e1 = jnp.exp(l0 - m), jnp.exp(l1 - m)
    p0, p1 = e0 / (e0 + e1), e1 / (e0 + e1)
    return p0 - p0, (p0 + p1) - p0


def _lb_fwd(logits):
    return _rowwise("lb", _lb_fn, [logits[0:1], logits[1:2]], [], [(MAIN_WIDTH, F32)] * 2)


def _lb_bwd(logits, dlb0, dlb1):
    def fn(l0, l1, d0, d1):
        _, vjp = jax.vjp(_lb_fn, l0, l1)
        return vjp((d0, d1))
    return _rowwise("lb_bwd", fn, [logits[0:1], logits[1:2], dlb0, dlb1], [], [(MAIN_WIDTH, F32)] * 2)


def _adamw_fn(w, g, m, v):
    m = ADAM_B1 * m + (1.0 - ADAM_B1) * g
    v = ADAM_B2 * v + (1.0 - ADAM_B2) * jnp.square(g)
    m_hat = m / (1.0 - ADAM_B1 ** ADAM_STEP)
    v_hat = v / (1.0 - ADAM_B2 ** ADAM_STEP)
    return -ADAM_LR * (m_hat / (jnp.sqrt(v_hat) + ADAM_EPS) + ADAM_WD * w), m, v


def _adamw(name, w, g, m, v):
    shape = w.shape
    C = shape[-1]
    two = lambda a: a.reshape(-1, C)
    R = two(w).shape[0]
    outs = _rowwise(name, _adamw_fn, [two(w), two(g), two(m), two(v)], [], [(C, F32)] * 3, tm=_row_tile(R, 512))
    return [o.reshape(shape) for o in outs]


def _mixer_fwd(l, x1, mem, W, lbs, shared):
    T = x1.shape[0]
    tag = f"l{l}"
    h = _rms_fwd(tag + "_mixrms", x1, W["mix_norm"][l:l + 1])
    mem_n = _rms_fwd(tag + "_memrms", mem, W["mem_norm"][l:l + 1])
    kv = _mm(tag + "_memkv", [(mem_n, W["w_mem_kv"][l], NN)], [F32], mem.shape[0], 2 * MEM_WIDTH)
    if l < 2:
        proj = _proj_cols(tag + "_in", h, W["w_in_a"], l)
        main, o = _hgrn_fwd(tag + "_hgrn", proj, lbs[l], W["hgrn_o_gain"][l:l + 1])
    else:
        proj = _proj_cols(tag + "_in", h, W["w_in_b"], l - 2)
        main, o = _fox_fwd(tag + "_fox", proj, shared["kvf"], shared["cc"], shared["cr"], W["fox_q_gain"][l - 2:l - 1],
                           W["fox_k_gain"])
    mem_o = _mem_fwd(tag + "_mem", proj, kv, W["mem_q_gain"][l:l + 1], W["mem_k_gain"][l:l + 1])
    w_out = W["w_out"][l]
    x2 = _mm(tag + "_out", [(main, w_out[:MAIN_WIDTH], NN), (mem_o, w_out[MAIN_WIDTH:], NN)], [F32], T, D_MODEL,
             epi=lambda a, e: (e[0] + a[0] + a[1],), extras=[x1])
    return x2, dict(h=h, mem_n=mem_n, kv=kv, proj=proj, main=main, o=o, mem_o=mem_o)


def _mixer_bwd(l, x1, mem, W, lbs, shared, sv, dx2, acc):
    T = x1.shape[0]
    tag = f"l{l}b"
    w_out = W["w_out"][l]
    g = {}
    dmix = _mm(tag + "_dmix", [(dx2, w_out, NT)], [F32], T, D_MODEL)
    g["w_out"] = jnp.concatenate([
        _mm(tag + "_dwout_a", [(sv["main"], dx2, TN)], [BF16], MAIN_WIDTH, D_MODEL),
        _mm(tag + "_dwout_b", [(sv["mem_o"], dx2, TN)], [BF16], MEM_WIDTH, D_MODEL)], axis=0).reshape(N_DEV, -1, D_MODEL)
    dqm, dkv, g["mem_q_gain"], g["mem_k_gain"] = _mem_bwd(tag + "_mem", sv["proj"], sv["kv"], W["mem_q_gain"][l:l + 1],
                                                           W["mem_k_gain"][l:l + 1], dmix)
    if l < 2:
        dq, df, di, dg, g["lb"], g["hgrn_o_gain"] = _hgrn_bwd(tag + "_hgrn", sv["proj"], sv["o"], dmix, lbs[l],
                                                               W["hgrn_o_gain"][l:l + 1])
        dproj = jnp.concatenate([dq, df, di, dg, dqm], axis=1)
    else:
        dq, dgate, acc["dk"], acc["dv"], acc["dc"], g["fox_q_gain"], g["fox_k_gain"] = _fox_bwd(
            tag + "_fox", sv["proj"], shared["kvf"], shared["cc"], shared["cr"], W["fox_q_gain"][l - 2:l - 1], W["fox_k_gain"],
            sv["o"], dmix, acc["dk"], acc["dv"], acc["dc"])
        dproj = jnp.concatenate([dq, dgate, dqm], axis=1)
    dh, g["w_in"] = _proj_cols_bwd(tag + "_in", sv["h"], dproj, W["w_in_a"] if l < 2 else W["w_in_b"], l % 2)
    dx1, g["mix_norm"] = _rms_bwd(tag + "_mixrms", x1, W["mix_norm"][l:l + 1], dh, dres=dx2)
    dw_mem_kv = _mm(tag + "_dwmemkv", [(sv["mem_n"], dkv, TN)], [BF16], D_MODEL, 2 * MEM_WIDTH)
    g["w_mem_kv"] = _pad_cols(dw_mem_kv, D_MODEL).reshape(N_DEV, -1, D_MODEL)
    dmem_n = _mm(tag + "_dmemn", [(dkv, W["w_mem_kv"][l], NT)], [F32], mem.shape[0], D_MODEL)
    _, g["mem_norm"] = _rms_bwd(tag + "_memrms", mem, W["mem_norm"][l:l + 1], dmem_n)
    return dx1, g


def _forget_cols(kvf):
    return kvf[:, 2 * MAIN_WIDTH:2 * MAIN_WIDTH + LANES]


def _log_forget(kvf, bias):
    return _rowwise("kv_logf", lambda f, b: jax.nn.log_sigmoid(f + b), [_forget_cols(kvf)], [bias], [(LANES, F32)])[0]


def _step(x, mem, target, W, lb_logits):
    T = x.shape[0]
    W = dict(W, fox_k_gain=W["fox_k_gain"].reshape(1, -1))
    lbs = _lb_fwd(lb_logits)
    fox_bias = jnp.pad(W["fox_f_bias"], (0, LANES - FOX_HEADS)).reshape(1, LANES)
    w_kv = W["w_kv"]
    n_l = W["ffn1_norm"].shape[0]
    ffn = lambda which, l: (W[which + "_norm"][l:l + 1], W["cols352"], W["rows1024"],
                            (0 if which == "ffn1" else 2 * n_l) + l, (n_l if which == "ffn1" else 3 * n_l) + l,
                            (0 if which == "ffn1" else n_l) + l)

    saved, shared = [], {}
    for l in range(4):
        x1 = _ffn_fwd(f"l{l}_ffn1", x, *ffn("ffn1", l))
        x2, sv = _mixer_fwd(l, x1, mem, W, lbs, shared)
        x3 = _ffn_fwd(f"l{l}_ffn2", x2, *ffn("ffn2", l))
        sv.update(x=x, x1=x1, x2=x2)
        saved.append(sv)
        x = x3
        if l == 1:
            hk = _rms_fwd("kv_rms", x, W["kv_norm"].reshape(1, -1))
            kvf = _mm("kv_proj", [(hk, w_kv, NN)], [F32], T, w_kv.shape[1])
            cum = _cumsum_rows("kv_cum", _log_forget(kvf, fox_bias))[:, :FOX_HEADS].T
            shared = dict(kvf=kvf, cc=cum[:, :, None], cr=cum[:, None, :], hk=hk, x=x)

    def loss_fn(y, t):
        err = y - t
        return err * (1.0 / D_MODEL), jnp.sum(0.5 / D_MODEL * err * err, axis=0, keepdims=True)
    dx, loss = _rowwise("loss", loss_fn, [x, target], [], [(D_MODEL, F32)], [((1, D_MODEL), F32)])

    grads = [None] * 4
    acc = dict(dk=jnp.zeros((T, MAIN_WIDTH), F32), dv=jnp.zeros((T, MAIN_WIDTH), F32), dc=jnp.zeros((FOX_HEADS, 1, T), F32))
    gkv = {}
    for l in reversed(range(4)):
        sv = saved[l]
        if l == 1:
            dcum = jnp.pad(acc["dc"][:, 0, :].T, ((0, 0), (0, LANES - FOX_HEADS)))
            dlf = _cumsum_rows("kv_dcum", dcum, reverse=True)
            def dlogf_fn(d, f, b):
                p = d * _sigmoid(-(f + b))
                return p, jnp.sum(p, axis=0, keepdims=True)
            dfl, gkv["fox_f_bias"] = _rowwise("kv_dlogf", dlogf_fn, [dlf, _forget_cols(shared["kvf"])], [fox_bias],
                                              [(LANES, BF16)], [((1, LANES), F32)])
            dkvf = _pad_cols(jnp.concatenate([acc["dk"].astype(BF16), acc["dv"].astype(BF16), dfl], axis=1), w_kv.shape[1])
            dw_kv = _mm("kv_dw", [(shared["hk"], dkvf, TN)], [BF16], D_MODEL, dkvf.shape[1])
            gkv["w_kv"] = jnp.concatenate([dw_kv[:, :KV_SPLIT].reshape(N_DEV, -1, KV_SPLIT),
                                           dw_kv[:, KV_SPLIT:].reshape(N_DEV, -1, KV_SPLIT)], axis=1)
            dhk = _mm("kv_dh", [(dkvf, w_kv, NT)], [F32], T, D_MODEL)
            dx, gkv["kv_norm"] = _rms_bwd("kv_rmsb", shared["x"], W["kv_norm"].reshape(1, -1), dhk, dres=dx)
        g = {}
        dx2, g["ffn2_norm"], g["ffn2_w_gate"], g["ffn2_w_up"], g["ffn2_w_down"] = _ffn_bwd(f"l{l}b_ffn2", sv["x2"], *ffn("ffn2", l), dx)
        dx1, gm = _mixer_bwd(l, sv["x1"], mem, W, lbs, shared, sv, dx2, acc)
        g.update(gm)
        dx, g["ffn1_norm"], g["ffn1_w_gate"], g["ffn1_w_up"], g["ffn1_w_down"] = _ffn_bwd(f"l{l}b_ffn1", sv["x"], *ffn("ffn1", l), dx1)
        grads[l] = g

    rows = lambda names, ls=range(4): [grads[l][n] for n in names for l in ls]
    out = dict(cols352=jnp.concatenate(rows(COLS352), axis=1),
               cols416=jnp.concatenate(rows(["w_in"], (0, 1)), axis=1), cols224=jnp.concatenate(rows(["w_in"], (2, 3)), axis=1),
               rows1024=jnp.concatenate(rows(["ffn1_w_down", "ffn2_w_down", "w_out", "w_mem_kv"]) + [gkv["w_kv"]], axis=1))
    for n in ["ffn1_norm", "mix_norm", "mem_norm", "mem_q_gain", "mem_k_gain", "ffn2_norm"]:
        out[n] = jnp.concatenate([grads[l][n] for l in range(4)], axis=0)
    out["hgrn_o_gain"] = jnp.concatenate([grads[l]["hgrn_o_gain"] for l in (0, 1)], axis=0)
    out["fox_q_gain"] = jnp.concatenate([grads[l]["fox_q_gain"] for l in (2, 3)], axis=0)
    out["fox_k_gain"] = (grads[2]["fox_k_gain"] + grads[3]["fox_k_gain"]).reshape(-1)
    out["kv_norm"] = gkv["kv_norm"].reshape(-1)
    out["fox_f_bias"] = gkv["fox_f_bias"][0, :FOX_HEADS]
    dl0, dl1 = _lb_bwd(lb_logits, grads[0]["lb"], grads[1]["lb"])
    out["hgrn_lb_logits"] = jnp.concatenate([dl0, dl1], axis=0)
    return loss, dx, out


WEIGHTS = ["ffn1_norm", "ffn1_w_gate", "ffn1_w_up", "ffn1_w_down", "mix_norm", "mem_norm", "w_mem_kv", "mem_q_gain",
           "mem_k_gain", "w_in_a", "hgrn_lb_logits", "hgrn_o_gain", "w_in_b", "fox_q_gain", "kv_norm", "w_kv", "fox_f_bias",
           "fox_k_gain", "w_out", "ffn2_norm", "ffn2_w_gate", "ffn2_w_up", "ffn2_w_down"]
BIG = COLS352 + ["w_in_a", "w_in_b", "ffn1_w_down", "ffn2_w_down", "w_out", "w_mem_kv", "w_kv"]


def _train_step(a):
    bf = lambda n: a[n].astype(BF16)
    local = [jnp.concatenate([_rows2d(bf(n)) for n in COLS352], axis=0), _rows2d(bf("w_in_a")), _rows2d(bf("w_in_b")),
             _pack_rows1024(bf("ffn1_w_down"), bf("ffn2_w_down"), bf("w_out"), bf("w_mem_kv"), bf("w_kv"))]
    cols352, w_in_a, w_in_b, rows1024 = _all_gather("ag_weights", local)
    W = dict(cols352=cols352, w_in_a=w_in_a, w_in_b=w_in_b, rows1024=rows1024)
    n_l, r_out = a["w_out"].shape[:2]
    off = a["ffn1_w_down"].shape[0] * a["ffn1_w_down"].shape[1] * 2


    def whole(o):
        blocks = rows1024[:, o:o + n_l * r_out].reshape(N_DEV, n_l, r_out, D_MODEL)
        return blocks.transpose(1, 0, 2, 3).reshape(n_l, N_DEV * r_out, D_MODEL)

    W["w_out"] = whole(off)
    W["w_mem_kv"] = whole(off + n_l * r_out)[:, :, :a["w_mem_kv"].shape[2]]
    kv0 = off + 2 * n_l * r_out
    r_kv = a["w_kv"].shape[0]
    W["w_kv"] = jnp.concatenate([rows1024[:, kv0:kv0 + r_kv].reshape(-1, D_MODEL),
                                 rows1024[:, kv0 + r_kv:kv0 + 2 * r_kv].reshape(-1, D_MODEL)], axis=1)
    W.update({n: a[n] for n in SMALL})
    lb_shape = a["hgrn_lb_logits"].shape
    lb_all = _all_gather("ag_lb", [_small_pack([a["hgrn_lb_logits"]])])[0].reshape(N_DEV, -1)[:, :lb_shape[0] * lb_shape[1]]
    lb_logits = lb_all.reshape((N_DEV,) + lb_shape).transpose(1, 0, 2).reshape(lb_shape[0], -1)

    loss_part, dx, g = _step(a["x"][0], a["mem"][0], a["loss_target"][0], W, lb_logits)

    s352, s416, s224, s1024 = _reduce_scatter([g["cols352"], g["cols416"], g["cols224"], g["rows1024"]])
    grad = {n: s352[i * n_l * D_MODEL:(i + 1) * n_l * D_MODEL].reshape(a[n].shape) for i, n in enumerate(COLS352)}
    grad["w_in_a"], grad["w_in_b"] = s416.reshape(a["w_in_a"].shape), s224.reshape(a["w_in_b"].shape)
    names1024 = ["ffn1_w_down", "ffn2_w_down", "w_out", "w_mem_kv", "w_kv"]
    grad.update(dict(zip(names1024, _unpack_rows1024(s1024, {n: a[n].shape for n in names1024}))))

    zeros = [jnp.zeros(lb_logits.shape, F32), jnp.zeros(loss_part.shape, F32)]
    small_shapes = [a[n].shape for n in SMALL] + [lb_logits.shape, loss_part.shape]
    small_part = _small_pack([g[n] for n in SMALL] + [g["hgrn_lb_logits"], loss_part])
    small_sum = _sum_slabs("small_sum", _all_gather("ag_small", [small_part])[0], F32)
    small = _small_unpack(small_sum, small_shapes)
    grad.update(dict(zip(SMALL, small)))
    loss = jnp.sum(small[-1])
    me = 4 * lax.axis_index("x") + 2 * lax.axis_index("y") + lax.axis_index("c")
    grad["hgrn_lb_logits"] = lax.dynamic_slice_in_dim(small[-2], me * lb_shape[1], lb_shape[1], axis=1)

    delta, new_m, new_v = {}, {}, {}
    for n in BIG + ["hgrn_lb_logits"]:
        delta[n], new_m[n], new_v[n] = _adamw("adam_" + n, a[n], grad[n], a["m_" + n], a["v_" + n])
    packs = [_small_pack([a[p + n] for n in SMALL] + zeros) for p in ("", "m_", "v_")]
    upd = _rowwise("adam_small", _adamw_fn, [packs[0], small_sum, packs[1], packs[2]], [], [(LANES, F32)] * 3, tm=packs[0].shape[0])
    for d, u in zip((delta, new_m, new_v), upd):
        d.update(dict(zip(SMALL, _small_unpack(u, small_shapes))))
    return (loss, dx[None], *[grad[n] for n in WEIGHTS], *[delta[n] for n in WEIGHTS], *[new_m[n] for n in WEIGHTS],
            *[new_v[n] for n in WEIGHTS])


def kernel(x, mem, ffn1_norm, ffn1_w_gate, ffn1_w_up, ffn1_w_down, mix_norm, mem_norm, w_mem_kv, mem_q_gain, mem_k_gain, w_in_a, hgrn_lb_logits, hgrn_o_gain, w_in_b, fox_q_gain, kv_norm, w_kv, fox_f_bias, fox_k_gain, w_out, ffn2_norm, ffn2_w_gate, ffn2_w_up, ffn2_w_down, loss_target, m_ffn1_norm, m_ffn1_w_gate, m_ffn1_w_up, m_ffn1_w_down, m_mix_norm, m_mem_norm, m_w_mem_kv, m_mem_q_gain, m_mem_k_gain, m_w_in_a, m_hgrn_lb_logits, m_hgrn_o_gain, m_w_in_b, m_fox_q_gain, m_kv_norm, m_w_kv, m_fox_f_bias, m_fox_k_gain, m_w_out, m_ffn2_norm, m_ffn2_w_gate, m_ffn2_w_up, m_ffn2_w_down, v_ffn1_norm, v_ffn1_w_gate, v_ffn1_w_up, v_ffn1_w_down, v_mix_norm, v_mem_norm, v_w_mem_kv, v_mem_q_gain, v_mem_k_gain, v_w_in_a, v_hgrn_lb_logits, v_hgrn_o_gain, v_w_in_b, v_fox_q_gain, v_kv_norm, v_w_kv, v_fox_f_bias, v_fox_k_gain, v_w_out, v_ffn2_norm, v_ffn2_w_gate, v_ffn2_w_up, v_ffn2_w_down):
    return _train_step(dict(locals()))
```

```python
import functools

import jax
import jax.numpy as jnp
from jax import lax
from jax.experimental import pallas as pl
from jax.experimental.pallas import tpu as pltpu

F32, BF16 = jnp.float32, jnp.bfloat16
EPS = 1e-6
V7X_VMEM_LIMIT = 56 * 1024 * 1024
LANES = 128
N_DEV = 8

D_MODEL = 1024
MAIN_WIDTH = 768
MEM_WIDTH = 256
HG_HEAD_DIM = 128
HG_HEADS = 6
FOX_HEAD_DIM = 64
FOX_HEADS = 12
MEM_HEADS = 4
MEM_HEAD_DIM = 64
HG_BLOCK = 16

ADAM_LR, ADAM_B1, ADAM_B2, ADAM_EPS, ADAM_WD, ADAM_STEP = 0.001, 0.9, 0.999, 1e-08, 0.01, 10

NN = ((1,), (0,))
NT = ((1,), (1,))
TN = ((0,), (0,))


def _dot(a, b, dims, precision=None):
    return lax.dot_general(a, b, (dims, ((), ())), preferred_element_type=F32, precision=precision)


def _bdot(a, b, dims):
    return _dot(a.astype(BF16), b.astype(BF16), dims)


def _split(a):
    hi = a.astype(BF16)
    return hi, (a - hi.astype(F32)).astype(BF16)


def _fdot(a, b, dims):
    ah, al = _split(a)
    bh, bl = _split(b)
    return _dot(ah, bh, dims) + (_dot(ah, bl, dims) + _dot(al, bh, dims))


def _params(n_grid):
    return pltpu.CompilerParams(dimension_semantics=("arbitrary",) * n_grid, vmem_limit_bytes=V7X_VMEM_LIMIT)


def _rms(x, g):
    return x * lax.rsqrt(jnp.mean(x * x, axis=-1, keepdims=True) + EPS) * g


def _sigmoid(x):
    return jax.nn.sigmoid(x)


def _silu(x):
    return x * jax.nn.sigmoid(x)


MESH = pl.DeviceIdType.MESH
ANY = pl.BlockSpec(memory_space=pl.ANY)


def _mesh_pos():
    return lax.axis_index("x"), lax.axis_index("y"), lax.axis_index("c")


class _Comm:
    def __init__(self, gather=(), relay=(), scatter=()):
        self.gather, self.relay, self.scatter = list(gather), list(relay), list(scatter)
        self.arrays = self.gather + self.relay + self.scatter
        self.n_remote = 4 * len(self.gather) + 3 * len(self.relay) + 7 * len(self.scatter)
        self.n_local = len(self.gather) + len(self.scatter)

    def out_shapes(self):
        return ([jax.ShapeDtypeStruct((N_DEV,) + x.shape, x.dtype) for x in self.gather]
                + [jax.ShapeDtypeStruct(g.shape, g.dtype) for g in self.relay + self.scatter])

    def scratch(self):
        return [pltpu.SemaphoreType.DMA((self.n_remote,)), pltpu.SemaphoreType.DMA((self.n_remote,)),
                pltpu.SemaphoreType.DMA((max(self.n_local, 1),))]

    def _copies(self, ins, outs, send, recv, local, arrivals=True):
        mx, my, mc = _mesh_pos()
        flip = lambda v, f: 1 - v if f else v
        idx = lambda p: 4 * p[0] + 2 * p[1] + p[2]
        me = (mx, my, mc)
        count = [0, 0]
        loc, out, arrive = [], [], []

        def pair(src, dst, lands, to):
            k = count[0]
            count[0] += 1
            mk = lambda d: pltpu.make_async_remote_copy(src_ref=src, dst_ref=d, send_sem=send.at[k], recv_sem=recv.at[k],
                                                        device_id=to, device_id_type=MESH)
            out.append(mk(dst))
            if arrivals:
                arrive.append(mk(lands))

        def local_copy(src, dst):
            loc.append(pltpu.make_async_copy(src, dst, local.at[count[1]]))
            count[1] += 1

        refs = list(zip(ins, outs))
        near = [(0, 0, 1), (1, 0, 0), (0, 1, 0), (1, 1, 0)]
        for x, G in refs[:len(self.gather)]:
            local_copy(x, G.at[idx(me)])
            for f in near:
                peer = tuple(flip(v, b) for v, b in zip(me, f))
                pair(x, G.at[idx(me)], G.at[idx(peer)], peer)
        sibling = (mx, my, 1 - mc)
        for Gin, Gout in refs[len(self.gather):len(self.gather) + len(self.relay)]:
            for f in near[1:]:
                chip = (flip(mx, f[0]), flip(my, f[1]))
                pair(Gin.at[idx((*chip, mc))], Gout.at[idx((*chip, mc))], Gout.at[idx((*chip, 1 - mc))], sibling)
        every = near + [(1, 0, 1), (0, 1, 1), (1, 1, 1)]
        for g, R in refs[len(self.gather) + len(self.relay):]:
            local_copy(g.at[idx(me)], R.at[idx(me)])
            for f in every:
                peer = tuple(flip(v, b) for v, b in zip(me, f))
                pair(g.at[idx(peer)], R.at[idx(me)], R.at[idx(peer)], peer)
        return loc, out, arrive

    def start(self, ins, outs, send, recv, local):
        loc, out, _ = self._copies(ins, outs, send, recv, local, arrivals=False)
        for cp in loc + out:
            cp.start()

    def finish(self, ins, outs, send, recv, local):
        loc, out, arrive = self._copies(ins, outs, send, recv, local)
        for cp in arrive:
            cp.wait_recv()
        for cp in out:
            cp.wait_send()
        for cp in loc:
            cp.wait()


def _call(body, operands, out_shape, *, name, grid=(), in_specs=None, out_specs=None, scratch=(), comm=None):
    outs = list(out_shape) if isinstance(out_shape, (list, tuple)) else [out_shape]
    single = not isinstance(out_shape, (list, tuple))
    params = _params(len(grid))
    if comm is None or not comm.arrays:
        res = pl.pallas_call(body, grid=grid, in_specs=in_specs, out_specs=out_specs, out_shape=out_shape,
                             scratch_shapes=list(scratch), name=name, compiler_params=params)(*operands)
        return ([res] if single else list(res)), []
    n_in, n_out, n_s, n_c = len(operands), len(outs), len(scratch), len(comm.arrays)

    def wrapped(*refs):
        pos = [0]

        def take(n):
            pos[0] += n
            return refs[pos[0] - n:pos[0]]

        b_in, c_in, b_out, c_out, b_s, sems = take(n_in), take(n_c), take(n_out), take(n_c), take(n_s), take(3)
        ids = [pl.program_id(d) for d in range(len(grid))]
        first, last = True, True
        for d, i in enumerate(ids):
            first = (i == 0) & first
            last = (i == grid[d] - 1) & last
        if grid:
            pl.when(first)(lambda: comm.start(c_in, c_out, *sems))
        else:
            comm.start(c_in, c_out, *sems)
        body(*b_in, *b_out, *b_s)
        if grid:
            pl.when(last)(lambda: comm.finish(c_in, c_out, *sems))
        else:
            comm.finish(c_in, c_out, *sems)

    n_g = len(comm.gather)
    aliases = {n_in + n_g + r: n_out + n_g + r for r in range(len(comm.relay))}
    out_specs_l = list(out_specs) if isinstance(out_specs, (list, tuple)) else [out_specs]
    res = pl.pallas_call(
        wrapped, grid=grid, in_specs=list(in_specs) + [ANY] * n_c, out_specs=out_specs_l + [ANY] * n_c,
        out_shape=outs + comm.out_shapes(), scratch_shapes=list(scratch) + comm.scratch(), input_output_aliases=aliases,
        name=name, compiler_params=params)(*operands, *comm.arrays)
    return list(res[:n_out]), list(res[n_out:])


def _rowwise(name, fn, rows, consts, out_rows, out_reds=(), tm=256):
    R = rows[0].shape[0]
    tm = min(tm, R)
    assert R % tm == 0
    n_in, n_o = len(rows) + len(consts), len(out_rows)

    def body(*refs):
        outs = fn(*[r[...] for r in refs[:n_in]])
        if not isinstance(outs, (tuple, list)):
            outs = (outs,)
        for r, o in zip(refs[n_in:n_in + n_o], outs[:n_o]):
            r[...] = o.astype(r.dtype)
        red_refs = refs[n_in + n_o:]
        if red_refs:
            @pl.when(pl.program_id(0) == 0)
            def _():
                for r in red_refs:
                    r[...] = jnp.zeros(r.shape, r.dtype)
            for r, o in zip(red_refs, outs[n_o:]):
                r[...] += o

    zero = lambda n: (lambda i: (0,) * n)
    in_specs = [pl.BlockSpec((tm, a.shape[1]), lambda i: (i, 0)) for a in rows]
    in_specs += [pl.BlockSpec(c.shape, zero(c.ndim)) for c in consts]
    out_specs = [pl.BlockSpec((tm, c), lambda i: (i, 0)) for c, _ in out_rows]
    out_specs += [pl.BlockSpec(s, zero(len(s))) for s, _ in out_reds]
    out_shape = [jax.ShapeDtypeStruct((R, c), dt) for c, dt in out_rows]
    out_shape += [jax.ShapeDtypeStruct(s, dt) for s, dt in out_reds]
    return pl.pallas_call(body, grid=(R // tm,), in_specs=in_specs, out_specs=out_specs, out_shape=out_shape,
                          name=name, compiler_params=_params(1))(*rows, *consts)


def _tile(n, cap):
    best = None
    for t in range(LANES, min(n, cap) + 1, LANES):
        if n % t == 0:
            best = t
    return best or n


def _mm(name, pairs, out_dtypes, M, N, epi=None, extras=(), tm=512, tn=512):
    tm, tn = _tile(M, tm), _tile(N, tn)
    n_p, n_e = len(pairs), len(extras)
    modes = [m for _, _, m in pairs]

    def body(*refs):
        accs = [_bdot(refs[2 * k][...], refs[2 * k + 1][...], modes[k]) for k in range(n_p)]
        ex = [r[...] for r in refs[2 * n_p:2 * n_p + n_e]]
        outs = epi(accs, ex) if epi is not None else accs
        for r, o in zip(refs[2 * n_p + n_e:], outs):
            r[...] = o.astype(r.dtype)

    in_specs = []
    ops = []
    for a, b, mode in pairs:
        if mode == NN:
            K = a.shape[1]
            assert a.shape == (M, K) and b.shape == (K, N), (name, a.shape, b.shape)
            in_specs += [pl.BlockSpec((tm, K), lambda i, j: (i, 0)), pl.BlockSpec((K, tn), lambda i, j: (0, j))]
        elif mode == NT:
            K = a.shape[1]
            assert a.shape == (M, K) and b.shape == (N, K), (name, a.shape, b.shape)
            in_specs += [pl.BlockSpec((tm, K), lambda i, j: (i, 0)), pl.BlockSpec((tn, K), lambda i, j: (j, 0))]
        else:
            K = a.shape[0]
            assert a.shape == (K, M) and b.shape == (K, N), (name, a.shape, b.shape)
            in_specs += [pl.BlockSpec((K, tm), lambda i, j: (0, i)), pl.BlockSpec((K, tn), lambda i, j: (0, j))]
        ops += [a, b]
    in_specs += [pl.BlockSpec((tm, tn), lambda i, j: (i, j)) for _ in extras]
    out_specs = [pl.BlockSpec((tm, tn), lambda i, j: (i, j)) for _ in out_dtypes]
    out_shape = [jax.ShapeDtypeStruct((M, N), dt) for dt in out_dtypes]
    res = pl.pallas_call(body, grid=(M // tm, N // tn), in_specs=in_specs, out_specs=out_specs, out_shape=out_shape,
                         name=name, compiler_params=_params(2))(*ops, *extras)
    return res[0] if len(res) == 1 else res


def _rms_fwd(name, x, gain, dtype=BF16):
    return _rowwise(name, _rms, [x], [gain], [(x.shape[1], dtype)])[0]


def _rms_bwd(name, x, gain, dh, dres=None):
    def fn(x, dh, *rest):
        g = rest[-1]
        _, vjp = jax.vjp(_rms, x, g)
        dx, dg = vjp(dh)
        if dres is not None:
            dx = dx + rest[0]
        return dx, dg
    rows = [x, dh] + ([dres] if dres is not None else [])
    d = x.shape[1]
    return _rowwise(name, fn, rows, [gain], [(d, F32)], [((1, d), F32)])


def _ffn_specs(gcols, grows, ig, iu, idn):
    n = gcols.shape[2]
    D = grows.shape[2]
    wg = pl.BlockSpec((None, D, n), lambda i, j: (j, ig, 0))
    wu = pl.BlockSpec((None, D, n), lambda i, j: (j, iu, 0))
    wd = pl.BlockSpec((None, n, D), lambda i, j: (j, idn, 0))
    return n, wg, wu, wd


def _ffn_fwd(name, x, gain, gcols, grows, ig, iu, idn, tm=1024, comm=None):
    T, D = x.shape
    tm = min(T, tm)
    n, wg_s, wu_s, wd_s = _ffn_specs(gcols, grows, ig, iu, idn)
    last = N_DEV - 1

    def body(x_ref, g_ref, wg_ref, wu_ref, wd_ref, y_ref, h_s, acc):
        j = pl.program_id(1)

        @pl.when(j == 0)
        def _():
            h_s[...] = _rms(x_ref[...], g_ref[...]).astype(BF16)
            acc[...] = jnp.zeros(acc.shape, F32)
        h = h_s[...]
        z = _silu(_dot(h, wg_ref[...], NN)) * _dot(h, wu_ref[...], NN)
        acc[...] += _dot(z.astype(BF16), wd_ref[...], NN)

        @pl.when(j == last)
        def _():
            y_ref[...] = x_ref[...] + 0.5 * acc[...]

    row = pl.BlockSpec((tm, D), lambda i, j: (i, 0))
    (y,), moved = _call(
        body, [x, gain, gcols, gcols, grows], [jax.ShapeDtypeStruct((T, D), F32)], name=name, grid=(T // tm, N_DEV),
        in_specs=[row, pl.BlockSpec((1, D), lambda i, j: (0, 0)), wg_s, wu_s, wd_s], out_specs=[row],
        scratch=[pltpu.VMEM((tm, D), BF16), pltpu.VMEM((tm, D), F32)], comm=comm)
    return y, moved


def _ffn_bwd(tag, x, gain, gcols, grows, ig, iu, idn, dy, tm=512, comm_a=None, comm_w=None):
    T, D = x.shape
    tm = min(T, tm)
    n, wg_s, wu_s, wd_s = _ffn_specs(gcols, grows, ig, iu, idn)
    last = N_DEV - 1

    def body(x_ref, dy_ref, g_ref, wg_ref, wu_ref, wd_ref, dx_ref, dg_ref, h_ref, z_ref, da_ref, db_ref, dh_acc):
        i, j = pl.program_id(0), pl.program_id(1)

        @pl.when(j == 0)
        def _():
            h_ref[...] = _rms(x_ref[...], g_ref[...]).astype(BF16)
            dh_acc[...] = jnp.zeros(dh_acc.shape, F32)

        @pl.when((i == 0) & (j == 0))
        def _():
            dg_ref[...] = jnp.zeros(dg_ref.shape, F32)
        h = h_ref[...]
        a, b = _dot(h, wg_ref[...], NN), _dot(h, wu_ref[...], NN)
        dz = 0.5 * _dot(dy_ref[...].astype(BF16), wd_ref[...], NT)
        s = _sigmoid(a)
        si = a * s
        da = (dz * b * (s + si * (1.0 - s))).astype(BF16)
        db = (dz * si).astype(BF16)
        z_ref[...] = (si * b).astype(BF16)
        da_ref[...] = da
        db_ref[...] = db
        dh_acc[...] += _dot(da, wg_ref[...], NT) + _dot(db, wu_ref[...], NT)

        @pl.when(j == last)
        def _():
            _, vjp = jax.vjp(_rms, x_ref[...], g_ref[...])
            dx, dg = vjp(dh_acc[...])
            dx_ref[...] = dx + dy_ref[...]
            dg_ref[...] += dg

    row = pl.BlockSpec((tm, D), lambda i, j: (i, 0))
    vec = pl.BlockSpec((1, D), lambda i, j: (0, 0))
    hid = pl.BlockSpec((None, tm, n), lambda i, j: (j, i, 0))
    hidden = jax.ShapeDtypeStruct((N_DEV, T, n), BF16)
    (dx, dgain, h, z, da, db), moved_a = _call(
        body, [x, dy, gain, gcols, gcols, grows],
        [jax.ShapeDtypeStruct((T, D), F32), jax.ShapeDtypeStruct((1, D), F32), jax.ShapeDtypeStruct((T, D), BF16),
         hidden, hidden, hidden],
        name=tag + "_a", grid=(T // tm, N_DEV), in_specs=[row, row, vec, wg_s, wu_s, wd_s],
        out_specs=[row, vec, row, hid, hid, hid], scratch=[pltpu.VMEM((tm, D), F32)], comm=comm_a)

    def wbody(h_ref, dy_ref, z_ref, da_ref, db_ref, dwg_ref, dwu_ref, dwd_ref):
        h = h_ref[...]
        dwg_ref[...] = _dot(h, da_ref[...], TN).astype(BF16)
        dwu_ref[...] = _dot(h, db_ref[...], TN).astype(BF16)
        dwd_ref[...] = (0.5 * _dot(z_ref[...], dy_ref[...].astype(BF16), TN)).astype(BF16)

    full = pl.BlockSpec((T, D), lambda j: (0, 0))
    hid_all = pl.BlockSpec((None, T, n), lambda j: (j, 0, 0))
    (dwg, dwu, dwd), moved_w = _call(
        wbody, [h, dy, z, da, db],
        [jax.ShapeDtypeStruct((N_DEV, D, n), BF16)] * 2 + [jax.ShapeDtypeStruct((N_DEV, n, D), BF16)],
        name=tag + "_w", grid=(N_DEV,), in_specs=[full, full, hid_all, hid_all, hid_all],
        out_specs=[pl.BlockSpec((None, D, n), lambda j: (j, 0, 0))] * 2 + [pl.BlockSpec((None, n, D), lambda j: (j, 0, 0))],
        comm=comm_w)
    return dx, dgain, dwg, dwu, dwd, moved_a, moved_w


def _wcols_spec(gw, l, grid_rank):
    _, _, n = gw.shape
    K = D_MODEL
    zero = (lambda i: (0, l, 0)) if grid_rank == 1 else (lambda i, j: (0, l, 0))
    return n, K, pl.BlockSpec((N_DEV, K, n), zero)


def _proj_cols(name, h, gw, l, tm=512, comm=None):
    T = h.shape[0]
    tm = min(T, tm)
    n, K, wspec = _wcols_spec(gw, l, 1)

    def body(h_ref, w_ref, o_ref):
        h = h_ref[...]
        for j in range(N_DEV):
            o_ref[:, pl.ds(j * n, n)] = _dot(h, w_ref[j], NN)

    (proj,), moved = _call(
        body, [h, gw], [jax.ShapeDtypeStruct((T, N_DEV * n), F32)], name=name, grid=(T // tm,),
        in_specs=[pl.BlockSpec((tm, K), lambda i: (i, 0)), wspec], out_specs=[pl.BlockSpec((tm, N_DEV * n), lambda i: (i, 0))],
        comm=comm)
    return proj, moved


def _proj_cols_bwd(tag, h, dproj, gw, l, tm=512, tk=512, comm=None):
    T = h.shape[0]
    tm = min(T, tm)
    n, K, wspec = _wcols_spec(gw, l, 1)

    def dh_body(dp_ref, w_ref, o_ref):
        acc = jnp.zeros(o_ref.shape, F32)
        for j in range(N_DEV):
            acc = acc + _dot(dp_ref[:, pl.ds(j * n, n)], w_ref[j], NT)
        o_ref[...] = acc

    (dh,), moved = _call(
        dh_body, [dproj, gw], [jax.ShapeDtypeStruct((T, K), F32)], name=tag + "_dh", grid=(T // tm,),
        in_specs=[pl.BlockSpec((tm, N_DEV * n), lambda i: (i, 0)), wspec], out_specs=[pl.BlockSpec((tm, K), lambda i: (i, 0))],
        comm=comm)

    def dw_body(h_ref, dp_ref, o_ref):
        h = h_ref[...]
        for j in range(N_DEV):
            o_ref[j] = _dot(h, dp_ref[:, pl.ds(j * n, n)], TN).astype(BF16)

    dw = pl.pallas_call(
        dw_body, grid=(K // tk,), in_specs=[pl.BlockSpec((T, tk), lambda i: (0, i)), pl.BlockSpec((T, N_DEV * n), lambda i: (0, 0))],
        out_specs=pl.BlockSpec((N_DEV, tk, n), lambda i: (0, i, 0)), out_shape=jax.ShapeDtypeStruct((N_DEV, K, n), BF16),
        name=tag + "_dw", compiler_params=_params(1))(h, dproj)
    return dh, dw, moved


def _block_tri(n, reverse=False):
    r = lax.broadcasted_iota(jnp.int32, (n, n), 0)
    c = lax.broadcasted_iota(jnp.int32, (n, n), 1)
    same = (r // HG_BLOCK) == (c // HG_BLOCK)
    return (same & ((c >= r) if reverse else (c <= r))).astype(F32)


def _hgrn_prep(q_ref, f_ref, lbv, qs, ks, cs, T):
    pt = min(T, 256)
    tri = _block_tri(pt)
    for p in range(T // pt):
        rows = pl.ds(p * pt, pt)
        f = lbv + (1.0 - lbv) * _sigmoid(f_ref[rows, :])
        qs[rows, :] = _silu(q_ref[rows, :])
        ks[rows, :] = 1.0 - f
        cs[rows, :] = _dot(tri, jnp.log(f), NN, precision=lax.Precision.HIGHEST)


def _strided(t, nb):
    return pl.ds(t, nb, stride=HG_BLOCK)


def _gate_out(o, og, g):
    return _rms(o, og) * _silu(g)


HG_UNROLL = 4


def _block_rows(n):
    return pl.ds(pl.multiple_of(n * HG_BLOCK, HG_BLOCK), HG_BLOCK)


def _blocks_loop(nb, fn):
    u = HG_UNROLL if nb % HG_UNROLL == 0 else 1

    def step(i, carry):
        for k in range(u):
            fn(i * u + k)
        return carry

    lax.fori_loop(0, nb // u, step, 0)


def _scan_states(buf, cs, nb, reverse=False):
    def step(m, st):
        n = nb - 1 - m if reverse else m
        own = buf[n]
        buf[n] = st
        rows = _block_rows(n)
        return jnp.exp(cs[rows, :][HG_BLOCK - 1:HG_BLOCK, :]) * st + own

    lax.fori_loop(0, nb, step, jnp.zeros(buf.shape[1:], F32))


def _hgrn_states(i_ref, ks, cs, states, nb):
    def own_step(n):
        rows = _block_rows(n)
        c = cs[rows, :]
        states[n] = _fdot(i_ref[rows, :], ks[rows, :] * jnp.exp(c[HG_BLOCK - 1:HG_BLOCK, :] - c), TN)

    _blocks_loop(nb, own_step)
    _scan_states(states, cs, nb)


def _hgrn_fwd(name, proj, lb, og, comm=None):
    T = proj.shape[0]
    nb = T // HG_BLOCK
    hd = HG_HEAD_DIM

    def body(q_ref, f_ref, i_ref, g_ref, lb_ref, og_ref, main_ref, o_ref, qs, ks, cs, states):
        _hgrn_prep(q_ref, f_ref, lb_ref[...], qs, ks, cs, T)
        for t in range(HG_BLOCK):
            qt, ct = qs[_strided(t, nb), :], cs[_strided(t, nb), :]
            acc = jnp.zeros((nb, hd), F32)
            for s in range(t + 1):
                w = qt * ks[_strided(s, nb), :] * jnp.exp(ct - cs[_strided(s, nb), :])
                acc = acc + jnp.sum(w, axis=-1, keepdims=True) * i_ref[_strided(s, nb), :]
            o_ref[_strided(t, nb), :] = acc

        _hgrn_states(i_ref, ks, cs, states, nb)

        def out_step(n):
            rows = _block_rows(n)
            o_ref[rows, :] += _fdot(qs[rows, :] * jnp.exp(cs[rows, :]), states[n], NT)

        _blocks_loop(nb, out_step)
        pt = min(T, 256)
        for p in range(T // pt):
            rows = pl.ds(p * pt, pt)
            main_ref[rows, :] = _gate_out(o_ref[rows, :], og_ref[...], g_ref[rows, :])

    nh = HG_HEADS
    col = lambda off: pl.BlockSpec((T, hd), lambda h, off=off: (0, off + h))
    (main, o), moved = _call(
        body, [proj, proj, proj, proj, lb, og], [jax.ShapeDtypeStruct((T, MAIN_WIDTH), F32)] * 2, name=name, grid=(nh,),
        in_specs=[col(0), col(nh), col(2 * nh), col(3 * nh), pl.BlockSpec((1, hd), lambda h: (0, h)),
                  pl.BlockSpec((1, hd), lambda h: (0, 0))],
        out_specs=[col(0), col(0)], scratch=[pltpu.VMEM((T, hd), F32)] * 3 + [pltpu.VMEM((nb, hd, hd), F32)], comm=comm)
    return main, o, moved


def _hgrn_bwd(name, proj, o, dmix, lb, og, comm=None):
    T = proj.shape[0]
    nb = T // HG_BLOCK
    hd = HG_HEAD_DIM
    pt = min(T, 256)

    def body(q_ref, f_ref, i_ref, g_ref, o_ref, dm_ref, lb_ref, og_ref,
             dq_ref, df_ref, di_ref, dg_ref, dlb_ref, dog_ref, qs, ks, cs, dos, dqs, dks, dvs, states, behind):
        lbv = lb_ref[...]
        _hgrn_prep(q_ref, f_ref, lbv, qs, ks, cs, T)
        dog = jnp.zeros((1, hd), F32)
        for p in range(T // pt):
            rows = pl.ds(p * pt, pt)
            _, vjp = jax.vjp(_gate_out, o_ref[rows, :], og_ref[...], g_ref[rows, :])
            do, dog_p, dg = vjp(dm_ref[rows, :])
            dos[rows, :] = do
            dg_ref[rows, :] = dg.astype(dg_ref.dtype)
            dog = dog + dog_p

        @pl.when(pl.program_id(0) == 0)
        def _():
            dog_ref[...] = jnp.zeros(dog_ref.shape, F32)
        dog_ref[...] += dog

        for t in range(HG_BLOCK):
            dqs[_strided(t, nb), :] = jnp.zeros((nb, hd), F32)
        for s in range(HG_BLOCK):
            k_s, c_s, v_s = ks[_strided(s, nb), :], cs[_strided(s, nb), :], i_ref[_strided(s, nb), :]
            dk = jnp.zeros((nb, hd), F32)
            dv = jnp.zeros((nb, hd), F32)
            for t in range(s, HG_BLOCK):
                q_t, do_t = qs[_strided(t, nb), :], dos[_strided(t, nb), :]
                e = jnp.exp(cs[_strided(t, nb), :] - c_s)
                a = jnp.sum(q_t * k_s * e, axis=-1, keepdims=True)
                g = jnp.sum(do_t * v_s, axis=-1, keepdims=True)
                dqs[_strided(t, nb), :] += g * k_s * e
                dk = dk + g * q_t * e
                dv = dv + a * do_t
            dks[_strided(s, nb), :] = dk
            dvs[_strided(s, nb), :] = dv

        _hgrn_states(i_ref, ks, cs, states, nb)

        def own_step(n):
            rows = _block_rows(n)
            behind[n] = _fdot(dos[rows, :], qs[rows, :] * jnp.exp(cs[rows, :]), TN)

        _blocks_loop(nb, own_step)
        _scan_states(behind, cs, nb, reverse=True)

        def grad_step(n):
            rows = _block_rows(n)
            c = cs[rows, :]
            ec, ek = jnp.exp(c), jnp.exp(c[HG_BLOCK - 1:HG_BLOCK, :] - c)
            dst = behind[n]
            dqs[rows, :] += _fdot(dos[rows, :], states[n], NN) * ec
            dks[rows, :] += _fdot(i_ref[rows, :], dst, NN) * ek
            dvs[rows, :] += _fdot(ks[rows, :] * ek, dst, NT)

        _blocks_loop(nb, grad_step)

        full = (lax.broadcasted_iota(jnp.int32, (pt, pt), 1) >= lax.broadcasted_iota(jnp.int32, (pt, pt), 0)).astype(F32)
        carry = jnp.zeros((1, hd), F32)
        dlb = jnp.zeros((1, hd), F32)
        for p in reversed(range(T // pt)):
            rows = pl.ds(p * pt, pt)
            q, k, dq, dk = qs[rows, :], ks[rows, :], dqs[rows, :], dks[rows, :]
            db = q * dq - k * dk
            dlf = _dot(full, db, NN, precision=lax.Precision.HIGHEST) + carry
            carry = carry + jnp.sum(db, axis=0, keepdims=True)
            sg = _sigmoid(f_ref[rows, :])
            df = dlf / (1.0 - k) - dk
            df_ref[rows, :] = (df * (1.0 - lbv) * sg * (1.0 - sg)).astype(df_ref.dtype)
            dlb = dlb + jnp.sum(df * (1.0 - sg), axis=0, keepdims=True)
            qr = q_ref[rows, :]
            sq = _sigmoid(qr)
            dq_ref[rows, :] = (dq * (sq + qr * sq * (1.0 - sq))).astype(dq_ref.dtype)
            di_ref[rows, :] = dvs[rows, :].astype(di_ref.dtype)
        dlb_ref[...] = dlb

    nh = HG_HEADS
    col = lambda off: pl.BlockSpec((T, hd), lambda h, off=off: (0, off + h))
    vec = pl.BlockSpec((1, hd), lambda h: (0, h))
    one = pl.BlockSpec((1, hd), lambda h: (0, 0))
    outs, moved = _call(
        body, [proj, proj, proj, proj, o, dmix, lb, og],
        [jax.ShapeDtypeStruct((T, MAIN_WIDTH), BF16)] * 4
        + [jax.ShapeDtypeStruct((1, MAIN_WIDTH), F32), jax.ShapeDtypeStruct((1, hd), F32)],
        name=name, grid=(nh,), in_specs=[col(0), col(nh), col(2 * nh), col(3 * nh), col(0), col(0), vec, one],
        out_specs=[col(0), col(0), col(0), col(0), vec, one],
        scratch=[pltpu.VMEM((T, hd), F32)] * 7 + [pltpu.VMEM((nb, hd, hd), F32)] * 2, comm=comm)
    return (*outs, moved)


def _softmax_rows(s):
    p = jnp.exp(s - jnp.max(s, axis=-1, keepdims=True))
    return p, jnp.sum(p, axis=-1, keepdims=True)


def _fox_logits(q, k, cc, cr, q0):
    s = _dot(q, k, NT) * (FOX_HEAD_DIM ** -0.5) + cc - cr
    row = lax.broadcasted_iota(jnp.int32, s.shape, 0) + q0
    col = lax.broadcasted_iota(jnp.int32, s.shape, 1)
    return jnp.where(col <= row, s, -jnp.inf)


def _fox_specs(T):
    w = 2 * FOX_HEAD_DIM
    n = MAIN_WIDTH // w
    col = lambda off: pl.BlockSpec((T, w), lambda p, off=off: (0, off + p))
    cc = pl.BlockSpec((2, T, 1), lambda p: (p, 0, 0))
    cr = pl.BlockSpec((2, 1, T), lambda p: (p, 0, 0))
    gain = pl.BlockSpec((1, FOX_HEAD_DIM), lambda p: (0, 0))
    return n, col, cc, cr, gain


def _fox_fwd(name, proj, kvf, cc, cr, gq, gk, comm=None):
    T = proj.shape[0]
    tq = min(T, 256)
    hd = FOX_HEAD_DIM

    def body(q_ref, g_ref, k_ref, v_ref, cc_ref, cr_ref, gq_ref, gk_ref, main_ref, o_ref):
        for hh in range(2):
            lanes = pl.ds(hh * hd, hd)
            k = _rms(k_ref[:, lanes], gk_ref[...]).astype(BF16)
            v = v_ref[:, lanes].astype(BF16)
            for qi in range(T // tq):
                rows, kl = pl.ds(qi * tq, tq), (qi + 1) * tq
                q = _rms(q_ref[rows, lanes], gq_ref[...]).astype(BF16)
                s = _fox_logits(q, k[:kl], cc_ref[hh, rows, :], cr_ref[hh, :, pl.ds(0, kl)], qi * tq)
                p, l = _softmax_rows(s)
                o = _dot(p.astype(BF16), v[:kl], NN) / l
                o_ref[rows, lanes] = o
                main_ref[rows, lanes] = o * _sigmoid(g_ref[rows, lanes])

    n, col, ccs, crs, gain = _fox_specs(T)
    (main, o), moved = _call(
        body, [proj, proj, kvf, kvf, cc, cr, gq, gk], [jax.ShapeDtypeStruct((T, MAIN_WIDTH), F32)] * 2, name=name, grid=(n,),
        in_specs=[col(0), col(n), col(0), col(n), ccs, crs, gain, gain], out_specs=[col(0), col(0)], comm=comm)
    return main, o, moved


def _fox_bwd(name, proj, kvf, cc, cr, gq, gk, o, dmix, pdk, pdv, pdc, comm=None):
    T = proj.shape[0]
    tq = min(T, 256)
    hd = FOX_HEAD_DIM
    scale = hd ** -0.5

    def body(q_ref, g_ref, k_ref, v_ref, cc_ref, cr_ref, gq_ref, gk_ref, o_ref, dm_ref, pdk_ref, pdv_ref, pdc_ref,
             dq_ref, dg_ref, dk_ref, dv_ref, dc_ref, dgq_ref, dgk_ref, dka, dva, dca):
        dgq = jnp.zeros((1, hd), F32)
        dgk = jnp.zeros((1, hd), F32)
        for hh in range(2):
            lanes = pl.ds(hh * hd, hd)
            k32, vjp_k = jax.vjp(_rms, k_ref[:, lanes], gk_ref[...])
            k = k32.astype(BF16)
            v = v_ref[:, lanes].astype(BF16)
            dka[...] = jnp.zeros(dka.shape, F32)
            dva[...] = jnp.zeros(dva.shape, F32)
            dca[...] = jnp.zeros(dca.shape, F32)
            for qi in range(T // tq):
                rows, kl = pl.ds(qi * tq, tq), (qi + 1) * tq
                q32, vjp_q = jax.vjp(_rms, q_ref[rows, lanes], gq_ref[...])
                q = q32.astype(BF16)
                s = _fox_logits(q, k[:kl], cc_ref[hh, rows, :], cr_ref[hh, :, pl.ds(0, kl)], qi * tq)
                p, l = _softmax_rows(s)
                p = p / l
                sg = _sigmoid(g_ref[rows, lanes])
                dm = dm_ref[rows, lanes]
                do = (dm * sg).astype(BF16)
                dg_ref[rows, lanes] = (dm * o_ref[rows, lanes] * sg * (1.0 - sg)).astype(dg_ref.dtype)
                dp = _dot(do, v[:kl], NT)
                ds = p * (dp - jnp.sum(p * dp, axis=-1, keepdims=True))
                dsb = ds.astype(BF16)
                dqr, dgq_p = vjp_q(_dot(dsb, k[:kl], NN) * scale)
                dq_ref[rows, lanes] = dqr.astype(dq_ref.dtype)
                dgq = dgq + dgq_p
                dka[pl.ds(0, kl), :] += _dot(dsb, q, TN) * scale
                dva[pl.ds(0, kl), :] += _dot(p.astype(BF16), do, TN)
                dca[:, pl.ds(0, kl)] -= jnp.sum(ds, axis=0, keepdims=True)
            dkr, dgk_p = vjp_k(dka[...])
            dgk = dgk + dgk_p
            dk_ref[:, lanes] = dkr + pdk_ref[:, lanes]
            dv_ref[:, lanes] = dva[...] + pdv_ref[:, lanes]
            dc_ref[hh] = dca[...] + pdc_ref[hh]

        @pl.when(pl.program_id(0) == 0)
        def _():
            dgq_ref[...] = jnp.zeros(dgq_ref.shape, F32)
            dgk_ref[...] = jnp.zeros(dgk_ref.shape, F32)
        dgq_ref[...] += dgq
        dgk_ref[...] += dgk

    n, col, ccs, crs, gain = _fox_specs(T)
    wide = jax.ShapeDtypeStruct((T, MAIN_WIDTH), F32)
    half = jax.ShapeDtypeStruct((T, MAIN_WIDTH), BF16)
    outs, moved = _call(
        body, [proj, proj, kvf, kvf, cc, cr, gq, gk, o, dmix, pdk, pdv, pdc],
        [half, half, wide, wide, jax.ShapeDtypeStruct((FOX_HEADS, 1, T), F32),
         jax.ShapeDtypeStruct((1, hd), F32), jax.ShapeDtypeStruct((1, hd), F32)],
        name=name, grid=(n,),
        in_specs=[col(0), col(n), col(0), col(n), ccs, crs, gain, gain, col(0), col(0), col(0), col(0), crs],
        out_specs=[col(0), col(0), col(0), col(0), crs, gain, gain],
        scratch=[pltpu.VMEM((T, hd), F32), pltpu.VMEM((T, hd), F32), pltpu.VMEM((1, T), F32)], comm=comm)
    return (*outs, moved)


def _mem_specs(T, width):
    tq = min(T, 512)
    q = pl.BlockSpec((tq, MEM_WIDTH), lambda i, c=(width - MEM_WIDTH) // MEM_WIDTH: (i, c))
    gain = pl.BlockSpec((1, MEM_HEAD_DIM), lambda i: (0, 0))
    return tq, q, gain


def _mem_fwd(name, proj, kv, gq, gk):
    T, W = proj.shape
    hd = MEM_HEAD_DIM
    tq, qspec, gain = _mem_specs(T, W)

    def body(q_ref, kv_ref, gq_ref, gk_ref, o_ref):
        for h in range(MEM_HEADS):
            lanes = pl.ds(h * hd, hd)
            q = _rms(q_ref[:, lanes], gq_ref[...]).astype(BF16)
            k = _rms(kv_ref[:, lanes], gk_ref[...]).astype(BF16)
            v = kv_ref[:, pl.ds(MEM_WIDTH + h * hd, hd)].astype(BF16)
            p, l = _softmax_rows(_dot(q, k, NT) * (hd ** -0.5))
            o_ref[:, lanes] = _dot(p.astype(BF16), v, NN) / l

    return pl.pallas_call(
        body, grid=(T // tq,),
        in_specs=[qspec, pl.BlockSpec(kv.shape, lambda i: (0, 0)), gain, gain],
        out_specs=pl.BlockSpec((tq, MEM_WIDTH), lambda i: (i, 0)),
        out_shape=jax.ShapeDtypeStruct((T, MEM_WIDTH), F32),
        name=name, compiler_params=_params(1))(proj, kv, gq, gk)


def _mem_bwd(name, proj, kv, gq, gk, dmix):
    T, W = proj.shape
    hd = MEM_HEAD_DIM
    scale = hd ** -0.5
    tq, qspec, gain = _mem_specs(T, W)

    def body(q_ref, kv_ref, gq_ref, gk_ref, dm_ref, dq_ref, dkv_ref, dgq_ref, dgk_ref):
        @pl.when(pl.program_id(0) == 0)
        def _():
            dkv_ref[...] = jnp.zeros(dkv_ref.shape, F32)
            dgq_ref[...] = jnp.zeros(dgq_ref.shape, F32)
            dgk_ref[...] = jnp.zeros(dgk_ref.shape, F32)
        for h in range(MEM_HEADS):
            lanes = pl.ds(h * hd, hd)
            vl = pl.ds(MEM_WIDTH + h * hd, hd)
            q32, vjp_q = jax.vjp(_rms, q_ref[:, lanes], gq_ref[...])
            k32, vjp_k = jax.vjp(_rms, kv_ref[:, lanes], gk_ref[...])
            q, k, v = q32.astype(BF16), k32.astype(BF16), kv_ref[:, vl].astype(BF16)
            p, l = _softmax_rows(_dot(q, k, NT) * scale)
            p = p / l
            do = dm_ref[:, lanes].astype(BF16)
            dp = _dot(do, v, NT)
            dsb = (p * (dp - jnp.sum(p * dp, axis=-1, keepdims=True))).astype(BF16)
            dqr, dgq_p = vjp_q(_dot(dsb, k, NN) * scale)
            dkr, dgk_p = vjp_k(_dot(dsb, q, TN) * scale)
            dq_ref[:, lanes] = dqr.astype(dq_ref.dtype)
            dkv_ref[:, lanes] += dkr
            dkv_ref[:, vl] += _dot(p.astype(BF16), do, TN)
            dgq_ref[...] += dgq_p
            dgk_ref[...] += dgk_p

    return pl.pallas_call(
        body, grid=(T // tq,),
        in_specs=[qspec, pl.BlockSpec(kv.shape, lambda i: (0, 0)), gain, gain,
                  pl.BlockSpec((tq, MEM_WIDTH), lambda i: (i, MAIN_WIDTH // MEM_WIDTH))],
        out_specs=[pl.BlockSpec((tq, MEM_WIDTH), lambda i: (i, 0)), pl.BlockSpec(kv.shape, lambda i: (0, 0)), gain, gain],
        out_shape=[jax.ShapeDtypeStruct((T, MEM_WIDTH), BF16), jax.ShapeDtypeStruct(kv.shape, F32),
                   jax.ShapeDtypeStruct((1, hd), F32), jax.ShapeDtypeStruct((1, hd), F32)],
        name=name, compiler_params=_params(1))(proj, kv, gq, gk, dmix)


def _cumsum_rows(name, x, reverse=False):
    T, C = x.shape
    pt = min(T, 256)

    def body(x_ref, o_ref):
        r = lax.broadcasted_iota(jnp.int32, (pt, pt), 0)
        c = lax.broadcasted_iota(jnp.int32, (pt, pt), 1)
        tri = ((c >= r) if reverse else (c <= r)).astype(F32)
        carry = jnp.zeros((1, C), F32)
        order = range(T // pt)
        for p in (reversed(order) if reverse else order):
            rows = pl.ds(p * pt, pt)
            blk = x_ref[rows, :]
            o_ref[rows, :] = _dot(tri, blk, NN, precision=lax.Precision.HIGHEST) + carry
            carry = carry + jnp.sum(blk, axis=0, keepdims=True)

    return pl.pallas_call(body, out_shape=jax.ShapeDtypeStruct((T, C), F32), name=name,
                          compiler_params=pltpu.CompilerParams(vmem_limit_bytes=V7X_VMEM_LIMIT))(x)


MESH = pl.DeviceIdType.MESH
ANY = pl.BlockSpec(memory_space=pl.ANY)


def _mesh_pos():
    return lax.axis_index("x"), lax.axis_index("y"), lax.axis_index("c")


def _all_gather(name, xs):
    n = len(xs)

    def body(*refs):
        x_refs, out_refs = refs[:n], refs[n:2 * n]
        send_sems, recv_sems, local_sems = refs[2 * n:]
        mx, my, mc = _mesh_pos()
        me, sibling = (mx, my, mc), (mx, my, 1 - mc)
        chips = [(1 - mx, my), (mx, 1 - my), (1 - mx, 1 - my)]

        def slot(a, px, py, pc):
            return out_refs[a].at[4 * px + 2 * py + pc]

        def copy(a, k, block, to, src=None):
            return pltpu.make_async_remote_copy(
                src_ref=slot(a, *block) if src is None else src, dst_ref=slot(a, *block),
                send_sem=send_sems.at[7 * a + k], recv_sem=recv_sems.at[7 * a + k], device_id=to, device_id_type=MESH)

        mine = [pltpu.make_async_copy(x_refs[a], slot(a, *me), local_sems.at[a]) for a in range(n)]
        first = []
        for a in range(n):
            mine[a].start()
            first.append(copy(a, 0, me, sibling, src=x_refs[a]))
            first += [copy(a, 1 + j, me, (*chip, mc), src=x_refs[a]) for j, chip in enumerate(chips)]
        for cp in first:
            cp.start()
        passed = []
        for j, chip in enumerate(chips):
            for a in range(n):
                copy(a, 1 + j, (*chip, mc), me).wait_recv()
                passed.append(copy(a, 4 + j, (*chip, mc), sibling))
                passed[-1].start()
        for a in range(n):
            copy(a, 0, sibling, me).wait_recv()
            for j, chip in enumerate(chips):
                copy(a, 4 + j, (*chip, 1 - mc), me).wait_recv()
        for cp in first + passed:
            cp.wait_send()
        for cp in mine:
            cp.wait()

    return pl.pallas_call(
        body, out_shape=[jax.ShapeDtypeStruct((N_DEV,) + x.shape, x.dtype) for x in xs], in_specs=[ANY] * n, out_specs=[ANY] * n,
        scratch_shapes=[pltpu.SemaphoreType.DMA((7 * n,)), pltpu.SemaphoreType.DMA((7 * n,)), pltpu.SemaphoreType.DMA((n,))],
        name=name)(*xs)


def _exchange_cores(name, gs):
    n = len(gs)

    def body(*refs):
        g_refs, recv_refs = refs[:n], refs[n:2 * n]
        send_sems, recv_sems = refs[2 * n:]
        mx, my, mc = _mesh_pos()
        swap = [pltpu.make_async_remote_copy(
            src_ref=g_refs[a].at[2 * q + (1 - mc)], dst_ref=recv_refs[a].at[q], send_sem=send_sems.at[4 * a + q],
            recv_sem=recv_sems.at[4 * a + q], device_id=(mx, my, 1 - mc), device_id_type=MESH) for a in range(n) for q in range(4)]
        for cp in swap:
            cp.start()
        for cp in swap:
            cp.wait()

    return pl.pallas_call(
        body, out_shape=[jax.ShapeDtypeStruct((4,) + g.shape[1:], g.dtype) for g in gs], in_specs=[ANY] * n, out_specs=[ANY] * n,
        scratch_shapes=[pltpu.SemaphoreType.DMA((4 * n,)), pltpu.SemaphoreType.DMA((4 * n,))],
        name=name)(*gs)


def _exchange_chips(name, ss):
    n = len(ss)

    def body(*refs):
        s_refs, recv_refs = refs[:n], refs[n:2 * n]
        send_sems, recv_sems, local_sems = refs[2 * n:]
        mx, my, mc = _mesh_pos()
        myq = 2 * mx + my
        chips = [(1 - mx, my), (mx, 1 - my), (1 - mx, 1 - my)]
        mine = [pltpu.make_async_copy(s_refs[a].at[myq], recv_refs[a].at[myq], local_sems.at[a]) for a in range(n)]
        for cp in mine:
            cp.start()
        swap = [pltpu.make_async_remote_copy(
            src_ref=s_refs[a].at[2 * px + py], dst_ref=recv_refs[a].at[myq], send_sem=send_sems.at[3 * a + k],
            recv_sem=recv_sems.at[3 * a + k], device_id=(px, py, mc), device_id_type=MESH)
            for a in range(n) for k, (px, py) in enumerate(chips)]
        for cp in swap:
            cp.start()
        for a in range(n):
            for k, (px, py) in enumerate(chips):
                pltpu.make_async_remote_copy(
                    src_ref=s_refs[a].at[myq], dst_ref=recv_refs[a].at[2 * px + py], send_sem=send_sems.at[3 * a + k],
                    recv_sem=recv_sems.at[3 * a + k], device_id=(px, py, mc), device_id_type=MESH).wait_recv()
        for cp in swap:
            cp.wait_send()
        for cp in mine:
            cp.wait()

    return pl.pallas_call(
        body, out_shape=[jax.ShapeDtypeStruct(s.shape, s.dtype) for s in ss], in_specs=[ANY] * n, out_specs=[ANY] * n,
        scratch_shapes=[pltpu.SemaphoreType.DMA((3 * n,)), pltpu.SemaphoreType.DMA((3 * n,)), pltpu.SemaphoreType.DMA((n,))],
        name=name)(*ss)


def _pair_sum(name, g, recv, mc):
    _, R, C = g.shape
    tm = _row_tile(R, 512)

    def body(mc_ref, own_ref, recv_ref, o_ref):
        o_ref[...] = (own_ref[...].astype(F32) + recv_ref[...].astype(F32)).astype(o_ref.dtype)

    spec = pl.BlockSpec((None, tm, C), lambda q, i, mc_ref: (q, i, 0))
    grid_spec = pltpu.PrefetchScalarGridSpec(
        num_scalar_prefetch=1, grid=(4, R // tm),
        in_specs=[pl.BlockSpec((None, tm, C), lambda q, i, mc_ref: (2 * q + mc_ref[0], i, 0)), spec], out_specs=spec)
    return pl.pallas_call(body, grid_spec=grid_spec, out_shape=jax.ShapeDtypeStruct((4, R, C), BF16), name=name,
                          compiler_params=_params(2))(mc, g, recv)


def _row_tile(R, cap):
    best = None
    for t in range(8, min(R, cap) + 1, 8):
        if R % t == 0:
            best = t
    return best or R


def _sum_slabs(name, a, out_dtype):
    n, R, C = a.shape
    tm = _row_tile(R, 512)

    def body(*refs):
        acc = refs[0][...].astype(F32)
        for r in refs[1:n]:
            acc = acc + r[...].astype(F32)
        refs[n][...] = acc.astype(out_dtype)

    return pl.pallas_call(
        body, grid=(R // tm,),
        in_specs=[pl.BlockSpec((None, tm, C), lambda i, q=q: (q, i, 0)) for q in range(n)],
        out_specs=pl.BlockSpec((tm, C), lambda i: (i, 0)), out_shape=jax.ShapeDtypeStruct((R, C), out_dtype),
        name=name, compiler_params=_params(1))(*([a] * n))


def _reduce_scatter(gs):
    mc = lax.axis_index("c").astype(jnp.int32).reshape(1)
    recvs = _exchange_cores("rs_cores", gs)
    pairs = [_pair_sum(f"rs_pair_sum{a}", g, r, mc) for a, (g, r) in enumerate(zip(gs, recvs))]
    return [_sum_slabs(f"rs_chip_sum{a}", r, F32) for a, r in enumerate(_exchange_chips("rs_chips", pairs))]


SMALL = ["ffn1_norm", "mix_norm", "mem_norm", "mem_q_gain", "mem_k_gain", "hgrn_o_gain", "fox_q_gain", "kv_norm",
         "fox_f_bias", "fox_k_gain", "ffn2_norm"]
COLS352 = ["ffn1_w_gate", "ffn1_w_up", "ffn2_w_gate", "ffn2_w_up"]
KV_SPLIT = 1024
KV_WIDTH = 2 * MAIN_WIDTH + FOX_HEADS


def _rows2d(w):
    return w.reshape(-1, w.shape[-1])


def _pad_cols(w, width):
    return jnp.pad(w, [(0, 0)] * (w.ndim - 1) + [(0, width - w.shape[-1])])


def _pack_rows1024(down1, down2, w_out, w_mem_kv, w_kv):
    kv = jnp.concatenate([w_kv[:, :KV_SPLIT], _pad_cols(w_kv[:, KV_SPLIT:], D_MODEL)], axis=0)
    return jnp.concatenate([_rows2d(down1), _rows2d(down2), _rows2d(w_out), _rows2d(_pad_cols(w_mem_kv, D_MODEL)), kv], axis=0)


def _unpack_rows1024(buf, shapes):
    out, off = [], 0
    for name in ("ffn1_w_down", "ffn2_w_down", "w_out", "w_mem_kv"):
        L, r, c = shapes[name]
        out.append(buf[off:off + L * r].reshape(L, r, D_MODEL)[:, :, :c])
        off += L * r
    r, c = shapes["w_kv"]
    out.append(jnp.concatenate([buf[off:off + r], buf[off + r:off + 2 * r, :c - KV_SPLIT]], axis=1))
    return out


def _pad128(a):
    flat = a.reshape(-1)
    return jnp.pad(flat, (0, -flat.shape[0] % LANES))


def _small_pack(parts):
    flat = jnp.concatenate([_pad128(p) for p in parts])
    rows = -(-flat.shape[0] // LANES)
    flat = jnp.pad(flat, (0, (-rows % 8) * LANES))
    return flat.reshape(-1, LANES)


def _small_unpack(buf, shapes):
    flat = buf.reshape(-1)
    out, off = [], 0
    for s in shapes:
        n = 1
        for d in s:
            n *= d
        out.append(flat[off:off + n].reshape(s))
        off += n + (-n % LANES)
    return out


def _lb_fn(l0, l1):
    m = lax.stop_gradient(jnp.maximum(l0, l1))
    e0, e1 = jnp.exp(l0 - m), jnp.exp(l1 - m)
    p0, p1 = e0 / (e0 + e1), e1 / (e0 + e1)
    return p0 - p0, (p0 + p1) - p0


def _lb_fwd(logits):
    return _rowwise("lb", _lb_fn, [logits[0:1], logits[1:2]], [], [(MAIN_WIDTH, F32)] * 2)


def _lb_bwd(logits, dlb0, dlb1):
    def fn(l0, l1, d0, d1):
        _, vjp = jax.vjp(_lb_fn, l0, l1)
        return vjp((d0, d1))
    return _rowwise("lb_bwd", fn, [logits[0:1], logits[1:2], dlb0, dlb1], [], [(MAIN_WIDTH, F32)] * 2)


def _adamw_fn(w, g, m, v):
    m = ADAM_B1 * m + (1.0 - ADAM_B1) * g
    v = ADAM_B2 * v + (1.0 - ADAM_B2) * jnp.square(g)
    m_hat = m / (1.0 - ADAM_B1 ** ADAM_STEP)
    v_hat = v / (1.0 - ADAM_B2 ** ADAM_STEP)
    return -ADAM_LR * (m_hat / (jnp.sqrt(v_hat) + ADAM_EPS) + ADAM_WD * w), m, v


def _adamw(name, w, g, m, v):
    shape = w.shape
    C = shape[-1]
    two = lambda a: a.reshape(-1, C)
    R = two(w).shape[0]
    outs = _rowwise(name, _adamw_fn, [two(w), two(g), two(m), two(v)], [], [(C, F32)] * 3, tm=_row_tile(R, 512))
    return [o.reshape(shape) for o in outs]


def _whole_rows(g, r0, r1):
    return g[:, r0:r1].reshape(N_DEV * (r1 - r0), g.shape[2])


def _w_out_of(Wl):
    n = Wl["d2"].shape[1]
    return _whole_rows(Wl["r1"], n, n + LANES)


def _w_mem_kv_of(Wl):
    n = Wl["d2"].shape[1]
    return _whole_rows(Wl["r1"], n + LANES, n + 2 * LANES)[:, :2 * MEM_WIDTH]


def _mixer_fwd(l, x1, mem, Wl, W, lbs, shared, nxt, Gn):
    T = x1.shape[0]
    tag = f"l{l}"
    h = _rms_fwd(tag + "_mixrms", x1, W["mix_norm"][l:l + 1])
    mem_n = _rms_fwd(tag + "_memrms", mem, W["mem_norm"][l:l + 1])
    kv = _mm(tag + "_memkv", [(mem_n, _w_mem_kv_of(Wl), NN)], [F32], mem.shape[0], 2 * MEM_WIDTH)
    proj, moved = _proj_cols(tag + "_in", h, Wl["win"], 0, comm=_Comm(relay=[Gn["gu1"]]) if nxt else None)
    along = _Comm(gather=[nxt["r1"], nxt["win"], nxt["gu2"]]) if nxt else None
    if nxt:
        Gn["gu1"] = moved[0]
    if l < 2:
        main, o, moved = _hgrn_fwd(tag + "_hgrn", proj, lbs[l], W["hgrn_o_gain"][l:l + 1], comm=along)
    else:
        main, o, moved = _fox_fwd(tag + "_fox", proj, shared["kvf"], shared["cc"], shared["cr"], W["fox_q_gain"][l - 2:l - 1],
                                  W["fox_k_gain"], comm=along)
    if nxt:
        Gn["r1"], Gn["win"], Gn["gu2"] = moved
    mem_o = _mem_fwd(tag + "_mem", proj, kv, W["mem_q_gain"][l:l + 1], W["mem_k_gain"][l:l + 1])
    w_out = _w_out_of(Wl)
    x2 = _mm(tag + "_out", [(main, w_out[:MAIN_WIDTH], NN), (mem_o, w_out[MAIN_WIDTH:], NN)], [F32], T, D_MODEL,
             epi=lambda a, e: (e[0] + a[0] + a[1],), extras=[x1])
    return x2, dict(h=h, mem_n=mem_n, kv=kv, proj=proj, main=main, o=o, mem_o=mem_o)


def _mixer_bwd(l, x1, mem, Wl, W, lbs, shared, sv, dx2, acc, ready, landed):
    T = x1.shape[0]
    tag = f"l{l}b"
    w_out = _w_out_of(Wl)
    g = {}
    dmix = _mm(tag + "_dmix", [(dx2, w_out, NT)], [F32], T, D_MODEL)
    dw_out = jnp.concatenate([
        _mm(tag + "_dwout_a", [(sv["main"], dx2, TN)], [BF16], MAIN_WIDTH, D_MODEL),
        _mm(tag + "_dwout_b", [(sv["mem_o"], dx2, TN)], [BF16], MEM_WIDTH, D_MODEL)], axis=0).reshape(N_DEV, -1, D_MODEL)
    dqm, dkv, g["mem_q_gain"], g["mem_k_gain"] = _mem_bwd(tag + "_mem", sv["proj"], sv["kv"], W["mem_q_gain"][l:l + 1],
                                                           W["mem_k_gain"][l:l + 1], dmix)
    along = _Comm(scatter=[v for _, v in ready])
    if l < 2:
        dq, df, di, dg, g["lb"], g["hgrn_o_gain"], moved = _hgrn_bwd(tag + "_hgrn", sv["proj"], sv["o"], dmix, lbs[l],
                                                                      W["hgrn_o_gain"][l:l + 1], comm=along)
        dproj = jnp.concatenate([dq, df, di, dg, dqm], axis=1)
    else:
        dq, dgate, acc["dk"], acc["dv"], acc["dc"], g["fox_q_gain"], g["fox_k_gain"], moved = _fox_bwd(
            tag + "_fox", sv["proj"], shared["kvf"], shared["cc"], shared["cr"], W["fox_q_gain"][l - 2:l - 1], W["fox_k_gain"],
            sv["o"], dmix, acc["dk"], acc["dv"], acc["dc"], comm=along)
        dproj = jnp.concatenate([dq, dgate, dqm], axis=1)
    landed.update({k: m for (k, _), m in zip(ready, moved)})
    dh, dw_in, moved = _proj_cols_bwd(tag + "_in", sv["h"], dproj, Wl["win"], 0, comm=_Comm(scatter=[dw_out]))
    landed[(l, "w_out")] = moved[0]
    dx1, g["mix_norm"] = _rms_bwd(tag + "_mixrms", x1, W["mix_norm"][l:l + 1], dh, dres=dx2)
    dw_mem_kv = _mm(tag + "_dwmemkv", [(sv["mem_n"], dkv, TN)], [BF16], D_MODEL, 2 * MEM_WIDTH)
    dmem_n = _mm(tag + "_dmemn", [(dkv, _w_mem_kv_of(Wl), NT)], [F32], mem.shape[0], D_MODEL)
    _, g["mem_norm"] = _rms_bwd(tag + "_memrms", mem, W["mem_norm"][l:l + 1], dmem_n)
    return dx1, g, [((l, "w_in"), dw_in), ((l, "w_mem_kv"), dw_mem_kv.reshape(N_DEV, -1, 2 * MEM_WIDTH))]


def _forget_cols(kvf):
    return kvf[:, 2 * MAIN_WIDTH:2 * MAIN_WIDTH + LANES]


def _log_forget(kvf, bias):
    return _rowwise("kv_logf", lambda f, b: jax.nn.log_sigmoid(f + b), [_forget_cols(kvf)], [bias], [(LANES, F32)])[0]


def _move(name, comm):
    def body(o_ref):
        o_ref[...] = jnp.zeros(o_ref.shape, F32)
    _, moved = _call(body, [], [jax.ShapeDtypeStruct((8, LANES), F32)], name=name, in_specs=[],
                     out_specs=[pl.BlockSpec(memory_space=pltpu.VMEM)], comm=comm)
    return moved


def _step(x, mem, target, W, lb_logits, G0, local):
    T = x.shape[0]
    W = dict(W, fox_k_gain=W["fox_k_gain"].reshape(1, -1))
    lbs = _lb_fwd(lb_logits)
    fox_bias = jnp.pad(W["fox_f_bias"], (0, LANES - FOX_HEADS)).reshape(1, LANES)
    w_kv = W["w_kv"]
    n_l = len(local)
    ffn1 = lambda l, Wl: (W["ffn1_norm"][l:l + 1], Wl["gu1"], Wl["r1"], 0, 1, 0)
    ffn2 = lambda l, Wl: (W["ffn2_norm"][l:l + 1], Wl["gu2"], Wl["d2"], 0, 1, 0)

    saved, shared, G = [], {}, [G0]
    for l in range(n_l):
        Wl, nxt, Gn = G[l], (local[l + 1] if l + 1 < n_l else None), {}
        along = _Comm(gather=[nxt["gu1"]] if nxt else [], relay=[Wl["d2"]] if l else [])
        x1, moved = _ffn_fwd(f"l{l}_ffn1", x, *ffn1(l, Wl), comm=along)
        if nxt:
            Gn["gu1"] = moved[0]
        if l:
            Wl["d2"] = moved[-1]
        x2, sv = _mixer_fwd(l, x1, mem, Wl, W, lbs, shared, nxt, Gn)
        along = _Comm(gather=[nxt["d2"]], relay=[Gn["r1"], Gn["win"], Gn["gu2"]]) if nxt else None
        x3, moved = _ffn_fwd(f"l{l}_ffn2", x2, *ffn2(l, Wl), comm=along)
        if nxt:
            Gn["d2"], Gn["r1"], Gn["win"], Gn["gu2"] = moved
            G.append(Gn)
        sv.update(x=x, x1=x1, x2=x2)
        saved.append(sv)
        x = x3
        if l == 1:
            hk = _rms_fwd("kv_rms", x, W["kv_norm"].reshape(1, -1))
            kvf = _mm("kv_proj", [(hk, w_kv, NN)], [F32], T, w_kv.shape[1])
            cum = _cumsum_rows("kv_cum", _log_forget(kvf, fox_bias))[:, :FOX_HEADS].T
            shared = dict(kvf=kvf, cc=cum[:, :, None], cr=cum[:, None, :], hk=hk, x=x)

    def loss_fn(y, t):
        err = y - t
        return err * (1.0 / D_MODEL), jnp.sum(0.5 / D_MODEL * err * err, axis=0, keepdims=True)
    dx, loss = _rowwise("loss", loss_fn, [x, target], [], [(D_MODEL, F32)], [((1, D_MODEL), F32)])

    grads = [None] * n_l
    acc = dict(dk=jnp.zeros((T, MAIN_WIDTH), F32), dv=jnp.zeros((T, MAIN_WIDTH), F32), dc=jnp.zeros((FOX_HEADS, 1, T), F32))
    gkv = {}
    landed, late = {}, []
    for l in reversed(range(n_l)):
        sv, Wl = saved[l], G[l]
        ready = []
        if l == 1:
            dcum = jnp.pad(acc["dc"][:, 0, :].T, ((0, 0), (0, LANES - FOX_HEADS)))
            dlf = _cumsum_rows("kv_dcum", dcum, reverse=True)
            def dlogf_fn(d, f, b):
                p = d * _sigmoid(-(f + b))
                return p, jnp.sum(p, axis=0, keepdims=True)
            dfl, gkv["fox_f_bias"] = _rowwise("kv_dlogf", dlogf_fn, [dlf, _forget_cols(shared["kvf"])], [fox_bias],
                                              [(LANES, BF16)], [((1, LANES), F32)])
            dkvf = _pad_cols(jnp.concatenate([acc["dk"].astype(BF16), acc["dv"].astype(BF16), dfl], axis=1), w_kv.shape[1])
            dw_kv = _mm("kv_dw", [(shared["hk"], dkvf, TN)], [BF16], D_MODEL, dkvf.shape[1])
            ready.append(((0, "w_kv"), dw_kv[:, :KV_WIDTH].reshape(N_DEV, -1, KV_WIDTH)))
            dhk = _mm("kv_dh", [(dkvf, w_kv, NT)], [F32], T, D_MODEL)
            dx, gkv["kv_norm"] = _rms_bwd("kv_rmsb", shared["x"], W["kv_norm"].reshape(1, -1), dhk, dres=dx)
        g = {}
        dx2, g["ffn2_norm"], dwg, dwu, dwd, moved_a, moved_w = _ffn_bwd(
            f"l{l}b_ffn2", sv["x2"], *ffn2(l, Wl), dx, comm_a=_Comm(scatter=[v for _, v in late[:2]]),
            comm_w=_Comm(scatter=[v for _, v in late[2:]]))
        landed.update({k: m for (k, _), m in zip(late, moved_a + moved_w)})
        ready += [((l, "ffn2_w_gate"), dwg), ((l, "ffn2_w_up"), dwu), ((l, "ffn2_w_down"), dwd)]
        dx1, gm, rest = _mixer_bwd(l, sv["x1"], mem, Wl, W, lbs, shared, sv, dx2, acc, ready, landed)
        g.update(gm)
        dx, g["ffn1_norm"], dwg, dwu, dwd, moved_a, _ = _ffn_bwd(f"l{l}b_ffn1", sv["x"], *ffn1(l, Wl), dx1,
                                                                 comm_a=_Comm(scatter=[v for _, v in rest]))
        landed.update({k: m for (k, _), m in zip(rest, moved_a)})
        late = [((l, "ffn1_w_gate"), dwg), ((l, "ffn1_w_up"), dwu), ((l, "ffn1_w_down"), dwd)]
        grads[l] = g
    landed.update({k: m for (k, _), m in zip(late, _move("rs_tail", _Comm(scatter=[v for _, v in late])))})

    out = {}
    for n in ["ffn1_norm", "mix_norm", "mem_norm", "mem_q_gain", "mem_k_gain", "ffn2_norm"]:
        out[n] = jnp.concatenate([grads[l][n] for l in range(4)], axis=0)
    out["hgrn_o_gain"] = jnp.concatenate([grads[l]["hgrn_o_gain"] for l in (0, 1)], axis=0)
    out["fox_q_gain"] = jnp.concatenate([grads[l]["fox_q_gain"] for l in (2, 3)], axis=0)
    out["fox_k_gain"] = (grads[2]["fox_k_gain"] + grads[3]["fox_k_gain"]).reshape(-1)
    out["kv_norm"] = gkv["kv_norm"].reshape(-1)
    out["fox_f_bias"] = gkv["fox_f_bias"][0, :FOX_HEADS]
    dl0, dl1 = _lb_bwd(lb_logits, grads[0]["lb"], grads[1]["lb"])
    out["hgrn_lb_logits"] = jnp.concatenate([dl0, dl1], axis=0)
    return loss, dx, out, landed


WEIGHTS = ["ffn1_norm", "ffn1_w_gate", "ffn1_w_up", "ffn1_w_down", "mix_norm", "mem_norm", "w_mem_kv", "mem_q_gain",
           "mem_k_gain", "w_in_a", "hgrn_lb_logits", "hgrn_o_gain", "w_in_b", "fox_q_gain", "kv_norm", "w_kv", "fox_f_bias",
           "fox_k_gain", "w_out", "ffn2_norm", "ffn2_w_gate", "ffn2_w_up", "ffn2_w_down"]
BIG = COLS352 + ["w_in_a", "w_in_b", "ffn1_w_down", "ffn2_w_down", "w_out", "w_mem_kv", "w_kv"]


def _train_step(a):
    bf = lambda w: w.astype(BF16)
    n_l = a["w_out"].shape[0]
    local = []
    for l in range(n_l):
        w_in = a["w_in_a"][l] if l < a["w_in_a"].shape[0] else a["w_in_b"][l - a["w_in_a"].shape[0]]
        local.append(dict(
            gu1=bf(jnp.concatenate([a["ffn1_w_gate"][l], a["ffn1_w_up"][l]], axis=0)),
            r1=bf(jnp.concatenate([a["ffn1_w_down"][l], a["w_out"][l], _pad_cols(a["w_mem_kv"][l], D_MODEL)], axis=0)),
            win=bf(w_in),
            gu2=bf(jnp.concatenate([a["ffn2_w_gate"][l], a["ffn2_w_up"][l]], axis=0)),
            d2=bf(a["ffn2_w_down"][l])))
    keys = ["gu1", "r1", "win", "gu2", "d2"]
    first = _all_gather("ag_first", [local[0][k] for k in keys] + [bf(a["w_kv"]), _small_pack([a["hgrn_lb_logits"]])])
    G0 = dict(zip(keys, first))
    W = {n: a[n] for n in SMALL}
    W["w_kv"] = _pad_cols(first[5].reshape(-1, a["w_kv"].shape[1]), 2 * D_MODEL)
    lb_shape = a["hgrn_lb_logits"].shape
    lb_all = first[6].reshape(N_DEV, -1)[:, :lb_shape[0] * lb_shape[1]]
    lb_logits = lb_all.reshape((N_DEV,) + lb_shape).transpose(1, 0, 2).reshape(lb_shape[0], -1)

    loss_part, dx, g, landed = _step(a["x"][0], a["mem"][0], a["loss_target"][0], W, lb_logits, G0, local)

    total = {k: _sum_slabs(f"rs_sum_l{k[0]}_{k[1]}", v, F32) for k, v in landed.items()}
    grad = {n: jnp.stack([total[(l, n)] for l in range(n_l)]) for n in COLS352 + ["ffn1_w_down", "ffn2_w_down", "w_out", "w_mem_kv"]}
    n_a = a["w_in_a"].shape[0]
    grad["w_in_a"] = jnp.stack([total[(l, "w_in")] for l in range(n_a)])
    grad["w_in_b"] = jnp.stack([total[(l, "w_in")] for l in range(n_a, n_l)])
    grad["w_kv"] = total[(0, "w_kv")]

    zeros = [jnp.zeros(lb_logits.shape, F32), jnp.zeros(loss_part.shape, F32)]
    small_shapes = [a[n].shape for n in SMALL] + [lb_logits.shape, loss_part.shape]
    small_part = _small_pack([g[n] for n in SMALL] + [g["hgrn_lb_logits"], loss_part])
    small_sum = _sum_slabs("small_sum", _all_gather("ag_small", [small_part])[0], F32)
    small = _small_unpack(small_sum, small_shapes)
    grad.update(dict(zip(SMALL, small)))
    loss = jnp.sum(small[-1])
    me = 4 * lax.axis_index("x") + 2 * lax.axis_index("y") + lax.axis_index("c")
    grad["hgrn_lb_logits"] = lax.dynamic_slice_in_dim(small[-2], me * lb_shape[1], lb_shape[1], axis=1)

    delta, new_m, new_v = {}, {}, {}
    for n in BIG + ["hgrn_lb_logits"]:
        delta[n], new_m[n], new_v[n] = _adamw("adam_" + n, a[n], grad[n], a["m_" + n], a["v_" + n])
    packs = [_small_pack([a[p + n] for n in SMALL] + zeros) for p in ("", "m_", "v_")]
    upd = _rowwise("adam_small", _adamw_fn, [packs[0], small_sum, packs[1], packs[2]], [], [(LANES, F32)] * 3, tm=packs[0].shape[0])
    for d, u in zip((delta, new_m, new_v), upd):
        d.update(dict(zip(SMALL, _small_unpack(u, small_shapes))))
    return (loss, dx[None], *[grad[n] for n in WEIGHTS], *[delta[n] for n in WEIGHTS], *[new_m[n] for n in WEIGHTS],
            *[new_v[n] for n in WEIGHTS])


def kernel(x, mem, ffn1_norm, ffn1_w_gate, ffn1_w_up, ffn1_w_down, mix_norm, mem_norm, w_mem_kv, mem_q_gain, mem_k_gain, w_in_a, hgrn_lb_logits, hgrn_o_gain, w_in_b, fox_q_gain, kv_norm, w_kv, fox_f_bias, fox_k_gain, w_out, ffn2_norm, ffn2_w_gate, ffn2_w_up, ffn2_w_down, loss_target, m_ffn1_norm, m_ffn1_w_gate, m_ffn1_w_up, m_ffn1_w_down, m_mix_norm, m_mem_norm, m_w_mem_kv, m_mem_q_gain, m_mem_k_gain, m_w_in_a, m_hgrn_lb_logits, m_hgrn_o_gain, m_w_in_b, m_fox_q_gain, m_kv_norm, m_w_kv, m_fox_f_bias, m_fox_k_gain, m_w_out, m_ffn2_norm, m_ffn2_w_gate, m_ffn2_w_up, m_ffn2_w_down, v_ffn1_norm, v_ffn1_w_gate, v_ffn1_w_up, v_ffn1_w_down, v_mix_norm, v_mem_norm, v_w_mem_kv, v_mem_q_gain, v_mem_k_gain, v_w_in_a, v_hgrn_lb_logits, v_hgrn_o_gain, v_w_in_b, v_fox_q_gain, v_kv_norm, v_w_kv, v_fox_f_bias, v_fox_k_gain, v_w_out, v_ffn2_norm, v_ffn2_w_gate, v_ffn2_w_up, v_ffn2_w_down):
    return _train_step(dict(locals()))
```

```python
import functools

import jax
import jax.numpy as jnp
from jax import lax
from jax.experimental import pallas as pl
from jax.experimental.pallas import tpu as pltpu

F32, BF16 = jnp.float32, jnp.bfloat16
EPS = 1e-6
V7X_VMEM_LIMIT = 56 * 1024 * 1024
LANES = 128
N_DEV = 8

D_MODEL = 1024
MAIN_WIDTH = 768
MEM_WIDTH = 256
HG_HEAD_DIM = 128
HG_HEADS = 6
FOX_HEAD_DIM = 64
FOX_HEADS = 12
MEM_HEADS = 4
MEM_HEAD_DIM = 64
HG_BLOCK = 16

ADAM_LR, ADAM_B1, ADAM_B2, ADAM_EPS, ADAM_WD, ADAM_STEP = 0.001, 0.9, 0.999, 1e-08, 0.01, 10

NN = ((1,), (0,))
NT = ((1,), (1,))
TN = ((0,), (0,))


def _dot(a, b, dims, precision=None):
    return lax.dot_general(a, b, (dims, ((), ())), preferred_element_type=F32, precision=precision)


def _bdot(a, b, dims):
    return _dot(a.astype(BF16), b.astype(BF16), dims)


def _split(a):
    hi = a.astype(BF16)
    return hi, (a - hi.astype(F32)).astype(BF16)


def _fdot(a, b, dims):
    ah, al = _split(a)
    bh, bl = _split(b)
    return _dot(ah, bh, dims) + (_dot(ah, bl, dims) + _dot(al, bh, dims))


def _params(n_grid):
    return pltpu.CompilerParams(dimension_semantics=("arbitrary",) * n_grid, vmem_limit_bytes=V7X_VMEM_LIMIT)


def _rms(x, g):
    return x * lax.rsqrt(jnp.mean(x * x, axis=-1, keepdims=True) + EPS) * g


def _sigmoid(x):
    return jax.nn.sigmoid(x)


def _silu(x):
    return x * jax.nn.sigmoid(x)


MESH = pl.DeviceIdType.MESH
ANY = pl.BlockSpec(memory_space=pl.ANY)


def _mesh_pos():
    return lax.axis_index("x"), lax.axis_index("y"), lax.axis_index("c")


class _Comm:
    def __init__(self, gather=(), relay=(), scatter=()):
        self.gather, self.relay, self.scatter = list(gather), list(relay), list(scatter)
        self.arrays = self.gather + self.relay + self.scatter
        self.n_remote = 4 * len(self.gather) + 3 * len(self.relay) + 7 * len(self.scatter)
        self.n_local = len(self.gather) + len(self.scatter)

    def out_shapes(self):
        return ([jax.ShapeDtypeStruct((N_DEV,) + x.shape, x.dtype) for x in self.gather]
                + [jax.ShapeDtypeStruct(g.shape, g.dtype) for g in self.relay + self.scatter])

    def scratch(self):
        return [pltpu.SemaphoreType.DMA((self.n_remote,)), pltpu.SemaphoreType.DMA((self.n_remote,)),
                pltpu.SemaphoreType.DMA((max(self.n_local, 1),))]

    def _copies(self, ins, outs, send, recv, local, arrivals=True):
        mx, my, mc = _mesh_pos()
        flip = lambda v, f: 1 - v if f else v
        idx = lambda p: 4 * p[0] + 2 * p[1] + p[2]
        me = (mx, my, mc)
        count = [0, 0]
        loc, out, arrive = [], [], []

        def pair(src, dst, lands, to):
            k = count[0]
            count[0] += 1
            mk = lambda d: pltpu.make_async_remote_copy(src_ref=src, dst_ref=d, send_sem=send.at[k], recv_sem=recv.at[k],
                                                        device_id=to, device_id_type=MESH)
            out.append(mk(dst))
            if arrivals:
                arrive.append(mk(lands))

        def local_copy(src, dst):
            loc.append(pltpu.make_async_copy(src, dst, local.at[count[1]]))
            count[1] += 1

        refs = list(zip(ins, outs))
        near = [(0, 0, 1), (1, 0, 0), (0, 1, 0), (1, 1, 0)]
        for x, G in refs[:len(self.gather)]:
            local_copy(x, G.at[idx(me)])
            for f in near:
                peer = tuple(flip(v, b) for v, b in zip(me, f))
                pair(x, G.at[idx(me)], G.at[idx(peer)], peer)
        sibling = (mx, my, 1 - mc)
        for Gin, Gout in refs[len(self.gather):len(self.gather) + len(self.relay)]:
            for f in near[1:]:
                chip = (flip(mx, f[0]), flip(my, f[1]))
                pair(Gin.at[idx((*chip, mc))], Gout.at[idx((*chip, mc))], Gout.at[idx((*chip, 1 - mc))], sibling)
        every = near + [(1, 0, 1), (0, 1, 1), (1, 1, 1)]
        for g, R in refs[len(self.gather) + len(self.relay):]:
            local_copy(g.at[idx(me)], R.at[idx(me)])
            for f in every:
                peer = tuple(flip(v, b) for v, b in zip(me, f))
                pair(g.at[idx(peer)], R.at[idx(me)], R.at[idx(peer)], peer)
        return loc, out, arrive

    def start(self, ins, outs, send, recv, local):
        loc, out, _ = self._copies(ins, outs, send, recv, local, arrivals=False)
        for cp in loc + out:
            cp.start()

    def finish(self, ins, outs, send, recv, local):
        loc, out, arrive = self._copies(ins, outs, send, recv, local)
        for cp in arrive:
            cp.wait_recv()
        for cp in out:
            cp.wait_send()
        for cp in loc:
            cp.wait()


def _call(body, operands, out_shape, *, name, grid=(), in_specs=None, out_specs=None, scratch=(), comm=None):
    outs = list(out_shape) if isinstance(out_shape, (list, tuple)) else [out_shape]
    single = not isinstance(out_shape, (list, tuple))
    params = _params(len(grid))
    if comm is None or not comm.arrays:
        res = pl.pallas_call(body, grid=grid, in_specs=in_specs, out_specs=out_specs, out_shape=out_shape,
                             scratch_shapes=list(scratch), name=name, compiler_params=params)(*operands)
        return ([res] if single else list(res)), []
    n_in, n_out, n_s, n_c = len(operands), len(outs), len(scratch), len(comm.arrays)

    def wrapped(*refs):
        pos = [0]

        def take(n):
            pos[0] += n
            return refs[pos[0] - n:pos[0]]

        b_in, c_in, b_out, c_out, b_s, sems = take(n_in), take(n_c), take(n_out), take(n_c), take(n_s), take(3)
        ids = [pl.program_id(d) for d in range(len(grid))]
        first, last = True, True
        for d, i in enumerate(ids):
            first = (i == 0) & first
            last = (i == grid[d] - 1) & last
        if grid:
            pl.when(first)(lambda: comm.start(c_in, c_out, *sems))
        else:
            comm.start(c_in, c_out, *sems)
        body(*b_in, *b_out, *b_s)
        if grid:
            pl.when(last)(lambda: comm.finish(c_in, c_out, *sems))
        else:
            comm.finish(c_in, c_out, *sems)

    n_g = len(comm.gather)
    aliases = {n_in + n_g + r: n_out + n_g + r for r in range(len(comm.relay))}
    out_specs_l = list(out_specs) if isinstance(out_specs, (list, tuple)) else [out_specs]
    res = pl.pallas_call(
        wrapped, grid=grid, in_specs=list(in_specs) + [ANY] * n_c, out_specs=out_specs_l + [ANY] * n_c,
        out_shape=outs + comm.out_shapes(), scratch_shapes=list(scratch) + comm.scratch(), input_output_aliases=aliases,
        name=name, compiler_params=params)(*operands, *comm.arrays)
    return list(res[:n_out]), list(res[n_out:])


def _rowwise(name, fn, rows, consts, out_rows, out_reds=(), tm=256):
    R = rows[0].shape[0]
    tm = min(tm, R)
    assert R % tm == 0
    n_in, n_o = len(rows) + len(consts), len(out_rows)

    def body(*refs):
        outs = fn(*[r[...] for r in refs[:n_in]])
        if not isinstance(outs, (tuple, list)):
            outs = (outs,)
        for r, o in zip(refs[n_in:n_in + n_o], outs[:n_o]):
            r[...] = o.astype(r.dtype)
        red_refs = refs[n_in + n_o:]
        if red_refs:
            @pl.when(pl.program_id(0) == 0)
            def _():
                for r in red_refs:
                    r[...] = jnp.zeros(r.shape, r.dtype)
            for r, o in zip(red_refs, outs[n_o:]):
                r[...] += o

    zero = lambda n: (lambda i: (0,) * n)
    in_specs = [pl.BlockSpec((tm, a.shape[1]), lambda i: (i, 0)) for a in rows]
    in_specs += [pl.BlockSpec(c.shape, zero(c.ndim)) for c in consts]
    out_specs = [pl.BlockSpec((tm, c), lambda i: (i, 0)) for c, _ in out_rows]
    out_specs += [pl.BlockSpec(s, zero(len(s))) for s, _ in out_reds]
    out_shape = [jax.ShapeDtypeStruct((R, c), dt) for c, dt in out_rows]
    out_shape += [jax.ShapeDtypeStruct(s, dt) for s, dt in out_reds]
    return pl.pallas_call(body, grid=(R // tm,), in_specs=in_specs, out_specs=out_specs, out_shape=out_shape,
                          name=name, compiler_params=_params(1))(*rows, *consts)


def _tile(n, cap):
    best = None
    for t in range(LANES, min(n, cap) + 1, LANES):
        if n % t == 0:
            best = t
    return best or n


def _mm(name, pairs, out_dtypes, M, N, epi=None, extras=(), tm=512, tn=512):
    tm, tn = _tile(M, tm), _tile(N, tn)
    n_p, n_e = len(pairs), len(extras)
    modes = [m for _, _, m in pairs]

    def body(*refs):
        accs = [_bdot(refs[2 * k][...], refs[2 * k + 1][...], modes[k]) for k in range(n_p)]
        ex = [r[...] for r in refs[2 * n_p:2 * n_p + n_e]]
        outs = epi(accs, ex) if epi is not None else accs
        for r, o in zip(refs[2 * n_p + n_e:], outs):
            r[...] = o.astype(r.dtype)

    in_specs = []
    ops = []
    for a, b, mode in pairs:
        if mode == NN:
            K = a.shape[1]
            assert a.shape == (M, K) and b.shape == (K, N), (name, a.shape, b.shape)
            in_specs += [pl.BlockSpec((tm, K), lambda i, j: (i, 0)), pl.BlockSpec((K, tn), lambda i, j: (0, j))]
        elif mode == NT:
            K = a.shape[1]
            assert a.shape == (M, K) and b.shape == (N, K), (name, a.shape, b.shape)
            in_specs += [pl.BlockSpec((tm, K), lambda i, j: (i, 0)), pl.BlockSpec((tn, K), lambda i, j: (j, 0))]
        else:
            K = a.shape[0]
            assert a.shape == (K, M) and b.shape == (K, N), (name, a.shape, b.shape)
            in_specs += [pl.BlockSpec((K, tm), lambda i, j: (0, i)), pl.BlockSpec((K, tn), lambda i, j: (0, j))]
        ops += [a, b]
    in_specs += [pl.BlockSpec((tm, tn), lambda i, j: (i, j)) for _ in extras]
    out_specs = [pl.BlockSpec((tm, tn), lambda i, j: (i, j)) for _ in out_dtypes]
    out_shape = [jax.ShapeDtypeStruct((M, N), dt) for dt in out_dtypes]
    res = pl.pallas_call(body, grid=(M // tm, N // tn), in_specs=in_specs, out_specs=out_specs, out_shape=out_shape,
                         name=name, compiler_params=_params(2))(*ops, *extras)
    return res[0] if len(res) == 1 else res


def _rms_fwd(name, x, gain, dtype=BF16):
    return _rowwise(name, _rms, [x], [gain], [(x.shape[1], dtype)])[0]


def _rms_bwd(name, x, gain, dh, dres=None):
    def fn(x, dh, *rest):
        g = rest[-1]
        _, vjp = jax.vjp(_rms, x, g)
        dx, dg = vjp(dh)
        if dres is not None:
            dx = dx + rest[0]
        return dx, dg
    rows = [x, dh] + ([dres] if dres is not None else [])
    d = x.shape[1]
    return _rowwise(name, fn, rows, [gain], [(d, F32)], [((1, d), F32)])


def _ffn_specs(gcols, grows, ig, iu, idn):
    n = gcols.shape[2]
    D = grows.shape[2]
    wg = pl.BlockSpec((None, D, n), lambda i, j: (j, ig, 0))
    wu = pl.BlockSpec((None, D, n), lambda i, j: (j, iu, 0))
    wd = pl.BlockSpec((None, n, D), lambda i, j: (j, idn, 0))
    return n, wg, wu, wd


def _ffn_fwd(name, x, gain, gcols, grows, ig, iu, idn, tm=1024, comm=None):
    T, D = x.shape
    tm = min(T, tm)
    n, wg_s, wu_s, wd_s = _ffn_specs(gcols, grows, ig, iu, idn)
    last = N_DEV - 1

    def body(x_ref, g_ref, wg_ref, wu_ref, wd_ref, y_ref, h_s, acc):
        j = pl.program_id(1)

        @pl.when(j == 0)
        def _():
            h_s[...] = _rms(x_ref[...], g_ref[...]).astype(BF16)
            acc[...] = jnp.zeros(acc.shape, F32)
        h = h_s[...]
        z = _silu(_dot(h, wg_ref[...], NN)) * _dot(h, wu_ref[...], NN)
        acc[...] += _dot(z.astype(BF16), wd_ref[...], NN)

        @pl.when(j == last)
        def _():
            y_ref[...] = x_ref[...] + 0.5 * acc[...]

    row = pl.BlockSpec((tm, D), lambda i, j: (i, 0))
    (y,), moved = _call(
        body, [x, gain, gcols, gcols, grows], [jax.ShapeDtypeStruct((T, D), F32)], name=name, grid=(T // tm, N_DEV),
        in_specs=[row, pl.BlockSpec((1, D), lambda i, j: (0, 0)), wg_s, wu_s, wd_s], out_specs=[row],
        scratch=[pltpu.VMEM((tm, D), BF16), pltpu.VMEM((tm, D), F32)], comm=comm)
    return y, moved


def _ffn_bwd(tag, x, gain, gcols, grows, ig, iu, idn, dy, tm=512, comm_a=None, comm_w=None):
    T, D = x.shape
    tm = min(T, tm)
    n, wg_s, wu_s, wd_s = _ffn_specs(gcols, grows, ig, iu, idn)
    last = N_DEV - 1

    def body(x_ref, dy_ref, g_ref, wg_ref, wu_ref, wd_ref, dx_ref, dg_ref, h_ref, z_ref, da_ref, db_ref, dh_acc):
        i, j = pl.program_id(0), pl.program_id(1)

        @pl.when(j == 0)
        def _():
            h_ref[...] = _rms(x_ref[...], g_ref[...]).astype(BF16)
            dh_acc[...] = jnp.zeros(dh_acc.shape, F32)

        @pl.when((i == 0) & (j == 0))
        def _():
            dg_ref[...] = jnp.zeros(dg_ref.shape, F32)
        h = h_ref[...]
        a, b = _dot(h, wg_ref[...], NN), _dot(h, wu_ref[...], NN)
        dz = 0.5 * _dot(dy_ref[...].astype(BF16), wd_ref[...], NT)
        s = _sigmoid(a)
        si = a * s
        da = (dz * b * (s + si * (1.0 - s))).astype(BF16)
        db = (dz * si).astype(BF16)
        z_ref[...] = (si * b).astype(BF16)
        da_ref[...] = da
        db_ref[...] = db
        dh_acc[...] += _dot(da, wg_ref[...], NT) + _dot(db, wu_ref[...], NT)

        @pl.when(j == last)
        def _():
            _, vjp = jax.vjp(_rms, x_ref[...], g_ref[...])
            dx, dg = vjp(dh_acc[...])
            dx_ref[...] = dx + dy_ref[...]
            dg_ref[...] += dg

    row = pl.BlockSpec((tm, D), lambda i, j: (i, 0))
    vec = pl.BlockSpec((1, D), lambda i, j: (0, 0))
    hid = pl.BlockSpec((None, tm, n), lambda i, j: (j, i, 0))
    hidden = jax.ShapeDtypeStruct((N_DEV, T, n), BF16)
    (dx, dgain, h, z, da, db), moved_a = _call(
        body, [x, dy, gain, gcols, gcols, grows],
        [jax.ShapeDtypeStruct((T, D), F32), jax.ShapeDtypeStruct((1, D), F32), jax.ShapeDtypeStruct((T, D), BF16),
         hidden, hidden, hidden],
        name=tag + "_a", grid=(T // tm, N_DEV), in_specs=[row, row, vec, wg_s, wu_s, wd_s],
        out_specs=[row, vec, row, hid, hid, hid], scratch=[pltpu.VMEM((tm, D), F32)], comm=comm_a)

    def wbody(h_ref, dy_ref, z_ref, da_ref, db_ref, dwg_ref, dwu_ref, dwd_ref):
        h = h_ref[...]
        dwg_ref[...] = _dot(h, da_ref[...], TN).astype(BF16)
        dwu_ref[...] = _dot(h, db_ref[...], TN).astype(BF16)
        dwd_ref[...] = (0.5 * _dot(z_ref[...], dy_ref[...].astype(BF16), TN)).astype(BF16)

    full = pl.BlockSpec((T, D), lambda j: (0, 0))
    hid_all = pl.BlockSpec((None, T, n), lambda j: (j, 0, 0))
    (dwg, dwu, dwd), moved_w = _call(
        wbody, [h, dy, z, da, db],
        [jax.ShapeDtypeStruct((N_DEV, D, n), BF16)] * 2 + [jax.ShapeDtypeStruct((N_DEV, n, D), BF16)],
        name=tag + "_w", grid=(N_DEV,), in_specs=[full, full, hid_all, hid_all, hid_all],
        out_specs=[pl.BlockSpec((None, D, n), lambda j: (j, 0, 0))] * 2 + [pl.BlockSpec((None, n, D), lambda j: (j, 0, 0))],
        comm=comm_w)
    return dx, dgain, dwg, dwu, dwd, moved_a, moved_w


def _wcols_spec(gw, l, grid_rank):
    _, _, n = gw.shape
    K = D_MODEL
    zero = (lambda i: (0, l, 0)) if grid_rank == 1 else (lambda i, j: (0, l, 0))
    return n, K, pl.BlockSpec((N_DEV, K, n), zero)


def _proj_cols(name, h, gw, l, tm=512, comm=None):
    T = h.shape[0]
    tm = min(T, tm)
    n, K, wspec = _wcols_spec(gw, l, 1)

    def body(h_ref, w_ref, o_ref):
        h = h_ref[...]
        for j in range(N_DEV):
            o_ref[:, pl.ds(j * n, n)] = _dot(h, w_ref[j], NN)

    (proj,), moved = _call(
        body, [h, gw], [jax.ShapeDtypeStruct((T, N_DEV * n), F32)], name=name, grid=(T // tm,),
        in_specs=[pl.BlockSpec((tm, K), lambda i: (i, 0)), wspec], out_specs=[pl.BlockSpec((tm, N_DEV * n), lambda i: (i, 0))],
        comm=comm)
    return proj, moved


def _proj_cols_bwd(tag, h, dproj, gw, l, tm=512, tk=512, comm=None):
    T = h.shape[0]
    tm = min(T, tm)
    n, K, wspec = _wcols_spec(gw, l, 1)

    def dh_body(dp_ref, w_ref, o_ref):
        acc = jnp.zeros(o_ref.shape, F32)
        for j in range(N_DEV):
            acc = acc + _dot(dp_ref[:, pl.ds(j * n, n)], w_ref[j], NT)
        o_ref[...] = acc

    (dh,), moved = _call(
        dh_body, [dproj, gw], [jax.ShapeDtypeStruct((T, K), F32)], name=tag + "_dh", grid=(T // tm,),
        in_specs=[pl.BlockSpec((tm, N_DEV * n), lambda i: (i, 0)), wspec], out_specs=[pl.BlockSpec((tm, K), lambda i: (i, 0))],
        comm=comm)

    def dw_body(h_ref, dp_ref, o_ref):
        h = h_ref[...]
        for j in range(N_DEV):
            o_ref[j] = _dot(h, dp_ref[:, pl.ds(j * n, n)], TN).astype(BF16)

    dw = pl.pallas_call(
        dw_body, grid=(K // tk,), in_specs=[pl.BlockSpec((T, tk), lambda i: (0, i)), pl.BlockSpec((T, N_DEV * n), lambda i: (0, 0))],
        out_specs=pl.BlockSpec((N_DEV, tk, n), lambda i: (0, i, 0)), out_shape=jax.ShapeDtypeStruct((N_DEV, K, n), BF16),
        name=tag + "_dw", compiler_params=_params(1))(h, dproj)
    return dh, dw, moved


def _block_tri(n, reverse=False):
    r = lax.broadcasted_iota(jnp.int32, (n, n), 0)
    c = lax.broadcasted_iota(jnp.int32, (n, n), 1)
    same = (r // HG_BLOCK) == (c // HG_BLOCK)
    return (same & ((c >= r) if reverse else (c <= r))).astype(F32)


def _hgrn_prep(q_ref, f_ref, lbv, qs, ks, cs, T):
    pt = min(T, 256)
    tri = _block_tri(pt)
    for p in range(T // pt):
        rows = pl.ds(p * pt, pt)
        f = lbv + (1.0 - lbv) * _sigmoid(f_ref[rows, :])
        qs[rows, :] = _silu(q_ref[rows, :])
        ks[rows, :] = 1.0 - f
        cs[rows, :] = _dot(tri, jnp.log(f), NN, precision=lax.Precision.HIGHEST)


HG_GROUP = 128


def _groups_loop(nb, fn):
    gp = HG_GROUP if nb % HG_GROUP == 0 else nb

    def step(i, carry):
        base = pl.multiple_of(i * (gp * HG_BLOCK), gp * HG_BLOCK)
        fn(lambda t: pl.ds(base + t, gp, stride=HG_BLOCK))
        return carry

    lax.fori_loop(0, nb // gp, step, 0)


def _gate_out(o, og, g):
    return _rms(o, og) * _silu(g)


HG_UNROLL = 4


def _block_rows(n):
    return pl.ds(pl.multiple_of(n * HG_BLOCK, HG_BLOCK), HG_BLOCK)


def _blocks_loop(nb, fn):
    u = HG_UNROLL if nb % HG_UNROLL == 0 else 1

    def step(i, carry):
        for k in range(u):
            fn(i * u + k)
        return carry

    lax.fori_loop(0, nb // u, step, 0)


def _scan_states(buf, cs, nb, reverse=False):
    def step(m, st):
        n = nb - 1 - m if reverse else m
        own = buf[n]
        buf[n] = st
        rows = _block_rows(n)
        return jnp.exp(cs[rows, :][HG_BLOCK - 1:HG_BLOCK, :]) * st + own

    lax.fori_loop(0, nb, step, jnp.zeros(buf.shape[1:], F32))


def _hgrn_states(i_ref, ks, cs, states, nb):
    def own_step(n):
        rows = _block_rows(n)
        c = cs[rows, :]
        states[n] = _fdot(i_ref[rows, :], ks[rows, :] * jnp.exp(c[HG_BLOCK - 1:HG_BLOCK, :] - c), TN)

    _blocks_loop(nb, own_step)
    _scan_states(states, cs, nb)


def _hgrn_fwd(name, proj, lb, og, comm=None):
    T = proj.shape[0]
    nb = T // HG_BLOCK
    hd = HG_HEAD_DIM

    def body(q_ref, f_ref, i_ref, g_ref, lb_ref, og_ref, main_ref, o_ref, qs, ks, cs, states):
        _hgrn_prep(q_ref, f_ref, lb_ref[...], qs, ks, cs, T)
        def pairs(at):
            for t in range(HG_BLOCK):
                qt, ct = qs[at(t), :], cs[at(t), :]
                acc = jnp.zeros(qt.shape, F32)
                for s in range(t + 1):
                    w = qt * ks[at(s), :] * jnp.exp(ct - cs[at(s), :])
                    acc = acc + jnp.sum(w, axis=-1, keepdims=True) * i_ref[at(s), :]
                o_ref[at(t), :] = acc

        _groups_loop(nb, pairs)

        _hgrn_states(i_ref, ks, cs, states, nb)

        def out_step(n):
            rows = _block_rows(n)
            o_ref[rows, :] += _fdot(qs[rows, :] * jnp.exp(cs[rows, :]), states[n], NT)

        _blocks_loop(nb, out_step)
        pt = min(T, 256)
        for p in range(T // pt):
            rows = pl.ds(p * pt, pt)
            main_ref[rows, :] = _gate_out(o_ref[rows, :], og_ref[...], g_ref[rows, :])

    nh = HG_HEADS
    col = lambda off: pl.BlockSpec((T, hd), lambda h, off=off: (0, off + h))
    (main, o), moved = _call(
        body, [proj, proj, proj, proj, lb, og], [jax.ShapeDtypeStruct((T, MAIN_WIDTH), F32)] * 2, name=name, grid=(nh,),
        in_specs=[col(0), col(nh), col(2 * nh), col(3 * nh), pl.BlockSpec((1, hd), lambda h: (0, h)),
                  pl.BlockSpec((1, hd), lambda h: (0, 0))],
        out_specs=[col(0), col(0)], scratch=[pltpu.VMEM((T, hd), F32)] * 3 + [pltpu.VMEM((nb, hd, hd), F32)], comm=comm)
    return main, o, moved


def _hgrn_bwd(name, proj, o, dmix, lb, og, comm=None):
    T = proj.shape[0]
    nb = T // HG_BLOCK
    hd = HG_HEAD_DIM
    pt = min(T, 256)

    def body(q_ref, f_ref, i_ref, g_ref, o_ref, dm_ref, lb_ref, og_ref,
             dq_ref, df_ref, di_ref, dg_ref, dlb_ref, dog_ref, qs, ks, cs, dos, dqs, dks, dvs, states, behind):
        lbv = lb_ref[...]
        _hgrn_prep(q_ref, f_ref, lbv, qs, ks, cs, T)
        dog = jnp.zeros((1, hd), F32)
        for p in range(T // pt):
            rows = pl.ds(p * pt, pt)
            _, vjp = jax.vjp(_gate_out, o_ref[rows, :], og_ref[...], g_ref[rows, :])
            do, dog_p, dg = vjp(dm_ref[rows, :])
            dos[rows, :] = do
            dg_ref[rows, :] = dg.astype(dg_ref.dtype)
            dog = dog + dog_p

        @pl.when(pl.program_id(0) == 0)
        def _():
            dog_ref[...] = jnp.zeros(dog_ref.shape, F32)
        dog_ref[...] += dog

        def pairs(at):
            for t in range(HG_BLOCK):
                dqs[at(t), :] = jnp.zeros((HG_GROUP if nb % HG_GROUP == 0 else nb, hd), F32)
            for s in range(HG_BLOCK):
                k_s, c_s, v_s = ks[at(s), :], cs[at(s), :], i_ref[at(s), :]
                dk = jnp.zeros(k_s.shape, F32)
                dv = jnp.zeros(k_s.shape, F32)
                for t in range(s, HG_BLOCK):
                    q_t, do_t = qs[at(t), :], dos[at(t), :]
                    e = jnp.exp(cs[at(t), :] - c_s)
                    a = jnp.sum(q_t * k_s * e, axis=-1, keepdims=True)
                    g = jnp.sum(do_t * v_s, axis=-1, keepdims=True)
                    dqs[at(t), :] += g * k_s * e
                    dk = dk + g * q_t * e
                    dv = dv + a * do_t
                dks[at(s), :] = dk
                dvs[at(s), :] = dv

        _groups_loop(nb, pairs)

        _hgrn_states(i_ref, ks, cs, states, nb)

        def own_step(n):
            rows = _block_rows(n)
            behind[n] = _fdot(dos[rows, :], qs[rows, :] * jnp.exp(cs[rows, :]), TN)

        _blocks_loop(nb, own_step)
        _scan_states(behind, cs, nb, reverse=True)

        def grad_step(n):
            rows = _block_rows(n)
            c = cs[rows, :]
            ec, ek = jnp.exp(c), jnp.exp(c[HG_BLOCK - 1:HG_BLOCK, :] - c)
            dst = behind[n]
            dqs[rows, :] += _fdot(dos[rows, :], states[n], NN) * ec
            dks[rows, :] += _fdot(i_ref[rows, :], dst, NN) * ek
            dvs[rows, :] += _fdot(ks[rows, :] * ek, dst, NT)

        _blocks_loop(nb, grad_step)

        full = (lax.broadcasted_iota(jnp.int32, (pt, pt), 1) >= lax.broadcasted_iota(jnp.int32, (pt, pt), 0)).astype(F32)
        carry = jnp.zeros((1, hd), F32)
        dlb = jnp.zeros((1, hd), F32)
        for p in reversed(range(T // pt)):
            rows = pl.ds(p * pt, pt)
            q, k, dq, dk = qs[rows, :], ks[rows, :], dqs[rows, :], dks[rows, :]
            db = q * dq - k * dk
            dlf = _dot(full, db, NN, precision=lax.Precision.HIGHEST) + carry
            carry = carry + jnp.sum(db, axis=0, keepdims=True)
            sg = _sigmoid(f_ref[rows, :])
            df = dlf / (1.0 - k) - dk
            df_ref[rows, :] = (df * (1.0 - lbv) * sg * (1.0 - sg)).astype(df_ref.dtype)
            dlb = dlb + jnp.sum(df * (1.0 - sg), axis=0, keepdims=True)
            qr = q_ref[rows, :]
            sq = _sigmoid(qr)
            dq_ref[rows, :] = (dq * (sq + qr * sq * (1.0 - sq))).astype(dq_ref.dtype)
            di_ref[rows, :] = dvs[rows, :].astype(di_ref.dtype)
        dlb_ref[...] = dlb

    nh = HG_HEADS
    col = lambda off: pl.BlockSpec((T, hd), lambda h, off=off: (0, off + h))
    vec = pl.BlockSpec((1, hd), lambda h: (0, h))
    one = pl.BlockSpec((1, hd), lambda h: (0, 0))
    outs, moved = _call(
        body, [proj, proj, proj, proj, o, dmix, lb, og],
        [jax.ShapeDtypeStruct((T, MAIN_WIDTH), BF16)] * 4
        + [jax.ShapeDtypeStruct((1, MAIN_WIDTH), F32), jax.ShapeDtypeStruct((1, hd), F32)],
        name=name, grid=(nh,), in_specs=[col(0), col(nh), col(2 * nh), col(3 * nh), col(0), col(0), vec, one],
        out_specs=[col(0), col(0), col(0), col(0), vec, one],
        scratch=[pltpu.VMEM((T, hd), F32)] * 7 + [pltpu.VMEM((nb, hd, hd), F32)] * 2, comm=comm)
    return (*outs, moved)


def _softmax_rows(s):
    p = jnp.exp(s - jnp.max(s, axis=-1, keepdims=True))
    return p, jnp.sum(p, axis=-1, keepdims=True)


def _fox_probs(q, k, cr_ref, hh, qi, tq):
    q0 = qi * tq
    pieces = ([(0, q0)] if qi else []) + [(q0, q0 + tq)]
    ss = []
    for a, b in pieces:
        s = _dot(q, k[a:b], NT) - cr_ref[hh, :, pl.ds(a, b - a)]
        if a == q0:
            causal = lax.broadcasted_iota(jnp.int32, s.shape, 1) <= lax.broadcasted_iota(jnp.int32, s.shape, 0)
            s = jnp.where(causal, s, -jnp.inf)
        ss.append(s)
    m = functools.reduce(jnp.maximum, [jnp.max(s, axis=-1, keepdims=True) for s in ss])
    ps = [jnp.exp(s - m) for s in ss]
    l = functools.reduce(jnp.add, [jnp.sum(p, axis=-1, keepdims=True) for p in ps])
    return [(a, b, p) for (a, b), p in zip(pieces, ps)], l


def _fox_specs(T):
    w = 2 * FOX_HEAD_DIM
    n = MAIN_WIDTH // w
    col = lambda off: pl.BlockSpec((T, w), lambda p, off=off: (0, off + p))
    cr = pl.BlockSpec((2, 1, T), lambda p: (p, 0, 0))
    gain = pl.BlockSpec((1, FOX_HEAD_DIM), lambda p: (0, 0))
    return n, col, cr, gain


def _fox_fwd(name, proj, kvf, cr, gq, gk, comm=None):
    T = proj.shape[0]
    tq = min(T, 256)
    hd = FOX_HEAD_DIM
    scale = hd ** -0.5

    def body(q_ref, g_ref, k_ref, v_ref, cr_ref, gq_ref, gk_ref, main_ref, o_ref):
        for hh in range(2):
            lanes = pl.ds(hh * hd, hd)
            k = _rms(k_ref[:, lanes], gk_ref[...]).astype(BF16)
            v = v_ref[:, lanes].astype(BF16)
            for qi in range(T // tq):
                rows = pl.ds(qi * tq, tq)
                q = (_rms(q_ref[rows, lanes], gq_ref[...]) * scale).astype(BF16)
                ps, l = _fox_probs(q, k, cr_ref, hh, qi, tq)
                o = functools.reduce(jnp.add, [_dot(p.astype(BF16), v[a:b], NN) for a, b, p in ps]) / l
                o_ref[rows, lanes] = o
                main_ref[rows, lanes] = o * _sigmoid(g_ref[rows, lanes])

    n, col, crs, gain = _fox_specs(T)
    (main, o), moved = _call(
        body, [proj, proj, kvf, kvf, cr, gq, gk], [jax.ShapeDtypeStruct((T, MAIN_WIDTH), F32)] * 2, name=name, grid=(n,),
        in_specs=[col(0), col(n), col(0), col(n), crs, gain, gain], out_specs=[col(0), col(0)], comm=comm)
    return main, o, moved


def _fox_bwd(name, proj, kvf, cr, gq, gk, o, dmix, pdk, pdv, pdc, comm=None):
    T = proj.shape[0]
    tq = min(T, 256)
    hd = FOX_HEAD_DIM
    scale = hd ** -0.5

    def body(q_ref, g_ref, k_ref, v_ref, cr_ref, gq_ref, gk_ref, o_ref, dm_ref, pdk_ref, pdv_ref, pdc_ref,
             dq_ref, dg_ref, dk_ref, dv_ref, dc_ref, dgq_ref, dgk_ref, dka, dva, dca):
        dgq = jnp.zeros((1, hd), F32)
        dgk = jnp.zeros((1, hd), F32)
        for hh in range(2):
            lanes = pl.ds(hh * hd, hd)
            k32, vjp_k = jax.vjp(_rms, k_ref[:, lanes], gk_ref[...])
            k = k32.astype(BF16)
            v = v_ref[:, lanes].astype(BF16)
            dka[...] = jnp.zeros(dka.shape, F32)
            dva[...] = jnp.zeros(dva.shape, F32)
            dca[...] = jnp.zeros(dca.shape, F32)
            for qi in range(T // tq):
                rows = pl.ds(qi * tq, tq)
                q32, vjp_q = jax.vjp(_rms, q_ref[rows, lanes], gq_ref[...])
                q = (q32 * scale).astype(BF16)
                ps, l = _fox_probs(q, k, cr_ref, hh, qi, tq)
                ps = [(a, b, p / l) for a, b, p in ps]
                sg = _sigmoid(g_ref[rows, lanes])
                dm = dm_ref[rows, lanes]
                do = (dm * sg).astype(BF16)
                dg_ref[rows, lanes] = (dm * o_ref[rows, lanes] * sg * (1.0 - sg)).astype(dg_ref.dtype)
                dps = [_dot(do, v[a:b], NT) for a, b, _ in ps]
                delta = functools.reduce(jnp.add, [jnp.sum(p * dp, axis=-1, keepdims=True) for (_, _, p), dp in zip(ps, dps)])
                dq = jnp.zeros((tq, hd), F32)
                for (a, b, p), dp in zip(ps, dps):
                    ds = p * (dp - delta)
                    dsb = ds.astype(BF16)
                    dq = dq + _dot(dsb, k[a:b], NN)
                    dka[:, pl.ds(a, b - a)] += _dot(q, dsb, TN)
                    dva[:, pl.ds(a, b - a)] += _dot(do, p.astype(BF16), TN)
                    dca[:, pl.ds(a, b - a)] -= jnp.sum(ds, axis=0, keepdims=True)
                dqr, dgq_p = vjp_q(dq * scale)
                dq_ref[rows, lanes] = dqr.astype(dq_ref.dtype)
                dgq = dgq + dgq_p
            dkr, dgk_p = vjp_k(dka[...].T)
            dgk = dgk + dgk_p
            dk_ref[:, lanes] = dkr + pdk_ref[:, lanes]
            dv_ref[:, lanes] = dva[...].T + pdv_ref[:, lanes]
            dc_ref[hh] = dca[...] + pdc_ref[hh]

        @pl.when(pl.program_id(0) == 0)
        def _():
            dgq_ref[...] = jnp.zeros(dgq_ref.shape, F32)
            dgk_ref[...] = jnp.zeros(dgk_ref.shape, F32)
        dgq_ref[...] += dgq
        dgk_ref[...] += dgk

    n, col, crs, gain = _fox_specs(T)
    wide = jax.ShapeDtypeStruct((T, MAIN_WIDTH), F32)
    half = jax.ShapeDtypeStruct((T, MAIN_WIDTH), BF16)
    outs, moved = _call(
        body, [proj, proj, kvf, kvf, cr, gq, gk, o, dmix, pdk, pdv, pdc],
        [half, half, wide, wide, jax.ShapeDtypeStruct((FOX_HEADS, 1, T), F32),
         jax.ShapeDtypeStruct((1, hd), F32), jax.ShapeDtypeStruct((1, hd), F32)],
        name=name, grid=(n,),
        in_specs=[col(0), col(n), col(0), col(n), crs, gain, gain, col(0), col(0), col(0), col(0), crs],
        out_specs=[col(0), col(0), col(0), col(0), crs, gain, gain],
        scratch=[pltpu.VMEM((hd, T), F32), pltpu.VMEM((hd, T), F32), pltpu.VMEM((1, T), F32)], comm=comm)
    return (*outs, moved)


def _mem_specs(T, width):
    tq = min(T, 512)
    q = pl.BlockSpec((tq, MEM_WIDTH), lambda i, c=(width - MEM_WIDTH) // MEM_WIDTH: (i, c))
    gain = pl.BlockSpec((1, MEM_HEAD_DIM), lambda i: (0, 0))
    return tq, q, gain


def _mem_fwd(name, proj, kv, gq, gk):
    T, W = proj.shape
    hd = MEM_HEAD_DIM
    tq, qspec, gain = _mem_specs(T, W)

    def body(q_ref, kv_ref, gq_ref, gk_ref, o_ref):
        for h in range(MEM_HEADS):
            lanes = pl.ds(h * hd, hd)
            q = _rms(q_ref[:, lanes], gq_ref[...]).astype(BF16)
            k = _rms(kv_ref[:, lanes], gk_ref[...]).astype(BF16)
            v = kv_ref[:, pl.ds(MEM_WIDTH + h * hd, hd)].astype(BF16)
            p, l = _softmax_rows(_dot(q, k, NT) * (hd ** -0.5))
            o_ref[:, lanes] = _dot(p.astype(BF16), v, NN) / l

    return pl.pallas_call(
        body, grid=(T // tq,),
        in_specs=[qspec, pl.BlockSpec(kv.shape, lambda i: (0, 0)), gain, gain],
        out_specs=pl.BlockSpec((tq, MEM_WIDTH), lambda i: (i, 0)),
        out_shape=jax.ShapeDtypeStruct((T, MEM_WIDTH), F32),
        name=name, compiler_params=_params(1))(proj, kv, gq, gk)


def _mem_bwd(name, proj, kv, gq, gk, dmix):
    T, W = proj.shape
    hd = MEM_HEAD_DIM
    scale = hd ** -0.5
    tq, qspec, gain = _mem_specs(T, W)

    def body(q_ref, kv_ref, gq_ref, gk_ref, dm_ref, dq_ref, dkv_ref, dgq_ref, dgk_ref):
        @pl.when(pl.program_id(0) == 0)
        def _():
            dkv_ref[...] = jnp.zeros(dkv_ref.shape, F32)
            dgq_ref[...] = jnp.zeros(dgq_ref.shape, F32)
            dgk_ref[...] = jnp.zeros(dgk_ref.shape, F32)
        for h in range(MEM_HEADS):
            lanes = pl.ds(h * hd, hd)
            vl = pl.ds(MEM_WIDTH + h * hd, hd)
            q32, vjp_q = jax.vjp(_rms, q_ref[:, lanes], gq_ref[...])
            k32, vjp_k = jax.vjp(_rms, kv_ref[:, lanes], gk_ref[...])
            q, k, v = q32.astype(BF16), k32.astype(BF16), kv_ref[:, vl].astype(BF16)
            p, l = _softmax_rows(_dot(q, k, NT) * scale)
            p = p / l
            do = dm_ref[:, lanes].astype(BF16)
            dp = _dot(do, v, NT)
            dsb = (p * (dp - jnp.sum(p * dp, axis=-1, keepdims=True))).astype(BF16)
            dqr, dgq_p = vjp_q(_dot(dsb, k, NN) * scale)
            dkr, dgk_p = vjp_k(_dot(dsb, q, TN) * scale)
            dq_ref[:, lanes] = dqr.astype(dq_ref.dtype)
            dkv_ref[:, lanes] += dkr
            dkv_ref[:, vl] += _dot(p.astype(BF16), do, TN)
            dgq_ref[...] += dgq_p
            dgk_ref[...] += dgk_p

    return pl.pallas_call(
        body, grid=(T // tq,),
        in_specs=[qspec, pl.BlockSpec(kv.shape, lambda i: (0, 0)), gain, gain,
                  pl.BlockSpec((tq, MEM_WIDTH), lambda i: (i, MAIN_WIDTH // MEM_WIDTH))],
        out_specs=[pl.BlockSpec((tq, MEM_WIDTH), lambda i: (i, 0)), pl.BlockSpec(kv.shape, lambda i: (0, 0)), gain, gain],
        out_shape=[jax.ShapeDtypeStruct((T, MEM_WIDTH), BF16), jax.ShapeDtypeStruct(kv.shape, F32),
                   jax.ShapeDtypeStruct((1, hd), F32), jax.ShapeDtypeStruct((1, hd), F32)],
        name=name, compiler_params=_params(1))(proj, kv, gq, gk, dmix)


def _cumsum_rows(name, x, reverse=False):
    T, C = x.shape
    pt = min(T, 256)

    def body(x_ref, o_ref):
        r = lax.broadcasted_iota(jnp.int32, (pt, pt), 0)
        c = lax.broadcasted_iota(jnp.int32, (pt, pt), 1)
        tri = ((c >= r) if reverse else (c <= r)).astype(F32)
        carry = jnp.zeros((1, C), F32)
        order = range(T // pt)
        for p in (reversed(order) if reverse else order):
            rows = pl.ds(p * pt, pt)
            blk = x_ref[rows, :]
            o_ref[rows, :] = _dot(tri, blk, NN, precision=lax.Precision.HIGHEST) + carry
            carry = carry + jnp.sum(blk, axis=0, keepdims=True)

    return pl.pallas_call(body, out_shape=jax.ShapeDtypeStruct((T, C), F32), name=name,
                          compiler_params=pltpu.CompilerParams(vmem_limit_bytes=V7X_VMEM_LIMIT))(x)


MESH = pl.DeviceIdType.MESH
ANY = pl.BlockSpec(memory_space=pl.ANY)


def _mesh_pos():
    return lax.axis_index("x"), lax.axis_index("y"), lax.axis_index("c")


def _all_gather(name, xs):
    n = len(xs)

    def body(*refs):
        x_refs, out_refs = refs[:n], refs[n:2 * n]
        send_sems, recv_sems, local_sems = refs[2 * n:]
        mx, my, mc = _mesh_pos()
        me, sibling = (mx, my, mc), (mx, my, 1 - mc)
        chips = [(1 - mx, my), (mx, 1 - my), (1 - mx, 1 - my)]

        def slot(a, px, py, pc):
            return out_refs[a].at[4 * px + 2 * py + pc]

        def copy(a, k, block, to, src=None):
            return pltpu.make_async_remote_copy(
                src_ref=slot(a, *block) if src is None else src, dst_ref=slot(a, *block),
                send_sem=send_sems.at[7 * a + k], recv_sem=recv_sems.at[7 * a + k], device_id=to, device_id_type=MESH)

        mine = [pltpu.make_async_copy(x_refs[a], slot(a, *me), local_sems.at[a]) for a in range(n)]
        first = []
        for a in range(n):
            mine[a].start()
            first.append(copy(a, 0, me, sibling, src=x_refs[a]))
            first += [copy(a, 1 + j, me, (*chip, mc), src=x_refs[a]) for j, chip in enumerate(chips)]
        for cp in first:
            cp.start()
        passed = []
        for j, chip in enumerate(chips):
            for a in range(n):
                copy(a, 1 + j, (*chip, mc), me).wait_recv()
                passed.append(copy(a, 4 + j, (*chip, mc), sibling))
                passed[-1].start()
        for a in range(n):
            copy(a, 0, sibling, me).wait_recv()
            for j, chip in enumerate(chips):
                copy(a, 4 + j, (*chip, 1 - mc), me).wait_recv()
        for cp in first + passed:
            cp.wait_send()
        for cp in mine:
            cp.wait()

    return pl.pallas_call(
        body, out_shape=[jax.ShapeDtypeStruct((N_DEV,) + x.shape, x.dtype) for x in xs], in_specs=[ANY] * n, out_specs=[ANY] * n,
        scratch_shapes=[pltpu.SemaphoreType.DMA((7 * n,)), pltpu.SemaphoreType.DMA((7 * n,)), pltpu.SemaphoreType.DMA((n,))],
        name=name)(*xs)


def _exchange_cores(name, gs):
    n = len(gs)

    def body(*refs):
        g_refs, recv_refs = refs[:n], refs[n:2 * n]
        send_sems, recv_sems = refs[2 * n:]
        mx, my, mc = _mesh_pos()
        swap = [pltpu.make_async_remote_copy(
            src_ref=g_refs[a].at[2 * q + (1 - mc)], dst_ref=recv_refs[a].at[q], send_sem=send_sems.at[4 * a + q],
            recv_sem=recv_sems.at[4 * a + q], device_id=(mx, my, 1 - mc), device_id_type=MESH) for a in range(n) for q in range(4)]
        for cp in swap:
            cp.start()
        for cp in swap:
            cp.wait()

    return pl.pallas_call(
        body, out_shape=[jax.ShapeDtypeStruct((4,) + g.shape[1:], g.dtype) for g in gs], in_specs=[ANY] * n, out_specs=[ANY] * n,
        scratch_shapes=[pltpu.SemaphoreType.DMA((4 * n,)), pltpu.SemaphoreType.DMA((4 * n,))],
        name=name)(*gs)


def _exchange_chips(name, ss):
    n = len(ss)

    def body(*refs):
        s_refs, recv_refs = refs[:n], refs[n:2 * n]
        send_sems, recv_sems, local_sems = refs[2 * n:]
        mx, my, mc = _mesh_pos()
        myq = 2 * mx + my
        chips = [(1 - mx, my), (mx, 1 - my), (1 - mx, 1 - my)]
        mine = [pltpu.make_async_copy(s_refs[a].at[myq], recv_refs[a].at[myq], local_sems.at[a]) for a in range(n)]
        for cp in mine:
            cp.start()
        swap = [pltpu.make_async_remote_copy(
            src_ref=s_refs[a].at[2 * px + py], dst_ref=recv_refs[a].at[myq], send_sem=send_sems.at[3 * a + k],
            recv_sem=recv_sems.at[3 * a + k], device_id=(px, py, mc), device_id_type=MESH)
            for a in range(n) for k, (px, py) in enumerate(chips)]
        for cp in swap:
            cp.start()
        for a in range(n):
            for k, (px, py) in enumerate(chips):
                pltpu.make_async_remote_copy(
                    src_ref=s_refs[a].at[myq], dst_ref=recv_refs[a].at[2 * px + py], send_sem=send_sems.at[3 * a + k],
                    recv_sem=recv_sems.at[3 * a + k], device_id=(px, py, mc), device_id_type=MESH).wait_recv()
        for cp in swap:
            cp.wait_send()
        for cp in mine:
            cp.wait()

    return pl.pallas_call(
        body, out_shape=[jax.ShapeDtypeStruct(s.shape, s.dtype) for s in ss], in_specs=[ANY] * n, out_specs=[ANY] * n,
        scratch_shapes=[pltpu.SemaphoreType.DMA((3 * n,)), pltpu.SemaphoreType.DMA((3 * n,)), pltpu.SemaphoreType.DMA((n,))],
        name=name)(*ss)


def _pair_sum(name, g, recv, mc):
    _, R, C = g.shape
    tm = _row_tile(R, 512)

    def body(mc_ref, own_ref, recv_ref, o_ref):
        o_ref[...] = (own_ref[...].astype(F32) + recv_ref[...].astype(F32)).astype(o_ref.dtype)

    spec = pl.BlockSpec((None, tm, C), lambda q, i, mc_ref: (q, i, 0))
    grid_spec = pltpu.PrefetchScalarGridSpec(
        num_scalar_prefetch=1, grid=(4, R // tm),
        in_specs=[pl.BlockSpec((None, tm, C), lambda q, i, mc_ref: (2 * q + mc_ref[0], i, 0)), spec], out_specs=spec)
    return pl.pallas_call(body, grid_spec=grid_spec, out_shape=jax.ShapeDtypeStruct((4, R, C), BF16), name=name,
                          compiler_params=_params(2))(mc, g, recv)


def _row_tile(R, cap):
    best = None
    for t in range(8, min(R, cap) + 1, 8):
        if R % t == 0:
            best = t
    return best or R


def _sum_slabs(name, a, out_dtype):
    n, R, C = a.shape
    tm = _row_tile(R, 512)

    def body(*refs):
        acc = refs[0][...].astype(F32)
        for r in refs[1:n]:
            acc = acc + r[...].astype(F32)
        refs[n][...] = acc.astype(out_dtype)

    return pl.pallas_call(
        body, grid=(R // tm,),
        in_specs=[pl.BlockSpec((None, tm, C), lambda i, q=q: (q, i, 0)) for q in range(n)],
        out_specs=pl.BlockSpec((tm, C), lambda i: (i, 0)), out_shape=jax.ShapeDtypeStruct((R, C), out_dtype),
        name=name, compiler_params=_params(1))(*([a] * n))


def _reduce_scatter(gs):
    mc = lax.axis_index("c").astype(jnp.int32).reshape(1)
    recvs = _exchange_cores("rs_cores", gs)
    pairs = [_pair_sum(f"rs_pair_sum{a}", g, r, mc) for a, (g, r) in enumerate(zip(gs, recvs))]
    return [_sum_slabs(f"rs_chip_sum{a}", r, F32) for a, r in enumerate(_exchange_chips("rs_chips", pairs))]


SMALL = ["ffn1_norm", "mix_norm", "mem_norm", "mem_q_gain", "mem_k_gain", "hgrn_o_gain", "fox_q_gain", "kv_norm",
         "fox_f_bias", "fox_k_gain", "ffn2_norm"]
COLS352 = ["ffn1_w_gate", "ffn1_w_up", "ffn2_w_gate", "ffn2_w_up"]
KV_SPLIT = 1024
KV_WIDTH = 2 * MAIN_WIDTH + FOX_HEADS


def _rows2d(w):
    return w.reshape(-1, w.shape[-1])


def _pad_cols(w, width):
    return jnp.pad(w, [(0, 0)] * (w.ndim - 1) + [(0, width - w.shape[-1])])


def _pack_rows1024(down1, down2, w_out, w_mem_kv, w_kv):
    kv = jnp.concatenate([w_kv[:, :KV_SPLIT], _pad_cols(w_kv[:, KV_SPLIT:], D_MODEL)], axis=0)
    return jnp.concatenate([_rows2d(down1), _rows2d(down2), _rows2d(w_out), _rows2d(_pad_cols(w_mem_kv, D_MODEL)), kv], axis=0)


def _unpack_rows1024(buf, shapes):
    out, off = [], 0
    for name in ("ffn1_w_down", "ffn2_w_down", "w_out", "w_mem_kv"):
        L, r, c = shapes[name]
        out.append(buf[off:off + L * r].reshape(L, r, D_MODEL)[:, :, :c])
        off += L * r
    r, c = shapes["w_kv"]
    out.append(jnp.concatenate([buf[off:off + r], buf[off + r:off + 2 * r, :c - KV_SPLIT]], axis=1))
    return out


def _pad128(a):
    flat = a.reshape(-1)
    return jnp.pad(flat, (0, -flat.shape[0] % LANES))


def _small_pack(parts):
    flat = jnp.concatenate([_pad128(p) for p in parts])
    rows = -(-flat.shape[0] // LANES)
    flat = jnp.pad(flat, (0, (-rows % 8) * LANES))
    return flat.reshape(-1, LANES)


def _small_unpack(buf, shapes):
    flat = buf.reshape(-1)
    out, off = [], 0
    for s in shapes:
        n = 1
        for d in s:
            n *= d
        out.append(flat[off:off + n].reshape(s))
        off += n + (-n % LANES)
    return out


def _lb_fn(l0, l1):
    m = lax.stop_gradient(jnp.maximum(l0, l1))
    e0, e1 = jnp.exp(l0 - m), jnp.exp(l1 - m)
    p0, p1 = e0 / (e0 + e1), e1 / (e0 + e1)
    return p0 - p0, (p0 + p1) - p0


def _lb_fwd(logits):
    return _rowwise("lb", _lb_fn, [logits[0:1], logits[1:2]], [], [(MAIN_WIDTH, F32)] * 2)


def _lb_bwd(logits, dlb0, dlb1):
    def fn(l0, l1, d0, d1):
        _, vjp = jax.vjp(_lb_fn, l0, l1)
        return vjp((d0, d1))
    return _rowwise("lb_bwd", fn, [logits[0:1], logits[1:2], dlb0, dlb1], [], [(MAIN_WIDTH, F32)] * 2)


def _adamw_fn(w, g, m, v):
    m = ADAM_B1 * m + (1.0 - ADAM_B1) * g
    v = ADAM_B2 * v + (1.0 - ADAM_B2) * jnp.square(g)
    m_hat = m / (1.0 - ADAM_B1 ** ADAM_STEP)
    v_hat = v / (1.0 - ADAM_B2 ** ADAM_STEP)
    return -ADAM_LR * (m_hat / (jnp.sqrt(v_hat) + ADAM_EPS) + ADAM_WD * w), m, v


def _sum_adamw(name, landed, w, m, v):
    L, r, c = w.shape
    tm = _row_tile(r, 128)
    n_i = r // tm

    def body(*refs):
        land, (w_ref, m_ref, v_ref), outs = refs[:L], refs[L:L + 3], refs[L + 3:]
        for k in range(L):
            @pl.when(pl.program_id(0) == k)
            def _(k=k):
                g = land[k][0].astype(F32)
                for s in range(1, N_DEV):
                    g = g + land[k][s].astype(F32)
                for ref, val in zip(outs, (g,) + _adamw_fn(w_ref[...], g, m_ref[...], v_ref[...])):
                    ref[...] = val

    held = lambda k: (lambda l, i: (0, jnp.where(l < k, 0, jnp.where(l == k, i, n_i - 1)), 0))
    cur = pl.BlockSpec((None, tm, c), lambda l, i: (l, i, 0))
    return pl.pallas_call(
        body, grid=(L, n_i), in_specs=[pl.BlockSpec((N_DEV, tm, c), held(k)) for k in range(L)] + [cur] * 3,
        out_specs=[cur] * 4, out_shape=[jax.ShapeDtypeStruct((L, r, c), F32)] * 4, name=name,
        compiler_params=_params(2))(*landed, w, m, v)


def _adamw(name, w, g, m, v):
    shape = w.shape
    C = shape[-1]
    two = lambda a: a.reshape(-1, C)
    R = two(w).shape[0]
    outs = _rowwise(name, _adamw_fn, [two(w), two(g), two(m), two(v)], [], [(C, F32)] * 3, tm=_row_tile(R, 512))
    return [o.reshape(shape) for o in outs]


def _whole_rows(g, r0, r1):
    return g[:, r0:r1].reshape(N_DEV * (r1 - r0), g.shape[2])


def _w_out_of(Wl):
    n = Wl["d2"].shape[1]
    return _whole_rows(Wl["r1"], n, n + LANES)


def _w_mem_kv_of(Wl):
    n = Wl["d2"].shape[1]
    return _whole_rows(Wl["r1"], n + LANES, n + 2 * LANES)[:, :2 * MEM_WIDTH]


def _mixer_fwd(l, x1, mem, G, W, lbs, shared, local, units):
    T = x1.shape[0]
    tag = f"l{l}"
    Wl = G[l]
    h = _rms_fwd(tag + "_mixrms", x1, W["mix_norm"][l:l + 1])
    mem_n = _rms_fwd(tag + "_memrms", mem, W["mem_norm"][l:l + 1])
    kv = _mm(tag + "_memkv", [(mem_n, _w_mem_kv_of(Wl), NN)], [F32], mem.shape[0], 2 * MEM_WIDTH)
    proj, moved = _proj_cols(tag + "_in", h, Wl["win"], 0, comm=_Comm(relay=[G[l + 1]["gu1"]]) if l + 1 < len(G) else None)
    if moved:
        G[l + 1]["gu1"] = moved[0]
    along = _Comm(gather=[local[n][k] for n, k in units])
    if l < 2:
        main, o, moved = _hgrn_fwd(tag + "_hgrn", proj, lbs[l], W["hgrn_o_gain"][l:l + 1], comm=along)
    else:
        main, o, moved = _fox_fwd(tag + "_fox", proj, shared["kvf"], shared["cr"], W["fox_q_gain"][l - 2:l - 1],
                                  W["fox_k_gain"], comm=along)
    for (n, k), m in zip(units, moved):
        G[n][k] = m
    mem_o = _mem_fwd(tag + "_mem", proj, kv, W["mem_q_gain"][l:l + 1], W["mem_k_gain"][l:l + 1])
    w_out = _w_out_of(Wl)
    x2 = _mm(tag + "_out", [(main, w_out[:MAIN_WIDTH], NN), (mem_o, w_out[MAIN_WIDTH:], NN)], [F32], T, D_MODEL,
             epi=lambda a, e: (e[0] + a[0] + a[1],), extras=[x1])
    return x2, dict(h=h, mem_n=mem_n, kv=kv, proj=proj, main=main, o=o, mem_o=mem_o)


def _mixer_bwd(l, x1, mem, Wl, W, lbs, shared, sv, dx2, acc, ready, landed):
    T = x1.shape[0]
    tag = f"l{l}b"
    w_out = _w_out_of(Wl)
    g = {}
    dmix = _mm(tag + "_dmix", [(dx2, w_out, NT)], [F32], T, D_MODEL)
    dw_out = jnp.concatenate([
        _mm(tag + "_dwout_a", [(sv["main"], dx2, TN)], [BF16], MAIN_WIDTH, D_MODEL),
        _mm(tag + "_dwout_b", [(sv["mem_o"], dx2, TN)], [BF16], MEM_WIDTH, D_MODEL)], axis=0).reshape(N_DEV, -1, D_MODEL)
    dqm, dkv, g["mem_q_gain"], g["mem_k_gain"] = _mem_bwd(tag + "_mem", sv["proj"], sv["kv"], W["mem_q_gain"][l:l + 1],
                                                           W["mem_k_gain"][l:l + 1], dmix)
    along = _Comm(scatter=[v for _, v in ready])
    if l < 2:
        dq, df, di, dg, g["lb"], g["hgrn_o_gain"], moved = _hgrn_bwd(tag + "_hgrn", sv["proj"], sv["o"], dmix, lbs[l],
                                                                      W["hgrn_o_gain"][l:l + 1], comm=along)
        dproj = jnp.concatenate([dq, df, di, dg, dqm], axis=1)
    else:
        dq, dgate, acc["dk"], acc["dv"], acc["dc"], g["fox_q_gain"], g["fox_k_gain"], moved = _fox_bwd(
            tag + "_fox", sv["proj"], shared["kvf"], shared["cr"], W["fox_q_gain"][l - 2:l - 1], W["fox_k_gain"],
            sv["o"], dmix, acc["dk"], acc["dv"], acc["dc"], comm=along)
        dproj = jnp.concatenate([dq, dgate, dqm], axis=1)
    landed.update({k: m for (k, _), m in zip(ready, moved)})
    dh, dw_in, moved = _proj_cols_bwd(tag + "_in", sv["h"], dproj, Wl["win"], 0, comm=_Comm(scatter=[dw_out]))
    landed[(l, "w_out")] = moved[0]
    dx1, g["mix_norm"] = _rms_bwd(tag + "_mixrms", x1, W["mix_norm"][l:l + 1], dh, dres=dx2)
    dw_mem_kv = _mm(tag + "_dwmemkv", [(sv["mem_n"], dkv, TN)], [BF16], D_MODEL, 2 * MEM_WIDTH)
    dmem_n = _mm(tag + "_dmemn", [(dkv, _w_mem_kv_of(Wl), NT)], [F32], mem.shape[0], D_MODEL)
    _, g["mem_norm"] = _rms_bwd(tag + "_memrms", mem, W["mem_norm"][l:l + 1], dmem_n)
    return dx1, g, [((l, "w_in"), dw_in), ((l, "w_mem_kv"), dw_mem_kv.reshape(N_DEV, -1, 2 * MEM_WIDTH))]


def _forget_cols(kvf):
    return kvf[:, 2 * MAIN_WIDTH:2 * MAIN_WIDTH + LANES]


def _log_forget(kvf, bias):
    return _rowwise("kv_logf", lambda f, b: jax.nn.log_sigmoid(f + b), [_forget_cols(kvf)], [bias], [(LANES, F32)])[0]


def _move(name, comm):
    def body(o_ref):
        o_ref[...] = jnp.zeros(o_ref.shape, F32)
    _, moved = _call(body, [], [jax.ShapeDtypeStruct((8, LANES), F32)], name=name, in_specs=[],
                     out_specs=[pl.BlockSpec(memory_space=pltpu.VMEM)], comm=comm)
    return moved


def _step(x, mem, target, W, lb_logits, G0, local):
    T = x.shape[0]
    W = dict(W, fox_k_gain=W["fox_k_gain"].reshape(1, -1))
    lbs = _lb_fwd(lb_logits)
    fox_bias = jnp.pad(W["fox_f_bias"], (0, LANES - FOX_HEADS)).reshape(1, LANES)
    w_kv = W["w_kv"]
    n_l = len(local)
    ffn1 = lambda l, Wl: (W["ffn1_norm"][l:l + 1], Wl["gu1"], Wl["r1"], 0, 1, 0)
    ffn2 = lambda l, Wl: (W["ffn2_norm"][l:l + 1], Wl["gu2"], Wl["d2"], 0, 1, 0)

    on_ffn1 = lambda l: [(l + 1, "gu1")] if l + 1 < n_l else []
    on_mix = {0: [(1, "r1"), (1, "win"), (1, "gu2"), (3, "d2")], 1: [(2, "r1"), (2, "win"), (2, "gu2"), (3, "gu2")],
              2: [(3, "r1"), (3, "win")], 3: []}
    on_ffn2 = {0: [(1, "d2")], 1: [(2, "d2")], 2: [], 3: []}
    saved, shared, G = [], {}, [G0] + [{} for _ in range(n_l - 1)]
    for l in range(n_l):
        Wl = G[l]
        relay = on_ffn2[l - 1] if l else []
        along = _Comm(gather=[local[n][k] for n, k in on_ffn1(l)], relay=[G[n][k] for n, k in relay])
        x1, moved = _ffn_fwd(f"l{l}_ffn1", x, *ffn1(l, Wl), comm=along)
        for (n, k), m in zip(on_ffn1(l) + relay, moved):
            G[n][k] = m
        x2, sv = _mixer_fwd(l, x1, mem, G, W, lbs, shared, local, on_mix[l])
        along = _Comm(gather=[local[n][k] for n, k in on_ffn2[l]], relay=[G[n][k] for n, k in on_mix[l]])
        x3, moved = _ffn_fwd(f"l{l}_ffn2", x2, *ffn2(l, G[l]), comm=along)
        for (n, k), m in zip(on_ffn2[l] + on_mix[l], moved):
            G[n][k] = m
        sv.update(x=x, x1=x1, x2=x2)
        saved.append(sv)
        x = x3
        if l == 1:
            hk = _rms_fwd("kv_rms", x, W["kv_norm"].reshape(1, -1))
            kvf = _mm("kv_proj", [(hk, w_kv, NN)], [F32], T, w_kv.shape[1])
            cum = _cumsum_rows("kv_cum", _log_forget(kvf, fox_bias))[:, :FOX_HEADS].T
            shared = dict(kvf=kvf, cr=cum[:, None, :], hk=hk, x=x)

    def loss_fn(y, t):
        err = y - t
        return err * (1.0 / D_MODEL), jnp.sum(0.5 / D_MODEL * err * err, axis=0, keepdims=True)
    dx, loss = _rowwise("loss", loss_fn, [x, target], [], [(D_MODEL, F32)], [((1, D_MODEL), F32)])

    grads = [None] * n_l
    acc = dict(dk=jnp.zeros((T, MAIN_WIDTH), F32), dv=jnp.zeros((T, MAIN_WIDTH), F32), dc=jnp.zeros((FOX_HEADS, 1, T), F32))
    gkv = {}
    landed, late = {}, []
    for l in reversed(range(n_l)):
        sv, Wl = saved[l], G[l]
        ready = []
        if l == 1:
            dcum = jnp.pad(acc["dc"][:, 0, :].T, ((0, 0), (0, LANES - FOX_HEADS)))
            dlf = _cumsum_rows("kv_dcum", dcum, reverse=True)
            def dlogf_fn(d, f, b):
                p = d * _sigmoid(-(f + b))
                return p, jnp.sum(p, axis=0, keepdims=True)
            dfl, gkv["fox_f_bias"] = _rowwise("kv_dlogf", dlogf_fn, [dlf, _forget_cols(shared["kvf"])], [fox_bias],
                                              [(LANES, BF16)], [((1, LANES), F32)])
            dkvf = _pad_cols(jnp.concatenate([acc["dk"].astype(BF16), acc["dv"].astype(BF16), dfl], axis=1), w_kv.shape[1])
            dw_kv = _mm("kv_dw", [(shared["hk"], dkvf, TN)], [BF16], D_MODEL, dkvf.shape[1])
            ready.append(((0, "w_kv"), dw_kv[:, :KV_WIDTH].reshape(N_DEV, -1, KV_WIDTH)))
            dhk = _mm("kv_dh", [(dkvf, w_kv, NT)], [F32], T, D_MODEL)
            dx, gkv["kv_norm"] = _rms_bwd("kv_rmsb", shared["x"], W["kv_norm"].reshape(1, -1), dhk, dres=dx)
        g = {}
        if l < 2:
            ready, late = ready + late, []
        dx2, g["ffn2_norm"], dwg, dwu, dwd, moved_a, moved_w = _ffn_bwd(
            f"l{l}b_ffn2", sv["x2"], *ffn2(l, Wl), dx, comm_a=_Comm(scatter=[v for _, v in late[:2]]),
            comm_w=_Comm(scatter=[v for _, v in late[2:]]))
        landed.update({k: m for (k, _), m in zip(late, moved_a + moved_w)})
        ready += [((l, "ffn2_w_gate"), dwg), ((l, "ffn2_w_up"), dwu), ((l, "ffn2_w_down"), dwd)]
        dx1, gm, rest = _mixer_bwd(l, sv["x1"], mem, Wl, W, lbs, shared, sv, dx2, acc, ready, landed)
        g.update(gm)
        dx, g["ffn1_norm"], dwg, dwu, dwd, moved_a, _ = _ffn_bwd(f"l{l}b_ffn1", sv["x"], *ffn1(l, Wl), dx1,
                                                                 comm_a=_Comm(scatter=[v for _, v in rest]))
        landed.update({k: m for (k, _), m in zip(rest, moved_a)})
        late = [((l, "ffn1_w_gate"), dwg), ((l, "ffn1_w_up"), dwu), ((l, "ffn1_w_down"), dwd)]
        grads[l] = g
    landed.update({k: m for (k, _), m in zip(late, _move("rs_tail", _Comm(scatter=[v for _, v in late])))})

    out = {}
    for n in ["ffn1_norm", "mix_norm", "mem_norm", "mem_q_gain", "mem_k_gain", "ffn2_norm"]:
        out[n] = jnp.concatenate([grads[l][n] for l in range(4)], axis=0)
    out["hgrn_o_gain"] = jnp.concatenate([grads[l]["hgrn_o_gain"] for l in (0, 1)], axis=0)
    out["fox_q_gain"] = jnp.concatenate([grads[l]["fox_q_gain"] for l in (2, 3)], axis=0)
    out["fox_k_gain"] = (grads[2]["fox_k_gain"] + grads[3]["fox_k_gain"]).reshape(-1)
    out["kv_norm"] = gkv["kv_norm"].reshape(-1)
    out["fox_f_bias"] = gkv["fox_f_bias"][0, :FOX_HEADS]
    dl0, dl1 = _lb_bwd(lb_logits, grads[0]["lb"], grads[1]["lb"])
    out["hgrn_lb_logits"] = jnp.concatenate([dl0, dl1], axis=0)
    return loss, dx, out, landed


WEIGHTS = ["ffn1_norm", "ffn1_w_gate", "ffn1_w_up", "ffn1_w_down", "mix_norm", "mem_norm", "w_mem_kv", "mem_q_gain",
           "mem_k_gain", "w_in_a", "hgrn_lb_logits", "hgrn_o_gain", "w_in_b", "fox_q_gain", "kv_norm", "w_kv", "fox_f_bias",
           "fox_k_gain", "w_out", "ffn2_norm", "ffn2_w_gate", "ffn2_w_up", "ffn2_w_down"]
BIG = COLS352 + ["w_in_a", "w_in_b", "ffn1_w_down", "ffn2_w_down", "w_out", "w_mem_kv", "w_kv"]


def _train_step(a):
    bf = lambda w: w.astype(BF16)
    n_l = a["w_out"].shape[0]
    local = []
    for l in range(n_l):
        w_in = a["w_in_a"][l] if l < a["w_in_a"].shape[0] else a["w_in_b"][l - a["w_in_a"].shape[0]]
        local.append(dict(
            gu1=bf(jnp.concatenate([a["ffn1_w_gate"][l], a["ffn1_w_up"][l]], axis=0)),
            r1=bf(jnp.concatenate([a["ffn1_w_down"][l], a["w_out"][l], _pad_cols(a["w_mem_kv"][l], D_MODEL)], axis=0)),
            win=bf(w_in),
            gu2=bf(jnp.concatenate([a["ffn2_w_gate"][l], a["ffn2_w_up"][l]], axis=0)),
            d2=bf(a["ffn2_w_down"][l])))
    keys = ["gu1", "r1", "win", "gu2", "d2"]
    first = _all_gather("ag_first", [local[0][k] for k in keys] + [bf(a["w_kv"]), _small_pack([a["hgrn_lb_logits"]])])
    G0 = dict(zip(keys, first))
    W = {n: a[n] for n in SMALL}
    W["w_kv"] = _pad_cols(first[5].reshape(-1, a["w_kv"].shape[1]), 2 * D_MODEL)
    lb_shape = a["hgrn_lb_logits"].shape
    lb_all = first[6].reshape(N_DEV, -1)[:, :lb_shape[0] * lb_shape[1]]
    lb_logits = lb_all.reshape((N_DEV,) + lb_shape).transpose(1, 0, 2).reshape(lb_shape[0], -1)

    loss_part, dx, g, landed = _step(a["x"][0], a["mem"][0], a["loss_target"][0], W, lb_logits, G0, local)

    n_a = a["w_in_a"].shape[0]
    slots = {n: [landed[(l, n)] for l in range(n_l)] for n in COLS352 + ["ffn1_w_down", "ffn2_w_down", "w_out", "w_mem_kv"]}
    slots["w_in_a"] = [landed[(l, "w_in")] for l in range(n_a)]
    slots["w_in_b"] = [landed[(l, "w_in")] for l in range(n_a, n_l)]
    slots["w_kv"] = [landed[(0, "w_kv")]]
    grad, delta, new_m, new_v = {}, {}, {}, {}
    for n in BIG:
        lead = (lambda t: t[None]) if a[n].ndim == 2 else (lambda t: t)
        res = _sum_adamw("adam_" + n, slots[n], lead(a[n]), lead(a["m_" + n]), lead(a["v_" + n]))
        grad[n], delta[n], new_m[n], new_v[n] = [t.reshape(a[n].shape) for t in res]

    zeros = [jnp.zeros(lb_logits.shape, F32), jnp.zeros(loss_part.shape, F32)]
    small_shapes = [a[n].shape for n in SMALL] + [lb_logits.shape, loss_part.shape]
    small_part = _small_pack([g[n] for n in SMALL] + [g["hgrn_lb_logits"], loss_part])
    small_sum = _sum_slabs("small_sum", _all_gather("ag_small", [small_part])[0], F32)
    small = _small_unpack(small_sum, small_shapes)
    grad.update(dict(zip(SMALL, small)))
    loss = jnp.sum(small[-1])
    me = 4 * lax.axis_index("x") + 2 * lax.axis_index("y") + lax.axis_index("c")
    grad["hgrn_lb_logits"] = lax.dynamic_slice_in_dim(small[-2], me * lb_shape[1], lb_shape[1], axis=1)

    n = "hgrn_lb_logits"
    delta[n], new_m[n], new_v[n] = _adamw("adam_" + n, a[n], grad[n], a["m_" + n], a["v_" + n])
    packs = [_small_pack([a[p + n] for n in SMALL] + zeros) for p in ("", "m_", "v_")]
    upd = _rowwise("adam_small", _adamw_fn, [packs[0], small_sum, packs[1], packs[2]], [], [(LANES, F32)] * 3, tm=packs[0].shape[0])
    for d, u in zip((delta, new_m, new_v), upd):
        d.update(dict(zip(SMALL, _small_unpack(u, small_shapes))))
    return (loss, dx[None], *[grad[n] for n in WEIGHTS], *[delta[n] for n in WEIGHTS], *[new_m[n] for n in WEIGHTS],
            *[new_v[n] for n in WEIGHTS])


def kernel(x, mem, ffn1_norm, ffn1_w_gate, ffn1_w_up, ffn1_w_down, mix_norm, mem_norm, w_mem_kv, mem_q_gain, mem_k_gain, w_in_a, hgrn_lb_logits, hgrn_o_gain, w_in_b, fox_q_gain, kv_norm, w_kv, fox_f_bias, fox_k_gain, w_out, ffn2_norm, ffn2_w_gate, ffn2_w_up, ffn2_w_down, loss_target, m_ffn1_norm, m_ffn1_w_gate, m_ffn1_w_up, m_ffn1_w_down, m_mix_norm, m_mem_norm, m_w_mem_kv, m_mem_q_gain, m_mem_k_gain, m_w_in_a, m_hgrn_lb_logits, m_hgrn_o_gain, m_w_in_b, m_fox_q_gain, m_kv_norm, m_w_kv, m_fox_f_bias, m_fox_k_gain, m_w_out, m_ffn2_norm, m_ffn2_w_gate, m_ffn2_w_up, m_ffn2_w_down, v_ffn1_norm, v_ffn1_w_gate, v_ffn1_w_up, v_ffn1_w_down, v_mix_norm, v_mem_norm, v_w_mem_kv, v_mem_q_gain, v_mem_k_gain, v_w_in_a, v_hgrn_lb_logits, v_hgrn_o_gain, v_w_in_b, v_fox_q_gain, v_kv_norm, v_w_kv, v_fox_f_bias, v_fox_k_gain, v_w_out, v_ffn2_norm, v_ffn2_w_gate, v_ffn2_w_up, v_ffn2_w_down):
    return _train_step(dict(locals()))
```

```python
import functools

import jax
import jax.numpy as jnp
from jax import lax
from jax.experimental import pallas as pl
from jax.experimental.pallas import tpu as pltpu

F32, BF16 = jnp.float32, jnp.bfloat16
EPS = 1e-6
V7X_VMEM_LIMIT = 56 * 1024 * 1024
LANES = 128
N_DEV = 8

D_MODEL = 1024
MAIN_WIDTH = 768
MEM_WIDTH = 256
HG_HEAD_DIM = 128
HG_HEADS = 6
FOX_HEAD_DIM = 64
FOX_HEADS = 12
MEM_HEADS = 4
MEM_HEAD_DIM = 64
HG_BLOCK = 16

ADAM_LR, ADAM_B1, ADAM_B2, ADAM_EPS, ADAM_WD, ADAM_STEP = 0.001, 0.9, 0.999, 1e-08, 0.01, 10

NN = ((1,), (0,))
NT = ((1,), (1,))
TN = ((0,), (0,))


def _dot(a, b, dims, precision=None):
    return lax.dot_general(a, b, (dims, ((), ())), preferred_element_type=F32, precision=precision)


def _bdot(a, b, dims):
    return _dot(a.astype(BF16), b.astype(BF16), dims)


def _split(a):
    hi = a.astype(BF16)
    return hi, (a - hi.astype(F32)).astype(BF16)


def _fdot(a, b, dims):
    ah, al = _split(a)
    bh, bl = _split(b)
    return _dot(ah, bh, dims) + (_dot(ah, bl, dims) + _dot(al, bh, dims))


def _params(n_grid):
    return pltpu.CompilerParams(dimension_semantics=("arbitrary",) * n_grid, vmem_limit_bytes=V7X_VMEM_LIMIT)


def _rms(x, g):
    return x * lax.rsqrt(jnp.mean(x * x, axis=-1, keepdims=True) + EPS) * g


def _sigmoid(x):
    return jax.nn.sigmoid(x)


def _silu(x):
    return x * jax.nn.sigmoid(x)


MESH = pl.DeviceIdType.MESH
ANY = pl.BlockSpec(memory_space=pl.ANY)


def _mesh_pos():
    return lax.axis_index("x"), lax.axis_index("y"), lax.axis_index("c")


class _Comm:
    def __init__(self, gather=(), relay=(), scatter=()):
        self.gather, self.relay, self.scatter = list(gather), list(relay), list(scatter)
        self.arrays = self.gather + self.relay + self.scatter
        self.n_remote = 4 * len(self.gather) + 3 * len(self.relay) + 7 * len(self.scatter)
        self.n_local = len(self.gather) + len(self.scatter)

    def out_shapes(self):
        return ([jax.ShapeDtypeStruct((N_DEV,) + x.shape, x.dtype) for x in self.gather]
                + [jax.ShapeDtypeStruct(g.shape, g.dtype) for g in self.relay + self.scatter])

    def scratch(self):
        return [pltpu.SemaphoreType.DMA((self.n_remote,)), pltpu.SemaphoreType.DMA((self.n_remote,)),
                pltpu.SemaphoreType.DMA((max(self.n_local, 1),))]

    def _copies(self, ins, outs, send, recv, local, arrivals=True):
        mx, my, mc = _mesh_pos()
        flip = lambda v, f: 1 - v if f else v
        idx = lambda p: 4 * p[0] + 2 * p[1] + p[2]
        me = (mx, my, mc)
        count = [0, 0]
        loc, out, arrive = [], [], []

        def pair(src, dst, lands, to):
            k = count[0]
            count[0] += 1
            mk = lambda d: pltpu.make_async_remote_copy(src_ref=src, dst_ref=d, send_sem=send.at[k], recv_sem=recv.at[k],
                                                        device_id=to, device_id_type=MESH)
            out.append(mk(dst))
            if arrivals:
                arrive.append(mk(lands))

        def local_copy(src, dst):
            loc.append(pltpu.make_async_copy(src, dst, local.at[count[1]]))
            count[1] += 1

        refs = list(zip(ins, outs))
        near = [(0, 0, 1), (1, 0, 0), (0, 1, 0), (1, 1, 0)]
        for x, G in refs[:len(self.gather)]:
            local_copy(x, G.at[idx(me)])
            for f in near:
                peer = tuple(flip(v, b) for v, b in zip(me, f))
                pair(x, G.at[idx(me)], G.at[idx(peer)], peer)
        sibling = (mx, my, 1 - mc)
        for Gin, Gout in refs[len(self.gather):len(self.gather) + len(self.relay)]:
            for f in near[1:]:
                chip = (flip(mx, f[0]), flip(my, f[1]))
                pair(Gin.at[idx((*chip, mc))], Gout.at[idx((*chip, mc))], Gout.at[idx((*chip, 1 - mc))], sibling)
        every = near + [(1, 0, 1), (0, 1, 1), (1, 1, 1)]
        for g, R in refs[len(self.gather) + len(self.relay):]:
            local_copy(g.at[idx(me)], R.at[idx(me)])
            for f in every:
                peer = tuple(flip(v, b) for v, b in zip(me, f))
                pair(g.at[idx(peer)], R.at[idx(me)], R.at[idx(peer)], peer)
        return loc, out, arrive

    def start(self, ins, outs, send, recv, local):
        loc, out, _ = self._copies(ins, outs, send, recv, local, arrivals=False)
        for cp in loc + out:
            cp.start()

    def finish(self, ins, outs, send, recv, local):
        loc, out, arrive = self._copies(ins, outs, send, recv, local)
        for cp in arrive:
            cp.wait_recv()
        for cp in out:
            cp.wait_send()
        for cp in loc:
            cp.wait()


def _call(body, operands, out_shape, *, name, grid=(), in_specs=None, out_specs=None, scratch=(), comm=None):
    outs = list(out_shape) if isinstance(out_shape, (list, tuple)) else [out_shape]
    single = not isinstance(out_shape, (list, tuple))
    params = _params(len(grid))
    if comm is None or not comm.arrays:
        res = pl.pallas_call(body, grid=grid, in_specs=in_specs, out_specs=out_specs, out_shape=out_shape,
                             scratch_shapes=list(scratch), name=name, compiler_params=params)(*operands)
        return ([res] if single else list(res)), []
    n_in, n_out, n_s, n_c = len(operands), len(outs), len(scratch), len(comm.arrays)

    def wrapped(*refs):
        pos = [0]

        def take(n):
            pos[0] += n
            return refs[pos[0] - n:pos[0]]

        b_in, c_in, b_out, c_out, b_s, sems = take(n_in), take(n_c), take(n_out), take(n_c), take(n_s), take(3)
        ids = [pl.program_id(d) for d in range(len(grid))]
        first, last = True, True
        for d, i in enumerate(ids):
            first = (i == 0) & first
            last = (i == grid[d] - 1) & last
        if grid:
            pl.when(first)(lambda: comm.start(c_in, c_out, *sems))
        else:
            comm.start(c_in, c_out, *sems)
        body(*b_in, *b_out, *b_s)
        if grid:
            pl.when(last)(lambda: comm.finish(c_in, c_out, *sems))
        else:
            comm.finish(c_in, c_out, *sems)

    n_g = len(comm.gather)
    aliases = {n_in + n_g + r: n_out + n_g + r for r in range(len(comm.relay))}
    out_specs_l = list(out_specs) if isinstance(out_specs, (list, tuple)) else [out_specs]
    res = pl.pallas_call(
        wrapped, grid=grid, in_specs=list(in_specs) + [ANY] * n_c, out_specs=out_specs_l + [ANY] * n_c,
        out_shape=outs + comm.out_shapes(), scratch_shapes=list(scratch) + comm.scratch(), input_output_aliases=aliases,
        name=name, compiler_params=params)(*operands, *comm.arrays)
    return list(res[:n_out]), list(res[n_out:])


def _rowwise(name, fn, rows, consts, out_rows, out_reds=(), tm=512):
    R = rows[0].shape[0]
    tm = min(tm, R)
    assert R % tm == 0
    n_in, n_o = len(rows) + len(consts), len(out_rows)

    def body(*refs):
        outs = fn(*[r[...] for r in refs[:n_in]])
        if not isinstance(outs, (tuple, list)):
            outs = (outs,)
        for r, o in zip(refs[n_in:n_in + n_o], outs[:n_o]):
            r[...] = o.astype(r.dtype)
        red_refs = refs[n_in + n_o:]
        if red_refs:
            @pl.when(pl.program_id(0) == 0)
            def _():
                for r in red_refs:
                    r[...] = jnp.zeros(r.shape, r.dtype)
            for r, o in zip(red_refs, outs[n_o:]):
                r[...] += o

    zero = lambda n: (lambda i: (0,) * n)
    in_specs = [pl.BlockSpec((tm, a.shape[1]), lambda i: (i, 0)) for a in rows]
    in_specs += [pl.BlockSpec(c.shape, zero(c.ndim)) for c in consts]
    out_specs = [pl.BlockSpec((tm, c), lambda i: (i, 0)) for c, _ in out_rows]
    out_specs += [pl.BlockSpec(s, zero(len(s))) for s, _ in out_reds]
    out_shape = [jax.ShapeDtypeStruct((R, c), dt) for c, dt in out_rows]
    out_shape += [jax.ShapeDtypeStruct(s, dt) for s, dt in out_reds]
    return pl.pallas_call(body, grid=(R // tm,), in_specs=in_specs, out_specs=out_specs, out_shape=out_shape,
                          name=name, compiler_params=_params(1))(*rows, *consts)


def _tile(n, cap):
    best = None
    for t in range(LANES, min(n, cap) + 1, LANES):
        if n % t == 0:
            best = t
    return best or n


def _mm(name, pairs, out_dtypes, M, N, epi=None, extras=(), tm=512, tn=512):
    tm, tn = _tile(M, tm), _tile(N, tn)
    n_p, n_e = len(pairs), len(extras)
    modes = [m for _, _, m in pairs]

    def body(*refs):
        accs = [_bdot(refs[2 * k][...], refs[2 * k + 1][...], modes[k]) for k in range(n_p)]
        ex = [r[...] for r in refs[2 * n_p:2 * n_p + n_e]]
        outs = epi(accs, ex) if epi is not None else accs
        for r, o in zip(refs[2 * n_p + n_e:], outs):
            r[...] = o.astype(r.dtype)

    in_specs = []
    ops = []
    for a, b, mode in pairs:
        if mode == NN:
            K = a.shape[1]
            assert a.shape == (M, K) and b.shape == (K, N), (name, a.shape, b.shape)
            in_specs += [pl.BlockSpec((tm, K), lambda i, j: (i, 0)), pl.BlockSpec((K, tn), lambda i, j: (0, j))]
        elif mode == NT:
            K = a.shape[1]
            assert a.shape == (M, K) and b.shape == (N, K), (name, a.shape, b.shape)
            in_specs += [pl.BlockSpec((tm, K), lambda i, j: (i, 0)), pl.BlockSpec((tn, K), lambda i, j: (j, 0))]
        else:
            K = a.shape[0]
            assert a.shape == (K, M) and b.shape == (K, N), (name, a.shape, b.shape)
            in_specs += [pl.BlockSpec((K, tm), lambda i, j: (0, i)), pl.BlockSpec((K, tn), lambda i, j: (0, j))]
        ops += [a, b]
    in_specs += [pl.BlockSpec((tm, tn), lambda i, j: (i, j)) for _ in extras]
    out_specs = [pl.BlockSpec((tm, tn), lambda i, j: (i, j)) for _ in out_dtypes]
    out_shape = [jax.ShapeDtypeStruct((M, N), dt) for dt in out_dtypes]
    res = pl.pallas_call(body, grid=(M // tm, N // tn), in_specs=in_specs, out_specs=out_specs, out_shape=out_shape,
                         name=name, compiler_params=_params(2))(*ops, *extras)
    return res[0] if len(res) == 1 else res


def _rms_fwd(name, x, gain, dtype=BF16):
    return _rowwise(name, _rms, [x], [gain], [(x.shape[1], dtype)])[0]


def _rms_bwd(name, x, gain, dh, dres=None):
    def fn(x, dh, *rest):
        g = rest[-1]
        _, vjp = jax.vjp(_rms, x, g)
        dx, dg = vjp(dh)
        if dres is not None:
            dx = dx + rest[0]
        return dx, dg
    rows = [x, dh] + ([dres] if dres is not None else [])
    d = x.shape[1]
    return _rowwise(name, fn, rows, [gain], [(d, F32)], [((1, d), F32)])


def _ffn_specs(gcols, grows, ig, iu, idn):
    n = gcols.shape[2]
    D = grows.shape[2]
    wg = pl.BlockSpec((None, D, n), lambda i, j: (j, ig, 0))
    wu = pl.BlockSpec((None, D, n), lambda i, j: (j, iu, 0))
    wd = pl.BlockSpec((None, n, D), lambda i, j: (j, idn, 0))
    return n, wg, wu, wd


def _ffn_fwd(name, x, gain, gcols, grows, ig, iu, idn, tm=1024, comm=None):
    T, D = x.shape
    tm = min(T, tm)
    n, wg_s, wu_s, wd_s = _ffn_specs(gcols, grows, ig, iu, idn)
    last = N_DEV - 1

    def body(x_ref, g_ref, wg_ref, wu_ref, wd_ref, y_ref, h_s, acc):
        j = pl.program_id(1)

        @pl.when(j == 0)
        def _():
            h_s[...] = _rms(x_ref[...], g_ref[...]).astype(BF16)
            acc[...] = jnp.zeros(acc.shape, F32)
        h = h_s[...]
        z = _silu(_dot(h, wg_ref[...], NN)) * _dot(h, wu_ref[...], NN)
        acc[...] += _dot(z.astype(BF16), wd_ref[...], NN)

        @pl.when(j == last)
        def _():
            y_ref[...] = x_ref[...] + 0.5 * acc[...]

    row = pl.BlockSpec((tm, D), lambda i, j: (i, 0))
    (y,), moved = _call(
        body, [x, gain, gcols, gcols, grows], [jax.ShapeDtypeStruct((T, D), F32)], name=name, grid=(T // tm, N_DEV),
        in_specs=[row, pl.BlockSpec((1, D), lambda i, j: (0, 0)), wg_s, wu_s, wd_s], out_specs=[row],
        scratch=[pltpu.VMEM((tm, D), BF16), pltpu.VMEM((tm, D), F32)], comm=comm)
    return y, moved


def _ffn_bwd(tag, x, gain, gcols, grows, ig, iu, idn, dy, tm=512, comm_a=None, comm_w=None):
    T, D = x.shape
    tm = min(T, tm)
    n, wg_s, wu_s, wd_s = _ffn_specs(gcols, grows, ig, iu, idn)
    last = N_DEV - 1

    def body(x_ref, dy_ref, g_ref, wg_ref, wu_ref, wd_ref, dx_ref, dg_ref, h_ref, z_ref, da_ref, db_ref, dh_acc):
        i, j = pl.program_id(0), pl.program_id(1)

        @pl.when(j == 0)
        def _():
            h_ref[...] = _rms(x_ref[...], g_ref[...]).astype(BF16)
            dh_acc[...] = jnp.zeros(dh_acc.shape, F32)

        @pl.when((i == 0) & (j == 0))
        def _():
            dg_ref[...] = jnp.zeros(dg_ref.shape, F32)
        h = h_ref[...]
        a, b = _dot(h, wg_ref[...], NN), _dot(h, wu_ref[...], NN)
        dz = 0.5 * _dot(dy_ref[...].astype(BF16), wd_ref[...], NT)
        s = _sigmoid(a)
        si = a * s
        da = (dz * b * (s + si * (1.0 - s))).astype(BF16)
        db = (dz * si).astype(BF16)
        z_ref[...] = (si * b).astype(BF16)
        da_ref[...] = da
        db_ref[...] = db
        dh_acc[...] += _dot(da, wg_ref[...], NT) + _dot(db, wu_ref[...], NT)

        @pl.when(j == last)
        def _():
            _, vjp = jax.vjp(_rms, x_ref[...], g_ref[...])
            dx, dg = vjp(dh_acc[...])
            dx_ref[...] = dx + dy_ref[...]
            dg_ref[...] += dg

    row = pl.BlockSpec((tm, D), lambda i, j: (i, 0))
    vec = pl.BlockSpec((1, D), lambda i, j: (0, 0))
    hid = pl.BlockSpec((None, tm, n), lambda i, j: (j, i, 0))
    hidden = jax.ShapeDtypeStruct((N_DEV, T, n), BF16)
    (dx, dgain, h, z, da, db), moved_a = _call(
        body, [x, dy, gain, gcols, gcols, grows],
        [jax.ShapeDtypeStruct((T, D), F32), jax.ShapeDtypeStruct((1, D), F32), jax.ShapeDtypeStruct((T, D), BF16),
         hidden, hidden, hidden],
        name=tag + "_a", grid=(T // tm, N_DEV), in_specs=[row, row, vec, wg_s, wu_s, wd_s],
        out_specs=[row, vec, row, hid, hid, hid], scratch=[pltpu.VMEM((tm, D), F32)], comm=comm_a)

    def wbody(h_ref, dy_ref, z_ref, da_ref, db_ref, dwg_ref, dwu_ref, dwd_ref):
        h = h_ref[...]
        dwg_ref[...] = _dot(h, da_ref[...], TN).astype(BF16)
        dwu_ref[...] = _dot(h, db_ref[...], TN).astype(BF16)
        dwd_ref[...] = (0.5 * _dot(z_ref[...], dy_ref[...].astype(BF16), TN)).astype(BF16)

    full = pl.BlockSpec((T, D), lambda j: (0, 0))
    hid_all = pl.BlockSpec((None, T, n), lambda j: (j, 0, 0))
    (dwg, dwu, dwd), moved_w = _call(
        wbody, [h, dy, z, da, db],
        [jax.ShapeDtypeStruct((N_DEV, D, n), BF16)] * 2 + [jax.ShapeDtypeStruct((N_DEV, n, D), BF16)],
        name=tag + "_w", grid=(N_DEV,), in_specs=[full, full, hid_all, hid_all, hid_all],
        out_specs=[pl.BlockSpec((None, D, n), lambda j: (j, 0, 0))] * 2 + [pl.BlockSpec((None, n, D), lambda j: (j, 0, 0))],
        comm=comm_w)
    return dx, dgain, dwg, dwu, dwd, moved_a, moved_w


def _wcols_spec(gw, l, grid_rank):
    _, _, n = gw.shape
    K = D_MODEL
    zero = (lambda i: (0, l, 0)) if grid_rank == 1 else (lambda i, j: (0, l, 0))
    return n, K, pl.BlockSpec((N_DEV, K, n), zero)


def _proj_cols(name, h, gw, l, tm=512, comm=None):
    T = h.shape[0]
    tm = min(T, tm)
    n, K, wspec = _wcols_spec(gw, l, 1)

    def body(h_ref, w_ref, o_ref):
        h = h_ref[...]
        for j in range(N_DEV):
            o_ref[:, pl.ds(j * n, n)] = _dot(h, w_ref[j], NN)

    (proj,), moved = _call(
        body, [h, gw], [jax.ShapeDtypeStruct((T, N_DEV * n), F32)], name=name, grid=(T // tm,),
        in_specs=[pl.BlockSpec((tm, K), lambda i: (i, 0)), wspec], out_specs=[pl.BlockSpec((tm, N_DEV * n), lambda i: (i, 0))],
        comm=comm)
    return proj, moved


def _proj_cols_bwd(tag, h, dproj, gw, l, tm=512, tk=512, comm=None):
    T = h.shape[0]
    tm = min(T, tm)
    n, K, wspec = _wcols_spec(gw, l, 1)

    def dh_body(dp_ref, w_ref, o_ref):
        acc = jnp.zeros(o_ref.shape, F32)
        for j in range(N_DEV):
            acc = acc + _dot(dp_ref[:, pl.ds(j * n, n)], w_ref[j], NT)
        o_ref[...] = acc

    (dh,), moved = _call(
        dh_body, [dproj, gw], [jax.ShapeDtypeStruct((T, K), F32)], name=tag + "_dh", grid=(T // tm,),
        in_specs=[pl.BlockSpec((tm, N_DEV * n), lambda i: (i, 0)), wspec], out_specs=[pl.BlockSpec((tm, K), lambda i: (i, 0))],
        comm=comm)

    def dw_body(h_ref, dp_ref, o_ref):
        h = h_ref[...]
        for j in range(N_DEV):
            o_ref[j] = _dot(h, dp_ref[:, pl.ds(j * n, n)], TN).astype(BF16)

    dw = pl.pallas_call(
        dw_body, grid=(K // tk,), in_specs=[pl.BlockSpec((T, tk), lambda i: (0, i)), pl.BlockSpec((T, N_DEV * n), lambda i: (0, 0))],
        out_specs=pl.BlockSpec((N_DEV, tk, n), lambda i: (0, i, 0)), out_shape=jax.ShapeDtypeStruct((N_DEV, K, n), BF16),
        name=tag + "_dw", compiler_params=_params(1))(h, dproj)
    return dh, dw, moved


def _block_tri(n, reverse=False):
    r = lax.broadcasted_iota(jnp.int32, (n, n), 0)
    c = lax.broadcasted_iota(jnp.int32, (n, n), 1)
    same = (r // HG_BLOCK) == (c // HG_BLOCK)
    return (same & ((c >= r) if reverse else (c <= r))).astype(F32)


def _hgrn_prep(q_ref, f_ref, lbv, qs, ks, cs, T):
    pt = min(T, 256)
    tri = _block_tri(pt)
    for p in range(T // pt):
        rows = pl.ds(p * pt, pt)
        f = lbv + (1.0 - lbv) * _sigmoid(f_ref[rows, :])
        qs[rows, :] = _silu(q_ref[rows, :])
        ks[rows, :] = 1.0 - f
        cs[rows, :] = _dot(tri, jnp.log(f), NN, precision=lax.Precision.HIGHEST)


HG_GROUP = 128


def _groups_loop(nb, fn):
    gp = HG_GROUP if nb % HG_GROUP == 0 else nb

    def step(i, carry):
        base = pl.multiple_of(i * (gp * HG_BLOCK), gp * HG_BLOCK)
        fn(lambda t: pl.ds(base + t, gp, stride=HG_BLOCK))
        return carry

    lax.fori_loop(0, nb // gp, step, 0)


def _gate_out(o, og, g):
    return _rms(o, og) * _silu(g)


HG_UNROLL = 16


def _block_rows(n):
    return pl.ds(pl.multiple_of(n * HG_BLOCK, HG_BLOCK), HG_BLOCK)


def _blocks_loop(nb, fn):
    u = HG_UNROLL if nb % HG_UNROLL == 0 else 1

    def step(i, carry):
        for k in range(u):
            fn(i * u + k)
        return carry

    lax.fori_loop(0, nb // u, step, 0)


def _scan_states(buf, cs, nb, reverse=False):
    def step(m, st):
        n = nb - 1 - m if reverse else m
        own = buf[n]
        buf[n] = st
        rows = _block_rows(n)
        return jnp.exp(cs[rows, :][HG_BLOCK - 1:HG_BLOCK, :]) * st + own

    lax.fori_loop(0, nb, step, jnp.zeros(buf.shape[1:], F32))


def _hgrn_states(i_ref, ks, cs, states, nb):
    def own_step(n):
        rows = _block_rows(n)
        c = cs[rows, :]
        states[n] = _fdot(i_ref[rows, :], ks[rows, :] * jnp.exp(c[HG_BLOCK - 1:HG_BLOCK, :] - c), TN)

    _blocks_loop(nb, own_step)
    _scan_states(states, cs, nb)


def _hgrn_fwd(name, proj, lb, og, comm=None):
    T = proj.shape[0]
    nb = T // HG_BLOCK
    hd = HG_HEAD_DIM

    def body(q_ref, f_ref, i_ref, g_ref, lb_ref, og_ref, main_ref, o_ref, qs, ks, cs, states):
        _hgrn_prep(q_ref, f_ref, lb_ref[...], qs, ks, cs, T)
        def pairs(at):
            for t in range(HG_BLOCK):
                qt, ct = qs[at(t), :], cs[at(t), :]
                acc = jnp.zeros(qt.shape, F32)
                for s in range(t + 1):
                    w = qt * ks[at(s), :] * jnp.exp(ct - cs[at(s), :])
                    acc = acc + jnp.sum(w, axis=-1, keepdims=True) * i_ref[at(s), :]
                o_ref[at(t), :] = acc

        _groups_loop(nb, pairs)

        _hgrn_states(i_ref, ks, cs, states, nb)

        def out_step(n):
            rows = _block_rows(n)
            o_ref[rows, :] += _fdot(qs[rows, :] * jnp.exp(cs[rows, :]), states[n], NT)

        _blocks_loop(nb, out_step)
        pt = min(T, 256)
        for p in range(T // pt):
            rows = pl.ds(p * pt, pt)
            main_ref[rows, :] = _gate_out(o_ref[rows, :], og_ref[...], g_ref[rows, :])

    nh = HG_HEADS
    col = lambda off: pl.BlockSpec((T, hd), lambda h, off=off: (0, off + h))
    (main, o), moved = _call(
        body, [proj, proj, proj, proj, lb, og], [jax.ShapeDtypeStruct((T, MAIN_WIDTH), F32)] * 2, name=name, grid=(nh,),
        in_specs=[col(0), col(nh), col(2 * nh), col(3 * nh), pl.BlockSpec((1, hd), lambda h: (0, h)),
                  pl.BlockSpec((1, hd), lambda h: (0, 0))],
        out_specs=[col(0), col(0)], scratch=[pltpu.VMEM((T, hd), F32)] * 3 + [pltpu.VMEM((nb, hd, hd), F32)], comm=comm)
    return main, o, moved


def _hgrn_bwd(name, proj, o, dmix, lb, og, comm=None):
    T = proj.shape[0]
    nb = T // HG_BLOCK
    hd = HG_HEAD_DIM
    pt = min(T, 256)

    def body(q_ref, f_ref, i_ref, g_ref, o_ref, dm_ref, lb_ref, og_ref,
             dq_ref, df_ref, di_ref, dg_ref, dlb_ref, dog_ref, qs, ks, cs, dos, dqs, dks, dvs, states, behind):
        lbv = lb_ref[...]
        _hgrn_prep(q_ref, f_ref, lbv, qs, ks, cs, T)
        dog = jnp.zeros((1, hd), F32)
        for p in range(T // pt):
            rows = pl.ds(p * pt, pt)
            _, vjp = jax.vjp(_gate_out, o_ref[rows, :], og_ref[...], g_ref[rows, :])
            do, dog_p, dg = vjp(dm_ref[rows, :])
            dos[rows, :] = do
            dg_ref[rows, :] = dg.astype(dg_ref.dtype)
            dog = dog + dog_p

        @pl.when(pl.program_id(0) == 0)
        def _():
            dog_ref[...] = jnp.zeros(dog_ref.shape, F32)
        dog_ref[...] += dog

        def pairs(at):
            for t in range(HG_BLOCK):
                dqs[at(t), :] = jnp.zeros((HG_GROUP if nb % HG_GROUP == 0 else nb, hd), F32)
            for s in range(HG_BLOCK):
                k_s, c_s, v_s = ks[at(s), :], cs[at(s), :], i_ref[at(s), :]
                dk = jnp.zeros(k_s.shape, F32)
                dv = jnp.zeros(k_s.shape, F32)
                for t in range(s, HG_BLOCK):
                    q_t, do_t = qs[at(t), :], dos[at(t), :]
                    e = jnp.exp(cs[at(t), :] - c_s)
                    a = jnp.sum(q_t * k_s * e, axis=-1, keepdims=True)
                    g = jnp.sum(do_t * v_s, axis=-1, keepdims=True)
                    dqs[at(t), :] += g * k_s * e
                    dk = dk + g * q_t * e
                    dv = dv + a * do_t
                dks[at(s), :] = dk
                dvs[at(s), :] = dv

        _groups_loop(nb, pairs)

        _hgrn_states(i_ref, ks, cs, states, nb)

        def own_step(n):
            rows = _block_rows(n)
            behind[n] = _fdot(dos[rows, :], qs[rows, :] * jnp.exp(cs[rows, :]), TN)

        _blocks_loop(nb, own_step)
        _scan_states(behind, cs, nb, reverse=True)

        def grad_step(n):
            rows = _block_rows(n)
            c = cs[rows, :]
            ec, ek = jnp.exp(c), jnp.exp(c[HG_BLOCK - 1:HG_BLOCK, :] - c)
            dst = behind[n]
            dqs[rows, :] += _fdot(dos[rows, :], states[n], NN) * ec
            dks[rows, :] += _fdot(i_ref[rows, :], dst, NN) * ek
            dvs[rows, :] += _fdot(ks[rows, :] * ek, dst, NT)

        _blocks_loop(nb, grad_step)

        full = (lax.broadcasted_iota(jnp.int32, (pt, pt), 1) >= lax.broadcasted_iota(jnp.int32, (pt, pt), 0)).astype(F32)
        carry = jnp.zeros((1, hd), F32)
        dlb = jnp.zeros((1, hd), F32)
        for p in reversed(range(T // pt)):
            rows = pl.ds(p * pt, pt)
            q, k, dq, dk = qs[rows, :], ks[rows, :], dqs[rows, :], dks[rows, :]
            db = q * dq - k * dk
            dlf = _dot(full, db, NN, precision=lax.Precision.HIGHEST) + carry
            carry = carry + jnp.sum(db, axis=0, keepdims=True)
            sg = _sigmoid(f_ref[rows, :])
            df = dlf / (1.0 - k) - dk
            df_ref[rows, :] = (df * (1.0 - lbv) * sg * (1.0 - sg)).astype(df_ref.dtype)
            dlb = dlb + jnp.sum(df * (1.0 - sg), axis=0, keepdims=True)
            qr = q_ref[rows, :]
            sq = _sigmoid(qr)
            dq_ref[rows, :] = (dq * (sq + qr * sq * (1.0 - sq))).astype(dq_ref.dtype)
            di_ref[rows, :] = dvs[rows, :].astype(di_ref.dtype)
        dlb_ref[...] = dlb

    nh = HG_HEADS
    col = lambda off: pl.BlockSpec((T, hd), lambda h, off=off: (0, off + h))
    vec = pl.BlockSpec((1, hd), lambda h: (0, h))
    one = pl.BlockSpec((1, hd), lambda h: (0, 0))
    outs, moved = _call(
        body, [proj, proj, proj, proj, o, dmix, lb, og],
        [jax.ShapeDtypeStruct((T, MAIN_WIDTH), BF16)] * 4
        + [jax.ShapeDtypeStruct((1, MAIN_WIDTH), F32), jax.ShapeDtypeStruct((1, hd), F32)],
        name=name, grid=(nh,), in_specs=[col(0), col(nh), col(2 * nh), col(3 * nh), col(0), col(0), vec, one],
        out_specs=[col(0), col(0), col(0), col(0), vec, one],
        scratch=[pltpu.VMEM((T, hd), F32)] * 7 + [pltpu.VMEM((nb, hd, hd), F32)] * 2, comm=comm)
    return (*outs, moved)


def _softmax_rows(s):
    p = jnp.exp(s - jnp.max(s, axis=-1, keepdims=True))
    return p, jnp.sum(p, axis=-1, keepdims=True)


def _fox_probs(q, k, cr_ref, hh, qi, tq):
    q0 = qi * tq
    pieces = ([(0, q0)] if qi else []) + [(q0, q0 + tq)]
    ss = []
    for a, b in pieces:
        s = _dot(q, k[a:b], NT) - cr_ref[hh, :, pl.ds(a, b - a)]
        if a == q0:
            causal = lax.broadcasted_iota(jnp.int32, s.shape, 1) <= lax.broadcasted_iota(jnp.int32, s.shape, 0)
            s = jnp.where(causal, s, -jnp.inf)
        ss.append(s)
    m = functools.reduce(jnp.maximum, [jnp.max(s, axis=-1, keepdims=True) for s in ss])
    ps = [jnp.exp(s - m) for s in ss]
    l = functools.reduce(jnp.add, [jnp.sum(p, axis=-1, keepdims=True) for p in ps])
    return [(a, b, p) for (a, b), p in zip(pieces, ps)], l


def _fox_specs(T):
    w = 2 * FOX_HEAD_DIM
    n = MAIN_WIDTH // w
    col = lambda off: pl.BlockSpec((T, w), lambda p, off=off: (0, off + p))
    cr = pl.BlockSpec((2, 1, T), lambda p: (p, 0, 0))
    gain = pl.BlockSpec((1, FOX_HEAD_DIM), lambda p: (0, 0))
    return n, col, cr, gain


def _fox_fwd(name, proj, kvf, cr, gq, gk, comm=None):
    T = proj.shape[0]
    tq = min(T, 256)
    hd = FOX_HEAD_DIM
    scale = hd ** -0.5

    def body(q_ref, g_ref, k_ref, v_ref, cr_ref, gq_ref, gk_ref, main_ref, o_ref):
        for hh in range(2):
            lanes = pl.ds(hh * hd, hd)
            k = _rms(k_ref[:, lanes], gk_ref[...]).astype(BF16)
            v = v_ref[:, lanes].astype(BF16)
            for qi in range(T // tq):
                rows = pl.ds(qi * tq, tq)
                q = (_rms(q_ref[rows, lanes], gq_ref[...]) * scale).astype(BF16)
                ps, l = _fox_probs(q, k, cr_ref, hh, qi, tq)
                o = functools.reduce(jnp.add, [_dot(p.astype(BF16), v[a:b], NN) for a, b, p in ps]) / l
                o_ref[rows, lanes] = o
                main_ref[rows, lanes] = o * _sigmoid(g_ref[rows, lanes])

    n, col, crs, gain = _fox_specs(T)
    (main, o), moved = _call(
        body, [proj, proj, kvf, kvf, cr, gq, gk], [jax.ShapeDtypeStruct((T, MAIN_WIDTH), F32)] * 2, name=name, grid=(n,),
        in_specs=[col(0), col(n), col(0), col(n), crs, gain, gain], out_specs=[col(0), col(0)], comm=comm)
    return main, o, moved


def _fox_bwd(name, proj, kvf, cr, gq, gk, o, dmix, pdk, pdv, pdc, comm=None):
    T = proj.shape[0]
    tq = min(T, 256)
    hd = FOX_HEAD_DIM
    scale = hd ** -0.5

    def body(q_ref, g_ref, k_ref, v_ref, cr_ref, gq_ref, gk_ref, o_ref, dm_ref, pdk_ref, pdv_ref, pdc_ref,
             dq_ref, dg_ref, dk_ref, dv_ref, dc_ref, dgq_ref, dgk_ref, dka, dva, dca):
        dgq = jnp.zeros((1, hd), F32)
        dgk = jnp.zeros((1, hd), F32)
        for hh in range(2):
            lanes = pl.ds(hh * hd, hd)
            k32, vjp_k = jax.vjp(_rms, k_ref[:, lanes], gk_ref[...])
            k = k32.astype(BF16)
            v = v_ref[:, lanes].astype(BF16)
            dka[...] = jnp.zeros(dka.shape, F32)
            dva[...] = jnp.zeros(dva.shape, F32)
            dca[...] = jnp.zeros(dca.shape, F32)
            for qi in range(T // tq):
                rows = pl.ds(qi * tq, tq)
                q32, vjp_q = jax.vjp(_rms, q_ref[rows, lanes], gq_ref[...])
                q = (q32 * scale).astype(BF16)
                ps, l = _fox_probs(q, k, cr_ref, hh, qi, tq)
                ps = [(a, b, p / l) for a, b, p in ps]
                sg = _sigmoid(g_ref[rows, lanes])
                dm = dm_ref[rows, lanes]
                do = (dm * sg).astype(BF16)
                dg_ref[rows, lanes] = (dm * o_ref[rows, lanes] * sg * (1.0 - sg)).astype(dg_ref.dtype)
                dps = [_dot(do, v[a:b], NT) for a, b, _ in ps]
                delta = functools.reduce(jnp.add, [jnp.sum(p * dp, axis=-1, keepdims=True) for (_, _, p), dp in zip(ps, dps)])
                dq = jnp.zeros((tq, hd), F32)
                for (a, b, p), dp in zip(ps, dps):
                    ds = p * (dp - delta)
                    dsb = ds.astype(BF16)
                    dq = dq + _dot(dsb, k[a:b], NN)
                    dka[:, pl.ds(a, b - a)] += _dot(q, dsb, TN)
                    dva[:, pl.ds(a, b - a)] += _dot(do, p.astype(BF16), TN)
                    dca[:, pl.ds(a, b - a)] -= jnp.sum(ds, axis=0, keepdims=True)
                dqr, dgq_p = vjp_q(dq * scale)
                dq_ref[rows, lanes] = dqr.astype(dq_ref.dtype)
                dgq = dgq + dgq_p
            dkr, dgk_p = vjp_k(dka[...].T)
            dgk = dgk + dgk_p
            dk_ref[:, lanes] = dkr + pdk_ref[:, lanes]
            dv_ref[:, lanes] = dva[...].T + pdv_ref[:, lanes]
            dc_ref[hh] = dca[...] + pdc_ref[hh]

        @pl.when(pl.program_id(0) == 0)
        def _():
            dgq_ref[...] = jnp.zeros(dgq_ref.shape, F32)
            dgk_ref[...] = jnp.zeros(dgk_ref.shape, F32)
        dgq_ref[...] += dgq
        dgk_ref[...] += dgk

    n, col, crs, gain = _fox_specs(T)
    wide = jax.ShapeDtypeStruct((T, MAIN_WIDTH), F32)
    half = jax.ShapeDtypeStruct((T, MAIN_WIDTH), BF16)
    outs, moved = _call(
        body, [proj, proj, kvf, kvf, cr, gq, gk, o, dmix, pdk, pdv, pdc],
        [half, half, wide, wide, jax.ShapeDtypeStruct((FOX_HEADS, 1, T), F32),
         jax.ShapeDtypeStruct((1, hd), F32), jax.ShapeDtypeStruct((1, hd), F32)],
        name=name, grid=(n,),
        in_specs=[col(0), col(n), col(0), col(n), crs, gain, gain, col(0), col(0), col(0), col(0), crs],
        out_specs=[col(0), col(0), col(0), col(0), crs, gain, gain],
        scratch=[pltpu.VMEM((hd, T), F32), pltpu.VMEM((hd, T), F32), pltpu.VMEM((1, T), F32)], comm=comm)
    return (*outs, moved)


def _mem_specs(T, width):
    tq = min(T, 512)
    q = pl.BlockSpec((tq, MEM_WIDTH), lambda i, c=(width - MEM_WIDTH) // MEM_WIDTH: (i, c))
    gain = pl.BlockSpec((1, MEM_HEAD_DIM), lambda i: (0, 0))
    return tq, q, gain


def _mem_fwd(name, proj, kv, gq, gk):
    T, W = proj.shape
    hd = MEM_HEAD_DIM
    tq, qspec, gain = _mem_specs(T, W)

    def body(q_ref, kv_ref, gq_ref, gk_ref, o_ref):
        for h in range(MEM_HEADS):
            lanes = pl.ds(h * hd, hd)
            q = _rms(q_ref[:, lanes], gq_ref[...]).astype(BF16)
            k = _rms(kv_ref[:, lanes], gk_ref[...]).astype(BF16)
            v = kv_ref[:, pl.ds(MEM_WIDTH + h * hd, hd)].astype(BF16)
            p, l = _softmax_rows(_dot(q, k, NT) * (hd ** -0.5))
            o_ref[:, lanes] = _dot(p.astype(BF16), v, NN) / l

    return pl.pallas_call(
        body, grid=(T // tq,),
        in_specs=[qspec, pl.BlockSpec(kv.shape, lambda i: (0, 0)), gain, gain],
        out_specs=pl.BlockSpec((tq, MEM_WIDTH), lambda i: (i, 0)),
        out_shape=jax.ShapeDtypeStruct((T, MEM_WIDTH), F32),
        name=name, compiler_params=_params(1))(proj, kv, gq, gk)


def _mem_bwd(name, proj, kv, gq, gk, dmix):
    T, W = proj.shape
    hd = MEM_HEAD_DIM
    scale = hd ** -0.5
    tq, qspec, gain = _mem_specs(T, W)

    def body(q_ref, kv_ref, gq_ref, gk_ref, dm_ref, dq_ref, dkv_ref, dgq_ref, dgk_ref):
        @pl.when(pl.program_id(0) == 0)
        def _():
            dkv_ref[...] = jnp.zeros(dkv_ref.shape, F32)
            dgq_ref[...] = jnp.zeros(dgq_ref.shape, F32)
            dgk_ref[...] = jnp.zeros(dgk_ref.shape, F32)
        for h in range(MEM_HEADS):
            lanes = pl.ds(h * hd, hd)
            vl = pl.ds(MEM_WIDTH + h * hd, hd)
            q32, vjp_q = jax.vjp(_rms, q_ref[:, lanes], gq_ref[...])
            k32, vjp_k = jax.vjp(_rms, kv_ref[:, lanes], gk_ref[...])
            q, k, v = q32.astype(BF16), k32.astype(BF16), kv_ref[:, vl].astype(BF16)
            p, l = _softmax_rows(_dot(q, k, NT) * scale)
            p = p / l
            do = dm_ref[:, lanes].astype(BF16)
            dp = _dot(do, v, NT)
            dsb = (p * (dp - jnp.sum(p * dp, axis=-1, keepdims=True))).astype(BF16)
            dqr, dgq_p = vjp_q(_dot(dsb, k, NN) * scale)
            dkr, dgk_p = vjp_k(_dot(dsb, q, TN) * scale)
            dq_ref[:, lanes] = dqr.astype(dq_ref.dtype)
            dkv_ref[:, lanes] += dkr
            dkv_ref[:, vl] += _dot(p.astype(BF16), do, TN)
            dgq_ref[...] += dgq_p
            dgk_ref[...] += dgk_p

    return pl.pallas_call(
        body, grid=(T // tq,),
        in_specs=[qspec, pl.BlockSpec(kv.shape, lambda i: (0, 0)), gain, gain,
                  pl.BlockSpec((tq, MEM_WIDTH), lambda i: (i, MAIN_WIDTH // MEM_WIDTH))],
        out_specs=[pl.BlockSpec((tq, MEM_WIDTH), lambda i: (i, 0)), pl.BlockSpec(kv.shape, lambda i: (0, 0)), gain, gain],
        out_shape=[jax.ShapeDtypeStruct((T, MEM_WIDTH), BF16), jax.ShapeDtypeStruct(kv.shape, F32),
                   jax.ShapeDtypeStruct((1, hd), F32), jax.ShapeDtypeStruct((1, hd), F32)],
        name=name, compiler_params=_params(1))(proj, kv, gq, gk, dmix)


def _cumsum_rows(name, x, reverse=False):
    T, C = x.shape
    pt = min(T, 256)

    def body(x_ref, o_ref):
        r = lax.broadcasted_iota(jnp.int32, (pt, pt), 0)
        c = lax.broadcasted_iota(jnp.int32, (pt, pt), 1)
        tri = ((c >= r) if reverse else (c <= r)).astype(F32)
        carry = jnp.zeros((1, C), F32)
        order = range(T // pt)
        for p in (reversed(order) if reverse else order):
            rows = pl.ds(p * pt, pt)
            blk = x_ref[rows, :]
            o_ref[rows, :] = _dot(tri, blk, NN, precision=lax.Precision.HIGHEST) + carry
            carry = carry + jnp.sum(blk, axis=0, keepdims=True)

    return pl.pallas_call(body, out_shape=jax.ShapeDtypeStruct((T, C), F32), name=name,
                          compiler_params=pltpu.CompilerParams(vmem_limit_bytes=V7X_VMEM_LIMIT))(x)


MESH = pl.DeviceIdType.MESH
ANY = pl.BlockSpec(memory_space=pl.ANY)


def _mesh_pos():
    return lax.axis_index("x"), lax.axis_index("y"), lax.axis_index("c")


def _all_gather(name, xs):
    n = len(xs)

    def body(*refs):
        x_refs, out_refs = refs[:n], refs[n:2 * n]
        send_sems, recv_sems, local_sems = refs[2 * n:]
        mx, my, mc = _mesh_pos()
        me, sibling = (mx, my, mc), (mx, my, 1 - mc)
        chips = [(1 - mx, my), (mx, 1 - my), (1 - mx, 1 - my)]

        def slot(a, px, py, pc):
            return out_refs[a].at[4 * px + 2 * py + pc]

        def copy(a, k, block, to, src=None):
            return pltpu.make_async_remote_copy(
                src_ref=slot(a, *block) if src is None else src, dst_ref=slot(a, *block),
                send_sem=send_sems.at[7 * a + k], recv_sem=recv_sems.at[7 * a + k], device_id=to, device_id_type=MESH)

        mine = [pltpu.make_async_copy(x_refs[a], slot(a, *me), local_sems.at[a]) for a in range(n)]
        first = []
        for a in range(n):
            mine[a].start()
            first.append(copy(a, 0, me, sibling, src=x_refs[a]))
            first += [copy(a, 1 + j, me, (*chip, mc), src=x_refs[a]) for j, chip in enumerate(chips)]
        for cp in first:
            cp.start()
        passed = []
        for j, chip in enumerate(chips):
            for a in range(n):
                copy(a, 1 + j, (*chip, mc), me).wait_recv()
                passed.append(copy(a, 4 + j, (*chip, mc), sibling))
                passed[-1].start()
        for a in range(n):
            copy(a, 0, sibling, me).wait_recv()
            for j, chip in enumerate(chips):
                copy(a, 4 + j, (*chip, 1 - mc), me).wait_recv()
        for cp in first + passed:
            cp.wait_send()
        for cp in mine:
            cp.wait()

    return pl.pallas_call(
        body, out_shape=[jax.ShapeDtypeStruct((N_DEV,) + x.shape, x.dtype) for x in xs], in_specs=[ANY] * n, out_specs=[ANY] * n,
        scratch_shapes=[pltpu.SemaphoreType.DMA((7 * n,)), pltpu.SemaphoreType.DMA((7 * n,)), pltpu.SemaphoreType.DMA((n,))],
        name=name)(*xs)


def _exchange_cores(name, gs):
    n = len(gs)

    def body(*refs):
        g_refs, recv_refs = refs[:n], refs[n:2 * n]
        send_sems, recv_sems = refs[2 * n:]
        mx, my, mc = _mesh_pos()
        swap = [pltpu.make_async_remote_copy(
            src_ref=g_refs[a].at[2 * q + (1 - mc)], dst_ref=recv_refs[a].at[q], send_sem=send_sems.at[4 * a + q],
            recv_sem=recv_sems.at[4 * a + q], device_id=(mx, my, 1 - mc), device_id_type=MESH) for a in range(n) for q in range(4)]
        for cp in swap:
            cp.start()
        for cp in swap:
            cp.wait()

    return pl.pallas_call(
        body, out_shape=[jax.ShapeDtypeStruct((4,) + g.shape[1:], g.dtype) for g in gs], in_specs=[ANY] * n, out_specs=[ANY] * n,
        scratch_shapes=[pltpu.SemaphoreType.DMA((4 * n,)), pltpu.SemaphoreType.DMA((4 * n,))],
        name=name)(*gs)


def _exchange_chips(name, ss):
    n = len(ss)

    def body(*refs):
        s_refs, recv_refs = refs[:n], refs[n:2 * n]
        send_sems, recv_sems, local_sems = refs[2 * n:]
        mx, my, mc = _mesh_pos()
        myq = 2 * mx + my
        chips = [(1 - mx, my), (mx, 1 - my), (1 - mx, 1 - my)]
        mine = [pltpu.make_async_copy(s_refs[a].at[myq], recv_refs[a].at[myq], local_sems.at[a]) for a in range(n)]
        for cp in mine:
            cp.start()
        swap = [pltpu.make_async_remote_copy(
            src_ref=s_refs[a].at[2 * px + py], dst_ref=recv_refs[a].at[myq], send_sem=send_sems.at[3 * a + k],
            recv_sem=recv_sems.at[3 * a + k], device_id=(px, py, mc), device_id_type=MESH)
            for a in range(n) for k, (px, py) in enumerate(chips)]
        for cp in swap:
            cp.start()
        for a in range(n):
            for k, (px, py) in enumerate(chips):
                pltpu.make_async_remote_copy(
                    src_ref=s_refs[a].at[myq], dst_ref=recv_refs[a].at[2 * px + py], send_sem=send_sems.at[3 * a + k],
                    recv_sem=recv_sems.at[3 * a + k], device_id=(px, py, mc), device_id_type=MESH).wait_recv()
        for cp in swap:
            cp.wait_send()
        for cp in mine:
            cp.wait()

    return pl.pallas_call(
        body, out_shape=[jax.ShapeDtypeStruct(s.shape, s.dtype) for s in ss], in_specs=[ANY] * n, out_specs=[ANY] * n,
        scratch_shapes=[pltpu.SemaphoreType.DMA((3 * n,)), pltpu.SemaphoreType.DMA((3 * n,)), pltpu.SemaphoreType.DMA((n,))],
        name=name)(*ss)


def _pair_sum(name, g, recv, mc):
    _, R, C = g.shape
    tm = _row_tile(R, 512)

    def body(mc_ref, own_ref, recv_ref, o_ref):
        o_ref[...] = (own_ref[...].astype(F32) + recv_ref[...].astype(F32)).astype(o_ref.dtype)

    spec = pl.BlockSpec((None, tm, C), lambda q, i, mc_ref: (q, i, 0))
    grid_spec = pltpu.PrefetchScalarGridSpec(
        num_scalar_prefetch=1, grid=(4, R // tm),
        in_specs=[pl.BlockSpec((None, tm, C), lambda q, i, mc_ref: (2 * q + mc_ref[0], i, 0)), spec], out_specs=spec)
    return pl.pallas_call(body, grid_spec=grid_spec, out_shape=jax.ShapeDtypeStruct((4, R, C), BF16), name=name,
                          compiler_params=_params(2))(mc, g, recv)


def _row_tile(R, cap):
    best = None
    for t in range(8, min(R, cap) + 1, 8):
        if R % t == 0:
            best = t
    return best or R


def _sum_slabs(name, a, out_dtype):
    n, R, C = a.shape
    tm = _row_tile(R, 512)

    def body(*refs):
        acc = refs[0][...].astype(F32)
        for r in refs[1:n]:
            acc = acc + r[...].astype(F32)
        refs[n][...] = acc.astype(out_dtype)

    return pl.pallas_call(
        body, grid=(R // tm,),
        in_specs=[pl.BlockSpec((None, tm, C), lambda i, q=q: (q, i, 0)) for q in range(n)],
        out_specs=pl.BlockSpec((tm, C), lambda i: (i, 0)), out_shape=jax.ShapeDtypeStruct((R, C), out_dtype),
        name=name, compiler_params=_params(1))(*([a] * n))


def _reduce_scatter(gs):
    mc = lax.axis_index("c").astype(jnp.int32).reshape(1)
    recvs = _exchange_cores("rs_cores", gs)
    pairs = [_pair_sum(f"rs_pair_sum{a}", g, r, mc) for a, (g, r) in enumerate(zip(gs, recvs))]
    return [_sum_slabs(f"rs_chip_sum{a}", r, F32) for a, r in enumerate(_exchange_chips("rs_chips", pairs))]


SMALL = ["ffn1_norm", "mix_norm", "mem_norm", "mem_q_gain", "mem_k_gain", "hgrn_o_gain", "fox_q_gain", "kv_norm",
         "fox_f_bias", "fox_k_gain", "ffn2_norm"]
COLS352 = ["ffn1_w_gate", "ffn1_w_up", "ffn2_w_gate", "ffn2_w_up"]
KV_SPLIT = 1024
KV_WIDTH = 2 * MAIN_WIDTH + FOX_HEADS


def _rows2d(w):
    return w.reshape(-1, w.shape[-1])


def _pad_cols(w, width):
    return jnp.pad(w, [(0, 0)] * (w.ndim - 1) + [(0, width - w.shape[-1])])


def _pack_rows1024(down1, down2, w_out, w_mem_kv, w_kv):
    kv = jnp.concatenate([w_kv[:, :KV_SPLIT], _pad_cols(w_kv[:, KV_SPLIT:], D_MODEL)], axis=0)
    return jnp.concatenate([_rows2d(down1), _rows2d(down2), _rows2d(w_out), _rows2d(_pad_cols(w_mem_kv, D_MODEL)), kv], axis=0)


def _unpack_rows1024(buf, shapes):
    out, off = [], 0
    for name in ("ffn1_w_down", "ffn2_w_down", "w_out", "w_mem_kv"):
        L, r, c = shapes[name]
        out.append(buf[off:off + L * r].reshape(L, r, D_MODEL)[:, :, :c])
        off += L * r
    r, c = shapes["w_kv"]
    out.append(jnp.concatenate([buf[off:off + r], buf[off + r:off + 2 * r, :c - KV_SPLIT]], axis=1))
    return out


def _pad128(a):
    flat = a.reshape(-1)
    return jnp.pad(flat, (0, -flat.shape[0] % LANES))


def _small_pack(parts):
    flat = jnp.concatenate([_pad128(p) for p in parts])
    rows = -(-flat.shape[0] // LANES)
    flat = jnp.pad(flat, (0, (-rows % 8) * LANES))
    return flat.reshape(-1, LANES)


def _small_unpack(buf, shapes):
    flat = buf.reshape(-1)
    out, off = [], 0
    for s in shapes:
        n = 1
        for d in s:
            n *= d
        out.append(flat[off:off + n].reshape(s))
        off += n + (-n % LANES)
    return out


def _lb_fn(l0, l1):
    m = lax.stop_gradient(jnp.maximum(l0, l1))
    e0, e1 = jnp.exp(l0 - m), jnp.exp(l1 - m)
    p0, p1 = e0 / (e0 + e1), e1 / (e0 + e1)
    return p0 - p0, (p0 + p1) - p0


def _lb_fwd(logits):
    return _rowwise("lb", _lb_fn, [logits[0:1], logits[1:2]], [], [(MAIN_WIDTH, F32)] * 2)


def _lb_bwd(logits, dlb0, dlb1):
    def fn(l0, l1, d0, d1):
        _, vjp = jax.vjp(_lb_fn, l0, l1)
        return vjp((d0, d1))
    return _rowwise("lb_bwd", fn, [logits[0:1], logits[1:2], dlb0, dlb1], [], [(MAIN_WIDTH, F32)] * 2)


def _adamw_fn(w, g, m, v):
    m = ADAM_B1 * m + (1.0 - ADAM_B1) * g
    v = ADAM_B2 * v + (1.0 - ADAM_B2) * jnp.square(g)
    m_hat = m / (1.0 - ADAM_B1 ** ADAM_STEP)
    v_hat = v / (1.0 - ADAM_B2 ** ADAM_STEP)
    return -ADAM_LR * (m_hat / (jnp.sqrt(v_hat) + ADAM_EPS) + ADAM_WD * w), m, v


def _sum_adamw(name, landed, w, m, v, comm=None):
    L, r, c = w.shape
    tm = _row_tile(r, 128)
    n_i = r // tm

    def body(*refs):
        land, (w_ref, m_ref, v_ref), outs = refs[:L], refs[L:L + 3], refs[L + 3:]
        for k in range(L):
            @pl.when(pl.program_id(0) == k)
            def _(k=k):
                g = land[k][0].astype(F32)
                for s in range(1, N_DEV):
                    g = g + land[k][s].astype(F32)
                for ref, val in zip(outs, (g,) + _adamw_fn(w_ref[...], g, m_ref[...], v_ref[...])):
                    ref[...] = val

    held = lambda k: (lambda l, i: (0, jnp.where(l < k, 0, jnp.where(l == k, i, n_i - 1)), 0))
    cur = pl.BlockSpec((None, tm, c), lambda l, i: (l, i, 0))
    return _call(body, [*landed, w, m, v], [jax.ShapeDtypeStruct((L, r, c), F32)] * 4, name=name, grid=(L, n_i),
                 in_specs=[pl.BlockSpec((N_DEV, tm, c), held(k)) for k in range(L)] + [cur] * 3, out_specs=[cur] * 4, comm=comm)


def _adamw(name, w, g, m, v):
    shape = w.shape
    C = shape[-1]
    two = lambda a: a.reshape(-1, C)
    R = two(w).shape[0]
    outs = _rowwise(name, _adamw_fn, [two(w), two(g), two(m), two(v)], [], [(C, F32)] * 3, tm=_row_tile(R, 512))
    return [o.reshape(shape) for o in outs]


def _whole_rows(g, r0, r1):
    return g[:, r0:r1].reshape(N_DEV * (r1 - r0), g.shape[2])


def _w_out_of(Wl):
    n = Wl["d2"].shape[1]
    return _whole_rows(Wl["r1"], n, n + LANES)


def _w_mem_kv_of(Wl):
    n = Wl["d2"].shape[1]
    return _whole_rows(Wl["r1"], n + LANES, n + 2 * LANES)[:, :2 * MEM_WIDTH]


def _mixer_fwd(l, x1, mem, G, W, lbs, shared, local, units):
    T = x1.shape[0]
    tag = f"l{l}"
    Wl = G[l]
    h = _rms_fwd(tag + "_mixrms", x1, W["mix_norm"][l:l + 1])
    mem_n = _rms_fwd(tag + "_memrms", mem, W["mem_norm"][l:l + 1])
    kv = _mm(tag + "_memkv", [(mem_n, _w_mem_kv_of(Wl), NN)], [F32], mem.shape[0], 2 * MEM_WIDTH)
    proj, moved = _proj_cols(tag + "_in", h, Wl["win"], 0, comm=_Comm(relay=[G[l + 1]["gu1"]]) if l + 1 < len(G) else None)
    if moved:
        G[l + 1]["gu1"] = moved[0]
    along = _Comm(gather=[local[n][k] for n, k in units])
    if l < 2:
        main, o, moved = _hgrn_fwd(tag + "_hgrn", proj, lbs[l], W["hgrn_o_gain"][l:l + 1], comm=along)
    else:
        main, o, moved = _fox_fwd(tag + "_fox", proj, shared["kvf"], shared["cr"], W["fox_q_gain"][l - 2:l - 1],
                                  W["fox_k_gain"], comm=along)
    for (n, k), m in zip(units, moved):
        G[n][k] = m
    mem_o = _mem_fwd(tag + "_mem", proj, kv, W["mem_q_gain"][l:l + 1], W["mem_k_gain"][l:l + 1])
    w_out = _w_out_of(Wl)
    x2 = _mm(tag + "_out", [(main, w_out[:MAIN_WIDTH], NN), (mem_o, w_out[MAIN_WIDTH:], NN)], [F32], T, D_MODEL,
             epi=lambda a, e: (e[0] + a[0] + a[1],), extras=[x1])
    return x2, dict(h=h, mem_n=mem_n, kv=kv, proj=proj, main=main, o=o, mem_o=mem_o)


def _mixer_bwd(l, x1, mem, Wl, W, lbs, shared, sv, dx2, acc, ready, landed):
    T = x1.shape[0]
    tag = f"l{l}b"
    w_out = _w_out_of(Wl)
    g = {}
    dmix = _mm(tag + "_dmix", [(dx2, w_out, NT)], [F32], T, D_MODEL)
    dw_out = jnp.concatenate([
        _mm(tag + "_dwout_a", [(sv["main"], dx2, TN)], [BF16], MAIN_WIDTH, D_MODEL),
        _mm(tag + "_dwout_b", [(sv["mem_o"], dx2, TN)], [BF16], MEM_WIDTH, D_MODEL)], axis=0).reshape(N_DEV, -1, D_MODEL)
    dqm, dkv, g["mem_q_gain"], g["mem_k_gain"] = _mem_bwd(tag + "_mem", sv["proj"], sv["kv"], W["mem_q_gain"][l:l + 1],
                                                           W["mem_k_gain"][l:l + 1], dmix)
    along = _Comm(scatter=[v for _, v in ready])
    if l < 2:
        dq, df, di, dg, g["lb"], g["hgrn_o_gain"], moved = _hgrn_bwd(tag + "_hgrn", sv["proj"], sv["o"], dmix, lbs[l],
                                                                      W["hgrn_o_gain"][l:l + 1], comm=along)
        dproj = jnp.concatenate([dq, df, di, dg, dqm], axis=1)
    else:
        dq, dgate, acc["dk"], acc["dv"], acc["dc"], g["fox_q_gain"], g["fox_k_gain"], moved = _fox_bwd(
            tag + "_fox", sv["proj"], shared["kvf"], shared["cr"], W["fox_q_gain"][l - 2:l - 1], W["fox_k_gain"],
            sv["o"], dmix, acc["dk"], acc["dv"], acc["dc"], comm=along)
        dproj = jnp.concatenate([dq, dgate, dqm], axis=1)
    landed.update({k: m for (k, _), m in zip(ready, moved)})
    dh, dw_in, moved = _proj_cols_bwd(tag + "_in", sv["h"], dproj, Wl["win"], 0, comm=_Comm(scatter=[dw_out]))
    landed[(l, "w_out")] = moved[0]
    dx1, g["mix_norm"] = _rms_bwd(tag + "_mixrms", x1, W["mix_norm"][l:l + 1], dh, dres=dx2)
    dw_mem_kv = _mm(tag + "_dwmemkv", [(sv["mem_n"], dkv, TN)], [BF16], D_MODEL, 2 * MEM_WIDTH)
    dmem_n = _mm(tag + "_dmemn", [(dkv, _w_mem_kv_of(Wl), NT)], [F32], mem.shape[0], D_MODEL)
    _, g["mem_norm"] = _rms_bwd(tag + "_memrms", mem, W["mem_norm"][l:l + 1], dmem_n)
    return dx1, g, [((l, "w_in"), dw_in), ((l, "w_mem_kv"), dw_mem_kv.reshape(N_DEV, -1, 2 * MEM_WIDTH))]


def _forget_cols(kvf):
    return kvf[:, 2 * MAIN_WIDTH:2 * MAIN_WIDTH + LANES]


def _log_forget(kvf, bias):
    return _rowwise("kv_logf", lambda f, b: jax.nn.log_sigmoid(f + b), [_forget_cols(kvf)], [bias], [(LANES, F32)])[0]


def _move(name, comm):
    def body(o_ref):
        o_ref[...] = jnp.zeros(o_ref.shape, F32)
    _, moved = _call(body, [], [jax.ShapeDtypeStruct((8, LANES), F32)], name=name, in_specs=[],
                     out_specs=[pl.BlockSpec(memory_space=pltpu.VMEM)], comm=comm)
    return moved


def _step(x, mem, target, W, lb_logits, G0, local):
    T = x.shape[0]
    W = dict(W, fox_k_gain=W["fox_k_gain"].reshape(1, -1))
    lbs = _lb_fwd(lb_logits)
    fox_bias = jnp.pad(W["fox_f_bias"], (0, LANES - FOX_HEADS)).reshape(1, LANES)
    w_kv = W["w_kv"]
    n_l = len(local)
    ffn1 = lambda l, Wl: (W["ffn1_norm"][l:l + 1], Wl["gu1"], Wl["r1"], 0, 1, 0)
    ffn2 = lambda l, Wl: (W["ffn2_norm"][l:l + 1], Wl["gu2"], Wl["d2"], 0, 1, 0)

    on_ffn1 = lambda l: [(l + 1, "gu1")] if l + 1 < n_l else []
    on_mix = {0: [(1, "r1"), (1, "win"), (1, "gu2"), (3, "d2")], 1: [(2, "r1"), (2, "win"), (2, "gu2"), (3, "gu2")],
              2: [(3, "r1"), (3, "win")], 3: []}
    on_ffn2 = {0: [(1, "d2")], 1: [(2, "d2")], 2: [], 3: []}
    saved, shared, G = [], {}, [G0] + [{} for _ in range(n_l - 1)]
    for l in range(n_l):
        Wl = G[l]
        relay = on_ffn2[l - 1] if l else []
        along = _Comm(gather=[local[n][k] for n, k in on_ffn1(l)], relay=[G[n][k] for n, k in relay])
        x1, moved = _ffn_fwd(f"l{l}_ffn1", x, *ffn1(l, Wl), comm=along)
        for (n, k), m in zip(on_ffn1(l) + relay, moved):
            G[n][k] = m
        x2, sv = _mixer_fwd(l, x1, mem, G, W, lbs, shared, local, on_mix[l])
        along = _Comm(gather=[local[n][k] for n, k in on_ffn2[l]], relay=[G[n][k] for n, k in on_mix[l]])
        x3, moved = _ffn_fwd(f"l{l}_ffn2", x2, *ffn2(l, G[l]), comm=along)
        for (n, k), m in zip(on_ffn2[l] + on_mix[l], moved):
            G[n][k] = m
        sv.update(x=x, x1=x1, x2=x2)
        saved.append(sv)
        x = x3
        if l == 1:
            hk = _rms_fwd("kv_rms", x, W["kv_norm"].reshape(1, -1))
            kvf = _mm("kv_proj", [(hk, w_kv, NN)], [F32], T, w_kv.shape[1])
            cum = _cumsum_rows("kv_cum", _log_forget(kvf, fox_bias))[:, :FOX_HEADS].T
            shared = dict(kvf=kvf, cr=cum[:, None, :], hk=hk, x=x)

    def loss_fn(y, t):
        err = y - t
        return err * (1.0 / D_MODEL), jnp.sum(0.5 / D_MODEL * err * err, axis=0, keepdims=True)
    dx, loss = _rowwise("loss", loss_fn, [x, target], [], [(D_MODEL, F32)], [((1, D_MODEL), F32)])

    grads = [None] * n_l
    acc = dict(dk=jnp.zeros((T, MAIN_WIDTH), F32), dv=jnp.zeros((T, MAIN_WIDTH), F32), dc=jnp.zeros((FOX_HEADS, 1, T), F32))
    gkv = {}
    landed, late = {}, []
    for l in reversed(range(n_l)):
        sv, Wl = saved[l], G[l]
        ready = []
        if l == 1:
            dcum = jnp.pad(acc["dc"][:, 0, :].T, ((0, 0), (0, LANES - FOX_HEADS)))
            dlf = _cumsum_rows("kv_dcum", dcum, reverse=True)
            def dlogf_fn(d, f, b):
                p = d * _sigmoid(-(f + b))
                return p, jnp.sum(p, axis=0, keepdims=True)
            dfl, gkv["fox_f_bias"] = _rowwise("kv_dlogf", dlogf_fn, [dlf, _forget_cols(shared["kvf"])], [fox_bias],
                                              [(LANES, BF16)], [((1, LANES), F32)])
            dkvf = _pad_cols(jnp.concatenate([acc["dk"].astype(BF16), acc["dv"].astype(BF16), dfl], axis=1), w_kv.shape[1])
            dw_kv = _mm("kv_dw", [(shared["hk"], dkvf, TN)], [BF16], D_MODEL, dkvf.shape[1])
            ready.append(((0, "w_kv"), dw_kv[:, :KV_WIDTH].reshape(N_DEV, -1, KV_WIDTH)))
            dhk = _mm("kv_dh", [(dkvf, w_kv, NT)], [F32], T, D_MODEL)
            dx, gkv["kv_norm"] = _rms_bwd("kv_rmsb", shared["x"], W["kv_norm"].reshape(1, -1), dhk, dres=dx)
        g = {}
        if l < 2:
            ready, late = ready + late, []
        dx2, g["ffn2_norm"], dwg, dwu, dwd, moved_a, moved_w = _ffn_bwd(
            f"l{l}b_ffn2", sv["x2"], *ffn2(l, Wl), dx, comm_a=_Comm(scatter=[v for _, v in late[:2]]),
            comm_w=_Comm(scatter=[v for _, v in late[2:]]))
        landed.update({k: m for (k, _), m in zip(late, moved_a + moved_w)})
        ready += [((l, "ffn2_w_gate"), dwg), ((l, "ffn2_w_up"), dwu), ((l, "ffn2_w_down"), dwd)]
        dx1, gm, rest = _mixer_bwd(l, sv["x1"], mem, Wl, W, lbs, shared, sv, dx2, acc, ready, landed)
        g.update(gm)
        dx, g["ffn1_norm"], dwg, dwu, dwd, moved_a, _ = _ffn_bwd(f"l{l}b_ffn1", sv["x"], *ffn1(l, Wl), dx1,
                                                                 comm_a=_Comm(scatter=[v for _, v in rest]))
        landed.update({k: m for (k, _), m in zip(rest, moved_a)})
        late = [((l, "ffn1_w_gate"), dwg), ((l, "ffn1_w_up"), dwu), ((l, "ffn1_w_down"), dwd)]
        grads[l] = g

    out = {}
    for n in ["ffn1_norm", "mix_norm", "mem_norm", "mem_q_gain", "mem_k_gain", "ffn2_norm"]:
        out[n] = jnp.concatenate([grads[l][n] for l in range(4)], axis=0)
    out["hgrn_o_gain"] = jnp.concatenate([grads[l]["hgrn_o_gain"] for l in (0, 1)], axis=0)
    out["fox_q_gain"] = jnp.concatenate([grads[l]["fox_q_gain"] for l in (2, 3)], axis=0)
    out["fox_k_gain"] = (grads[2]["fox_k_gain"] + grads[3]["fox_k_gain"]).reshape(-1)
    out["kv_norm"] = gkv["kv_norm"].reshape(-1)
    out["fox_f_bias"] = gkv["fox_f_bias"][0, :FOX_HEADS]
    dl0, dl1 = _lb_bwd(lb_logits, grads[0]["lb"], grads[1]["lb"])
    out["hgrn_lb_logits"] = jnp.concatenate([dl0, dl1], axis=0)
    return loss, dx, out, landed, late


WEIGHTS = ["ffn1_norm", "ffn1_w_gate", "ffn1_w_up", "ffn1_w_down", "mix_norm", "mem_norm", "w_mem_kv", "mem_q_gain",
           "mem_k_gain", "w_in_a", "hgrn_lb_logits", "hgrn_o_gain", "w_in_b", "fox_q_gain", "kv_norm", "w_kv", "fox_f_bias",
           "fox_k_gain", "w_out", "ffn2_norm", "ffn2_w_gate", "ffn2_w_up", "ffn2_w_down"]
BIG = COLS352 + ["w_in_a", "w_in_b", "ffn1_w_down", "ffn2_w_down", "w_out", "w_mem_kv", "w_kv"]


def _train_step(a):
    bf = lambda w: w.astype(BF16)
    n_l = a["w_out"].shape[0]
    local = []
    for l in range(n_l):
        w_in = a["w_in_a"][l] if l < a["w_in_a"].shape[0] else a["w_in_b"][l - a["w_in_a"].shape[0]]
        local.append(dict(
            gu1=bf(jnp.concatenate([a["ffn1_w_gate"][l], a["ffn1_w_up"][l]], axis=0)),
            r1=bf(jnp.concatenate([a["ffn1_w_down"][l], a["w_out"][l], _pad_cols(a["w_mem_kv"][l], D_MODEL)], axis=0)),
            win=bf(w_in),
            gu2=bf(jnp.concatenate([a["ffn2_w_gate"][l], a["ffn2_w_up"][l]], axis=0)),
            d2=bf(a["ffn2_w_down"][l])))
    keys = ["gu1", "r1", "win", "gu2", "d2"]
    first = _all_gather("ag_first", [local[0][k] for k in keys] + [bf(a["w_kv"]), _small_pack([a["hgrn_lb_logits"]])])
    G0 = dict(zip(keys, first))
    W = {n: a[n] for n in SMALL}
    W["w_kv"] = _pad_cols(first[5].reshape(-1, a["w_kv"].shape[1]), 2 * D_MODEL)
    lb_shape = a["hgrn_lb_logits"].shape
    lb_all = first[6].reshape(N_DEV, -1)[:, :lb_shape[0] * lb_shape[1]]
    lb_logits = lb_all.reshape((N_DEV,) + lb_shape).transpose(1, 0, 2).reshape(lb_shape[0], -1)

    loss_part, dx, g, landed, tail = _step(a["x"][0], a["mem"][0], a["loss_target"][0], W, lb_logits, G0, local)

    n_a = a["w_in_a"].shape[0]
    grad, delta, new_m, new_v = {}, {}, {}, {}
    order = [n for n in BIG if n.startswith("ffn2")] + [n for n in BIG if not n.startswith("ffn")]
    for n in order + [n for n in BIG if n.startswith("ffn1")]:
        ls = range(n_a) if n == "w_in_a" else range(n_a, n_l) if n == "w_in_b" else range(1) if n == "w_kv" else range(n_l)
        key = "w_in" if n.startswith("w_in") else n
        lead = (lambda t: t[None]) if a[n].ndim == 2 else (lambda t: t)
        riding, tail = tail[:1], tail[1:]
        res, moved = _sum_adamw("adam_" + n, [landed[(l, key)] for l in ls], lead(a[n]), lead(a["m_" + n]), lead(a["v_" + n]),
                                comm=_Comm(scatter=[v for _, v in riding]))
        landed.update({k: m for (k, _), m in zip(riding, moved)})
        grad[n], delta[n], new_m[n], new_v[n] = [t.reshape(a[n].shape) for t in res]

    zeros = [jnp.zeros(lb_logits.shape, F32), jnp.zeros(loss_part.shape, F32)]
    small_shapes = [a[n].shape for n in SMALL] + [lb_logits.shape, loss_part.shape]
    small_part = _small_pack([g[n] for n in SMALL] + [g["hgrn_lb_logits"], loss_part])
    small_sum = _sum_slabs("small_sum", _all_gather("ag_small", [small_part])[0], F32)
    small = _small_unpack(small_sum, small_shapes)
    grad.update(dict(zip(SMALL, small)))
    loss = jnp.sum(small[-1])
    me = 4 * lax.axis_index("x") + 2 * lax.axis_index("y") + lax.axis_index("c")
    grad["hgrn_lb_logits"] = lax.dynamic_slice_in_dim(small[-2], me * lb_shape[1], lb_shape[1], axis=1)

    n = "hgrn_lb_logits"
    delta[n], new_m[n], new_v[n] = _adamw("adam_" + n, a[n], grad[n], a["m_" + n], a["v_" + n])
    packs = [_small_pack([a[p + n] for n in SMALL] + zeros) for p in ("", "m_", "v_")]
    upd = _rowwise("adam_small", _adamw_fn, [packs[0], small_sum, packs[1], packs[2]], [], [(LANES, F32)] * 3, tm=packs[0].shape[0])
    for d, u in zip((delta, new_m, new_v), upd):
        d.update(dict(zip(SMALL, _small_unpack(u, small_shapes))))
    return (loss, dx[None], *[grad[n] for n in WEIGHTS], *[delta[n] for n in WEIGHTS], *[new_m[n] for n in WEIGHTS],
            *[new_v[n] for n in WEIGHTS])


def kernel(x, mem, ffn1_norm, ffn1_w_gate, ffn1_w_up, ffn1_w_down, mix_norm, mem_norm, w_mem_kv, mem_q_gain, mem_k_gain, w_in_a, hgrn_lb_logits, hgrn_o_gain, w_in_b, fox_q_gain, kv_norm, w_kv, fox_f_bias, fox_k_gain, w_out, ffn2_norm, ffn2_w_gate, ffn2_w_up, ffn2_w_down, loss_target, m_ffn1_norm, m_ffn1_w_gate, m_ffn1_w_up, m_ffn1_w_down, m_mix_norm, m_mem_norm, m_w_mem_kv, m_mem_q_gain, m_mem_k_gain, m_w_in_a, m_hgrn_lb_logits, m_hgrn_o_gain, m_w_in_b, m_fox_q_gain, m_kv_norm, m_w_kv, m_fox_f_bias, m_fox_k_gain, m_w_out, m_ffn2_norm, m_ffn2_w_gate, m_ffn2_w_up, m_ffn2_w_down, v_ffn1_norm, v_ffn1_w_gate, v_ffn1_w_up, v_ffn1_w_down, v_mix_norm, v_mem_norm, v_w_mem_kv, v_mem_q_gain, v_mem_k_gain, v_w_in_a, v_hgrn_lb_logits, v_hgrn_o_gain, v_w_in_b, v_fox_q_gain, v_kv_norm, v_w_kv, v_fox_f_bias, v_fox_k_gain, v_w_out, v_ffn2_norm, v_ffn2_w_gate, v_ffn2_w_up, v_ffn2_w_down):
    return _train_step(dict(locals()))
```

```python
import functools

import jax
import jax.numpy as jnp
from jax import lax
from jax.experimental import pallas as pl
from jax.experimental.pallas import tpu as pltpu

F32, BF16 = jnp.float32, jnp.bfloat16
EPS = 1e-6
V7X_VMEM_LIMIT = 56 * 1024 * 1024
LANES = 128
N_DEV = 8

D_MODEL = 1024
MAIN_WIDTH = 768
MEM_WIDTH = 256
HG_HEAD_DIM = 128
HG_HEADS = 6
FOX_HEAD_DIM = 64
FOX_HEADS = 12
MEM_HEADS = 4
MEM_HEAD_DIM = 64
HG_BLOCK = 16

ADAM_LR, ADAM_B1, ADAM_B2, ADAM_EPS, ADAM_WD, ADAM_STEP = 0.001, 0.9, 0.999, 1e-08, 0.01, 10

NN = ((1,), (0,))
NT = ((1,), (1,))
TN = ((0,), (0,))


def _dot(a, b, dims, precision=None):
    return lax.dot_general(a, b, (dims, ((), ())), preferred_element_type=F32, precision=precision)


def _bdot(a, b, dims):
    return _dot(a.astype(BF16), b.astype(BF16), dims)


def _split(a):
    hi = a.astype(BF16)
    return hi, (a - hi.astype(F32)).astype(BF16)


def _fdot(a, b, dims):
    ah, al = _split(a)
    bh, bl = _split(b)
    return _dot(ah, bh, dims) + (_dot(ah, bl, dims) + _dot(al, bh, dims))


def _params(n_grid):
    return pltpu.CompilerParams(dimension_semantics=("arbitrary",) * n_grid, vmem_limit_bytes=V7X_VMEM_LIMIT)


def _rms(x, g):
    return x * lax.rsqrt(jnp.mean(x * x, axis=-1, keepdims=True) + EPS) * g


def _sigmoid(x):
    return jax.nn.sigmoid(x)


def _silu(x):
    return x * jax.nn.sigmoid(x)


MESH = pl.DeviceIdType.MESH
ANY = pl.BlockSpec(memory_space=pl.ANY)


def _mesh_pos():
    return lax.axis_index("x"), lax.axis_index("y"), lax.axis_index("c")


class _Comm:
    def __init__(self, gather=(), relay=(), scatter=()):
        self.gather, self.relay, self.scatter = list(gather), list(relay), list(scatter)
        self.arrays = self.gather + self.relay + self.scatter
        self.n_remote = 4 * len(self.gather) + 3 * len(self.relay) + 7 * len(self.scatter)
        self.n_local = len(self.gather) + len(self.scatter)

    def out_shapes(self):
        return ([jax.ShapeDtypeStruct((N_DEV,) + x.shape, x.dtype) for x in self.gather]
                + [jax.ShapeDtypeStruct(g.shape, g.dtype) for g in self.relay + self.scatter])

    def scratch(self):
        return [pltpu.SemaphoreType.DMA((self.n_remote,)), pltpu.SemaphoreType.DMA((self.n_remote,)),
                pltpu.SemaphoreType.DMA((max(self.n_local, 1),))]

    def _copies(self, ins, outs, send, recv, local, arrivals=True):
        mx, my, mc = _mesh_pos()
        flip = lambda v, f: 1 - v if f else v
        idx = lambda p: 4 * p[0] + 2 * p[1] + p[2]
        me = (mx, my, mc)
        count = [0, 0]
        loc, out, arrive = [], [], []

        def pair(src, dst, lands, to):
            k = count[0]
            count[0] += 1
            mk = lambda d: pltpu.make_async_remote_copy(src_ref=src, dst_ref=d, send_sem=send.at[k], recv_sem=recv.at[k],
                                                        device_id=to, device_id_type=MESH)
            out.append(mk(dst))
            if arrivals:
                arrive.append(mk(lands))

        def local_copy(src, dst):
            loc.append(pltpu.make_async_copy(src, dst, local.at[count[1]]))
            count[1] += 1

        refs = list(zip(ins, outs))
        near = [(0, 0, 1), (1, 0, 0), (0, 1, 0), (1, 1, 0)]
        for x, G in refs[:len(self.gather)]:
            local_copy(x, G.at[idx(me)])
            for f in near:
                peer = tuple(flip(v, b) for v, b in zip(me, f))
                pair(x, G.at[idx(me)], G.at[idx(peer)], peer)
        sibling = (mx, my, 1 - mc)
        for Gin, Gout in refs[len(self.gather):len(self.gather) + len(self.relay)]:
            for f in near[1:]:
                chip = (flip(mx, f[0]), flip(my, f[1]))
                pair(Gin.at[idx((*chip, mc))], Gout.at[idx((*chip, mc))], Gout.at[idx((*chip, 1 - mc))], sibling)
        every = near + [(1, 0, 1), (0, 1, 1), (1, 1, 1)]
        for g, R in refs[len(self.gather) + len(self.relay):]:
            local_copy(g.at[idx(me)], R.at[idx(me)])
            for f in every:
                peer = tuple(flip(v, b) for v, b in zip(me, f))
                pair(g.at[idx(peer)], R.at[idx(me)], R.at[idx(peer)], peer)
        return loc, out, arrive

    def start(self, ins, outs, send, recv, local):
        loc, out, _ = self._copies(ins, outs, send, recv, local, arrivals=False)
        for cp in loc + out:
            cp.start()

    def finish(self, ins, outs, send, recv, local):
        loc, out, arrive = self._copies(ins, outs, send, recv, local)
        for cp in arrive:
            cp.wait_recv()
        for cp in out:
            cp.wait_send()
        for cp in loc:
            cp.wait()


def _call(body, operands, out_shape, *, name, grid=(), in_specs=None, out_specs=None, scratch=(), comm=None):
    outs = list(out_shape) if isinstance(out_shape, (list, tuple)) else [out_shape]
    single = not isinstance(out_shape, (list, tuple))
    params = _params(len(grid))
    if comm is None or not comm.arrays:
        res = pl.pallas_call(body, grid=grid, in_specs=in_specs, out_specs=out_specs, out_shape=out_shape,
                             scratch_shapes=list(scratch), name=name, compiler_params=params)(*operands)
        return ([res] if single else list(res)), []
    n_in, n_out, n_s, n_c = len(operands), len(outs), len(scratch), len(comm.arrays)

    def wrapped(*refs):
        pos = [0]

        def take(n):
            pos[0] += n
            return refs[pos[0] - n:pos[0]]

        b_in, c_in, b_out, c_out, b_s, sems = take(n_in), take(n_c), take(n_out), take(n_c), take(n_s), take(3)
        ids = [pl.program_id(d) for d in range(len(grid))]
        first, last = True, True
        for d, i in enumerate(ids):
            first = (i == 0) & first
            last = (i == grid[d] - 1) & last
        if grid:
            pl.when(first)(lambda: comm.start(c_in, c_out, *sems))
        else:
            comm.start(c_in, c_out, *sems)
        body(*b_in, *b_out, *b_s)
        if grid:
            pl.when(last)(lambda: comm.finish(c_in, c_out, *sems))
        else:
            comm.finish(c_in, c_out, *sems)

    n_g = len(comm.gather)
    aliases = {n_in + n_g + r: n_out + n_g + r for r in range(len(comm.relay))}
    out_specs_l = list(out_specs) if isinstance(out_specs, (list, tuple)) else [out_specs]
    res = pl.pallas_call(
        wrapped, grid=grid, in_specs=list(in_specs) + [ANY] * n_c, out_specs=out_specs_l + [ANY] * n_c,
        out_shape=outs + comm.out_shapes(), scratch_shapes=list(scratch) + comm.scratch(), input_output_aliases=aliases,
        name=name, compiler_params=params)(*operands, *comm.arrays)
    return list(res[:n_out]), list(res[n_out:])


def _rowwise(name, fn, rows, consts, out_rows, out_reds=(), tm=512):
    R = rows[0].shape[0]
    tm = min(tm, R)
    assert R % tm == 0
    n_in, n_o = len(rows) + len(consts), len(out_rows)

    def body(*refs):
        outs = fn(*[r[...] for r in refs[:n_in]])
        if not isinstance(outs, (tuple, list)):
            outs = (outs,)
        for r, o in zip(refs[n_in:n_in + n_o], outs[:n_o]):
            r[...] = o.astype(r.dtype)
        red_refs = refs[n_in + n_o:]
        if red_refs:
            @pl.when(pl.program_id(0) == 0)
            def _():
                for r in red_refs:
                    r[...] = jnp.zeros(r.shape, r.dtype)
            for r, o in zip(red_refs, outs[n_o:]):
                r[...] += o

    zero = lambda n: (lambda i: (0,) * n)
    in_specs = [pl.BlockSpec((tm, a.shape[1]), lambda i: (i, 0)) for a in rows]
    in_specs += [pl.BlockSpec(c.shape, zero(c.ndim)) for c in consts]
    out_specs = [pl.BlockSpec((tm, c), lambda i: (i, 0)) for c, _ in out_rows]
    out_specs += [pl.BlockSpec(s, zero(len(s))) for s, _ in out_reds]
    out_shape = [jax.ShapeDtypeStruct((R, c), dt) for c, dt in out_rows]
    out_shape += [jax.ShapeDtypeStruct(s, dt) for s, dt in out_reds]
    return pl.pallas_call(body, grid=(R // tm,), in_specs=in_specs, out_specs=out_specs, out_shape=out_shape,
                          name=name, compiler_params=_params(1))(*rows, *consts)


def _tile(n, cap):
    best = None
    for t in range(LANES, min(n, cap) + 1, LANES):
        if n % t == 0:
            best = t
    return best or n


def _mm(name, pairs, out_dtypes, M, N, epi=None, extras=(), tm=512, tn=512):
    tm, tn = _tile(M, tm), _tile(N, tn)
    n_p, n_e = len(pairs), len(extras)
    modes = [m for _, _, m in pairs]

    def body(*refs):
        accs = [_bdot(refs[2 * k][...], refs[2 * k + 1][...], modes[k]) for k in range(n_p)]
        ex = [r[...] for r in refs[2 * n_p:2 * n_p + n_e]]
        outs = epi(accs, ex) if epi is not None else accs
        for r, o in zip(refs[2 * n_p + n_e:], outs):
            r[...] = o.astype(r.dtype)

    in_specs = []
    ops = []
    for a, b, mode in pairs:
        if mode == NN:
            K = a.shape[1]
            assert a.shape == (M, K) and b.shape == (K, N), (name, a.shape, b.shape)
            in_specs += [pl.BlockSpec((tm, K), lambda i, j: (i, 0)), pl.BlockSpec((K, tn), lambda i, j: (0, j))]
        elif mode == NT:
            K = a.shape[1]
            assert a.shape == (M, K) and b.shape == (N, K), (name, a.shape, b.shape)
            in_specs += [pl.BlockSpec((tm, K), lambda i, j: (i, 0)), pl.BlockSpec((tn, K), lambda i, j: (j, 0))]
        else:
            K = a.shape[0]
            assert a.shape == (K, M) and b.shape == (K, N), (name, a.shape, b.shape)
            in_specs += [pl.BlockSpec((K, tm), lambda i, j: (0, i)), pl.BlockSpec((K, tn), lambda i, j: (0, j))]
        ops += [a, b]
    in_specs += [pl.BlockSpec((tm, tn), lambda i, j: (i, j)) for _ in extras]
    out_specs = [pl.BlockSpec((tm, tn), lambda i, j: (i, j)) for _ in out_dtypes]
    out_shape = [jax.ShapeDtypeStruct((M, N), dt) for dt in out_dtypes]
    res = pl.pallas_call(body, grid=(M // tm, N // tn), in_specs=in_specs, out_specs=out_specs, out_shape=out_shape,
                         name=name, compiler_params=_params(2))(*ops, *extras)
    return res[0] if len(res) == 1 else res


def _rms_fwd(name, x, gain, dtype=BF16):
    return _rowwise(name, _rms, [x], [gain], [(x.shape[1], dtype)])[0]


def _rms_bwd(name, x, gain, dh, dres=None):
    def fn(x, dh, *rest):
        g = rest[-1]
        _, vjp = jax.vjp(_rms, x, g)
        dx, dg = vjp(dh)
        if dres is not None:
            dx = dx + rest[0]
        return dx, dg
    rows = [x, dh] + ([dres] if dres is not None else [])
    d = x.shape[1]
    return _rowwise(name, fn, rows, [gain], [(d, F32)], [((1, d), F32)])


def _ffn_specs(gcols, grows, ig, iu, idn):
    n = gcols.shape[2]
    D = grows.shape[2]
    wg = pl.BlockSpec((None, D, n), lambda i, j: (j, ig, 0))
    wu = pl.BlockSpec((None, D, n), lambda i, j: (j, iu, 0))
    wd = pl.BlockSpec((None, n, D), lambda i, j: (j, idn, 0))
    return n, wg, wu, wd


def _ffn_fwd(name, x, gain, gcols, grows, ig, iu, idn, tm=1024, comm=None):
    T, D = x.shape
    tm = min(T, tm)
    n, wg_s, wu_s, wd_s = _ffn_specs(gcols, grows, ig, iu, idn)
    last = N_DEV - 1

    def body(x_ref, g_ref, wg_ref, wu_ref, wd_ref, y_ref, h_s, acc):
        j = pl.program_id(1)

        @pl.when(j == 0)
        def _():
            h_s[...] = _rms(x_ref[...], g_ref[...]).astype(BF16)
            acc[...] = jnp.zeros(acc.shape, F32)
        h = h_s[...]
        z = _silu(_dot(h, wg_ref[...], NN)) * _dot(h, wu_ref[...], NN)
        acc[...] += _dot(z.astype(BF16), wd_ref[...], NN)

        @pl.when(j == last)
        def _():
            y_ref[...] = x_ref[...] + 0.5 * acc[...]

    row = pl.BlockSpec((tm, D), lambda i, j: (i, 0))
    (y,), moved = _call(
        body, [x, gain, gcols, gcols, grows], [jax.ShapeDtypeStruct((T, D), F32)], name=name, grid=(T // tm, N_DEV),
        in_specs=[row, pl.BlockSpec((1, D), lambda i, j: (0, 0)), wg_s, wu_s, wd_s], out_specs=[row],
        scratch=[pltpu.VMEM((tm, D), BF16), pltpu.VMEM((tm, D), F32)], comm=comm)
    return y, moved


def _ffn_bwd(tag, x, gain, gcols, grows, ig, iu, idn, dy, tm=512, comm_a=None, comm_w=None):
    T, D = x.shape
    tm = min(T, tm)
    n, wg_s, wu_s, wd_s = _ffn_specs(gcols, grows, ig, iu, idn)
    last = N_DEV - 1

    def body(x_ref, dy_ref, g_ref, wg_ref, wu_ref, wd_ref, dx_ref, dg_ref, h_ref, z_ref, da_ref, db_ref, dh_acc):
        i, j = pl.program_id(0), pl.program_id(1)

        @pl.when(j == 0)
        def _():
            h_ref[...] = _rms(x_ref[...], g_ref[...]).astype(BF16)
            dh_acc[...] = jnp.zeros(dh_acc.shape, F32)

        @pl.when((i == 0) & (j == 0))
        def _():
            dg_ref[...] = jnp.zeros(dg_ref.shape, F32)
        h = h_ref[...]
        a, b = _dot(h, wg_ref[...], NN), _dot(h, wu_ref[...], NN)
        dz = 0.5 * _dot(dy_ref[...].astype(BF16), wd_ref[...], NT)
        s = _sigmoid(a)
        si = a * s
        da = (dz * b * (s + si * (1.0 - s))).astype(BF16)
        db = (dz * si).astype(BF16)
        z_ref[...] = (si * b).astype(BF16)
        da_ref[...] = da
        db_ref[...] = db
        dh_acc[...] += _dot(da, wg_ref[...], NT) + _dot(db, wu_ref[...], NT)

        @pl.when(j == last)
        def _():
            _, vjp = jax.vjp(_rms, x_ref[...], g_ref[...])
            dx, dg = vjp(dh_acc[...])
            dx_ref[...] = dx + dy_ref[...]
            dg_ref[...] += dg

    row = pl.BlockSpec((tm, D), lambda i, j: (i, 0))
    vec = pl.BlockSpec((1, D), lambda i, j: (0, 0))
    hid = pl.BlockSpec((None, tm, n), lambda i, j: (j, i, 0))
    hidden = jax.ShapeDtypeStruct((N_DEV, T, n), BF16)
    (dx, dgain, h, z, da, db), moved_a = _call(
        body, [x, dy, gain, gcols, gcols, grows],
        [jax.ShapeDtypeStruct((T, D), F32), jax.ShapeDtypeStruct((1, D), F32), jax.ShapeDtypeStruct((T, D), BF16),
         hidden, hidden, hidden],
        name=tag + "_a", grid=(T // tm, N_DEV), in_specs=[row, row, vec, wg_s, wu_s, wd_s],
        out_specs=[row, vec, row, hid, hid, hid], scratch=[pltpu.VMEM((tm, D), F32)], comm=comm_a)

    def wbody(h_ref, dy_ref, z_ref, da_ref, db_ref, dwg_ref, dwu_ref, dwd_ref):
        h = h_ref[...]
        dwg_ref[...] = _dot(h, da_ref[...], TN).astype(BF16)
        dwu_ref[...] = _dot(h, db_ref[...], TN).astype(BF16)
        dwd_ref[...] = (0.5 * _dot(z_ref[...], dy_ref[...].astype(BF16), TN)).astype(BF16)

    full = pl.BlockSpec((T, D), lambda j: (0, 0))
    hid_all = pl.BlockSpec((None, T, n), lambda j: (j, 0, 0))
    (dwg, dwu, dwd), moved_w = _call(
        wbody, [h, dy, z, da, db],
        [jax.ShapeDtypeStruct((N_DEV, D, n), BF16)] * 2 + [jax.ShapeDtypeStruct((N_DEV, n, D), BF16)],
        name=tag + "_w", grid=(N_DEV,), in_specs=[full, full, hid_all, hid_all, hid_all],
        out_specs=[pl.BlockSpec((None, D, n), lambda j: (j, 0, 0))] * 2 + [pl.BlockSpec((None, n, D), lambda j: (j, 0, 0))],
        comm=comm_w)
    return dx, dgain, dwg, dwu, dwd, moved_a, moved_w


def _wcols_spec(gw, l, grid_rank):
    _, _, n = gw.shape
    K = D_MODEL
    zero = (lambda i: (0, l, 0)) if grid_rank == 1 else (lambda i, j: (0, l, 0))
    return n, K, pl.BlockSpec((N_DEV, K, n), zero)


def _proj_cols(name, h, gw, l, tm=512, comm=None):
    T = h.shape[0]
    tm = min(T, tm)
    n, K, wspec = _wcols_spec(gw, l, 1)

    def body(h_ref, w_ref, o_ref):
        h = h_ref[...]
        for j in range(N_DEV):
            o_ref[:, pl.ds(j * n, n)] = _dot(h, w_ref[j], NN)

    (proj,), moved = _call(
        body, [h, gw], [jax.ShapeDtypeStruct((T, N_DEV * n), F32)], name=name, grid=(T // tm,),
        in_specs=[pl.BlockSpec((tm, K), lambda i: (i, 0)), wspec], out_specs=[pl.BlockSpec((tm, N_DEV * n), lambda i: (i, 0))],
        comm=comm)
    return proj, moved


def _proj_cols_bwd(tag, h, dproj, gw, l, tm=512, tk=512, comm=None):
    T = h.shape[0]
    tm = min(T, tm)
    n, K, wspec = _wcols_spec(gw, l, 1)

    def dh_body(dp_ref, w_ref, o_ref):
        acc = jnp.zeros(o_ref.shape, F32)
        for j in range(N_DEV):
            acc = acc + _dot(dp_ref[:, pl.ds(j * n, n)], w_ref[j], NT)
        o_ref[...] = acc

    (dh,), moved = _call(
        dh_body, [dproj, gw], [jax.ShapeDtypeStruct((T, K), F32)], name=tag + "_dh", grid=(T // tm,),
        in_specs=[pl.BlockSpec((tm, N_DEV * n), lambda i: (i, 0)), wspec], out_specs=[pl.BlockSpec((tm, K), lambda i: (i, 0))],
        comm=comm)

    def dw_body(h_ref, dp_ref, o_ref):
        h = h_ref[...]
        for j in range(N_DEV):
            o_ref[j] = _dot(h, dp_ref[:, pl.ds(j * n, n)], TN).astype(BF16)

    dw = pl.pallas_call(
        dw_body, grid=(K // tk,), in_specs=[pl.BlockSpec((T, tk), lambda i: (0, i)), pl.BlockSpec((T, N_DEV * n), lambda i: (0, 0))],
        out_specs=pl.BlockSpec((N_DEV, tk, n), lambda i: (0, i, 0)), out_shape=jax.ShapeDtypeStruct((N_DEV, K, n), BF16),
        name=tag + "_dw", compiler_params=_params(1))(h, dproj)
    return dh, dw, moved


def _block_tri(n, reverse=False):
    r = lax.broadcasted_iota(jnp.int32, (n, n), 0)
    c = lax.broadcasted_iota(jnp.int32, (n, n), 1)
    same = (r // HG_BLOCK) == (c // HG_BLOCK)
    return (same & ((c >= r) if reverse else (c <= r))).astype(F32)


def _hgrn_prep(q_ref, f_ref, lbv, qs, ks, cs, T):
    pt = min(T, 256)
    tri = _block_tri(pt)
    for p in range(T // pt):
        rows = pl.ds(p * pt, pt)
        f = lbv + (1.0 - lbv) * _sigmoid(f_ref[rows, :])
        qs[rows, :] = _silu(q_ref[rows, :])
        ks[rows, :] = 1.0 - f
        cs[rows, :] = _dot(tri, jnp.log(f), NN, precision=lax.Precision.HIGHEST)


HG_GROUP = 128


def _groups_loop(nb, fn):
    gp = HG_GROUP if nb % HG_GROUP == 0 else nb

    def step(i, carry):
        base = pl.multiple_of(i * (gp * HG_BLOCK), gp * HG_BLOCK)
        fn(lambda t: pl.ds(base + t, gp, stride=HG_BLOCK))
        return carry

    lax.fori_loop(0, nb // gp, step, 0)


def _gate_out(o, og, g):
    return _rms(o, og) * _silu(g)


HG_UNROLL = 16


def _block_rows(n):
    return pl.ds(pl.multiple_of(n * HG_BLOCK, HG_BLOCK), HG_BLOCK)


def _blocks_loop(nb, fn):
    u = HG_UNROLL if nb % HG_UNROLL == 0 else 1

    def step(i, carry):
        for k in range(u):
            fn(i * u + k)
        return carry

    lax.fori_loop(0, nb // u, step, 0)


def _scan_states(buf, cs, nb, reverse=False):
    def step(m, st):
        n = nb - 1 - m if reverse else m
        own = buf[n]
        buf[n] = st
        rows = _block_rows(n)
        return jnp.exp(cs[rows, :][HG_BLOCK - 1:HG_BLOCK, :]) * st + own

    lax.fori_loop(0, nb, step, jnp.zeros(buf.shape[1:], F32))


def _hgrn_states(i_ref, ks, cs, states, nb):
    def own_step(n):
        rows = _block_rows(n)
        c = cs[rows, :]
        states[n] = _fdot(i_ref[rows, :], ks[rows, :] * jnp.exp(c[HG_BLOCK - 1:HG_BLOCK, :] - c), TN)

    _blocks_loop(nb, own_step)
    _scan_states(states, cs, nb)


def _hgrn_fwd(name, proj, lb, og, comm=None):
    T = proj.shape[0]
    nb = T // HG_BLOCK
    hd = HG_HEAD_DIM

    def body(q_ref, f_ref, i_ref, g_ref, lb_ref, og_ref, main_ref, o_ref, qs, ks, cs, states):
        _hgrn_prep(q_ref, f_ref, lb_ref[...], qs, ks, cs, T)
        def pairs(at):
            for t in range(HG_BLOCK):
                qt, ct = qs[at(t), :], cs[at(t), :]
                acc = jnp.zeros(qt.shape, F32)
                for s in range(t + 1):
                    w = qt * ks[at(s), :] * jnp.exp(ct - cs[at(s), :])
                    acc = acc + jnp.sum(w, axis=-1, keepdims=True) * i_ref[at(s), :]
                o_ref[at(t), :] = acc

        _groups_loop(nb, pairs)

        _hgrn_states(i_ref, ks, cs, states, nb)

        def out_step(n):
            rows = _block_rows(n)
            o_ref[rows, :] += _fdot(qs[rows, :] * jnp.exp(cs[rows, :]), states[n], NT)

        _blocks_loop(nb, out_step)
        pt = min(T, 256)
        for p in range(T // pt):
            rows = pl.ds(p * pt, pt)
            main_ref[rows, :] = _gate_out(o_ref[rows, :], og_ref[...], g_ref[rows, :])

    nh = HG_HEADS
    col = lambda off: pl.BlockSpec((T, hd), lambda h, off=off: (0, off + h))
    (main, o), moved = _call(
        body, [proj, proj, proj, proj, lb, og], [jax.ShapeDtypeStruct((T, MAIN_WIDTH), F32)] * 2, name=name, grid=(nh,),
        in_specs=[col(0), col(nh), col(2 * nh), col(3 * nh), pl.BlockSpec((1, hd), lambda h: (0, h)),
                  pl.BlockSpec((1, hd), lambda h: (0, 0))],
        out_specs=[col(0), col(0)], scratch=[pltpu.VMEM((T, hd), F32)] * 3 + [pltpu.VMEM((nb, hd, hd), F32)], comm=comm)
    return main, o, moved


def _hgrn_bwd(name, proj, o, dmix, lb, og, comm=None):
    T = proj.shape[0]
    nb = T // HG_BLOCK
    hd = HG_HEAD_DIM
    pt = min(T, 256)

    def body(q_ref, f_ref, i_ref, g_ref, o_ref, dm_ref, lb_ref, og_ref,
             dq_ref, df_ref, di_ref, dg_ref, dlb_ref, dog_ref, qs, ks, cs, dos, dqs, dks, dvs, states, behind):
        lbv = lb_ref[...]
        _hgrn_prep(q_ref, f_ref, lbv, qs, ks, cs, T)
        dog = jnp.zeros((1, hd), F32)
        for p in range(T // pt):
            rows = pl.ds(p * pt, pt)
            _, vjp = jax.vjp(_gate_out, o_ref[rows, :], og_ref[...], g_ref[rows, :])
            do, dog_p, dg = vjp(dm_ref[rows, :])
            dos[rows, :] = do
            dg_ref[rows, :] = dg.astype(dg_ref.dtype)
            dog = dog + dog_p

        @pl.when(pl.program_id(0) == 0)
        def _():
            dog_ref[...] = jnp.zeros(dog_ref.shape, F32)
        dog_ref[...] += dog

        def pairs(at):
            for t in range(HG_BLOCK):
                dqs[at(t), :] = jnp.zeros((HG_GROUP if nb % HG_GROUP == 0 else nb, hd), F32)
            for s in range(HG_BLOCK):
                k_s, c_s, v_s = ks[at(s), :], cs[at(s), :], i_ref[at(s), :]
                dk = jnp.zeros(k_s.shape, F32)
                dv = jnp.zeros(k_s.shape, F32)
                for t in range(s, HG_BLOCK):
                    q_t, do_t = qs[at(t), :], dos[at(t), :]
                    e = jnp.exp(cs[at(t), :] - c_s)
                    a = jnp.sum(q_t * k_s * e, axis=-1, keepdims=True)
                    g = jnp.sum(do_t * v_s, axis=-1, keepdims=True)
                    dqs[at(t), :] += g * k_s * e
                    dk = dk + g * q_t * e
                    dv = dv + a * do_t
                dks[at(s), :] = dk
                dvs[at(s), :] = dv

        _groups_loop(nb, pairs)

        _hgrn_states(i_ref, ks, cs, states, nb)

        def own_step(n):
            rows = _block_rows(n)
            behind[n] = _fdot(dos[rows, :], qs[rows, :] * jnp.exp(cs[rows, :]), TN)

        _blocks_loop(nb, own_step)
        _scan_states(behind, cs, nb, reverse=True)

        def grad_step(n):
            rows = _block_rows(n)
            c = cs[rows, :]
            ec, ek = jnp.exp(c), jnp.exp(c[HG_BLOCK - 1:HG_BLOCK, :] - c)
            dst = behind[n]
            dqs[rows, :] += _fdot(dos[rows, :], states[n], NN) * ec
            dks[rows, :] += _fdot(i_ref[rows, :], dst, NN) * ek
            dvs[rows, :] += _fdot(ks[rows, :] * ek, dst, NT)

        _blocks_loop(nb, grad_step)

        full = (lax.broadcasted_iota(jnp.int32, (pt, pt), 1) >= lax.broadcasted_iota(jnp.int32, (pt, pt), 0)).astype(F32)
        carry = jnp.zeros((1, hd), F32)
        dlb = jnp.zeros((1, hd), F32)
        for p in reversed(range(T // pt)):
            rows = pl.ds(p * pt, pt)
            q, k, dq, dk = qs[rows, :], ks[rows, :], dqs[rows, :], dks[rows, :]
            db = q * dq - k * dk
            dlf = _dot(full, db, NN, precision=lax.Precision.HIGHEST) + carry
            carry = carry + jnp.sum(db, axis=0, keepdims=True)
            sg = _sigmoid(f_ref[rows, :])
            df = dlf / (1.0 - k) - dk
            df_ref[rows, :] = (df * (1.0 - lbv) * sg * (1.0 - sg)).astype(df_ref.dtype)
            dlb = dlb + jnp.sum(df * (1.0 - sg), axis=0, keepdims=True)
            qr = q_ref[rows, :]
            sq = _sigmoid(qr)
            dq_ref[rows, :] = (dq * (sq + qr * sq * (1.0 - sq))).astype(dq_ref.dtype)
            di_ref[rows, :] = dvs[rows, :].astype(di_ref.dtype)
        dlb_ref[...] = dlb

    nh = HG_HEADS
    col = lambda off: pl.BlockSpec((T, hd), lambda h, off=off: (0, off + h))
    vec = pl.BlockSpec((1, hd), lambda h: (0, h))
    one = pl.BlockSpec((1, hd), lambda h: (0, 0))
    outs, moved = _call(
        body, [proj, proj, proj, proj, o, dmix, lb, og],
        [jax.ShapeDtypeStruct((T, MAIN_WIDTH), BF16)] * 4
        + [jax.ShapeDtypeStruct((1, MAIN_WIDTH), F32), jax.ShapeDtypeStruct((1, hd), F32)],
        name=name, grid=(nh,), in_specs=[col(0), col(nh), col(2 * nh), col(3 * nh), col(0), col(0), vec, one],
        out_specs=[col(0), col(0), col(0), col(0), vec, one],
        scratch=[pltpu.VMEM((T, hd), F32)] * 7 + [pltpu.VMEM((nb, hd, hd), F32)] * 2, comm=comm)
    return (*outs, moved)


def _softmax_rows(s):
    p = jnp.exp(s - jnp.max(s, axis=-1, keepdims=True))
    return p, jnp.sum(p, axis=-1, keepdims=True)


def _fox_probs(q, k, cr_ref, hh, qi, tq):
    q0 = qi * tq
    pieces = ([(0, q0)] if qi else []) + [(q0, q0 + tq)]
    ss = []
    for a, b in pieces:
        s = _dot(q, k[a:b], NT) - cr_ref[hh, :, pl.ds(a, b - a)]
        if a == q0:
            causal = lax.broadcasted_iota(jnp.int32, s.shape, 1) <= lax.broadcasted_iota(jnp.int32, s.shape, 0)
            s = jnp.where(causal, s, -jnp.inf)
        ss.append(s)
    m = functools.reduce(jnp.maximum, [jnp.max(s, axis=-1, keepdims=True) for s in ss])
    ps = [jnp.exp(s - m) for s in ss]
    l = functools.reduce(jnp.add, [jnp.sum(p, axis=-1, keepdims=True) for p in ps])
    return [(a, b, p) for (a, b), p in zip(pieces, ps)], l


def _fox_specs(T):
    w = 2 * FOX_HEAD_DIM
    n = MAIN_WIDTH // w
    col = lambda off: pl.BlockSpec((T, w), lambda p, off=off: (0, off + p))
    cr = pl.BlockSpec((2, 1, T), lambda p: (p, 0, 0))
    gain = pl.BlockSpec((1, FOX_HEAD_DIM), lambda p: (0, 0))
    return n, col, cr, gain


def _fox_fwd(name, proj, kvf, cr, gq, gk, comm=None):
    T = proj.shape[0]
    tq = min(T, 256)
    hd = FOX_HEAD_DIM
    scale = hd ** -0.5

    def body(q_ref, g_ref, k_ref, v_ref, cr_ref, gq_ref, gk_ref, main_ref, o_ref):
        for hh in range(2):
            lanes = pl.ds(hh * hd, hd)
            k = _rms(k_ref[:, lanes], gk_ref[...]).astype(BF16)
            v = v_ref[:, lanes].astype(BF16)
            for qi in range(T // tq):
                rows = pl.ds(qi * tq, tq)
                q = (_rms(q_ref[rows, lanes], gq_ref[...]) * scale).astype(BF16)
                ps, l = _fox_probs(q, k, cr_ref, hh, qi, tq)
                o = functools.reduce(jnp.add, [_dot(p.astype(BF16), v[a:b], NN) for a, b, p in ps]) / l
                o_ref[rows, lanes] = o
                main_ref[rows, lanes] = o * _sigmoid(g_ref[rows, lanes])

    n, col, crs, gain = _fox_specs(T)
    (main, o), moved = _call(
        body, [proj, proj, kvf, kvf, cr, gq, gk], [jax.ShapeDtypeStruct((T, MAIN_WIDTH), F32)] * 2, name=name, grid=(n,),
        in_specs=[col(0), col(n), col(0), col(n), crs, gain, gain], out_specs=[col(0), col(0)], comm=comm)
    return main, o, moved


def _fox_bwd(name, proj, kvf, cr, gq, gk, o, dmix, pdk, pdv, pdc, comm=None):
    T = proj.shape[0]
    tq = min(T, 256)
    hd = FOX_HEAD_DIM
    scale = hd ** -0.5

    def body(q_ref, g_ref, k_ref, v_ref, cr_ref, gq_ref, gk_ref, o_ref, dm_ref, pdk_ref, pdv_ref, pdc_ref,
             dq_ref, dg_ref, dk_ref, dv_ref, dc_ref, dgq_ref, dgk_ref, dka, dva, dca):
        dgq = jnp.zeros((1, hd), F32)
        dgk = jnp.zeros((1, hd), F32)
        for hh in range(2):
            lanes = pl.ds(hh * hd, hd)
            k32, vjp_k = jax.vjp(_rms, k_ref[:, lanes], gk_ref[...])
            k = k32.astype(BF16)
            v = v_ref[:, lanes].astype(BF16)
            dka[...] = jnp.zeros(dka.shape, F32)
            dva[...] = jnp.zeros(dva.shape, F32)
            dca[...] = jnp.zeros(dca.shape, F32)
            for qi in range(T // tq):
                rows = pl.ds(qi * tq, tq)
                q32, vjp_q = jax.vjp(_rms, q_ref[rows, lanes], gq_ref[...])
                q = (q32 * scale).astype(BF16)
                ps, l = _fox_probs(q, k, cr_ref, hh, qi, tq)
                ps = [(a, b, p / l) for a, b, p in ps]
                sg = _sigmoid(g_ref[rows, lanes])
                dm = dm_ref[rows, lanes]
                do = (dm * sg).astype(BF16)
                dg_ref[rows, lanes] = (dm * o_ref[rows, lanes] * sg * (1.0 - sg)).astype(dg_ref.dtype)
                dps = [_dot(do, v[a:b], NT) for a, b, _ in ps]
                delta = functools.reduce(jnp.add, [jnp.sum(p * dp, axis=-1, keepdims=True) for (_, _, p), dp in zip(ps, dps)])
                dq = jnp.zeros((tq, hd), F32)
                for (a, b, p), dp in zip(ps, dps):
                    ds = p * (dp - delta)
                    dsb = ds.astype(BF16)
                    dq = dq + _dot(dsb, k[a:b], NN)
                    dka[:, pl.ds(a, b - a)] += _dot(q, dsb, TN)
                    dva[:, pl.ds(a, b - a)] += _dot(do, p.astype(BF16), TN)
                    dca[:, pl.ds(a, b - a)] -= jnp.sum(ds, axis=0, keepdims=True)
                dqr, dgq_p = vjp_q(dq * scale)
                dq_ref[rows, lanes] = dqr.astype(dq_ref.dtype)
                dgq = dgq + dgq_p
            dkr, dgk_p = vjp_k(dka[...].T)
            dgk = dgk + dgk_p
            dk_ref[:, lanes] = dkr + pdk_ref[:, lanes]
            dv_ref[:, lanes] = dva[...].T + pdv_ref[:, lanes]
            dc_ref[hh] = dca[...] + pdc_ref[hh]

        @pl.when(pl.program_id(0) == 0)
        def _():
            dgq_ref[...] = jnp.zeros(dgq_ref.shape, F32)
            dgk_ref[...] = jnp.zeros(dgk_ref.shape, F32)
        dgq_ref[...] += dgq
        dgk_ref[...] += dgk

    n, col, crs, gain = _fox_specs(T)
    wide = jax.ShapeDtypeStruct((T, MAIN_WIDTH), F32)
    half = jax.ShapeDtypeStruct((T, MAIN_WIDTH), BF16)
    outs, moved = _call(
        body, [proj, proj, kvf, kvf, cr, gq, gk, o, dmix, pdk, pdv, pdc],
        [half, half, wide, wide, jax.ShapeDtypeStruct((FOX_HEADS, 1, T), F32),
         jax.ShapeDtypeStruct((1, hd), F32), jax.ShapeDtypeStruct((1, hd), F32)],
        name=name, grid=(n,),
        in_specs=[col(0), col(n), col(0), col(n), crs, gain, gain, col(0), col(0), col(0), col(0), crs],
        out_specs=[col(0), col(0), col(0), col(0), crs, gain, gain],
        scratch=[pltpu.VMEM((hd, T), F32), pltpu.VMEM((hd, T), F32), pltpu.VMEM((1, T), F32)], comm=comm)
    return (*outs, moved)


def _mem_specs(T, width):
    tq = min(T, 512)
    q = pl.BlockSpec((tq, MEM_WIDTH), lambda i, c=(width - MEM_WIDTH) // MEM_WIDTH: (i, c))
    gain = pl.BlockSpec((1, MEM_HEAD_DIM), lambda i: (0, 0))
    return tq, q, gain


def _mem_fwd(name, proj, kv, gq, gk):
    T, W = proj.shape
    hd = MEM_HEAD_DIM
    tq, qspec, gain = _mem_specs(T, W)

    def body(q_ref, kv_ref, gq_ref, gk_ref, o_ref):
        for h in range(MEM_HEADS):
            lanes = pl.ds(h * hd, hd)
            q = _rms(q_ref[:, lanes], gq_ref[...]).astype(BF16)
            k = _rms(kv_ref[:, lanes], gk_ref[...]).astype(BF16)
            v = kv_ref[:, pl.ds(MEM_WIDTH + h * hd, hd)].astype(BF16)
            p, l = _softmax_rows(_dot(q, k, NT) * (hd ** -0.5))
            o_ref[:, lanes] = _dot(p.astype(BF16), v, NN) / l

    return pl.pallas_call(
        body, grid=(T // tq,),
        in_specs=[qspec, pl.BlockSpec(kv.shape, lambda i: (0, 0)), gain, gain],
        out_specs=pl.BlockSpec((tq, MEM_WIDTH), lambda i: (i, 0)),
        out_shape=jax.ShapeDtypeStruct((T, MEM_WIDTH), F32),
        name=name, compiler_params=_params(1))(proj, kv, gq, gk)


def _mem_bwd(name, proj, kv, gq, gk, dmix):
    T, W = proj.shape
    hd = MEM_HEAD_DIM
    scale = hd ** -0.5
    tq, qspec, gain = _mem_specs(T, W)

    def body(q_ref, kv_ref, gq_ref, gk_ref, dm_ref, dq_ref, dkv_ref, dgq_ref, dgk_ref):
        @pl.when(pl.program_id(0) == 0)
        def _():
            dkv_ref[...] = jnp.zeros(dkv_ref.shape, F32)
            dgq_ref[...] = jnp.zeros(dgq_ref.shape, F32)
            dgk_ref[...] = jnp.zeros(dgk_ref.shape, F32)
        for h in range(MEM_HEADS):
            lanes = pl.ds(h * hd, hd)
            vl = pl.ds(MEM_WIDTH + h * hd, hd)
            q32, vjp_q = jax.vjp(_rms, q_ref[:, lanes], gq_ref[...])
            k32, vjp_k = jax.vjp(_rms, kv_ref[:, lanes], gk_ref[...])
            q, k, v = q32.astype(BF16), k32.astype(BF16), kv_ref[:, vl].astype(BF16)
            p, l = _softmax_rows(_dot(q, k, NT) * scale)
            p = p / l
            do = dm_ref[:, lanes].astype(BF16)
            dp = _dot(do, v, NT)
            dsb = (p * (dp - jnp.sum(p * dp, axis=-1, keepdims=True))).astype(BF16)
            dqr, dgq_p = vjp_q(_dot(dsb, k, NN) * scale)
            dkr, dgk_p = vjp_k(_dot(dsb, q, TN) * scale)
            dq_ref[:, lanes] = dqr.astype(dq_ref.dtype)
            dkv_ref[:, lanes] += dkr
            dkv_ref[:, vl] += _dot(p.astype(BF16), do, TN)
            dgq_ref[...] += dgq_p
            dgk_ref[...] += dgk_p

    return pl.pallas_call(
        body, grid=(T // tq,),
        in_specs=[qspec, pl.BlockSpec(kv.shape, lambda i: (0, 0)), gain, gain,
                  pl.BlockSpec((tq, MEM_WIDTH), lambda i: (i, MAIN_WIDTH // MEM_WIDTH))],
        out_specs=[pl.BlockSpec((tq, MEM_WIDTH), lambda i: (i, 0)), pl.BlockSpec(kv.shape, lambda i: (0, 0)), gain, gain],
        out_shape=[jax.ShapeDtypeStruct((T, MEM_WIDTH), BF16), jax.ShapeDtypeStruct(kv.shape, F32),
                   jax.ShapeDtypeStruct((1, hd), F32), jax.ShapeDtypeStruct((1, hd), F32)],
        name=name, compiler_params=_params(1))(proj, kv, gq, gk, dmix)


def _cumsum_rows(name, x, reverse=False):
    T, C = x.shape
    pt = min(T, 256)

    def body(x_ref, o_ref):
        r = lax.broadcasted_iota(jnp.int32, (pt, pt), 0)
        c = lax.broadcasted_iota(jnp.int32, (pt, pt), 1)
        tri = ((c >= r) if reverse else (c <= r)).astype(F32)
        carry = jnp.zeros((1, C), F32)
        order = range(T // pt)
        for p in (reversed(order) if reverse else order):
            rows = pl.ds(p * pt, pt)
            blk = x_ref[rows, :]
            o_ref[rows, :] = _dot(tri, blk, NN, precision=lax.Precision.HIGHEST) + carry
            carry = carry + jnp.sum(blk, axis=0, keepdims=True)

    return pl.pallas_call(body, out_shape=jax.ShapeDtypeStruct((T, C), F32), name=name,
                          compiler_params=pltpu.CompilerParams(vmem_limit_bytes=V7X_VMEM_LIMIT))(x)


MESH = pl.DeviceIdType.MESH
ANY = pl.BlockSpec(memory_space=pl.ANY)


def _mesh_pos():
    return lax.axis_index("x"), lax.axis_index("y"), lax.axis_index("c")


def _all_gather(name, xs):
    n = len(xs)

    def body(*refs):
        x_refs, out_refs = refs[:n], refs[n:2 * n]
        send_sems, recv_sems, local_sems = refs[2 * n:]
        mx, my, mc = _mesh_pos()
        me, sibling = (mx, my, mc), (mx, my, 1 - mc)
        chips = [(1 - mx, my), (mx, 1 - my), (1 - mx, 1 - my)]

        def slot(a, px, py, pc):
            return out_refs[a].at[4 * px + 2 * py + pc]

        def copy(a, k, block, to, src=None):
            return pltpu.make_async_remote_copy(
                src_ref=slot(a, *block) if src is None else src, dst_ref=slot(a, *block),
                send_sem=send_sems.at[7 * a + k], recv_sem=recv_sems.at[7 * a + k], device_id=to, device_id_type=MESH)

        mine = [pltpu.make_async_copy(x_refs[a], slot(a, *me), local_sems.at[a]) for a in range(n)]
        first = []
        for a in range(n):
            mine[a].start()
            first.append(copy(a, 0, me, sibling, src=x_refs[a]))
            first += [copy(a, 1 + j, me, (*chip, mc), src=x_refs[a]) for j, chip in enumerate(chips)]
        for cp in first:
            cp.start()
        passed = []
        for j, chip in enumerate(chips):
            for a in range(n):
                copy(a, 1 + j, (*chip, mc), me).wait_recv()
                passed.append(copy(a, 4 + j, (*chip, mc), sibling))
                passed[-1].start()
        for a in range(n):
            copy(a, 0, sibling, me).wait_recv()
            for j, chip in enumerate(chips):
                copy(a, 4 + j, (*chip, 1 - mc), me).wait_recv()
        for cp in first + passed:
            cp.wait_send()
        for cp in mine:
            cp.wait()

    return pl.pallas_call(
        body, out_shape=[jax.ShapeDtypeStruct((N_DEV,) + x.shape, x.dtype) for x in xs], in_specs=[ANY] * n, out_specs=[ANY] * n,
        scratch_shapes=[pltpu.SemaphoreType.DMA((7 * n,)), pltpu.SemaphoreType.DMA((7 * n,)), pltpu.SemaphoreType.DMA((n,))],
        name=name)(*xs)


def _exchange_cores(name, gs):
    n = len(gs)

    def body(*refs):
        g_refs, recv_refs = refs[:n], refs[n:2 * n]
        send_sems, recv_sems = refs[2 * n:]
        mx, my, mc = _mesh_pos()
        swap = [pltpu.make_async_remote_copy(
            src_ref=g_refs[a].at[2 * q + (1 - mc)], dst_ref=recv_refs[a].at[q], send_sem=send_sems.at[4 * a + q],
            recv_sem=recv_sems.at[4 * a + q], device_id=(mx, my, 1 - mc), device_id_type=MESH) for a in range(n) for q in range(4)]
        for cp in swap:
            cp.start()
        for cp in swap:
            cp.wait()

    return pl.pallas_call(
        body, out_shape=[jax.ShapeDtypeStruct((4,) + g.shape[1:], g.dtype) for g in gs], in_specs=[ANY] * n, out_specs=[ANY] * n,
        scratch_shapes=[pltpu.SemaphoreType.DMA((4 * n,)), pltpu.SemaphoreType.DMA((4 * n,))],
        name=name)(*gs)


def _exchange_chips(name, ss):
    n = len(ss)

    def body(*refs):
        s_refs, recv_refs = refs[:n], refs[n:2 * n]
        send_sems, recv_sems, local_sems = refs[2 * n:]
        mx, my, mc = _mesh_pos()
        myq = 2 * mx + my
        chips = [(1 - mx, my), (mx, 1 - my), (1 - mx, 1 - my)]
        mine = [pltpu.make_async_copy(s_refs[a].at[myq], recv_refs[a].at[myq], local_sems.at[a]) for a in range(n)]
        for cp in mine:
            cp.start()
        swap = [pltpu.make_async_remote_copy(
            src_ref=s_refs[a].at[2 * px + py], dst_ref=recv_refs[a].at[myq], send_sem=send_sems.at[3 * a + k],
            recv_sem=recv_sems.at[3 * a + k], device_id=(px, py, mc), device_id_type=MESH)
            for a in range(n) for k, (px, py) in enumerate(chips)]
        for cp in swap:
            cp.start()
        for a in range(n):
            for k, (px, py) in enumerate(chips):
                pltpu.make_async_remote_copy(
                    src_ref=s_refs[a].at[myq], dst_ref=recv_refs[a].at[2 * px + py], send_sem=send_sems.at[3 * a + k],
                    recv_sem=recv_sems.at[3 * a + k], device_id=(px, py, mc), device_id_type=MESH).wait_recv()
        for cp in swap:
            cp.wait_send()
        for cp in mine:
            cp.wait()

    return pl.pallas_call(
        body, out_shape=[jax.ShapeDtypeStruct(s.shape, s.dtype) for s in ss], in_specs=[ANY] * n, out_specs=[ANY] * n,
        scratch_shapes=[pltpu.SemaphoreType.DMA((3 * n,)), pltpu.SemaphoreType.DMA((3 * n,)), pltpu.SemaphoreType.DMA((n,))],
        name=name)(*ss)


def _pair_sum(name, g, recv, mc):
    _, R, C = g.shape
    tm = _row_tile(R, 512)

    def body(mc_ref, own_ref, recv_ref, o_ref):
        o_ref[...] = (own_ref[...].astype(F32) + recv_ref[...].astype(F32)).astype(o_ref.dtype)

    spec = pl.BlockSpec((None, tm, C), lambda q, i, mc_ref: (q, i, 0))
    grid_spec = pltpu.PrefetchScalarGridSpec(
        num_scalar_prefetch=1, grid=(4, R // tm),
        in_specs=[pl.BlockSpec((None, tm, C), lambda q, i, mc_ref: (2 * q + mc_ref[0], i, 0)), spec], out_specs=spec)
    return pl.pallas_call(body, grid_spec=grid_spec, out_shape=jax.ShapeDtypeStruct((4, R, C), BF16), name=name,
                          compiler_params=_params(2))(mc, g, recv)


def _row_tile(R, cap):
    best = None
    for t in range(8, min(R, cap) + 1, 8):
        if R % t == 0:
            best = t
    return best or R


def _sum_slabs(name, a, out_dtype):
    n, R, C = a.shape
    tm = _row_tile(R, 512)

    def body(*refs):
        acc = refs[0][...].astype(F32)
        for r in refs[1:n]:
            acc = acc + r[...].astype(F32)
        refs[n][...] = acc.astype(out_dtype)

    return pl.pallas_call(
        body, grid=(R // tm,),
        in_specs=[pl.BlockSpec((None, tm, C), lambda i, q=q: (q, i, 0)) for q in range(n)],
        out_specs=pl.BlockSpec((tm, C), lambda i: (i, 0)), out_shape=jax.ShapeDtypeStruct((R, C), out_dtype),
        name=name, compiler_params=_params(1))(*([a] * n))


def _reduce_scatter(gs):
    mc = lax.axis_index("c").astype(jnp.int32).reshape(1)
    recvs = _exchange_cores("rs_cores", gs)
    pairs = [_pair_sum(f"rs_pair_sum{a}", g, r, mc) for a, (g, r) in enumerate(zip(gs, recvs))]
    return [_sum_slabs(f"rs_chip_sum{a}", r, F32) for a, r in enumerate(_exchange_chips("rs_chips", pairs))]


SMALL = ["ffn1_norm", "mix_norm", "mem_norm", "mem_q_gain", "mem_k_gain", "hgrn_o_gain", "fox_q_gain", "kv_norm",
         "fox_f_bias", "fox_k_gain", "ffn2_norm"]
COLS352 = ["ffn1_w_gate", "ffn1_w_up", "ffn2_w_gate", "ffn2_w_up"]
KV_SPLIT = 1024
KV_WIDTH = 2 * MAIN_WIDTH + FOX_HEADS


def _rows2d(w):
    return w.reshape(-1, w.shape[-1])


def _pad_cols(w, width):
    return jnp.pad(w, [(0, 0)] * (w.ndim - 1) + [(0, width - w.shape[-1])])


def _pack_rows1024(down1, down2, w_out, w_mem_kv, w_kv):
    kv = jnp.concatenate([w_kv[:, :KV_SPLIT], _pad_cols(w_kv[:, KV_SPLIT:], D_MODEL)], axis=0)
    return jnp.concatenate([_rows2d(down1), _rows2d(down2), _rows2d(w_out), _rows2d(_pad_cols(w_mem_kv, D_MODEL)), kv], axis=0)


def _unpack_rows1024(buf, shapes):
    out, off = [], 0
    for name in ("ffn1_w_down", "ffn2_w_down", "w_out", "w_mem_kv"):
        L, r, c = shapes[name]
        out.append(buf[off:off + L * r].reshape(L, r, D_MODEL)[:, :, :c])
        off += L * r
    r, c = shapes["w_kv"]
    out.append(jnp.concatenate([buf[off:off + r], buf[off + r:off + 2 * r, :c - KV_SPLIT]], axis=1))
    return out


def _pad128(a):
    flat = a.reshape(-1)
    return jnp.pad(flat, (0, -flat.shape[0] % LANES))


def _small_pack(parts):
    flat = jnp.concatenate([_pad128(p) for p in parts])
    rows = -(-flat.shape[0] // LANES)
    flat = jnp.pad(flat, (0, (-rows % 8) * LANES))
    return flat.reshape(-1, LANES)


def _small_unpack(buf, shapes):
    flat = buf.reshape(-1)
    out, off = [], 0
    for s in shapes:
        n = 1
        for d in s:
            n *= d
        out.append(flat[off:off + n].reshape(s))
        off += n + (-n % LANES)
    return out


def _lb_fn(l0, l1):
    m = lax.stop_gradient(jnp.maximum(l0, l1))
    e0, e1 = jnp.exp(l0 - m), jnp.exp(l1 - m)
    p0, p1 = e0 / (e0 + e1), e1 / (e0 + e1)
    return p0 - p0, (p0 + p1) - p0


def _lb_fwd(logits):
    return _rowwise("lb", _lb_fn, [logits[0:1], logits[1:2]], [], [(MAIN_WIDTH, F32)] * 2)


def _lb_bwd(logits, dlb0, dlb1):
    def fn(l0, l1, d0, d1):
        _, vjp = jax.vjp(_lb_fn, l0, l1)
        return vjp((d0, d1))
    return _rowwise("lb_bwd", fn, [logits[0:1], logits[1:2], dlb0, dlb1], [], [(MAIN_WIDTH, F32)] * 2)


def _adamw_fn(w, g, m, v):
    m = ADAM_B1 * m + (1.0 - ADAM_B1) * g
    v = ADAM_B2 * v + (1.0 - ADAM_B2) * jnp.square(g)
    m_hat = m / (1.0 - ADAM_B1 ** ADAM_STEP)
    v_hat = v / (1.0 - ADAM_B2 ** ADAM_STEP)
    return -ADAM_LR * (m_hat / (jnp.sqrt(v_hat) + ADAM_EPS) + ADAM_WD * w), m, v


def _sum_adamw(name, landed, w, m, v, comm=None):
    L, r, c = w.shape
    tm = _row_tile(r, 128)
    n_i = r // tm

    def body(*refs):
        land, (w_ref, m_ref, v_ref), outs = refs[:L], refs[L:L + 3], refs[L + 3:]
        for k in range(L):
            @pl.when(pl.program_id(0) == k)
            def _(k=k):
                g = land[k][0].astype(F32)
                for s in range(1, N_DEV):
                    g = g + land[k][s].astype(F32)
                for ref, val in zip(outs, (g,) + _adamw_fn(w_ref[...], g, m_ref[...], v_ref[...])):
                    ref[...] = val

    held = lambda k: (lambda l, i: (0, jnp.where(l < k, 0, jnp.where(l == k, i, n_i - 1)), 0))
    cur = pl.BlockSpec((None, tm, c), lambda l, i: (l, i, 0))
    return _call(body, [*landed, w, m, v], [jax.ShapeDtypeStruct((L, r, c), F32)] * 4, name=name, grid=(L, n_i),
                 in_specs=[pl.BlockSpec((N_DEV, tm, c), held(k)) for k in range(L)] + [cur] * 3, out_specs=[cur] * 4, comm=comm)


def _adamw(name, w, g, m, v):
    shape = w.shape
    C = shape[-1]
    two = lambda a: a.reshape(-1, C)
    R = two(w).shape[0]
    outs = _rowwise(name, _adamw_fn, [two(w), two(g), two(m), two(v)], [], [(C, F32)] * 3, tm=_row_tile(R, 512))
    return [o.reshape(shape) for o in outs]


def _whole_rows(g, r0, r1):
    return g[:, r0:r1].reshape(N_DEV * (r1 - r0), g.shape[2])


def _w_out_of(Wl):
    n = Wl["d2"].shape[1]
    return _whole_rows(Wl["r1"], n, n + LANES)


def _w_mem_kv_of(Wl):
    n = Wl["d2"].shape[1]
    return _whole_rows(Wl["r1"], n + LANES, n + 2 * LANES)[:, :2 * MEM_WIDTH]


def _mixer_fwd(l, x1, mem, G, W, lbs, shared, local, units, relays):
    T = x1.shape[0]
    tag = f"l{l}"
    Wl = G[l]
    h = _rms_fwd(tag + "_mixrms", x1, W["mix_norm"][l:l + 1])
    mem_n = _rms_fwd(tag + "_memrms", mem, W["mem_norm"][l:l + 1])
    kv = _mm(tag + "_memkv", [(mem_n, _w_mem_kv_of(Wl), NN)], [F32], mem.shape[0], 2 * MEM_WIDTH)
    proj, moved = _proj_cols(tag + "_in", h, Wl["win"], 0, comm=_Comm(relay=[G[n][k] for n, k in relays]))
    for (n, k), m in zip(relays, moved):
        G[n][k] = m
    along = _Comm(gather=[local[n][k] for n, k in units])
    if l < 2:
        main, o, moved = _hgrn_fwd(tag + "_hgrn", proj, lbs[l], W["hgrn_o_gain"][l:l + 1], comm=along)
    else:
        main, o, moved = _fox_fwd(tag + "_fox", proj, shared["kvf"], shared["cr"], W["fox_q_gain"][l - 2:l - 1],
                                  W["fox_k_gain"], comm=along)
    for (n, k), m in zip(units, moved):
        G[n][k] = m
    mem_o = _mem_fwd(tag + "_mem", proj, kv, W["mem_q_gain"][l:l + 1], W["mem_k_gain"][l:l + 1])
    w_out = _w_out_of(Wl)
    x2 = _mm(tag + "_out", [(main, w_out[:MAIN_WIDTH], NN), (mem_o, w_out[MAIN_WIDTH:], NN)], [F32], T, D_MODEL,
             epi=lambda a, e: (e[0] + a[0] + a[1],), extras=[x1])
    return x2, dict(h=h, mem_n=mem_n, kv=kv, proj=proj, main=main, o=o, mem_o=mem_o)


def _mixer_bwd(l, x1, mem, Wl, W, lbs, shared, sv, dx2, acc, ready, landed):
    T = x1.shape[0]
    tag = f"l{l}b"
    w_out = _w_out_of(Wl)
    g = {}
    dmix = _mm(tag + "_dmix", [(dx2, w_out, NT)], [F32], T, D_MODEL)
    dw_out = jnp.concatenate([
        _mm(tag + "_dwout_a", [(sv["main"], dx2, TN)], [BF16], MAIN_WIDTH, D_MODEL),
        _mm(tag + "_dwout_b", [(sv["mem_o"], dx2, TN)], [BF16], MEM_WIDTH, D_MODEL)], axis=0).reshape(N_DEV, -1, D_MODEL)
    dqm, dkv, g["mem_q_gain"], g["mem_k_gain"] = _mem_bwd(tag + "_mem", sv["proj"], sv["kv"], W["mem_q_gain"][l:l + 1],
                                                           W["mem_k_gain"][l:l + 1], dmix)
    along = _Comm(scatter=[v for _, v in ready])
    if l < 2:
        dq, df, di, dg, g["lb"], g["hgrn_o_gain"], moved = _hgrn_bwd(tag + "_hgrn", sv["proj"], sv["o"], dmix, lbs[l],
                                                                      W["hgrn_o_gain"][l:l + 1], comm=along)
        dproj = jnp.concatenate([dq, df, di, dg, dqm], axis=1)
    else:
        dq, dgate, acc["dk"], acc["dv"], acc["dc"], g["fox_q_gain"], g["fox_k_gain"], moved = _fox_bwd(
            tag + "_fox", sv["proj"], shared["kvf"], shared["cr"], W["fox_q_gain"][l - 2:l - 1], W["fox_k_gain"],
            sv["o"], dmix, acc["dk"], acc["dv"], acc["dc"], comm=along)
        dproj = jnp.concatenate([dq, dgate, dqm], axis=1)
    landed.update({k: m for (k, _), m in zip(ready, moved)})
    dh, dw_in, moved = _proj_cols_bwd(tag + "_in", sv["h"], dproj, Wl["win"], 0, comm=_Comm(scatter=[dw_out]))
    landed[(l, "w_out")] = moved[0]
    dx1, g["mix_norm"] = _rms_bwd(tag + "_mixrms", x1, W["mix_norm"][l:l + 1], dh, dres=dx2)
    dw_mem_kv = _mm(tag + "_dwmemkv", [(sv["mem_n"], dkv, TN)], [BF16], D_MODEL, 2 * MEM_WIDTH)
    dmem_n = _mm(tag + "_dmemn", [(dkv, _w_mem_kv_of(Wl), NT)], [F32], mem.shape[0], D_MODEL)
    _, g["mem_norm"] = _rms_bwd(tag + "_memrms", mem, W["mem_norm"][l:l + 1], dmem_n)
    return dx1, g, [((l, "w_in"), dw_in), ((l, "w_mem_kv"), dw_mem_kv.reshape(N_DEV, -1, 2 * MEM_WIDTH))]


def _forget_cols(kvf):
    return kvf[:, 2 * MAIN_WIDTH:2 * MAIN_WIDTH + LANES]


def _log_forget(kvf, bias):
    return _rowwise("kv_logf", lambda f, b: jax.nn.log_sigmoid(f + b), [_forget_cols(kvf)], [bias], [(LANES, F32)])[0]


def _move(name, comm):
    def body(o_ref):
        o_ref[...] = jnp.zeros(o_ref.shape, F32)
    _, moved = _call(body, [], [jax.ShapeDtypeStruct((8, LANES), F32)], name=name, in_specs=[],
                     out_specs=[pl.BlockSpec(memory_space=pltpu.VMEM)], comm=comm)
    return moved


def _step(x, mem, target, W, lb_logits, G0, local):
    T = x.shape[0]
    W = dict(W, fox_k_gain=W["fox_k_gain"].reshape(1, -1))
    lbs = _lb_fwd(lb_logits)
    fox_bias = jnp.pad(W["fox_f_bias"], (0, LANES - FOX_HEADS)).reshape(1, LANES)
    w_kv = W["w_kv"]
    n_l = len(local)
    ffn1 = lambda l, Wl: (W["ffn1_norm"][l:l + 1], Wl["gu1"], Wl["r1"], 0, 1, 0)
    ffn2 = lambda l, Wl: (W["ffn2_norm"][l:l + 1], Wl["gu2"], Wl["d2"], 0, 1, 0)

    on_ffn1 = {0: [(1, "gu1")], 1: [(2, "gu1")], 2: [(3, "gu1")], 3: [(3, "gu2")]}
    on_mix = {0: [(1, "r1"), (1, "win"), (1, "gu2")], 1: [(2, "r1"), (2, "win"), (2, "gu2")], 2: [(3, "r1"), (3, "win")], 3: []}
    on_ffn2 = {0: [(1, "d2")], 1: [(2, "d2")], 2: [(3, "d2")], 3: []}
    saved, shared, G = [], {}, [G0] + [{} for _ in range(n_l - 1)]
    for l in range(n_l):
        Wl = G[l]
        relay = on_ffn2[l - 1] if l else []
        along = _Comm(gather=[local[n][k] for n, k in on_ffn1[l]], relay=[G[n][k] for n, k in relay])
        x1, moved = _ffn_fwd(f"l{l}_ffn1", x, *ffn1(l, Wl), comm=along)
        for (n, k), m in zip(on_ffn1[l] + relay, moved):
            G[n][k] = m
        x2, sv = _mixer_fwd(l, x1, mem, G, W, lbs, shared, local, on_mix[l], on_ffn1[l])
        along = _Comm(gather=[local[n][k] for n, k in on_ffn2[l]], relay=[G[n][k] for n, k in on_mix[l]])
        x3, moved = _ffn_fwd(f"l{l}_ffn2", x2, *ffn2(l, G[l]), comm=along)
        for (n, k), m in zip(on_ffn2[l] + on_mix[l], moved):
            G[n][k] = m
        sv.update(x=x, x1=x1, x2=x2)
        saved.append(sv)
        x = x3
        if l == 1:
            hk = _rms_fwd("kv_rms", x, W["kv_norm"].reshape(1, -1))
            kvf = _mm("kv_proj", [(hk, w_kv, NN)], [F32], T, w_kv.shape[1])
            cum = _cumsum_rows("kv_cum", _log_forget(kvf, fox_bias))[:, :FOX_HEADS].T
            shared = dict(kvf=kvf, cr=cum[:, None, :], hk=hk, x=x)

    def loss_fn(y, t):
        err = y - t
        return err * (1.0 / D_MODEL), jnp.sum(0.5 / D_MODEL * err * err, axis=0, keepdims=True)
    dx, loss = _rowwise("loss", loss_fn, [x, target], [], [(D_MODEL, F32)], [((1, D_MODEL), F32)])

    grads = [None] * n_l
    acc = dict(dk=jnp.zeros((T, MAIN_WIDTH), F32), dv=jnp.zeros((T, MAIN_WIDTH), F32), dc=jnp.zeros((FOX_HEADS, 1, T), F32))
    gkv = {}
    landed, late = {}, []
    for l in reversed(range(n_l)):
        sv, Wl = saved[l], G[l]
        ready, extra = [], []
        if l == 1:
            dcum = jnp.pad(acc["dc"][:, 0, :].T, ((0, 0), (0, LANES - FOX_HEADS)))
            dlf = _cumsum_rows("kv_dcum", dcum, reverse=True)
            def dlogf_fn(d, f, b):
                p = d * _sigmoid(-(f + b))
                return p, jnp.sum(p, axis=0, keepdims=True)
            dfl, gkv["fox_f_bias"] = _rowwise("kv_dlogf", dlogf_fn, [dlf, _forget_cols(shared["kvf"])], [fox_bias],
                                              [(LANES, BF16)], [((1, LANES), F32)])
            dkvf = _pad_cols(jnp.concatenate([acc["dk"].astype(BF16), acc["dv"].astype(BF16), dfl], axis=1), w_kv.shape[1])
            dw_kv = _mm("kv_dw", [(shared["hk"], dkvf, TN)], [BF16], D_MODEL, dkvf.shape[1])
            extra.append(((0, "w_kv"), dw_kv[:, :KV_WIDTH].reshape(N_DEV, -1, KV_WIDTH)))
            dhk = _mm("kv_dh", [(dkvf, w_kv, NT)], [F32], T, D_MODEL)
            dx, gkv["kv_norm"] = _rms_bwd("kv_rmsb", shared["x"], W["kv_norm"].reshape(1, -1), dhk, dres=dx)
        g = {}
        if l < 2:
            ready, late = ready + late[1:], late[:1]
        dx2, g["ffn2_norm"], dwg, dwu, dwd, moved_a, moved_w = _ffn_bwd(
            f"l{l}b_ffn2", sv["x2"], *ffn2(l, Wl), dx, comm_a=_Comm(scatter=[v for _, v in late[:2]]),
            comm_w=_Comm(scatter=[v for _, v in late[2:]]))
        landed.update({k: m for (k, _), m in zip(late, moved_a + moved_w)})
        ready += [((l, "ffn2_w_gate"), dwg), ((l, "ffn2_w_up"), dwu), ((l, "ffn2_w_down"), dwd)]
        dx1, gm, rest = _mixer_bwd(l, sv["x1"], mem, Wl, W, lbs, shared, sv, dx2, acc, ready, landed)
        rest = rest + extra
        g.update(gm)
        dx, g["ffn1_norm"], dwg, dwu, dwd, moved_a, _ = _ffn_bwd(f"l{l}b_ffn1", sv["x"], *ffn1(l, Wl), dx1,
                                                                 comm_a=_Comm(scatter=[v for _, v in rest]))
        landed.update({k: m for (k, _), m in zip(rest, moved_a)})
        late = [((l, "ffn1_w_gate"), dwg), ((l, "ffn1_w_up"), dwu), ((l, "ffn1_w_down"), dwd)]
        grads[l] = g

    out = {}
    for n in ["ffn1_norm", "mix_norm", "mem_norm", "mem_q_gain", "mem_k_gain", "ffn2_norm"]:
        out[n] = jnp.concatenate([grads[l][n] for l in range(4)], axis=0)
    out["hgrn_o_gain"] = jnp.concatenate([grads[l]["hgrn_o_gain"] for l in (0, 1)], axis=0)
    out["fox_q_gain"] = jnp.concatenate([grads[l]["fox_q_gain"] for l in (2, 3)], axis=0)
    out["fox_k_gain"] = (grads[2]["fox_k_gain"] + grads[3]["fox_k_gain"]).reshape(-1)
    out["kv_norm"] = gkv["kv_norm"].reshape(-1)
    out["fox_f_bias"] = gkv["fox_f_bias"][0, :FOX_HEADS]
    dl0, dl1 = _lb_bwd(lb_logits, grads[0]["lb"], grads[1]["lb"])
    out["hgrn_lb_logits"] = jnp.concatenate([dl0, dl1], axis=0)
    return loss, dx, out, landed, late


WEIGHTS = ["ffn1_norm", "ffn1_w_gate", "ffn1_w_up", "ffn1_w_down", "mix_norm", "mem_norm", "w_mem_kv", "mem_q_gain",
           "mem_k_gain", "w_in_a", "hgrn_lb_logits", "hgrn_o_gain", "w_in_b", "fox_q_gain", "kv_norm", "w_kv", "fox_f_bias",
           "fox_k_gain", "w_out", "ffn2_norm", "ffn2_w_gate", "ffn2_w_up", "ffn2_w_down"]
BIG = COLS352 + ["w_in_a", "w_in_b", "ffn1_w_down", "ffn2_w_down", "w_out", "w_mem_kv", "w_kv"]


def _train_step(a):
    bf = lambda w: w.astype(BF16)
    n_l = a["w_out"].shape[0]
    local = []
    for l in range(n_l):
        w_in = a["w_in_a"][l] if l < a["w_in_a"].shape[0] else a["w_in_b"][l - a["w_in_a"].shape[0]]
        local.append(dict(
            gu1=bf(jnp.concatenate([a["ffn1_w_gate"][l], a["ffn1_w_up"][l]], axis=0)),
            r1=bf(jnp.concatenate([a["ffn1_w_down"][l], a["w_out"][l], _pad_cols(a["w_mem_kv"][l], D_MODEL)], axis=0)),
            win=bf(w_in),
            gu2=bf(jnp.concatenate([a["ffn2_w_gate"][l], a["ffn2_w_up"][l]], axis=0)),
            d2=bf(a["ffn2_w_down"][l])))
    keys = ["gu1", "r1", "win", "gu2", "d2"]
    first = _all_gather("ag_first", [local[0][k] for k in keys] + [bf(a["w_kv"]), _small_pack([a["hgrn_lb_logits"]])])
    G0 = dict(zip(keys, first))
    W = {n: a[n] for n in SMALL}
    W["w_kv"] = _pad_cols(first[5].reshape(-1, a["w_kv"].shape[1]), 2 * D_MODEL)
    lb_shape = a["hgrn_lb_logits"].shape
    lb_all = first[6].reshape(N_DEV, -1)[:, :lb_shape[0] * lb_shape[1]]
    lb_logits = lb_all.reshape((N_DEV,) + lb_shape).transpose(1, 0, 2).reshape(lb_shape[0], -1)

    loss_part, dx, g, landed, tail = _step(a["x"][0], a["mem"][0], a["loss_target"][0], W, lb_logits, G0, local)

    n_a = a["w_in_a"].shape[0]
    grad, delta, new_m, new_v = {}, {}, {}, {}
    order = [n for n in BIG if n.startswith("ffn2")] + [n for n in BIG if not n.startswith("ffn")]
    for n in order + [n for n in BIG if n.startswith("ffn1")]:
        ls = range(n_a) if n == "w_in_a" else range(n_a, n_l) if n == "w_in_b" else range(1) if n == "w_kv" else range(n_l)
        key = "w_in" if n.startswith("w_in") else n
        lead = (lambda t: t[None]) if a[n].ndim == 2 else (lambda t: t)
        riding, tail = tail[:1], tail[1:]
        res, moved = _sum_adamw("adam_" + n, [landed[(l, key)] for l in ls], lead(a[n]), lead(a["m_" + n]), lead(a["v_" + n]),
                                comm=_Comm(scatter=[v for _, v in riding]))
        landed.update({k: m for (k, _), m in zip(riding, moved)})
        grad[n], delta[n], new_m[n], new_v[n] = [t.reshape(a[n].shape) for t in res]

    zeros = [jnp.zeros(lb_logits.shape, F32), jnp.zeros(loss_part.shape, F32)]
    small_shapes = [a[n].shape for n in SMALL] + [lb_logits.shape, loss_part.shape]
    small_part = _small_pack([g[n] for n in SMALL] + [g["hgrn_lb_logits"], loss_part])
    small_sum = _sum_slabs("small_sum", _all_gather("ag_small", [small_part])[0], F32)
    small = _small_unpack(small_sum, small_shapes)
    grad.update(dict(zip(SMALL, small)))
    loss = jnp.sum(small[-1])
    me = 4 * lax.axis_index("x") + 2 * lax.axis_index("y") + lax.axis_index("c")
    grad["hgrn_lb_logits"] = lax.dynamic_slice_in_dim(small[-2], me * lb_shape[1], lb_shape[1], axis=1)

    n = "hgrn_lb_logits"
    delta[n], new_m[n], new_v[n] = _adamw("adam_" + n, a[n], grad[n], a["m_" + n], a["v_" + n])
    packs = [_small_pack([a[p + n] for n in SMALL] + zeros) for p in ("", "m_", "v_")]
    upd = _rowwise("adam_small", _adamw_fn, [packs[0], small_sum, packs[1], packs[2]], [], [(LANES, F32)] * 3, tm=packs[0].shape[0])
    for d, u in zip((delta, new_m, new_v), upd):
        d.update(dict(zip(SMALL, _small_unpack(u, small_shapes))))
    return (loss, dx[None], *[grad[n] for n in WEIGHTS], *[delta[n] for n in WEIGHTS], *[new_m[n] for n in WEIGHTS],
            *[new_v[n] for n in WEIGHTS])


def kernel(x, mem, ffn1_norm, ffn1_w_gate, ffn1_w_up, ffn1_w_down, mix_norm, mem_norm, w_mem_kv, mem_q_gain, mem_k_gain, w_in_a, hgrn_lb_logits, hgrn_o_gain, w_in_b, fox_q_gain, kv_norm, w_kv, fox_f_bias, fox_k_gain, w_out, ffn2_norm, ffn2_w_gate, ffn2_w_up, ffn2_w_down, loss_target, m_ffn1_norm, m_ffn1_w_gate, m_ffn1_w_up, m_ffn1_w_down, m_mix_norm, m_mem_norm, m_w_mem_kv, m_mem_q_gain, m_mem_k_gain, m_w_in_a, m_hgrn_lb_logits, m_hgrn_o_gain, m_w_in_b, m_fox_q_gain, m_kv_norm, m_w_kv, m_fox_f_bias, m_fox_k_gain, m_w_out, m_ffn2_norm, m_ffn2_w_gate, m_ffn2_w_up, m_ffn2_w_down, v_ffn1_norm, v_ffn1_w_gate, v_ffn1_w_up, v_ffn1_w_down, v_mix_norm, v_mem_norm, v_w_mem_kv, v_mem_q_gain, v_mem_k_gain, v_w_in_a, v_hgrn_lb_logits, v_hgrn_o_gain, v_w_in_b, v_fox_q_gain, v_kv_norm, v_w_kv, v_fox_f_bias, v_fox_k_gain, v_w_out, v_ffn2_norm, v_ffn2_w_gate, v_ffn2_w_up, v_ffn2_w_down):
    return _train_step(dict(locals()))
```

```python
import functools

import jax
import jax.numpy as jnp
from jax import lax
from jax.experimental import pallas as pl
from jax.experimental.pallas import tpu as pltpu

F32, BF16 = jnp.float32, jnp.bfloat16
EPS = 1e-6
V7X_VMEM_LIMIT = 56 * 1024 * 1024
LANES = 128
N_DEV = 8

D_MODEL = 1024
MAIN_WIDTH = 768
MEM_WIDTH = 256
HG_HEAD_DIM = 128
HG_HEADS = 6
FOX_HEAD_DIM = 64
FOX_HEADS = 12
MEM_HEADS = 4
MEM_HEAD_DIM = 64
HG_BLOCK = 16

ADAM_LR, ADAM_B1, ADAM_B2, ADAM_EPS, ADAM_WD, ADAM_STEP = 0.001, 0.9, 0.999, 1e-08, 0.01, 10

NN = ((1,), (0,))
NT = ((1,), (1,))
TN = ((0,), (0,))


def _dot(a, b, dims, precision=None):
    return lax.dot_general(a, b, (dims, ((), ())), preferred_element_type=F32, precision=precision)


def _bdot(a, b, dims):
    return _dot(a.astype(BF16), b.astype(BF16), dims)


def _split(a):
    hi = a.astype(BF16)
    return hi, (a - hi.astype(F32)).astype(BF16)


def _fdot(a, b, dims):
    ah, al = _split(a)
    bh, bl = _split(b)
    return _dot(ah, bh, dims) + (_dot(ah, bl, dims) + _dot(al, bh, dims))


def _params(n_grid):
    return pltpu.CompilerParams(dimension_semantics=("arbitrary",) * n_grid, vmem_limit_bytes=V7X_VMEM_LIMIT)


def _rms(x, g):
    return x * lax.rsqrt(jnp.mean(x * x, axis=-1, keepdims=True) + EPS) * g


def _sigmoid(x):
    return jax.nn.sigmoid(x)


def _silu(x):
    return x * jax.nn.sigmoid(x)


MESH = pl.DeviceIdType.MESH
ANY = pl.BlockSpec(memory_space=pl.ANY)


def _mesh_pos():
    return lax.axis_index("x"), lax.axis_index("y"), lax.axis_index("c")


class _Comm:
    def __init__(self, gather=(), relay=(), scatter=()):
        self.gather, self.relay, self.scatter = list(gather), list(relay), list(scatter)
        self.arrays = self.gather + self.relay + self.scatter
        self.n_remote = 4 * len(self.gather) + 3 * len(self.relay) + 7 * len(self.scatter)
        self.n_local = len(self.gather) + len(self.scatter)

    def out_shapes(self):
        return ([jax.ShapeDtypeStruct((N_DEV,) + x.shape, x.dtype) for x in self.gather]
                + [jax.ShapeDtypeStruct(g.shape, g.dtype) for g in self.relay + self.scatter])

    def scratch(self):
        return [pltpu.SemaphoreType.DMA((self.n_remote,)), pltpu.SemaphoreType.DMA((self.n_remote,)),
                pltpu.SemaphoreType.DMA((max(self.n_local, 1),))]

    def _copies(self, ins, outs, send, recv, local, arrivals=True):
        mx, my, mc = _mesh_pos()
        flip = lambda v, f: 1 - v if f else v
        idx = lambda p: 4 * p[0] + 2 * p[1] + p[2]
        me = (mx, my, mc)
        count = [0, 0]
        loc, out, arrive = [], [], []

        def pair(src, dst, lands, to):
            k = count[0]
            count[0] += 1
            mk = lambda d: pltpu.make_async_remote_copy(src_ref=src, dst_ref=d, send_sem=send.at[k], recv_sem=recv.at[k],
                                                        device_id=to, device_id_type=MESH)
            out.append(mk(dst))
            if arrivals:
                arrive.append(mk(lands))

        def local_copy(src, dst):
            loc.append(pltpu.make_async_copy(src, dst, local.at[count[1]]))
            count[1] += 1

        refs = list(zip(ins, outs))
        near = [(0, 0, 1), (1, 0, 0), (0, 1, 0), (1, 1, 0)]
        for x, G in refs[:len(self.gather)]:
            local_copy(x, G.at[idx(me)])
            for f in near:
                peer = tuple(flip(v, b) for v, b in zip(me, f))
                pair(x, G.at[idx(me)], G.at[idx(peer)], peer)
        sibling = (mx, my, 1 - mc)
        for Gin, Gout in refs[len(self.gather):len(self.gather) + len(self.relay)]:
            for f in near[1:]:
                chip = (flip(mx, f[0]), flip(my, f[1]))
                pair(Gin.at[idx((*chip, mc))], Gout.at[idx((*chip, mc))], Gout.at[idx((*chip, 1 - mc))], sibling)
        every = near + [(1, 0, 1), (0, 1, 1), (1, 1, 1)]
        for g, R in refs[len(self.gather) + len(self.relay):]:
            local_copy(g.at[idx(me)], R.at[idx(me)])
            for f in every:
                peer = tuple(flip(v, b) for v, b in zip(me, f))
                pair(g.at[idx(peer)], R.at[idx(me)], R.at[idx(peer)], peer)
        return loc, out, arrive

    def start(self, ins, outs, send, recv, local):
        loc, out, _ = self._copies(ins, outs, send, recv, local, arrivals=False)
        for cp in loc + out:
            cp.start()

    def finish(self, ins, outs, send, recv, local):
        loc, out, arrive = self._copies(ins, outs, send, recv, local)
        for cp in arrive:
            cp.wait_recv()
        for cp in out:
            cp.wait_send()
        for cp in loc:
            cp.wait()


def _call(body, operands, out_shape, *, name, grid=(), in_specs=None, out_specs=None, scratch=(), comm=None):
    outs = list(out_shape) if isinstance(out_shape, (list, tuple)) else [out_shape]
    single = not isinstance(out_shape, (list, tuple))
    params = _params(len(grid))
    if comm is None or not comm.arrays:
        res = pl.pallas_call(body, grid=grid, in_specs=in_specs, out_specs=out_specs, out_shape=out_shape,
                             scratch_shapes=list(scratch), name=name, compiler_params=params)(*operands)
        return ([res] if single else list(res)), []
    n_in, n_out, n_s, n_c = len(operands), len(outs), len(scratch), len(comm.arrays)

    def wrapped(*refs):
        pos = [0]

        def take(n):
            pos[0] += n
            return refs[pos[0] - n:pos[0]]

        b_in, c_in, b_out, c_out, b_s, sems = take(n_in), take(n_c), take(n_out), take(n_c), take(n_s), take(3)
        ids = [pl.program_id(d) for d in range(len(grid))]
        first, last = True, True
        for d, i in enumerate(ids):
            first = (i == 0) & first
            last = (i == grid[d] - 1) & last
        if grid:
            pl.when(first)(lambda: comm.start(c_in, c_out, *sems))
        else:
            comm.start(c_in, c_out, *sems)
        body(*b_in, *b_out, *b_s)
        if grid:
            pl.when(last)(lambda: comm.finish(c_in, c_out, *sems))
        else:
            comm.finish(c_in, c_out, *sems)

    n_g = len(comm.gather)
    aliases = {n_in + n_g + r: n_out + n_g + r for r in range(len(comm.relay))}
    out_specs_l = list(out_specs) if isinstance(out_specs, (list, tuple)) else [out_specs]
    res = pl.pallas_call(
        wrapped, grid=grid, in_specs=list(in_specs) + [ANY] * n_c, out_specs=out_specs_l + [ANY] * n_c,
        out_shape=outs + comm.out_shapes(), scratch_shapes=list(scratch) + comm.scratch(), input_output_aliases=aliases,
        name=name, compiler_params=params)(*operands, *comm.arrays)
    return list(res[:n_out]), list(res[n_out:])


def _rowwise(name, fn, rows, consts, out_rows, out_reds=(), tm=512):
    R = rows[0].shape[0]
    tm = min(tm, R)
    assert R % tm == 0
    n_in, n_o = len(rows) + len(consts), len(out_rows)

    def body(*refs):
        outs = fn(*[r[...] for r in refs[:n_in]])
        if not isinstance(outs, (tuple, list)):
            outs = (outs,)
        for r, o in zip(refs[n_in:n_in + n_o], outs[:n_o]):
            r[...] = o.astype(r.dtype)
        red_refs = refs[n_in + n_o:]
        if red_refs:
            @pl.when(pl.program_id(0) == 0)
            def _():
                for r in red_refs:
                    r[...] = jnp.zeros(r.shape, r.dtype)
            for r, o in zip(red_refs, outs[n_o:]):
                r[...] += o

    zero = lambda n: (lambda i: (0,) * n)
    in_specs = [pl.BlockSpec((tm, a.shape[1]), lambda i: (i, 0)) for a in rows]
    in_specs += [pl.BlockSpec(c.shape, zero(c.ndim)) for c in consts]
    out_specs = [pl.BlockSpec((tm, c), lambda i: (i, 0)) for c, _ in out_rows]
    out_specs += [pl.BlockSpec(s, zero(len(s))) for s, _ in out_reds]
    out_shape = [jax.ShapeDtypeStruct((R, c), dt) for c, dt in out_rows]
    out_shape += [jax.ShapeDtypeStruct(s, dt) for s, dt in out_reds]
    return pl.pallas_call(body, grid=(R // tm,), in_specs=in_specs, out_specs=out_specs, out_shape=out_shape,
                          name=name, compiler_params=_params(1))(*rows, *consts)


def _tile(n, cap):
    best = None
    for t in range(LANES, min(n, cap) + 1, LANES):
        if n % t == 0:
            best = t
    return best or n


def _mm(name, pairs, out_dtypes, M, N, epi=None, extras=(), tm=512, tn=512):
    tm, tn = _tile(M, tm), _tile(N, tn)
    n_p, n_e = len(pairs), len(extras)
    modes = [m for _, _, m in pairs]

    def body(*refs):
        accs = [_bdot(refs[2 * k][...], refs[2 * k + 1][...], modes[k]) for k in range(n_p)]
        ex = [r[...] for r in refs[2 * n_p:2 * n_p + n_e]]
        outs = epi(accs, ex) if epi is not None else accs
        for r, o in zip(refs[2 * n_p + n_e:], outs):
            r[...] = o.astype(r.dtype)

    in_specs = []
    ops = []
    for a, b, mode in pairs:
        if mode == NN:
            K = a.shape[1]
            assert a.shape == (M, K) and b.shape == (K, N), (name, a.shape, b.shape)
            in_specs += [pl.BlockSpec((tm, K), lambda i, j: (i, 0)), pl.BlockSpec((K, tn), lambda i, j: (0, j))]
        elif mode == NT:
            K = a.shape[1]
            assert a.shape == (M, K) and b.shape == (N, K), (name, a.shape, b.shape)
            in_specs += [pl.BlockSpec((tm, K), lambda i, j: (i, 0)), pl.BlockSpec((tn, K), lambda i, j: (j, 0))]
        else:
            K = a.shape[0]
            assert a.shape == (K, M) and b.shape == (K, N), (name, a.shape, b.shape)
            in_specs += [pl.BlockSpec((K, tm), lambda i, j: (0, i)), pl.BlockSpec((K, tn), lambda i, j: (0, j))]
        ops += [a, b]
    in_specs += [pl.BlockSpec((tm, tn), lambda i, j: (i, j)) for _ in extras]
    out_specs = [pl.BlockSpec((tm, tn), lambda i, j: (i, j)) for _ in out_dtypes]
    out_shape = [jax.ShapeDtypeStruct((M, N), dt) for dt in out_dtypes]
    res = pl.pallas_call(body, grid=(M // tm, N // tn), in_specs=in_specs, out_specs=out_specs, out_shape=out_shape,
                         name=name, compiler_params=_params(2))(*ops, *extras)
    return res[0] if len(res) == 1 else res


def _rms_fwd(name, x, gain, dtype=BF16):
    return _rowwise(name, _rms, [x], [gain], [(x.shape[1], dtype)])[0]


def _rms_bwd(name, x, gain, dh, dres=None):
    def fn(x, dh, *rest):
        g = rest[-1]
        _, vjp = jax.vjp(_rms, x, g)
        dx, dg = vjp(dh)
        if dres is not None:
            dx = dx + rest[0]
        return dx, dg
    rows = [x, dh] + ([dres] if dres is not None else [])
    d = x.shape[1]
    return _rowwise(name, fn, rows, [gain], [(d, F32)], [((1, d), F32)])


def _ffn_specs(gcols, grows, ig, iu, idn):
    n = gcols.shape[2]
    D = grows.shape[2]
    wg = pl.BlockSpec((None, D, n), lambda i, j: (j, ig, 0))
    wu = pl.BlockSpec((None, D, n), lambda i, j: (j, iu, 0))
    wd = pl.BlockSpec((None, n, D), lambda i, j: (j, idn, 0))
    return n, wg, wu, wd


def _ffn_fwd(name, x, gain, gcols, grows, ig, iu, idn, tm=1024, comm=None):
    T, D = x.shape
    tm = min(T, tm)
    n, wg_s, wu_s, wd_s = _ffn_specs(gcols, grows, ig, iu, idn)
    last = N_DEV - 1

    def body(x_ref, g_ref, wg_ref, wu_ref, wd_ref, y_ref, h_s, acc):
        j = pl.program_id(1)

        @pl.when(j == 0)
        def _():
            h_s[...] = _rms(x_ref[...], g_ref[...]).astype(BF16)
            acc[...] = jnp.zeros(acc.shape, F32)
        h = h_s[...]
        z = _silu(_dot(h, wg_ref[...], NN)) * _dot(h, wu_ref[...], NN)
        acc[...] += _dot(z.astype(BF16), wd_ref[...], NN)

        @pl.when(j == last)
        def _():
            y_ref[...] = x_ref[...] + 0.5 * acc[...]

    row = pl.BlockSpec((tm, D), lambda i, j: (i, 0))
    (y,), moved = _call(
        body, [x, gain, gcols, gcols, grows], [jax.ShapeDtypeStruct((T, D), F32)], name=name, grid=(T // tm, N_DEV),
        in_specs=[row, pl.BlockSpec((1, D), lambda i, j: (0, 0)), wg_s, wu_s, wd_s], out_specs=[row],
        scratch=[pltpu.VMEM((tm, D), BF16), pltpu.VMEM((tm, D), F32)], comm=comm)
    return y, moved


def _ffn_bwd(tag, x, gain, gcols, grows, ig, iu, idn, dy, tm=512, comm_a=None, comm_w=None):
    T, D = x.shape
    tm = min(T, tm)
    n, wg_s, wu_s, wd_s = _ffn_specs(gcols, grows, ig, iu, idn)
    last = N_DEV - 1

    def body(x_ref, dy_ref, g_ref, wg_ref, wu_ref, wd_ref, dx_ref, dg_ref, h_ref, z_ref, da_ref, db_ref, dh_acc):
        i, j = pl.program_id(0), pl.program_id(1)

        @pl.when(j == 0)
        def _():
            h_ref[...] = _rms(x_ref[...], g_ref[...]).astype(BF16)
            dh_acc[...] = jnp.zeros(dh_acc.shape, F32)

        @pl.when((i == 0) & (j == 0))
        def _():
            dg_ref[...] = jnp.zeros(dg_ref.shape, F32)
        h = h_ref[...]
        a, b = _dot(h, wg_ref[...], NN), _dot(h, wu_ref[...], NN)
        dz = 0.5 * _dot(dy_ref[...].astype(BF16), wd_ref[...], NT)
        s = _sigmoid(a)
        si = a * s
        da = (dz * b * (s + si * (1.0 - s))).astype(BF16)
        db = (dz * si).astype(BF16)
        z_ref[...] = (si * b).astype(BF16)
        da_ref[...] = da
        db_ref[...] = db
        dh_acc[...] += _dot(da, wg_ref[...], NT) + _dot(db, wu_ref[...], NT)

        @pl.when(j == last)
        def _():
            _, vjp = jax.vjp(_rms, x_ref[...], g_ref[...])
            dx, dg = vjp(dh_acc[...])
            dx_ref[...] = dx + dy_ref[...]
            dg_ref[...] += dg

    row = pl.BlockSpec((tm, D), lambda i, j: (i, 0))
    vec = pl.BlockSpec((1, D), lambda i, j: (0, 0))
    hid = pl.BlockSpec((None, tm, n), lambda i, j: (j, i, 0))
    hidden = jax.ShapeDtypeStruct((N_DEV, T, n), BF16)
    (dx, dgain, h, z, da, db), moved_a = _call(
        body, [x, dy, gain, gcols, gcols, grows],
        [jax.ShapeDtypeStruct((T, D), F32), jax.ShapeDtypeStruct((1, D), F32), jax.ShapeDtypeStruct((T, D), BF16),
         hidden, hidden, hidden],
        name=tag + "_a", grid=(T // tm, N_DEV), in_specs=[row, row, vec, wg_s, wu_s, wd_s],
        out_specs=[row, vec, row, hid, hid, hid], scratch=[pltpu.VMEM((tm, D), F32)], comm=comm_a)

    def wbody(h_ref, dy_ref, z_ref, da_ref, db_ref, dwg_ref, dwu_ref, dwd_ref):
        h = h_ref[...]
        dwg_ref[...] = _dot(h, da_ref[...], TN).astype(BF16)
        dwu_ref[...] = _dot(h, db_ref[...], TN).astype(BF16)
        dwd_ref[...] = (0.5 * _dot(z_ref[...], dy_ref[...].astype(BF16), TN)).astype(BF16)

    full = pl.BlockSpec((T, D), lambda j: (0, 0))
    hid_all = pl.BlockSpec((None, T, n), lambda j: (j, 0, 0))
    (dwg, dwu, dwd), moved_w = _call(
        wbody, [h, dy, z, da, db],
        [jax.ShapeDtypeStruct((N_DEV, D, n), BF16)] * 2 + [jax.ShapeDtypeStruct((N_DEV, n, D), BF16)],
        name=tag + "_w", grid=(N_DEV,), in_specs=[full, full, hid_all, hid_all, hid_all],
        out_specs=[pl.BlockSpec((None, D, n), lambda j: (j, 0, 0))] * 2 + [pl.BlockSpec((None, n, D), lambda j: (j, 0, 0))],
        comm=comm_w)
    return dx, dgain, dwg, dwu, dwd, moved_a, moved_w


def _wcols_spec(gw, l, grid_rank):
    _, _, n = gw.shape
    K = D_MODEL
    zero = (lambda i: (0, l, 0)) if grid_rank == 1 else (lambda i, j: (0, l, 0))
    return n, K, pl.BlockSpec((N_DEV, K, n), zero)


def _proj_cols(name, h, gw, l, tm=512, comm=None):
    T = h.shape[0]
    tm = min(T, tm)
    n, K, wspec = _wcols_spec(gw, l, 1)

    def body(h_ref, w_ref, o_ref):
        h = h_ref[...]
        for j in range(N_DEV):
            o_ref[:, pl.ds(j * n, n)] = _dot(h, w_ref[j], NN)

    (proj,), moved = _call(
        body, [h, gw], [jax.ShapeDtypeStruct((T, N_DEV * n), F32)], name=name, grid=(T // tm,),
        in_specs=[pl.BlockSpec((tm, K), lambda i: (i, 0)), wspec], out_specs=[pl.BlockSpec((tm, N_DEV * n), lambda i: (i, 0))],
        comm=comm)
    return proj, moved


def _proj_cols_bwd(tag, h, dproj, gw, l, tm=512, tk=512, comm=None):
    T = h.shape[0]
    tm = min(T, tm)
    n, K, wspec = _wcols_spec(gw, l, 1)

    def dh_body(dp_ref, w_ref, o_ref):
        acc = jnp.zeros(o_ref.shape, F32)
        for j in range(N_DEV):
            acc = acc + _dot(dp_ref[:, pl.ds(j * n, n)], w_ref[j], NT)
        o_ref[...] = acc

    (dh,), moved = _call(
        dh_body, [dproj, gw], [jax.ShapeDtypeStruct((T, K), F32)], name=tag + "_dh", grid=(T // tm,),
        in_specs=[pl.BlockSpec((tm, N_DEV * n), lambda i: (i, 0)), wspec], out_specs=[pl.BlockSpec((tm, K), lambda i: (i, 0))],
        comm=comm)

    def dw_body(h_ref, dp_ref, o_ref):
        h = h_ref[...]
        for j in range(N_DEV):
            o_ref[j] = _dot(h, dp_ref[:, pl.ds(j * n, n)], TN).astype(BF16)

    dw = pl.pallas_call(
        dw_body, grid=(K // tk,), in_specs=[pl.BlockSpec((T, tk), lambda i: (0, i)), pl.BlockSpec((T, N_DEV * n), lambda i: (0, 0))],
        out_specs=pl.BlockSpec((N_DEV, tk, n), lambda i: (0, i, 0)), out_shape=jax.ShapeDtypeStruct((N_DEV, K, n), BF16),
        name=tag + "_dw", compiler_params=_params(1))(h, dproj)
    return dh, dw, moved


def _block_tri(n, reverse=False):
    r = lax.broadcasted_iota(jnp.int32, (n, n), 0)
    c = lax.broadcasted_iota(jnp.int32, (n, n), 1)
    same = (r // HG_BLOCK) == (c // HG_BLOCK)
    return (same & ((c >= r) if reverse else (c <= r))).astype(F32)


def _hgrn_prep(q_ref, f_ref, lbv, qs, ks, cs, T):
    pt = min(T, 256)
    tri = _block_tri(pt)
    for p in range(T // pt):
        rows = pl.ds(p * pt, pt)
        f = lbv + (1.0 - lbv) * _sigmoid(f_ref[rows, :])
        qs[rows, :] = _silu(q_ref[rows, :])
        ks[rows, :] = 1.0 - f
        cs[rows, :] = _dot(tri, jnp.log(f), NN, precision=lax.Precision.HIGHEST)


HG_GROUP = 128


def _groups_loop(nb, fn):
    gp = HG_GROUP if nb % HG_GROUP == 0 else nb

    def step(i, carry):
        base = pl.multiple_of(i * (gp * HG_BLOCK), gp * HG_BLOCK)
        fn(lambda t: pl.ds(base + t, gp, stride=HG_BLOCK))
        return carry

    lax.fori_loop(0, nb // gp, step, 0)


def _gate_out(o, og, g):
    return _rms(o, og) * _silu(g)


HG_UNROLL = 16


def _block_rows(n):
    return pl.ds(pl.multiple_of(n * HG_BLOCK, HG_BLOCK), HG_BLOCK)


def _blocks_loop(nb, fn):
    u = HG_UNROLL if nb % HG_UNROLL == 0 else 1

    def step(i, carry):
        for k in range(u):
            fn(i * u + k)
        return carry

    lax.fori_loop(0, nb // u, step, 0)


def _scan_states(buf, cs, nb, reverse=False):
    def step(m, st):
        n = nb - 1 - m if reverse else m
        own = buf[n]
        buf[n] = st
        rows = _block_rows(n)
        return jnp.exp(cs[rows, :][HG_BLOCK - 1:HG_BLOCK, :]) * st + own

    lax.fori_loop(0, nb, step, jnp.zeros(buf.shape[1:], F32))


def _hgrn_states(i_ref, ks, cs, states, nb):
    def own_step(n):
        rows = _block_rows(n)
        c = cs[rows, :]
        states[n] = _fdot(i_ref[rows, :], ks[rows, :] * jnp.exp(c[HG_BLOCK - 1:HG_BLOCK, :] - c), TN)

    _blocks_loop(nb, own_step)
    _scan_states(states, cs, nb)


def _hgrn_fwd(name, proj, lb, og, comm=None):
    T = proj.shape[0]
    nb = T // HG_BLOCK
    hd = HG_HEAD_DIM

    def body(q_ref, f_ref, i_ref, g_ref, lb_ref, og_ref, main_ref, o_ref, qs, ks, cs, states):
        _hgrn_prep(q_ref, f_ref, lb_ref[...], qs, ks, cs, T)
        def pairs(at):
            for t in range(HG_BLOCK):
                qt, ct = qs[at(t), :], cs[at(t), :]
                acc = jnp.zeros(qt.shape, F32)
                for s in range(t + 1):
                    w = qt * ks[at(s), :] * jnp.exp(ct - cs[at(s), :])
                    acc = acc + jnp.sum(w, axis=-1, keepdims=True) * i_ref[at(s), :]
                o_ref[at(t), :] = acc

        _groups_loop(nb, pairs)

        _hgrn_states(i_ref, ks, cs, states, nb)

        def out_step(n):
            rows = _block_rows(n)
            o_ref[rows, :] += _fdot(qs[rows, :] * jnp.exp(cs[rows, :]), states[n], NT)

        _blocks_loop(nb, out_step)
        pt = min(T, 256)
        for p in range(T // pt):
            rows = pl.ds(p * pt, pt)
            main_ref[rows, :] = _gate_out(o_ref[rows, :], og_ref[...], g_ref[rows, :])

    nh = HG_HEADS
    col = lambda off: pl.BlockSpec((T, hd), lambda h, off=off: (0, off + h))
    (main, o), moved = _call(
        body, [proj, proj, proj, proj, lb, og], [jax.ShapeDtypeStruct((T, MAIN_WIDTH), F32)] * 2, name=name, grid=(nh,),
        in_specs=[col(0), col(nh), col(2 * nh), col(3 * nh), pl.BlockSpec((1, hd), lambda h: (0, h)),
                  pl.BlockSpec((1, hd), lambda h: (0, 0))],
        out_specs=[col(0), col(0)], scratch=[pltpu.VMEM((T, hd), F32)] * 3 + [pltpu.VMEM((nb, hd, hd), F32)], comm=comm)
    return main, o, moved


def _hgrn_bwd(name, proj, o, dmix, lb, og, comm=None):
    T = proj.shape[0]
    nb = T // HG_BLOCK
    hd = HG_HEAD_DIM
    pt = min(T, 256)

    def body(q_ref, f_ref, i_ref, g_ref, o_ref, dm_ref, lb_ref, og_ref,
             dq_ref, df_ref, di_ref, dg_ref, dlb_ref, dog_ref, qs, ks, cs, dos, dqs, dks, dvs, states, behind):
        lbv = lb_ref[...]
        _hgrn_prep(q_ref, f_ref, lbv, qs, ks, cs, T)
        dog = jnp.zeros((1, hd), F32)
        for p in range(T // pt):
            rows = pl.ds(p * pt, pt)
            _, vjp = jax.vjp(_gate_out, o_ref[rows, :], og_ref[...], g_ref[rows, :])
            do, dog_p, dg = vjp(dm_ref[rows, :])
            dos[rows, :] = do
            dg_ref[rows, :] = dg.astype(dg_ref.dtype)
            dog = dog + dog_p

        @pl.when(pl.program_id(0) == 0)
        def _():
            dog_ref[...] = jnp.zeros(dog_ref.shape, F32)
        dog_ref[...] += dog

        def pairs(at):
            for t in range(HG_BLOCK):
                dqs[at(t), :] = jnp.zeros((HG_GROUP if nb % HG_GROUP == 0 else nb, hd), F32)
            for s in range(HG_BLOCK):
                k_s, c_s, v_s = ks[at(s), :], cs[at(s), :], i_ref[at(s), :]
                dk = jnp.zeros(k_s.shape, F32)
                dv = jnp.zeros(k_s.shape, F32)
                for t in range(s, HG_BLOCK):
                    q_t, do_t = qs[at(t), :], dos[at(t), :]
                    e = jnp.exp(cs[at(t), :] - c_s)
                    a = jnp.sum(q_t * k_s * e, axis=-1, keepdims=True)
                    g = jnp.sum(do_t * v_s, axis=-1, keepdims=True)
                    dqs[at(t), :] += g * k_s * e
                    dk = dk + g * q_t * e
                    dv = dv + a * do_t
                dks[at(s), :] = dk
                dvs[at(s), :] = dv

        _groups_loop(nb, pairs)

        _hgrn_states(i_ref, ks, cs, states, nb)

        def own_step(n):
            rows = _block_rows(n)
            behind[n] = _fdot(dos[rows, :], qs[rows, :] * jnp.exp(cs[rows, :]), TN)

        _blocks_loop(nb, own_step)
        _scan_states(behind, cs, nb, reverse=True)

        def grad_step(n):
            rows = _block_rows(n)
            c = cs[rows, :]
            ec, ek = jnp.exp(c), jnp.exp(c[HG_BLOCK - 1:HG_BLOCK, :] - c)
            dst = behind[n]
            dqs[rows, :] += _fdot(dos[rows, :], states[n], NN) * ec
            dks[rows, :] += _fdot(i_ref[rows, :], dst, NN) * ek
            dvs[rows, :] += _fdot(ks[rows, :] * ek, dst, NT)

        _blocks_loop(nb, grad_step)

        full = (lax.broadcasted_iota(jnp.int32, (pt, pt), 1) >= lax.broadcasted_iota(jnp.int32, (pt, pt), 0)).astype(F32)
        carry = jnp.zeros((1, hd), F32)
        dlb = jnp.zeros((1, hd), F32)
        for p in reversed(range(T // pt)):
            rows = pl.ds(p * pt, pt)
            q, k, dq, dk = qs[rows, :], ks[rows, :], dqs[rows, :], dks[rows, :]
            db = q * dq - k * dk
            dlf = _dot(full, db, NN, precision=lax.Precision.HIGHEST) + carry
            carry = carry + jnp.sum(db, axis=0, keepdims=True)
            sg = _sigmoid(f_ref[rows, :])
            df = dlf / (1.0 - k) - dk
            df_ref[rows, :] = (df * (1.0 - lbv) * sg * (1.0 - sg)).astype(df_ref.dtype)
            dlb = dlb + jnp.sum(df * (1.0 - sg), axis=0, keepdims=True)
            qr = q_ref[rows, :]
            sq = _sigmoid(qr)
            dq_ref[rows, :] = (dq * (sq + qr * sq * (1.0 - sq))).astype(dq_ref.dtype)
            di_ref[rows, :] = dvs[rows, :].astype(di_ref.dtype)
        dlb_ref[...] = dlb

    nh = HG_HEADS
    col = lambda off: pl.BlockSpec((T, hd), lambda h, off=off: (0, off + h))
    vec = pl.BlockSpec((1, hd), lambda h: (0, h))
    one = pl.BlockSpec((1, hd), lambda h: (0, 0))
    outs, moved = _call(
        body, [proj, proj, proj, proj, o, dmix, lb, og],
        [jax.ShapeDtypeStruct((T, MAIN_WIDTH), BF16)] * 4
        + [jax.ShapeDtypeStruct((1, MAIN_WIDTH), F32), jax.ShapeDtypeStruct((1, hd), F32)],
        name=name, grid=(nh,), in_specs=[col(0), col(nh), col(2 * nh), col(3 * nh), col(0), col(0), vec, one],
        out_specs=[col(0), col(0), col(0), col(0), vec, one],
        scratch=[pltpu.VMEM((T, hd), F32)] * 7 + [pltpu.VMEM((nb, hd, hd), F32)] * 2, comm=comm)
    return (*outs, moved)


def _softmax_rows(s):
    p = jnp.exp(s - jnp.max(s, axis=-1, keepdims=True))
    return p, jnp.sum(p, axis=-1, keepdims=True)


def _fox_probs(q, k, cr_ref, hh, qi, tq):
    q0 = qi * tq
    pieces = ([(0, q0)] if qi else []) + [(q0, q0 + tq)]
    ss = []
    for a, b in pieces:
        s = _dot(q, k[a:b], NT) - cr_ref[hh, :, pl.ds(a, b - a)]
        if a == q0:
            causal = lax.broadcasted_iota(jnp.int32, s.shape, 1) <= lax.broadcasted_iota(jnp.int32, s.shape, 0)
            s = jnp.where(causal, s, -jnp.inf)
        ss.append(s)
    m = functools.reduce(jnp.maximum, [jnp.max(s, axis=-1, keepdims=True) for s in ss])
    ps = [jnp.exp(s - m) for s in ss]
    l = functools.reduce(jnp.add, [jnp.sum(p, axis=-1, keepdims=True) for p in ps])
    return [(a, b, p) for (a, b), p in zip(pieces, ps)], l


def _fox_specs(T):
    w = 2 * FOX_HEAD_DIM
    n = MAIN_WIDTH // w
    col = lambda off: pl.BlockSpec((T, w), lambda p, off=off: (0, off + p))
    cr = pl.BlockSpec((2, 1, T), lambda p: (p, 0, 0))
    gain = pl.BlockSpec((1, FOX_HEAD_DIM), lambda p: (0, 0))
    return n, col, cr, gain


def _fox_fwd(name, proj, kvf, cr, gq, gk, comm=None):
    T = proj.shape[0]
    tq = min(T, 256)
    hd = FOX_HEAD_DIM
    scale = hd ** -0.5

    def body(q_ref, g_ref, k_ref, v_ref, cr_ref, gq_ref, gk_ref, main_ref, o_ref):
        for hh in range(2):
            lanes = pl.ds(hh * hd, hd)
            k = _rms(k_ref[:, lanes], gk_ref[...]).astype(BF16)
            v = v_ref[:, lanes].astype(BF16)
            for qi in range(T // tq):
                rows = pl.ds(qi * tq, tq)
                q = (_rms(q_ref[rows, lanes], gq_ref[...]) * scale).astype(BF16)
                ps, l = _fox_probs(q, k, cr_ref, hh, qi, tq)
                o = functools.reduce(jnp.add, [_dot(p.astype(BF16), v[a:b], NN) for a, b, p in ps]) / l
                o_ref[rows, lanes] = o
                main_ref[rows, lanes] = o * _sigmoid(g_ref[rows, lanes])

    n, col, crs, gain = _fox_specs(T)
    (main, o), moved = _call(
        body, [proj, proj, kvf, kvf, cr, gq, gk], [jax.ShapeDtypeStruct((T, MAIN_WIDTH), F32)] * 2, name=name, grid=(n,),
        in_specs=[col(0), col(n), col(0), col(n), crs, gain, gain], out_specs=[col(0), col(0)], comm=comm)
    return main, o, moved


def _fox_bwd(name, proj, kvf, cr, gq, gk, o, dmix, pdk, pdv, pdc, comm=None):
    T = proj.shape[0]
    tq = min(T, 256)
    hd = FOX_HEAD_DIM
    scale = hd ** -0.5

    def body(q_ref, g_ref, k_ref, v_ref, cr_ref, gq_ref, gk_ref, o_ref, dm_ref, pdk_ref, pdv_ref, pdc_ref,
             dq_ref, dg_ref, dk_ref, dv_ref, dc_ref, dgq_ref, dgk_ref, dka, dva, dca):
        dgq = jnp.zeros((1, hd), F32)
        dgk = jnp.zeros((1, hd), F32)
        for hh in range(2):
            lanes = pl.ds(hh * hd, hd)
            k32, vjp_k = jax.vjp(_rms, k_ref[:, lanes], gk_ref[...])
            k = k32.astype(BF16)
            v = v_ref[:, lanes].astype(BF16)
            dka[...] = jnp.zeros(dka.shape, F32)
            dva[...] = jnp.zeros(dva.shape, F32)
            dca[...] = jnp.zeros(dca.shape, F32)
            for qi in range(T // tq):
                rows = pl.ds(qi * tq, tq)
                q32, vjp_q = jax.vjp(_rms, q_ref[rows, lanes], gq_ref[...])
                q = (q32 * scale).astype(BF16)
                ps, l = _fox_probs(q, k, cr_ref, hh, qi, tq)
                ps = [(a, b, p / l) for a, b, p in ps]
                sg = _sigmoid(g_ref[rows, lanes])
                dm = dm_ref[rows, lanes]
                do = (dm * sg).astype(BF16)
                dg_ref[rows, lanes] = (dm * o_ref[rows, lanes] * sg * (1.0 - sg)).astype(dg_ref.dtype)
                dps = [_dot(do, v[a:b], NT) for a, b, _ in ps]
                delta = functools.reduce(jnp.add, [jnp.sum(p * dp, axis=-1, keepdims=True) for (_, _, p), dp in zip(ps, dps)])
                dq = jnp.zeros((tq, hd), F32)
                for (a, b, p), dp in zip(ps, dps):
                    ds = p * (dp - delta)
                    dsb = ds.astype(BF16)
                    dq = dq + _dot(dsb, k[a:b], NN)
                    dka[:, pl.ds(a, b - a)] += _dot(q, dsb, TN)
                    dva[:, pl.ds(a, b - a)] += _dot(do, p.astype(BF16), TN)
                    dca[:, pl.ds(a, b - a)] -= jnp.sum(ds, axis=0, keepdims=True)
                dqr, dgq_p = vjp_q(dq * scale)
                dq_ref[rows, lanes] = dqr.astype(dq_ref.dtype)
                dgq = dgq + dgq_p
            dkr, dgk_p = vjp_k(dka[...].T)
            dgk = dgk + dgk_p
            dk_ref[:, lanes] = dkr + pdk_ref[:, lanes]
            dv_ref[:, lanes] = dva[...].T + pdv_ref[:, lanes]
            dc_ref[hh] = dca[...] + pdc_ref[hh]

        @pl.when(pl.program_id(0) == 0)
        def _():
            dgq_ref[...] = jnp.zeros(dgq_ref.shape, F32)
            dgk_ref[...] = jnp.zeros(dgk_ref.shape, F32)
        dgq_ref[...] += dgq
        dgk_ref[...] += dgk

    n, col, crs, gain = _fox_specs(T)
    wide = jax.ShapeDtypeStruct((T, MAIN_WIDTH), F32)
    half = jax.ShapeDtypeStruct((T, MAIN_WIDTH), BF16)
    outs, moved = _call(
        body, [proj, proj, kvf, kvf, cr, gq, gk, o, dmix, pdk, pdv, pdc],
        [half, half, wide, wide, jax.ShapeDtypeStruct((FOX_HEADS, 1, T), F32),
         jax.ShapeDtypeStruct((1, hd), F32), jax.ShapeDtypeStruct((1, hd), F32)],
        name=name, grid=(n,),
        in_specs=[col(0), col(n), col(0), col(n), crs, gain, gain, col(0), col(0), col(0), col(0), crs],
        out_specs=[col(0), col(0), col(0), col(0), crs, gain, gain],
        scratch=[pltpu.VMEM((hd, T), F32), pltpu.VMEM((hd, T), F32), pltpu.VMEM((1, T), F32)], comm=comm)
    return (*outs, moved)


def _mem_specs(T, width):
    tq = min(T, 512)
    q = pl.BlockSpec((tq, MEM_WIDTH), lambda i, c=(width - MEM_WIDTH) // MEM_WIDTH: (i, c))
    gain = pl.BlockSpec((1, MEM_HEAD_DIM), lambda i: (0, 0))
    return tq, q, gain


def _mem_fwd(name, proj, kv, gq, gk):
    T, W = proj.shape
    hd = MEM_HEAD_DIM
    tq, qspec, gain = _mem_specs(T, W)

    def body(q_ref, kv_ref, gq_ref, gk_ref, o_ref):
        for h in range(MEM_HEADS):
            lanes = pl.ds(h * hd, hd)
            q = _rms(q_ref[:, lanes], gq_ref[...]).astype(BF16)
            k = _rms(kv_ref[:, lanes], gk_ref[...]).astype(BF16)
            v = kv_ref[:, pl.ds(MEM_WIDTH + h * hd, hd)].astype(BF16)
            p, l = _softmax_rows(_dot(q, k, NT) * (hd ** -0.5))
            o_ref[:, lanes] = _dot(p.astype(BF16), v, NN) / l

    return pl.pallas_call(
        body, grid=(T // tq,),
        in_specs=[qspec, pl.BlockSpec(kv.shape, lambda i: (0, 0)), gain, gain],
        out_specs=pl.BlockSpec((tq, MEM_WIDTH), lambda i: (i, 0)),
        out_shape=jax.ShapeDtypeStruct((T, MEM_WIDTH), F32),
        name=name, compiler_params=_params(1))(proj, kv, gq, gk)


def _mem_bwd(name, proj, kv, gq, gk, dmix):
    T, W = proj.shape
    hd = MEM_HEAD_DIM
    scale = hd ** -0.5
    tq, qspec, gain = _mem_specs(T, W)

    def body(q_ref, kv_ref, gq_ref, gk_ref, dm_ref, dq_ref, dkv_ref, dgq_ref, dgk_ref):
        @pl.when(pl.program_id(0) == 0)
        def _():
            dkv_ref[...] = jnp.zeros(dkv_ref.shape, F32)
            dgq_ref[...] = jnp.zeros(dgq_ref.shape, F32)
            dgk_ref[...] = jnp.zeros(dgk_ref.shape, F32)
        for h in range(MEM_HEADS):
            lanes = pl.ds(h * hd, hd)
            vl = pl.ds(MEM_WIDTH + h * hd, hd)
            q32, vjp_q = jax.vjp(_rms, q_ref[:, lanes], gq_ref[...])
            k32, vjp_k = jax.vjp(_rms, kv_ref[:, lanes], gk_ref[...])
            q, k, v = q32.astype(BF16), k32.astype(BF16), kv_ref[:, vl].astype(BF16)
            p, l = _softmax_rows(_dot(q, k, NT) * scale)
            p = p / l
            do = dm_ref[:, lanes].astype(BF16)
            dp = _dot(do, v, NT)
            dsb = (p * (dp - jnp.sum(p * dp, axis=-1, keepdims=True))).astype(BF16)
            dqr, dgq_p = vjp_q(_dot(dsb, k, NN) * scale)
            dkr, dgk_p = vjp_k(_dot(dsb, q, TN) * scale)
            dq_ref[:, lanes] = dqr.astype(dq_ref.dtype)
            dkv_ref[:, lanes] += dkr
            dkv_ref[:, vl] += _dot(p.astype(BF16), do, TN)
            dgq_ref[...] += dgq_p
            dgk_ref[...] += dgk_p

    return pl.pallas_call(
        body, grid=(T // tq,),
        in_specs=[qspec, pl.BlockSpec(kv.shape, lambda i: (0, 0)), gain, gain,
                  pl.BlockSpec((tq, MEM_WIDTH), lambda i: (i, MAIN_WIDTH // MEM_WIDTH))],
        out_specs=[pl.BlockSpec((tq, MEM_WIDTH), lambda i: (i, 0)), pl.BlockSpec(kv.shape, lambda i: (0, 0)), gain, gain],
        out_shape=[jax.ShapeDtypeStruct((T, MEM_WIDTH), BF16), jax.ShapeDtypeStruct(kv.shape, F32),
                   jax.ShapeDtypeStruct((1, hd), F32), jax.ShapeDtypeStruct((1, hd), F32)],
        name=name, compiler_params=_params(1))(proj, kv, gq, gk, dmix)


def _cumsum_rows(name, x, reverse=False):
    T, C = x.shape
    pt = min(T, 256)

    def body(x_ref, o_ref):
        r = lax.broadcasted_iota(jnp.int32, (pt, pt), 0)
        c = lax.broadcasted_iota(jnp.int32, (pt, pt), 1)
        tri = ((c >= r) if reverse else (c <= r)).astype(F32)
        carry = jnp.zeros((1, C), F32)
        order = range(T // pt)
        for p in (reversed(order) if reverse else order):
            rows = pl.ds(p * pt, pt)
            blk = x_ref[rows, :]
            o_ref[rows, :] = _dot(tri, blk, NN, precision=lax.Precision.HIGHEST) + carry
            carry = carry + jnp.sum(blk, axis=0, keepdims=True)

    return pl.pallas_call(body, out_shape=jax.ShapeDtypeStruct((T, C), F32), name=name,
                          compiler_params=pltpu.CompilerParams(vmem_limit_bytes=V7X_VMEM_LIMIT))(x)


def _all_gather(name, xs):
    n = len(xs)

    def body(*refs):
        x_refs, out_refs = refs[:n], refs[n:2 * n]
        send_sems, recv_sems, local_sems = refs[2 * n:]
        mx, my, mc = _mesh_pos()
        me, sibling = (mx, my, mc), (mx, my, 1 - mc)
        chips = [(1 - mx, my), (mx, 1 - my), (1 - mx, 1 - my)]

        def slot(a, px, py, pc):
            return out_refs[a].at[4 * px + 2 * py + pc]

        def copy(a, k, block, to, src=None):
            return pltpu.make_async_remote_copy(
                src_ref=slot(a, *block) if src is None else src, dst_ref=slot(a, *block),
                send_sem=send_sems.at[7 * a + k], recv_sem=recv_sems.at[7 * a + k], device_id=to, device_id_type=MESH)

        mine = [pltpu.make_async_copy(x_refs[a], slot(a, *me), local_sems.at[a]) for a in range(n)]
        first = []
        for a in range(n):
            mine[a].start()
            first.append(copy(a, 0, me, sibling, src=x_refs[a]))
            first += [copy(a, 1 + j, me, (*chip, mc), src=x_refs[a]) for j, chip in enumerate(chips)]
        for cp in first:
            cp.start()
        passed = []
        for j, chip in enumerate(chips):
            for a in range(n):
                copy(a, 1 + j, (*chip, mc), me).wait_recv()
                passed.append(copy(a, 4 + j, (*chip, mc), sibling))
                passed[-1].start()
        for a in range(n):
            copy(a, 0, sibling, me).wait_recv()
            for j, chip in enumerate(chips):
                copy(a, 4 + j, (*chip, 1 - mc), me).wait_recv()
        for cp in first + passed:
            cp.wait_send()
        for cp in mine:
            cp.wait()

    return pl.pallas_call(
        body, out_shape=[jax.ShapeDtypeStruct((N_DEV,) + x.shape, x.dtype) for x in xs], in_specs=[ANY] * n, out_specs=[ANY] * n,
        scratch_shapes=[pltpu.SemaphoreType.DMA((7 * n,)), pltpu.SemaphoreType.DMA((7 * n,)), pltpu.SemaphoreType.DMA((n,))],
        name=name)(*xs)


def _row_tile(R, cap):
    best = None
    for t in range(8, min(R, cap) + 1, 8):
        if R % t == 0:
            best = t
    return best or R


def _sum_slabs(name, a, out_dtype):
    n, R, C = a.shape
    tm = _row_tile(R, 512)

    def body(*refs):
        acc = refs[0][...].astype(F32)
        for r in refs[1:n]:
            acc = acc + r[...].astype(F32)
        refs[n][...] = acc.astype(out_dtype)

    return pl.pallas_call(
        body, grid=(R // tm,),
        in_specs=[pl.BlockSpec((None, tm, C), lambda i, q=q: (q, i, 0)) for q in range(n)],
        out_specs=pl.BlockSpec((tm, C), lambda i: (i, 0)), out_shape=jax.ShapeDtypeStruct((R, C), out_dtype),
        name=name, compiler_params=_params(1))(*([a] * n))


SMALL = ["ffn1_norm", "mix_norm", "mem_norm", "mem_q_gain", "mem_k_gain", "hgrn_o_gain", "fox_q_gain", "kv_norm",
         "fox_f_bias", "fox_k_gain", "ffn2_norm"]
COLS352 = ["ffn1_w_gate", "ffn1_w_up", "ffn2_w_gate", "ffn2_w_up"]
KV_WIDTH = 2 * MAIN_WIDTH + FOX_HEADS


def _pad_cols(w, width):
    return jnp.pad(w, [(0, 0)] * (w.ndim - 1) + [(0, width - w.shape[-1])])


def _pad128(a):
    flat = a.reshape(-1)
    return jnp.pad(flat, (0, -flat.shape[0] % LANES))


def _small_pack(parts):
    flat = jnp.concatenate([_pad128(p) for p in parts])
    rows = -(-flat.shape[0] // LANES)
    flat = jnp.pad(flat, (0, (-rows % 8) * LANES))
    return flat.reshape(-1, LANES)


def _small_unpack(buf, shapes):
    flat = buf.reshape(-1)
    out, off = [], 0
    for s in shapes:
        n = 1
        for d in s:
            n *= d
        out.append(flat[off:off + n].reshape(s))
        off += n + (-n % LANES)
    return out


def _lb_fn(l0, l1):
    m = lax.stop_gradient(jnp.maximum(l0, l1))
    e0, e1 = jnp.exp(l0 - m), jnp.exp(l1 - m)
    p0, p1 = e0 / (e0 + e1), e1 / (e0 + e1)
    return p0 - p0, (p0 + p1) - p0


def _lb_fwd(logits):
    return _rowwise("lb", _lb_fn, [logits[0:1], logits[1:2]], [], [(MAIN_WIDTH, F32)] * 2)


def _lb_bwd(logits, dlb0, dlb1):
    def fn(l0, l1, d0, d1):
        _, vjp = jax.vjp(_lb_fn, l0, l1)
        return vjp((d0, d1))
    return _rowwise("lb_bwd", fn, [logits[0:1], logits[1:2], dlb0, dlb1], [], [(MAIN_WIDTH, F32)] * 2)


def _adamw_fn(w, g, m, v):
    m = ADAM_B1 * m + (1.0 - ADAM_B1) * g
    v = ADAM_B2 * v + (1.0 - ADAM_B2) * jnp.square(g)
    m_hat = m / (1.0 - ADAM_B1 ** ADAM_STEP)
    v_hat = v / (1.0 - ADAM_B2 ** ADAM_STEP)
    return -ADAM_LR * (m_hat / (jnp.sqrt(v_hat) + ADAM_EPS) + ADAM_WD * w), m, v


def _sum_adamw(name, landed, w, m, v, comm=None):
    L, r, c = w.shape
    tm = _row_tile(r, 128)
    n_i = r // tm

    def body(*refs):
        land, (w_ref, m_ref, v_ref), outs = refs[:L], refs[L:L + 3], refs[L + 3:]
        for k in range(L):
            @pl.when(pl.program_id(0) == k)
            def _(k=k):
                g = land[k][0].astype(F32)
                for s in range(1, N_DEV):
                    g = g + land[k][s].astype(F32)
                for ref, val in zip(outs, (g,) + _adamw_fn(w_ref[...], g, m_ref[...], v_ref[...])):
                    ref[...] = val

    held = lambda k: (lambda l, i: (0, jnp.where(l < k, 0, jnp.where(l == k, i, n_i - 1)), 0))
    cur = pl.BlockSpec((None, tm, c), lambda l, i: (l, i, 0))
    return _call(body, [*landed, w, m, v], [jax.ShapeDtypeStruct((L, r, c), F32)] * 4, name=name, grid=(L, n_i),
                 in_specs=[pl.BlockSpec((N_DEV, tm, c), held(k)) for k in range(L)] + [cur] * 3, out_specs=[cur] * 4, comm=comm)


def _adamw(name, w, g, m, v):
    shape = w.shape
    C = shape[-1]
    two = lambda a: a.reshape(-1, C)
    R = two(w).shape[0]
    outs = _rowwise(name, _adamw_fn, [two(w), two(g), two(m), two(v)], [], [(C, F32)] * 3, tm=_row_tile(R, 512))
    return [o.reshape(shape) for o in outs]


def _whole_rows(g, r0, r1):
    return g[:, r0:r1].reshape(N_DEV * (r1 - r0), g.shape[2])


SHARD_ROWS = D_MODEL // N_DEV


def _w_out_of(Wl):
    n = Wl["d2"].shape[1]
    return _whole_rows(Wl["r1"], n, n + SHARD_ROWS)


def _w_mem_kv_of(Wl):
    n = Wl["d2"].shape[1]
    return _whole_rows(Wl["r1"], n + SHARD_ROWS, n + 2 * SHARD_ROWS)[:, :2 * MEM_WIDTH]


def _mixer_fwd(l, x1, mem, G, W, lbs, shared, local, units, relays):
    T = x1.shape[0]
    tag = f"l{l}"
    Wl = G[l]
    h = _rms_fwd(tag + "_mixrms", x1, W["mix_norm"][l:l + 1])
    mem_n = _rms_fwd(tag + "_memrms", mem, W["mem_norm"][l:l + 1])
    kv = _mm(tag + "_memkv", [(mem_n, _w_mem_kv_of(Wl), NN)], [F32], mem.shape[0], 2 * MEM_WIDTH)
    proj, moved = _proj_cols(tag + "_in", h, Wl["win"], 0, comm=_Comm(relay=[G[n][k] for n, k in relays]))
    for (n, k), m in zip(relays, moved):
        G[n][k] = m
    along = _Comm(gather=[local[n][k] for n, k in units])
    if l < 2:
        main, o, moved = _hgrn_fwd(tag + "_hgrn", proj, lbs[l], W["hgrn_o_gain"][l:l + 1], comm=along)
    else:
        main, o, moved = _fox_fwd(tag + "_fox", proj, shared["kvf"], shared["cr"], W["fox_q_gain"][l - 2:l - 1],
                                  W["fox_k_gain"], comm=along)
    for (n, k), m in zip(units, moved):
        G[n][k] = m
    mem_o = _mem_fwd(tag + "_mem", proj, kv, W["mem_q_gain"][l:l + 1], W["mem_k_gain"][l:l + 1])
    w_out = _w_out_of(Wl)
    x2 = _mm(tag + "_out", [(main, w_out[:MAIN_WIDTH], NN), (mem_o, w_out[MAIN_WIDTH:], NN)], [F32], T, D_MODEL,
             epi=lambda a, e: (e[0] + a[0] + a[1],), extras=[x1])
    return x2, dict(h=h, mem_n=mem_n, kv=kv, proj=proj, main=main, o=o, mem_o=mem_o)


def _mixer_bwd(l, x1, mem, Wl, W, lbs, shared, sv, dx2, acc, ready, landed):
    T = x1.shape[0]
    tag = f"l{l}b"
    w_out = _w_out_of(Wl)
    g = {}
    dmix = _mm(tag + "_dmix", [(dx2, w_out, NT)], [F32], T, D_MODEL)
    dw_out = jnp.concatenate([
        _mm(tag + "_dwout_a", [(sv["main"], dx2, TN)], [BF16], MAIN_WIDTH, D_MODEL),
        _mm(tag + "_dwout_b", [(sv["mem_o"], dx2, TN)], [BF16], MEM_WIDTH, D_MODEL)], axis=0).reshape(N_DEV, -1, D_MODEL)
    dqm, dkv, g["mem_q_gain"], g["mem_k_gain"] = _mem_bwd(tag + "_mem", sv["proj"], sv["kv"], W["mem_q_gain"][l:l + 1],
                                                           W["mem_k_gain"][l:l + 1], dmix)
    along = _Comm(scatter=[v for _, v in ready])
    if l < 2:
        dq, df, di, dg, g["lb"], g["hgrn_o_gain"], moved = _hgrn_bwd(tag + "_hgrn", sv["proj"], sv["o"], dmix, lbs[l],
                                                                      W["hgrn_o_gain"][l:l + 1], comm=along)
        dproj = jnp.concatenate([dq, df, di, dg, dqm], axis=1)
    else:
        dq, dgate, acc["dk"], acc["dv"], acc["dc"], g["fox_q_gain"], g["fox_k_gain"], moved = _fox_bwd(
            tag + "_fox", sv["proj"], shared["kvf"], shared["cr"], W["fox_q_gain"][l - 2:l - 1], W["fox_k_gain"],
            sv["o"], dmix, acc["dk"], acc["dv"], acc["dc"], comm=along)
        dproj = jnp.concatenate([dq, dgate, dqm], axis=1)
    landed.update({k: m for (k, _), m in zip(ready, moved)})
    dh, dw_in, moved = _proj_cols_bwd(tag + "_in", sv["h"], dproj, Wl["win"], 0, comm=_Comm(scatter=[dw_out]))
    landed[(l, "w_out")] = moved[0]
    dx1, g["mix_norm"] = _rms_bwd(tag + "_mixrms", x1, W["mix_norm"][l:l + 1], dh, dres=dx2)
    dw_mem_kv = _mm(tag + "_dwmemkv", [(sv["mem_n"], dkv, TN)], [BF16], D_MODEL, 2 * MEM_WIDTH)
    dmem_n = _mm(tag + "_dmemn", [(dkv, _w_mem_kv_of(Wl), NT)], [F32], mem.shape[0], D_MODEL)
    _, g["mem_norm"] = _rms_bwd(tag + "_memrms", mem, W["mem_norm"][l:l + 1], dmem_n)
    return dx1, g, [((l, "w_in"), dw_in), ((l, "w_mem_kv"), dw_mem_kv.reshape(N_DEV, -1, 2 * MEM_WIDTH))]


def _forget_cols(kvf):
    return kvf[:, 2 * MAIN_WIDTH:2 * MAIN_WIDTH + LANES]


def _log_forget(kvf, bias):
    return _rowwise("kv_logf", lambda f, b: jax.nn.log_sigmoid(f + b), [_forget_cols(kvf)], [bias], [(LANES, F32)])[0]


def _step(x, mem, target, W, lb_logits, G0, local):
    T = x.shape[0]
    W = dict(W, fox_k_gain=W["fox_k_gain"].reshape(1, -1))
    lbs = _lb_fwd(lb_logits)
    fox_bias = jnp.pad(W["fox_f_bias"], (0, LANES - FOX_HEADS)).reshape(1, LANES)
    w_kv = W["w_kv"]
    n_l = len(local)
    ffn1 = lambda l, Wl: (W["ffn1_norm"][l:l + 1], Wl["gu1"], Wl["r1"], 0, 1, 0)
    ffn2 = lambda l, Wl: (W["ffn2_norm"][l:l + 1], Wl["gu2"], Wl["d2"], 0, 1, 0)

    on_ffn1 = {0: [(1, "gu1")], 1: [(2, "gu1")], 2: [(3, "gu1")], 3: [(3, "gu2")]}
    on_mix = {0: [(1, "r1"), (1, "win"), (1, "gu2")], 1: [(2, "r1"), (2, "win"), (2, "gu2")], 2: [(3, "r1"), (3, "win")], 3: []}
    on_ffn2 = {0: [(1, "d2")], 1: [(2, "d2")], 2: [(3, "d2")], 3: []}
    saved, shared, G = [], {}, [G0] + [{} for _ in range(n_l - 1)]
    for l in range(n_l):
        Wl = G[l]
        relay = on_ffn2[l - 1] if l else []
        along = _Comm(gather=[local[n][k] for n, k in on_ffn1[l]], relay=[G[n][k] for n, k in relay])
        x1, moved = _ffn_fwd(f"l{l}_ffn1", x, *ffn1(l, Wl), comm=along)
        for (n, k), m in zip(on_ffn1[l] + relay, moved):
            G[n][k] = m
        x2, sv = _mixer_fwd(l, x1, mem, G, W, lbs, shared, local, on_mix[l], on_ffn1[l])
        along = _Comm(gather=[local[n][k] for n, k in on_ffn2[l]], relay=[G[n][k] for n, k in on_mix[l]])
        x3, moved = _ffn_fwd(f"l{l}_ffn2", x2, *ffn2(l, G[l]), comm=along)
        for (n, k), m in zip(on_ffn2[l] + on_mix[l], moved):
            G[n][k] = m
        sv.update(x=x, x1=x1, x2=x2)
        saved.append(sv)
        x = x3
        if l == 1:
            hk = _rms_fwd("kv_rms", x, W["kv_norm"].reshape(1, -1))
            kvf = _mm("kv_proj", [(hk, w_kv, NN)], [F32], T, w_kv.shape[1])
            cum = _cumsum_rows("kv_cum", _log_forget(kvf, fox_bias))[:, :FOX_HEADS].T
            shared = dict(kvf=kvf, cr=cum[:, None, :], hk=hk, x=x)

    def loss_fn(y, t):
        err = y - t
        return err * (1.0 / D_MODEL), jnp.sum(0.5 / D_MODEL * err * err, axis=0, keepdims=True)
    dx, loss = _rowwise("loss", loss_fn, [x, target], [], [(D_MODEL, F32)], [((1, D_MODEL), F32)])

    grads = [None] * n_l
    acc = dict(dk=jnp.zeros((T, MAIN_WIDTH), F32), dv=jnp.zeros((T, MAIN_WIDTH), F32), dc=jnp.zeros((FOX_HEADS, 1, T), F32))
    gkv = {}
    landed, late = {}, []
    for l in reversed(range(n_l)):
        sv, Wl = saved[l], G[l]
        ready, extra = [], []
        if l == 1:
            dcum = jnp.pad(acc["dc"][:, 0, :].T, ((0, 0), (0, LANES - FOX_HEADS)))
            dlf = _cumsum_rows("kv_dcum", dcum, reverse=True)
            def dlogf_fn(d, f, b):
                p = d * _sigmoid(-(f + b))
                return p, jnp.sum(p, axis=0, keepdims=True)
            dfl, gkv["fox_f_bias"] = _rowwise("kv_dlogf", dlogf_fn, [dlf, _forget_cols(shared["kvf"])], [fox_bias],
                                              [(LANES, BF16)], [((1, LANES), F32)])
            dkvf = _pad_cols(jnp.concatenate([acc["dk"].astype(BF16), acc["dv"].astype(BF16), dfl], axis=1), w_kv.shape[1])
            dw_kv = _mm("kv_dw", [(shared["hk"], dkvf, TN)], [BF16], D_MODEL, dkvf.shape[1])
            extra.append(((0, "w_kv"), dw_kv[:, :KV_WIDTH].reshape(N_DEV, -1, KV_WIDTH)))
            dhk = _mm("kv_dh", [(dkvf, w_kv, NT)], [F32], T, D_MODEL)
            dx, gkv["kv_norm"] = _rms_bwd("kv_rmsb", shared["x"], W["kv_norm"].reshape(1, -1), dhk, dres=dx)
        g = {}
        if l < 2:
            ready, late = ready + late[1:], late[:1]
        dx2, g["ffn2_norm"], dwg, dwu, dwd, moved_a, moved_w = _ffn_bwd(
            f"l{l}b_ffn2", sv["x2"], *ffn2(l, Wl), dx, comm_a=_Comm(scatter=[v for _, v in late[:2]]),
            comm_w=_Comm(scatter=[v for _, v in late[2:]]))
        landed.update({k: m for (k, _), m in zip(late, moved_a + moved_w)})
        ready += [((l, "ffn2_w_gate"), dwg), ((l, "ffn2_w_up"), dwu), ((l, "ffn2_w_down"), dwd)]
        dx1, gm, rest = _mixer_bwd(l, sv["x1"], mem, Wl, W, lbs, shared, sv, dx2, acc, ready, landed)
        g.update(gm)
        dx, g["ffn1_norm"], dwg, dwu, dwd, moved_a, moved_w = _ffn_bwd(
            f"l{l}b_ffn1", sv["x"], *ffn1(l, Wl), dx1, comm_a=_Comm(scatter=[v for _, v in rest]),
            comm_w=_Comm(scatter=[v for _, v in extra]))
        landed.update({k: m for (k, _), m in zip(rest + extra, moved_a + moved_w)})
        late = [((l, "ffn1_w_gate"), dwg), ((l, "ffn1_w_up"), dwu), ((l, "ffn1_w_down"), dwd)]
        grads[l] = g

    out = {}
    for n in ["ffn1_norm", "mix_norm", "mem_norm", "mem_q_gain", "mem_k_gain", "ffn2_norm"]:
        out[n] = jnp.concatenate([grads[l][n] for l in range(4)], axis=0)
    out["hgrn_o_gain"] = jnp.concatenate([grads[l]["hgrn_o_gain"] for l in (0, 1)], axis=0)
    out["fox_q_gain"] = jnp.concatenate([grads[l]["fox_q_gain"] for l in (2, 3)], axis=0)
    out["fox_k_gain"] = (grads[2]["fox_k_gain"] + grads[3]["fox_k_gain"]).reshape(-1)
    out["kv_norm"] = gkv["kv_norm"].reshape(-1)
    out["fox_f_bias"] = gkv["fox_f_bias"][0, :FOX_HEADS]
    dl0, dl1 = _lb_bwd(lb_logits, grads[0]["lb"], grads[1]["lb"])
    out["hgrn_lb_logits"] = jnp.concatenate([dl0, dl1], axis=0)
    return loss, dx, out, landed, late


WEIGHTS = ["ffn1_norm", "ffn1_w_gate", "ffn1_w_up", "ffn1_w_down", "mix_norm", "mem_norm", "w_mem_kv", "mem_q_gain",
           "mem_k_gain", "w_in_a", "hgrn_lb_logits", "hgrn_o_gain", "w_in_b", "fox_q_gain", "kv_norm", "w_kv", "fox_f_bias",
           "fox_k_gain", "w_out", "ffn2_norm", "ffn2_w_gate", "ffn2_w_up", "ffn2_w_down"]
BIG = COLS352 + ["w_in_a", "w_in_b", "ffn1_w_down", "ffn2_w_down", "w_out", "w_mem_kv", "w_kv"]


def _train_step(a):
    bf = lambda w: w.astype(BF16)
    n_l = a["w_out"].shape[0]
    local = []
    for l in range(n_l):
        w_in = a["w_in_a"][l] if l < a["w_in_a"].shape[0] else a["w_in_b"][l - a["w_in_a"].shape[0]]
        local.append(dict(
            gu1=bf(jnp.concatenate([a["ffn1_w_gate"][l], a["ffn1_w_up"][l]], axis=0)),
            r1=bf(jnp.concatenate([a["ffn1_w_down"][l], a["w_out"][l], _pad_cols(a["w_mem_kv"][l], D_MODEL)], axis=0)),
            win=bf(w_in),
            gu2=bf(jnp.concatenate([a["ffn2_w_gate"][l], a["ffn2_w_up"][l]], axis=0)),
            d2=bf(a["ffn2_w_down"][l])))
    keys = ["gu1", "r1", "win", "gu2", "d2"]
    first = _all_gather("ag_first", [local[0][k] for k in keys] + [bf(a["w_kv"]), _small_pack([a["hgrn_lb_logits"]])])
    G0 = dict(zip(keys, first))
    W = {n: a[n] for n in SMALL}
    W["w_kv"] = _pad_cols(first[5].reshape(-1, a["w_kv"].shape[1]), 2 * D_MODEL)
    lb_shape = a["hgrn_lb_logits"].shape
    lb_all = first[6].reshape(N_DEV, -1)[:, :lb_shape[0] * lb_shape[1]]
    lb_logits = lb_all.reshape((N_DEV,) + lb_shape).transpose(1, 0, 2).reshape(lb_shape[0], -1)

    loss_part, dx, g, landed, tail = _step(a["x"][0], a["mem"][0], a["loss_target"][0], W, lb_logits, G0, local)

    n_a = a["w_in_a"].shape[0]
    grad, delta, new_m, new_v = {}, {}, {}, {}
    order = [n for n in BIG if n.startswith("ffn2")] + [n for n in BIG if not n.startswith("ffn")]
    for n in order + [n for n in BIG if n.startswith("ffn1")]:
        ls = range(n_a) if n == "w_in_a" else range(n_a, n_l) if n == "w_in_b" else range(1) if n == "w_kv" else range(n_l)
        key = "w_in" if n.startswith("w_in") else n
        lead = (lambda t: t[None]) if a[n].ndim == 2 else (lambda t: t)
        riding, tail = tail[:1], tail[1:]
        res, moved = _sum_adamw("adam_" + n, [landed[(l, key)] for l in ls], lead(a[n]), lead(a["m_" + n]), lead(a["v_" + n]),
                                comm=_Comm(scatter=[v for _, v in riding]))
        landed.update({k: m for (k, _), m in zip(riding, moved)})
        grad[n], delta[n], new_m[n], new_v[n] = [t.reshape(a[n].shape) for t in res]

    zeros = [jnp.zeros(lb_logits.shape, F32), jnp.zeros(loss_part.shape, F32)]
    small_shapes = [a[n].shape for n in SMALL] + [lb_logits.shape, loss_part.shape]
    small_part = _small_pack([g[n] for n in SMALL] + [g["hgrn_lb_logits"], loss_part])
    small_sum = _sum_slabs("small_sum", _all_gather("ag_small", [small_part])[0], F32)
    small = _small_unpack(small_sum, small_shapes)
    grad.update(dict(zip(SMALL, small)))
    loss = jnp.sum(small[-1])
    me = 4 * lax.axis_index("x") + 2 * lax.axis_index("y") + lax.axis_index("c")
    grad["hgrn_lb_logits"] = lax.dynamic_slice_in_dim(small[-2], me * lb_shape[1], lb_shape[1], axis=1)

    n = "hgrn_lb_logits"
    delta[n], new_m[n], new_v[n] = _adamw("adam_" + n, a[n], grad[n], a["m_" + n], a["v_" + n])
    packs = [_small_pack([a[p + n] for n in SMALL] + zeros) for p in ("", "m_", "v_")]
    upd = _rowwise("adam_small", _adamw_fn, [packs[0], small_sum, packs[1], packs[2]], [], [(LANES, F32)] * 3, tm=packs[0].shape[0])
    for d, u in zip((delta, new_m, new_v), upd):
        d.update(dict(zip(SMALL, _small_unpack(u, small_shapes))))
    return (loss, dx[None], *[grad[n] for n in WEIGHTS], *[delta[n] for n in WEIGHTS], *[new_m[n] for n in WEIGHTS],
            *[new_v[n] for n in WEIGHTS])


def kernel(x, mem, ffn1_norm, ffn1_w_gate, ffn1_w_up, ffn1_w_down, mix_norm, mem_norm, w_mem_kv, mem_q_gain, mem_k_gain, w_in_a, hgrn_lb_logits, hgrn_o_gain, w_in_b, fox_q_gain, kv_norm, w_kv, fox_f_bias, fox_k_gain, w_out, ffn2_norm, ffn2_w_gate, ffn2_w_up, ffn2_w_down, loss_target, m_ffn1_norm, m_ffn1_w_gate, m_ffn1_w_up, m_ffn1_w_down, m_mix_norm, m_mem_norm, m_w_mem_kv, m_mem_q_gain, m_mem_k_gain, m_w_in_a, m_hgrn_lb_logits, m_hgrn_o_gain, m_w_in_b, m_fox_q_gain, m_kv_norm, m_w_kv, m_fox_f_bias, m_fox_k_gain, m_w_out, m_ffn2_norm, m_ffn2_w_gate, m_ffn2_w_up, m_ffn2_w_down, v_ffn1_norm, v_ffn1_w_gate, v_ffn1_w_up, v_ffn1_w_down, v_mix_norm, v_mem_norm, v_w_mem_kv, v_mem_q_gain, v_mem_k_gain, v_w_in_a, v_hgrn_lb_logits, v_hgrn_o_gain, v_w_in_b, v_fox_q_gain, v_kv_norm, v_w_kv, v_fox_f_bias, v_fox_k_gain, v_w_out, v_ffn2_norm, v_ffn2_w_gate, v_ffn2_w_up, v_ffn2_w_down):
    return _train_step(dict(locals()))
```

```python
import functools

import jax
import jax.numpy as jnp
from jax import lax
from jax.experimental import pallas as pl
from jax.experimental.pallas import tpu as pltpu

F32, BF16 = jnp.float32, jnp.bfloat16
EPS = 1e-6
V7X_VMEM_LIMIT = 56 * 1024 * 1024
LANES = 128
N_DEV = 8

D_MODEL = 1024
MAIN_WIDTH = 768
MEM_WIDTH = 256
HG_HEAD_DIM = 128
HG_HEADS = 6
FOX_HEAD_DIM = 64
FOX_HEADS = 12
MEM_HEADS = 4
MEM_HEAD_DIM = 64
HG_BLOCK = 16

ADAM_LR, ADAM_B1, ADAM_B2, ADAM_EPS, ADAM_WD, ADAM_STEP = 0.001, 0.9, 0.999, 1e-08, 0.01, 10

NN = ((1,), (0,))
NT = ((1,), (1,))
TN = ((0,), (0,))


def _dot(a, b, dims, precision=None):
    return lax.dot_general(a, b, (dims, ((), ())), preferred_element_type=F32, precision=precision)


def _bdot(a, b, dims):
    return _dot(a.astype(BF16), b.astype(BF16), dims)


def _split(a):
    hi = a.astype(BF16)
    return hi, (a - hi.astype(F32)).astype(BF16)


def _fdot(a, b, dims):
    ah, al = _split(a)
    bh, bl = _split(b)
    return _dot(ah, bh, dims) + (_dot(ah, bl, dims) + _dot(al, bh, dims))


def _params(n_grid):
    return pltpu.CompilerParams(dimension_semantics=("arbitrary",) * n_grid, vmem_limit_bytes=V7X_VMEM_LIMIT)


def _rms(x, g):
    return x * lax.rsqrt(jnp.mean(x * x, axis=-1, keepdims=True) + EPS) * g


def _sigmoid(x):
    return jax.nn.sigmoid(x)


def _silu(x):
    return x * jax.nn.sigmoid(x)


MESH = pl.DeviceIdType.MESH
ANY = pl.BlockSpec(memory_space=pl.ANY)


def _mesh_pos():
    return lax.axis_index("x"), lax.axis_index("y"), lax.axis_index("c")


class _Comm:
    def __init__(self, gather=(), relay=(), scatter=()):
        self.gather, self.relay, self.scatter = list(gather), list(relay), list(scatter)
        self.arrays = self.gather + self.relay + self.scatter
        self.n_remote = 4 * len(self.gather) + 3 * len(self.relay) + 7 * len(self.scatter)
        self.n_local = len(self.gather) + len(self.scatter)

    def out_shapes(self):
        return ([jax.ShapeDtypeStruct((N_DEV,) + x.shape, x.dtype) for x in self.gather]
                + [jax.ShapeDtypeStruct(g.shape, g.dtype) for g in self.relay + self.scatter])

    def scratch(self):
        return [pltpu.SemaphoreType.DMA((self.n_remote,)), pltpu.SemaphoreType.DMA((self.n_remote,)),
                pltpu.SemaphoreType.DMA((max(self.n_local, 1),))]

    def _copies(self, ins, outs, send, recv, local, arrivals=True):
        mx, my, mc = _mesh_pos()
        flip = lambda v, f: 1 - v if f else v
        idx = lambda p: 4 * p[0] + 2 * p[1] + p[2]
        me = (mx, my, mc)
        count = [0, 0]
        loc, out, arrive = [], [], []

        def pair(src, dst, lands, to):
            k = count[0]
            count[0] += 1
            mk = lambda d: pltpu.make_async_remote_copy(src_ref=src, dst_ref=d, send_sem=send.at[k], recv_sem=recv.at[k],
                                                        device_id=to, device_id_type=MESH)
            out.append(mk(dst))
            if arrivals:
                arrive.append(mk(lands))

        def local_copy(src, dst):
            loc.append(pltpu.make_async_copy(src, dst, local.at[count[1]]))
            count[1] += 1

        refs = list(zip(ins, outs))
        near = [(0, 0, 1), (1, 0, 0), (0, 1, 0), (1, 1, 0)]
        for x, G in refs[:len(self.gather)]:
            local_copy(x, G.at[idx(me)])
            for f in near:
                peer = tuple(flip(v, b) for v, b in zip(me, f))
                pair(x, G.at[idx(me)], G.at[idx(peer)], peer)
        sibling = (mx, my, 1 - mc)
        for Gin, Gout in refs[len(self.gather):len(self.gather) + len(self.relay)]:
            for f in near[1:]:
                chip = (flip(mx, f[0]), flip(my, f[1]))
                pair(Gin.at[idx((*chip, mc))], Gout.at[idx((*chip, mc))], Gout.at[idx((*chip, 1 - mc))], sibling)
        every = near + [(1, 0, 1), (0, 1, 1), (1, 1, 1)]
        for g, R in refs[len(self.gather) + len(self.relay):]:
            local_copy(g.at[idx(me)], R.at[idx(me)])
            for f in every:
                peer = tuple(flip(v, b) for v, b in zip(me, f))
                pair(g.at[idx(peer)], R.at[idx(me)], R.at[idx(peer)], peer)
        return loc, out, arrive

    def start(self, ins, outs, send, recv, local):
        loc, out, _ = self._copies(ins, outs, send, recv, local, arrivals=False)
        for cp in loc + out:
            cp.start()

    def finish(self, ins, outs, send, recv, local):
        loc, out, arrive = self._copies(ins, outs, send, recv, local)
        for cp in arrive:
            cp.wait_recv()
        for cp in out:
            cp.wait_send()
        for cp in loc:
            cp.wait()


def _call(body, operands, out_shape, *, name, grid=(), in_specs=None, out_specs=None, scratch=(), comm=None):
    outs = list(out_shape) if isinstance(out_shape, (list, tuple)) else [out_shape]
    single = not isinstance(out_shape, (list, tuple))
    params = _params(len(grid))
    if comm is None or not comm.arrays:
        res = pl.pallas_call(body, grid=grid, in_specs=in_specs, out_specs=out_specs, out_shape=out_shape,
                             scratch_shapes=list(scratch), name=name, compiler_params=params)(*operands)
        return ([res] if single else list(res)), []
    n_in, n_out, n_s, n_c = len(operands), len(outs), len(scratch), len(comm.arrays)

    def wrapped(*refs):
        pos = [0]

        def take(n):
            pos[0] += n
            return refs[pos[0] - n:pos[0]]

        b_in, c_in, b_out, c_out, b_s, sems = take(n_in), take(n_c), take(n_out), take(n_c), take(n_s), take(3)
        ids = [pl.program_id(d) for d in range(len(grid))]
        first, last = True, True
        for d, i in enumerate(ids):
            first = (i == 0) & first
            last = (i == grid[d] - 1) & last
        if grid:
            pl.when(first)(lambda: comm.start(c_in, c_out, *sems))
        else:
            comm.start(c_in, c_out, *sems)
        body(*b_in, *b_out, *b_s)
        if grid:
            pl.when(last)(lambda: comm.finish(c_in, c_out, *sems))
        else:
            comm.finish(c_in, c_out, *sems)

    n_g = len(comm.gather)
    aliases = {n_in + n_g + r: n_out + n_g + r for r in range(len(comm.relay))}
    out_specs_l = list(out_specs) if isinstance(out_specs, (list, tuple)) else [out_specs]
    res = pl.pallas_call(
        wrapped, grid=grid, in_specs=list(in_specs) + [ANY] * n_c, out_specs=out_specs_l + [ANY] * n_c,
        out_shape=outs + comm.out_shapes(), scratch_shapes=list(scratch) + comm.scratch(), input_output_aliases=aliases,
        name=name, compiler_params=params)(*operands, *comm.arrays)
    return list(res[:n_out]), list(res[n_out:])


def _rowwise(name, fn, rows, consts, out_rows, out_reds=(), tm=512):
    R = rows[0].shape[0]
    tm = min(tm, R)
    assert R % tm == 0
    n_in, n_o = len(rows) + len(consts), len(out_rows)

    def body(*refs):
        outs = fn(*[r[...] for r in refs[:n_in]])
        if not isinstance(outs, (tuple, list)):
            outs = (outs,)
        for r, o in zip(refs[n_in:n_in + n_o], outs[:n_o]):
            r[...] = o.astype(r.dtype)
        red_refs = refs[n_in + n_o:]
        if red_refs:
            @pl.when(pl.program_id(0) == 0)
            def _():
                for r in red_refs:
                    r[...] = jnp.zeros(r.shape, r.dtype)
            for r, o in zip(red_refs, outs[n_o:]):
                r[...] += o

    zero = lambda n: (lambda i: (0,) * n)
    in_specs = [pl.BlockSpec((tm, a.shape[1]), lambda i: (i, 0)) for a in rows]
    in_specs += [pl.BlockSpec(c.shape, zero(c.ndim)) for c in consts]
    out_specs = [pl.BlockSpec((tm, c), lambda i: (i, 0)) for c, _ in out_rows]
    out_specs += [pl.BlockSpec(s, zero(len(s))) for s, _ in out_reds]
    out_shape = [jax.ShapeDtypeStruct((R, c), dt) for c, dt in out_rows]
    out_shape += [jax.ShapeDtypeStruct(s, dt) for s, dt in out_reds]
    return pl.pallas_call(body, grid=(R // tm,), in_specs=in_specs, out_specs=out_specs, out_shape=out_shape,
                          name=name, compiler_params=_params(1))(*rows, *consts)


def _tile(n, cap):
    best = None
    for t in range(LANES, min(n, cap) + 1, LANES):
        if n % t == 0:
            best = t
    return best or n


def _mm(name, pairs, out_dtypes, M, N, epi=None, extras=(), tm=512, tn=512):
    tm, tn = _tile(M, tm), _tile(N, tn)
    n_p, n_e = len(pairs), len(extras)
    modes = [m for _, _, m in pairs]

    def body(*refs):
        accs = [_bdot(refs[2 * k][...], refs[2 * k + 1][...], modes[k]) for k in range(n_p)]
        ex = [r[...] for r in refs[2 * n_p:2 * n_p + n_e]]
        outs = epi(accs, ex) if epi is not None else accs
        for r, o in zip(refs[2 * n_p + n_e:], outs):
            r[...] = o.astype(r.dtype)

    in_specs = []
    ops = []
    for a, b, mode in pairs:
        if mode == NN:
            K = a.shape[1]
            assert a.shape == (M, K) and b.shape == (K, N), (name, a.shape, b.shape)
            in_specs += [pl.BlockSpec((tm, K), lambda i, j: (i, 0)), pl.BlockSpec((K, tn), lambda i, j: (0, j))]
        elif mode == NT:
            K = a.shape[1]
            assert a.shape == (M, K) and b.shape == (N, K), (name, a.shape, b.shape)
            in_specs += [pl.BlockSpec((tm, K), lambda i, j: (i, 0)), pl.BlockSpec((tn, K), lambda i, j: (j, 0))]
        else:
            K = a.shape[0]
            assert a.shape == (K, M) and b.shape == (K, N), (name, a.shape, b.shape)
            in_specs += [pl.BlockSpec((K, tm), lambda i, j: (0, i)), pl.BlockSpec((K, tn), lambda i, j: (0, j))]
        ops += [a, b]
    in_specs += [pl.BlockSpec((tm, tn), lambda i, j: (i, j)) for _ in extras]
    out_specs = [pl.BlockSpec((tm, tn), lambda i, j: (i, j)) for _ in out_dtypes]
    out_shape = [jax.ShapeDtypeStruct((M, N), dt) for dt in out_dtypes]
    res = pl.pallas_call(body, grid=(M // tm, N // tn), in_specs=in_specs, out_specs=out_specs, out_shape=out_shape,
                         name=name, compiler_params=_params(2))(*ops, *extras)
    return res[0] if len(res) == 1 else res


def _rms_fwd(name, x, gain, dtype=BF16):
    return _rowwise(name, _rms, [x], [gain], [(x.shape[1], dtype)])[0]


def _rms_bwd(name, x, gain, dh, dres=None):
    def fn(x, dh, *rest):
        g = rest[-1]
        _, vjp = jax.vjp(_rms, x, g)
        dx, dg = vjp(dh)
        if dres is not None:
            dx = dx + rest[0]
        return dx, dg
    rows = [x, dh] + ([dres] if dres is not None else [])
    d = x.shape[1]
    return _rowwise(name, fn, rows, [gain], [(d, F32)], [((1, d), F32)])


def _ffn_specs(gcols, grows, ig, iu, idn):
    n = gcols.shape[2]
    D = grows.shape[2]
    wg = pl.BlockSpec((None, D, n), lambda i, j: (j, ig, 0))
    wu = pl.BlockSpec((None, D, n), lambda i, j: (j, iu, 0))
    wd = pl.BlockSpec((None, n, D), lambda i, j: (j, idn, 0))
    return n, wg, wu, wd


def _ffn_fwd(name, x, gain, gcols, grows, ig, iu, idn, tm=1024, comm=None):
    T, D = x.shape
    tm = min(T, tm)
    n, wg_s, wu_s, wd_s = _ffn_specs(gcols, grows, ig, iu, idn)
    last = N_DEV - 1

    def body(x_ref, g_ref, wg_ref, wu_ref, wd_ref, y_ref, h_s, acc):
        j = pl.program_id(1)

        @pl.when(j == 0)
        def _():
            h_s[...] = _rms(x_ref[...], g_ref[...]).astype(BF16)
            acc[...] = jnp.zeros(acc.shape, F32)
        h = h_s[...]
        z = _silu(_dot(h, wg_ref[...], NN)) * _dot(h, wu_ref[...], NN)
        acc[...] += _dot(z.astype(BF16), wd_ref[...], NN)

        @pl.when(j == last)
        def _():
            y_ref[...] = x_ref[...] + 0.5 * acc[...]

    row = pl.BlockSpec((tm, D), lambda i, j: (i, 0))
    (y,), moved = _call(
        body, [x, gain, gcols, gcols, grows], [jax.ShapeDtypeStruct((T, D), F32)], name=name, grid=(T // tm, N_DEV),
        in_specs=[row, pl.BlockSpec((1, D), lambda i, j: (0, 0)), wg_s, wu_s, wd_s], out_specs=[row],
        scratch=[pltpu.VMEM((tm, D), BF16), pltpu.VMEM((tm, D), F32)], comm=comm)
    return y, moved


def _ffn_bwd(tag, x, gain, gcols, grows, ig, iu, idn, dy, tm=512, comm_a=None, comm_w=None):
    T, D = x.shape
    tm = min(T, tm)
    n, wg_s, wu_s, wd_s = _ffn_specs(gcols, grows, ig, iu, idn)
    last = N_DEV - 1

    def body(x_ref, dy_ref, g_ref, wg_ref, wu_ref, wd_ref, dx_ref, dg_ref, h_ref, z_ref, da_ref, db_ref, dh_acc):
        i, j = pl.program_id(0), pl.program_id(1)

        @pl.when(j == 0)
        def _():
            h_ref[...] = _rms(x_ref[...], g_ref[...]).astype(BF16)
            dh_acc[...] = jnp.zeros(dh_acc.shape, F32)

        @pl.when((i == 0) & (j == 0))
        def _():
            dg_ref[...] = jnp.zeros(dg_ref.shape, F32)
        h = h_ref[...]
        a, b = _dot(h, wg_ref[...], NN), _dot(h, wu_ref[...], NN)
        dz = 0.5 * _dot(dy_ref[...].astype(BF16), wd_ref[...], NT)
        s = _sigmoid(a)
        si = a * s
        da = (dz * b * (s + si * (1.0 - s))).astype(BF16)
        db = (dz * si).astype(BF16)
        z_ref[...] = (si * b).astype(BF16)
        da_ref[...] = da
        db_ref[...] = db
        dh_acc[...] += _dot(da, wg_ref[...], NT) + _dot(db, wu_ref[...], NT)

        @pl.when(j == last)
        def _():
            _, vjp = jax.vjp(_rms, x_ref[...], g_ref[...])
            dx, dg = vjp(dh_acc[...])
            dx_ref[...] = dx + dy_ref[...]
            dg_ref[...] += dg

    row = pl.BlockSpec((tm, D), lambda i, j: (i, 0))
    vec = pl.BlockSpec((1, D), lambda i, j: (0, 0))
    hid = pl.BlockSpec((None, tm, n), lambda i, j: (j, i, 0))
    hidden = jax.ShapeDtypeStruct((N_DEV, T, n), BF16)
    (dx, dgain, h, z, da, db), moved_a = _call(
        body, [x, dy, gain, gcols, gcols, grows],
        [jax.ShapeDtypeStruct((T, D), F32), jax.ShapeDtypeStruct((1, D), F32), jax.ShapeDtypeStruct((T, D), BF16),
         hidden, hidden, hidden],
        name=tag + "_a", grid=(T // tm, N_DEV), in_specs=[row, row, vec, wg_s, wu_s, wd_s],
        out_specs=[row, vec, row, hid, hid, hid], scratch=[pltpu.VMEM((tm, D), F32)], comm=comm_a)

    def wbody(h_ref, dy_ref, z_ref, da_ref, db_ref, dwg_ref, dwu_ref, dwd_ref):
        h = h_ref[...]
        dwg_ref[...] = _dot(h, da_ref[...], TN).astype(BF16)
        dwu_ref[...] = _dot(h, db_ref[...], TN).astype(BF16)
        dwd_ref[...] = (0.5 * _dot(z_ref[...], dy_ref[...].astype(BF16), TN)).astype(BF16)

    full = pl.BlockSpec((T, D), lambda j: (0, 0))
    hid_all = pl.BlockSpec((None, T, n), lambda j: (j, 0, 0))
    (dwg, dwu, dwd), moved_w = _call(
        wbody, [h, dy, z, da, db],
        [jax.ShapeDtypeStruct((N_DEV, D, n), BF16)] * 2 + [jax.ShapeDtypeStruct((N_DEV, n, D), BF16)],
        name=tag + "_w", grid=(N_DEV,), in_specs=[full, full, hid_all, hid_all, hid_all],
        out_specs=[pl.BlockSpec((None, D, n), lambda j: (j, 0, 0))] * 2 + [pl.BlockSpec((None, n, D), lambda j: (j, 0, 0))],
        comm=comm_w)
    return dx, dgain, dwg, dwu, dwd, moved_a, moved_w


def _wcols_spec(gw, l, grid_rank):
    _, _, n = gw.shape
    K = D_MODEL
    zero = (lambda i: (0, l, 0)) if grid_rank == 1 else (lambda i, j: (0, l, 0))
    return n, K, pl.BlockSpec((N_DEV, K, n), zero)


def _proj_cols(name, h, gw, l, tm=512, comm=None):
    T = h.shape[0]
    tm = min(T, tm)
    n, K, wspec = _wcols_spec(gw, l, 1)

    def body(h_ref, w_ref, o_ref):
        h = h_ref[...]
        for j in range(N_DEV):
            o_ref[:, pl.ds(j * n, n)] = _dot(h, w_ref[j], NN)

    (proj,), moved = _call(
        body, [h, gw], [jax.ShapeDtypeStruct((T, N_DEV * n), F32)], name=name, grid=(T // tm,),
        in_specs=[pl.BlockSpec((tm, K), lambda i: (i, 0)), wspec], out_specs=[pl.BlockSpec((tm, N_DEV * n), lambda i: (i, 0))],
        comm=comm)
    return proj, moved


def _proj_cols_bwd(tag, h, dproj, gw, l, tm=512, tk=512, comm=None):
    T = h.shape[0]
    tm = min(T, tm)
    n, K, wspec = _wcols_spec(gw, l, 1)

    def dh_body(dp_ref, w_ref, o_ref):
        acc = jnp.zeros(o_ref.shape, F32)
        for j in range(N_DEV):
            acc = acc + _dot(dp_ref[:, pl.ds(j * n, n)], w_ref[j], NT)
        o_ref[...] = acc

    (dh,), moved = _call(
        dh_body, [dproj, gw], [jax.ShapeDtypeStruct((T, K), F32)], name=tag + "_dh", grid=(T // tm,),
        in_specs=[pl.BlockSpec((tm, N_DEV * n), lambda i: (i, 0)), wspec], out_specs=[pl.BlockSpec((tm, K), lambda i: (i, 0))],
        comm=comm)

    def dw_body(h_ref, dp_ref, o_ref):
        h = h_ref[...]
        for j in range(N_DEV):
            o_ref[j] = _dot(h, dp_ref[:, pl.ds(j * n, n)], TN).astype(BF16)

    dw = pl.pallas_call(
        dw_body, grid=(K // tk,), in_specs=[pl.BlockSpec((T, tk), lambda i: (0, i)), pl.BlockSpec((T, N_DEV * n), lambda i: (0, 0))],
        out_specs=pl.BlockSpec((N_DEV, tk, n), lambda i: (0, i, 0)), out_shape=jax.ShapeDtypeStruct((N_DEV, K, n), BF16),
        name=tag + "_dw", compiler_params=_params(1))(h, dproj)
    return dh, dw, moved


def _block_tri(n, reverse=False):
    r = lax.broadcasted_iota(jnp.int32, (n, n), 0)
    c = lax.broadcasted_iota(jnp.int32, (n, n), 1)
    same = (r // HG_BLOCK) == (c // HG_BLOCK)
    return (same & ((c >= r) if reverse else (c <= r))).astype(F32)


def _hgrn_prep(q_ref, f_ref, lbv, qs, ks, cs, T):
    pt = min(T, 256)
    tri = _block_tri(pt)
    for p in range(T // pt):
        rows = pl.ds(p * pt, pt)
        f = lbv + (1.0 - lbv) * _sigmoid(f_ref[rows, :])
        qs[rows, :] = _silu(q_ref[rows, :])
        ks[rows, :] = 1.0 - f
        cs[rows, :] = _dot(tri, jnp.log(f), NN, precision=lax.Precision.HIGHEST)


HG_GROUP = 128


def _groups_loop(nb, fn):
    gp = HG_GROUP if nb % HG_GROUP == 0 else nb

    def step(i, carry):
        base = pl.multiple_of(i * (gp * HG_BLOCK), gp * HG_BLOCK)
        fn(lambda t: pl.ds(base + t, gp, stride=HG_BLOCK))
        return carry

    lax.fori_loop(0, nb // gp, step, 0)


def _gate_out(o, og, g):
    return _rms(o, og) * _silu(g)


HG_UNROLL = 16


def _block_rows(n):
    return pl.ds(pl.multiple_of(n * HG_BLOCK, HG_BLOCK), HG_BLOCK)


def _blocks_loop(nb, fn):
    u = HG_UNROLL if nb % HG_UNROLL == 0 else 1

    def step(i, carry):
        for k in range(u):
            fn(i * u + k)
        return carry

    lax.fori_loop(0, nb // u, step, 0)


def _scan_states(buf, cs, nb, reverse=False):
    def step(m, st):
        n = nb - 1 - m if reverse else m
        own = buf[n]
        buf[n] = st
        rows = _block_rows(n)
        return jnp.exp(cs[rows, :][HG_BLOCK - 1:HG_BLOCK, :]) * st + own

    lax.fori_loop(0, nb, step, jnp.zeros(buf.shape[1:], F32))


def _hgrn_states(i_ref, ks, cs, states, nb):
    def own_step(n):
        rows = _block_rows(n)
        c = cs[rows, :]
        states[n] = _fdot(i_ref[rows, :], ks[rows, :] * jnp.exp(c[HG_BLOCK - 1:HG_BLOCK, :] - c), TN)

    _blocks_loop(nb, own_step)
    _scan_states(states, cs, nb)


def _hgrn_fwd(name, proj, lb, og, comm=None):
    T = proj.shape[0]
    nb = T // HG_BLOCK
    hd = HG_HEAD_DIM

    def body(q_ref, f_ref, i_ref, g_ref, lb_ref, og_ref, main_ref, o_ref, qs, ks, cs, states):
        _hgrn_prep(q_ref, f_ref, lb_ref[...], qs, ks, cs, T)
        def pairs(at):
            for t in range(HG_BLOCK):
                qt, ct = qs[at(t), :], cs[at(t), :]
                acc = jnp.zeros(qt.shape, F32)
                for s in range(t + 1):
                    w = qt * ks[at(s), :] * jnp.exp(ct - cs[at(s), :])
                    acc = acc + jnp.sum(w, axis=-1, keepdims=True) * i_ref[at(s), :]
                o_ref[at(t), :] = acc

        _groups_loop(nb, pairs)

        _hgrn_states(i_ref, ks, cs, states, nb)

        def out_step(n):
            rows = _block_rows(n)
            o_ref[rows, :] += _fdot(qs[rows, :] * jnp.exp(cs[rows, :]), states[n], NT)

        _blocks_loop(nb, out_step)
        pt = min(T, 256)
        for p in range(T // pt):
            rows = pl.ds(p * pt, pt)
            main_ref[rows, :] = _gate_out(o_ref[rows, :], og_ref[...], g_ref[rows, :])

    nh = HG_HEADS
    col = lambda off: pl.BlockSpec((T, hd), lambda h, off=off: (0, off + h))
    (main, o), moved = _call(
        body, [proj, proj, proj, proj, lb, og], [jax.ShapeDtypeStruct((T, MAIN_WIDTH), F32)] * 2, name=name, grid=(nh,),
        in_specs=[col(0), col(nh), col(2 * nh), col(3 * nh), pl.BlockSpec((1, hd), lambda h: (0, h)),
                  pl.BlockSpec((1, hd), lambda h: (0, 0))],
        out_specs=[col(0), col(0)], scratch=[pltpu.VMEM((T, hd), F32)] * 3 + [pltpu.VMEM((nb, hd, hd), F32)], comm=comm)
    return main, o, moved


def _hgrn_bwd(name, proj, o, dmix, lb, og, comm=None):
    T = proj.shape[0]
    nb = T // HG_BLOCK
    hd = HG_HEAD_DIM
    pt = min(T, 256)

    def body(q_ref, f_ref, i_ref, g_ref, o_ref, dm_ref, lb_ref, og_ref,
             dq_ref, df_ref, di_ref, dg_ref, dlb_ref, dog_ref, qs, ks, cs, dos, dqs, dks, dvs, states, behind):
        lbv = lb_ref[...]
        _hgrn_prep(q_ref, f_ref, lbv, qs, ks, cs, T)
        dog = jnp.zeros((1, hd), F32)
        for p in range(T // pt):
            rows = pl.ds(p * pt, pt)
            _, vjp = jax.vjp(_gate_out, o_ref[rows, :], og_ref[...], g_ref[rows, :])
            do, dog_p, dg = vjp(dm_ref[rows, :])
            dos[rows, :] = do
            dg_ref[rows, :] = dg.astype(dg_ref.dtype)
            dog = dog + dog_p

        @pl.when(pl.program_id(0) == 0)
        def _():
            dog_ref[...] = jnp.zeros(dog_ref.shape, F32)
        dog_ref[...] += dog

        def pairs(at):
            for t in range(HG_BLOCK):
                dqs[at(t), :] = jnp.zeros((HG_GROUP if nb % HG_GROUP == 0 else nb, hd), F32)
            for s in range(HG_BLOCK):
                k_s, c_s, v_s = ks[at(s), :], cs[at(s), :], i_ref[at(s), :]
                dk = jnp.zeros(k_s.shape, F32)
                dv = jnp.zeros(k_s.shape, F32)
                for t in range(s, HG_BLOCK):
                    q_t, do_t = qs[at(t), :], dos[at(t), :]
                    e = jnp.exp(cs[at(t), :] - c_s)
                    a = jnp.sum(q_t * k_s * e, axis=-1, keepdims=True)
                    g = jnp.sum(do_t * v_s, axis=-1, keepdims=True)
                    dqs[at(t), :] += g * k_s * e
                    dk = dk + g * q_t * e
                    dv = dv + a * do_t
                dks[at(s), :] = dk
                dvs[at(s), :] = dv

        _groups_loop(nb, pairs)

        _hgrn_states(i_ref, ks, cs, states, nb)

        def own_step(n):
            rows = _block_rows(n)
            behind[n] = _fdot(dos[rows, :], qs[rows, :] * jnp.exp(cs[rows, :]), TN)

        _blocks_loop(nb, own_step)
        _scan_states(behind, cs, nb, reverse=True)

        def grad_step(n):
            rows = _block_rows(n)
            c = cs[rows, :]
            ec, ek = jnp.exp(c), jnp.exp(c[HG_BLOCK - 1:HG_BLOCK, :] - c)
            dst = behind[n]
            dqs[rows, :] += _fdot(dos[rows, :], states[n], NN) * ec
            dks[rows, :] += _fdot(i_ref[rows, :], dst, NN) * ek
            dvs[rows, :] += _fdot(ks[rows, :] * ek, dst, NT)

        _blocks_loop(nb, grad_step)

        full = (lax.broadcasted_iota(jnp.int32, (pt, pt), 1) >= lax.broadcasted_iota(jnp.int32, (pt, pt), 0)).astype(F32)
        carry = jnp.zeros((1, hd), F32)
        dlb = jnp.zeros((1, hd), F32)
        for p in reversed(range(T // pt)):
            rows = pl.ds(p * pt, pt)
            q, k, dq, dk = qs[rows, :], ks[rows, :], dqs[rows, :], dks[rows, :]
            db = q * dq - k * dk
            dlf = _dot(full, db, NN, precision=lax.Precision.HIGHEST) + carry
            carry = carry + jnp.sum(db, axis=0, keepdims=True)
            sg = _sigmoid(f_ref[rows, :])
            df = dlf / (1.0 - k) - dk
            df_ref[rows, :] = (df * (1.0 - lbv) * sg * (1.0 - sg)).astype(df_ref.dtype)
            dlb = dlb + jnp.sum(df * (1.0 - sg), axis=0, keepdims=True)
            qr = q_ref[rows, :]
            sq = _sigmoid(qr)
            dq_ref[rows, :] = (dq * (sq + qr * sq * (1.0 - sq))).astype(dq_ref.dtype)
            di_ref[rows, :] = dvs[rows, :].astype(di_ref.dtype)
        dlb_ref[...] = dlb

    nh = HG_HEADS
    col = lambda off: pl.BlockSpec((T, hd), lambda h, off=off: (0, off + h))
    vec = pl.BlockSpec((1, hd), lambda h: (0, h))
    one = pl.BlockSpec((1, hd), lambda h: (0, 0))
    outs, moved = _call(
        body, [proj, proj, proj, proj, o, dmix, lb, og],
        [jax.ShapeDtypeStruct((T, MAIN_WIDTH), BF16)] * 4
        + [jax.ShapeDtypeStruct((1, MAIN_WIDTH), F32), jax.ShapeDtypeStruct((1, hd), F32)],
        name=name, grid=(nh,), in_specs=[col(0), col(nh), col(2 * nh), col(3 * nh), col(0), col(0), vec, one],
        out_specs=[col(0), col(0), col(0), col(0), vec, one],
        scratch=[pltpu.VMEM((T, hd), F32)] * 7 + [pltpu.VMEM((nb, hd, hd), F32)] * 2, comm=comm)
    return (*outs, moved)


def _softmax_rows(s):
    p = jnp.exp(s - jnp.max(s, axis=-1, keepdims=True))
    return p, jnp.sum(p, axis=-1, keepdims=True)


def _fox_probs(q, k, cr_ref, hh, qi, tq):
    q0 = qi * tq
    pieces = ([(0, q0)] if qi else []) + [(q0, q0 + tq)]
    ss = []
    for a, b in pieces:
        s = _dot(q, k[a:b], NT) - cr_ref[hh, :, pl.ds(a, b - a)]
        if a == q0:
            causal = lax.broadcasted_iota(jnp.int32, s.shape, 1) <= lax.broadcasted_iota(jnp.int32, s.shape, 0)
            s = jnp.where(causal, s, -jnp.inf)
        ss.append(s)
    m = functools.reduce(jnp.maximum, [jnp.max(s, axis=-1, keepdims=True) for s in ss])
    ps = [jnp.exp(s - m) for s in ss]
    l = functools.reduce(jnp.add, [jnp.sum(p, axis=-1, keepdims=True) for p in ps])
    return [(a, b, p) for (a, b), p in zip(pieces, ps)], l


def _fox_specs(T):
    w = 2 * FOX_HEAD_DIM
    n = MAIN_WIDTH // w
    col = lambda off: pl.BlockSpec((T, w), lambda p, off=off: (0, off + p))
    cr = pl.BlockSpec((2, 1, T), lambda p: (p, 0, 0))
    gain = pl.BlockSpec((1, FOX_HEAD_DIM), lambda p: (0, 0))
    return n, col, cr, gain


def _fox_fwd(name, proj, kvf, cr, gq, gk, comm=None):
    T = proj.shape[0]
    tq = min(T, 256)
    hd = FOX_HEAD_DIM
    scale = hd ** -0.5

    def body(q_ref, g_ref, k_ref, v_ref, cr_ref, gq_ref, gk_ref, main_ref, o_ref):
        for hh in range(2):
            lanes = pl.ds(hh * hd, hd)
            k = _rms(k_ref[:, lanes], gk_ref[...]).astype(BF16)
            v = v_ref[:, lanes].astype(BF16)
            for qi in range(T // tq):
                rows = pl.ds(qi * tq, tq)
                q = (_rms(q_ref[rows, lanes], gq_ref[...]) * scale).astype(BF16)
                ps, l = _fox_probs(q, k, cr_ref, hh, qi, tq)
                o = functools.reduce(jnp.add, [_dot(p.astype(BF16), v[a:b], NN) for a, b, p in ps]) / l
                o_ref[rows, lanes] = o
                main_ref[rows, lanes] = o * _sigmoid(g_ref[rows, lanes])

    n, col, crs, gain = _fox_specs(T)
    (main, o), moved = _call(
        body, [proj, proj, kvf, kvf, cr, gq, gk], [jax.ShapeDtypeStruct((T, MAIN_WIDTH), F32)] * 2, name=name, grid=(n,),
        in_specs=[col(0), col(n), col(0), col(n), crs, gain, gain], out_specs=[col(0), col(0)], comm=comm)
    return main, o, moved


def _fox_bwd(name, proj, kvf, cr, gq, gk, o, dmix, pdk, pdv, pdc, comm=None):
    T = proj.shape[0]
    tq = min(T, 256)
    hd = FOX_HEAD_DIM
    scale = hd ** -0.5

    def body(q_ref, g_ref, k_ref, v_ref, cr_ref, gq_ref, gk_ref, o_ref, dm_ref, pdk_ref, pdv_ref, pdc_ref,
             dq_ref, dg_ref, dk_ref, dv_ref, dc_ref, dgq_ref, dgk_ref, dka, dva, dca):
        dgq = jnp.zeros((1, hd), F32)
        dgk = jnp.zeros((1, hd), F32)
        for hh in range(2):
            lanes = pl.ds(hh * hd, hd)
            k32, vjp_k = jax.vjp(_rms, k_ref[:, lanes], gk_ref[...])
            k = k32.astype(BF16)
            v = v_ref[:, lanes].astype(BF16)
            dka[...] = jnp.zeros(dka.shape, F32)
            dva[...] = jnp.zeros(dva.shape, F32)
            dca[...] = jnp.zeros(dca.shape, F32)
            for qi in range(T // tq):
                rows = pl.ds(qi * tq, tq)
                q32, vjp_q = jax.vjp(_rms, q_ref[rows, lanes], gq_ref[...])
                q = (q32 * scale).astype(BF16)
                ps, l = _fox_probs(q, k, cr_ref, hh, qi, tq)
                ps = [(a, b, p / l) for a, b, p in ps]
                sg = _sigmoid(g_ref[rows, lanes])
                dm = dm_ref[rows, lanes]
                do = (dm * sg).astype(BF16)
                dg_ref[rows, lanes] = (dm * o_ref[rows, lanes] * sg * (1.0 - sg)).astype(dg_ref.dtype)
                dps = [_dot(do, v[a:b], NT) for a, b, _ in ps]
                delta = functools.reduce(jnp.add, [jnp.sum(p * dp, axis=-1, keepdims=True) for (_, _, p), dp in zip(ps, dps)])
                dq = jnp.zeros((tq, hd), F32)
                for (a, b, p), dp in zip(ps, dps):
                    ds = p * (dp - delta)
                    dsb = ds.astype(BF16)
                    dq = dq + _dot(dsb, k[a:b], NN)
                    dka[:, pl.ds(a, b - a)] += _dot(q, dsb, TN)
                    dva[:, pl.ds(a, b - a)] += _dot(do, p.astype(BF16), TN)
                    dca[:, pl.ds(a, b - a)] -= jnp.sum(ds, axis=0, keepdims=True)
                dqr, dgq_p = vjp_q(dq * scale)
                dq_ref[rows, lanes] = dqr.astype(dq_ref.dtype)
                dgq = dgq + dgq_p
            dkr, dgk_p = vjp_k(dka[...].T)
            dgk = dgk + dgk_p
            dk_ref[:, lanes] = dkr + pdk_ref[:, lanes]
            dv_ref[:, lanes] = dva[...].T + pdv_ref[:, lanes]
            dc_ref[hh] = dca[...] + pdc_ref[hh]

        @pl.when(pl.program_id(0) == 0)
        def _():
            dgq_ref[...] = jnp.zeros(dgq_ref.shape, F32)
            dgk_ref[...] = jnp.zeros(dgk_ref.shape, F32)
        dgq_ref[...] += dgq
        dgk_ref[...] += dgk

    n, col, crs, gain = _fox_specs(T)
    wide = jax.ShapeDtypeStruct((T, MAIN_WIDTH), F32)
    half = jax.ShapeDtypeStruct((T, MAIN_WIDTH), BF16)
    outs, moved = _call(
        body, [proj, proj, kvf, kvf, cr, gq, gk, o, dmix, pdk, pdv, pdc],
        [half, half, wide, wide, jax.ShapeDtypeStruct((FOX_HEADS, 1, T), F32),
         jax.ShapeDtypeStruct((1, hd), F32), jax.ShapeDtypeStruct((1, hd), F32)],
        name=name, grid=(n,),
        in_specs=[col(0), col(n), col(0), col(n), crs, gain, gain, col(0), col(0), col(0), col(0), crs],
        out_specs=[col(0), col(0), col(0), col(0), crs, gain, gain],
        scratch=[pltpu.VMEM((hd, T), F32), pltpu.VMEM((hd, T), F32), pltpu.VMEM((1, T), F32)], comm=comm)
    return (*outs, moved)


def _mem_specs(T, width):
    tq = min(T, 512)
    q = pl.BlockSpec((tq, MEM_WIDTH), lambda i, c=(width - MEM_WIDTH) // MEM_WIDTH: (i, c))
    gain = pl.BlockSpec((1, MEM_HEAD_DIM), lambda i: (0, 0))
    return tq, q, gain


def _mem_fwd(name, proj, kv, gq, gk):
    T, W = proj.shape
    hd = MEM_HEAD_DIM
    tq, qspec, gain = _mem_specs(T, W)

    def body(q_ref, kv_ref, gq_ref, gk_ref, o_ref):
        for h in range(MEM_HEADS):
            lanes = pl.ds(h * hd, hd)
            q = _rms(q_ref[:, lanes], gq_ref[...]).astype(BF16)
            k = _rms(kv_ref[:, lanes], gk_ref[...]).astype(BF16)
            v = kv_ref[:, pl.ds(MEM_WIDTH + h * hd, hd)].astype(BF16)
            p, l = _softmax_rows(_dot(q, k, NT) * (hd ** -0.5))
            o_ref[:, lanes] = _dot(p.astype(BF16), v, NN) / l

    return pl.pallas_call(
        body, grid=(T // tq,),
        in_specs=[qspec, pl.BlockSpec(kv.shape, lambda i: (0, 0)), gain, gain],
        out_specs=pl.BlockSpec((tq, MEM_WIDTH), lambda i: (i, 0)),
        out_shape=jax.ShapeDtypeStruct((T, MEM_WIDTH), F32),
        name=name, compiler_params=_params(1))(proj, kv, gq, gk)


def _mem_bwd(name, proj, kv, gq, gk, dmix):
    T, W = proj.shape
    hd = MEM_HEAD_DIM
    scale = hd ** -0.5
    tq, qspec, gain = _mem_specs(T, W)

    def body(q_ref, kv_ref, gq_ref, gk_ref, dm_ref, dq_ref, dkv_ref, dgq_ref, dgk_ref):
        @pl.when(pl.program_id(0) == 0)
        def _():
            dkv_ref[...] = jnp.zeros(dkv_ref.shape, F32)
            dgq_ref[...] = jnp.zeros(dgq_ref.shape, F32)
            dgk_ref[...] = jnp.zeros(dgk_ref.shape, F32)
        for h in range(MEM_HEADS):
            lanes = pl.ds(h * hd, hd)
            vl = pl.ds(MEM_WIDTH + h * hd, hd)
            q32, vjp_q = jax.vjp(_rms, q_ref[:, lanes], gq_ref[...])
            k32, vjp_k = jax.vjp(_rms, kv_ref[:, lanes], gk_ref[...])
            q, k, v = q32.astype(BF16), k32.astype(BF16), kv_ref[:, vl].astype(BF16)
            p, l = _softmax_rows(_dot(q, k, NT) * scale)
            p = p / l
            do = dm_ref[:, lanes].astype(BF16)
            dp = _dot(do, v, NT)
            dsb = (p * (dp - jnp.sum(p * dp, axis=-1, keepdims=True))).astype(BF16)
            dqr, dgq_p = vjp_q(_dot(dsb, k, NN) * scale)
            dkr, dgk_p = vjp_k(_dot(dsb, q, TN) * scale)
            dq_ref[:, lanes] = dqr.astype(dq_ref.dtype)
            dkv_ref[:, lanes] += dkr
            dkv_ref[:, vl] += _dot(p.astype(BF16), do, TN)
            dgq_ref[...] += dgq_p
            dgk_ref[...] += dgk_p

    return pl.pallas_call(
        body, grid=(T // tq,),
        in_specs=[qspec, pl.BlockSpec(kv.shape, lambda i: (0, 0)), gain, gain,
                  pl.BlockSpec((tq, MEM_WIDTH), lambda i: (i, MAIN_WIDTH // MEM_WIDTH))],
        out_specs=[pl.BlockSpec((tq, MEM_WIDTH), lambda i: (i, 0)), pl.BlockSpec(kv.shape, lambda i: (0, 0)), gain, gain],
        out_shape=[jax.ShapeDtypeStruct((T, MEM_WIDTH), BF16), jax.ShapeDtypeStruct(kv.shape, F32),
                   jax.ShapeDtypeStruct((1, hd), F32), jax.ShapeDtypeStruct((1, hd), F32)],
        name=name, compiler_params=_params(1))(proj, kv, gq, gk, dmix)


def _cumsum_rows(name, x, reverse=False):
    T, C = x.shape
    pt = min(T, 256)

    def body(x_ref, o_ref):
        r = lax.broadcasted_iota(jnp.int32, (pt, pt), 0)
        c = lax.broadcasted_iota(jnp.int32, (pt, pt), 1)
        tri = ((c >= r) if reverse else (c <= r)).astype(F32)
        carry = jnp.zeros((1, C), F32)
        order = range(T // pt)
        for p in (reversed(order) if reverse else order):
            rows = pl.ds(p * pt, pt)
            blk = x_ref[rows, :]
            o_ref[rows, :] = _dot(tri, blk, NN, precision=lax.Precision.HIGHEST) + carry
            carry = carry + jnp.sum(blk, axis=0, keepdims=True)

    return pl.pallas_call(body, out_shape=jax.ShapeDtypeStruct((T, C), F32), name=name,
                          compiler_params=pltpu.CompilerParams(vmem_limit_bytes=V7X_VMEM_LIMIT))(x)


def _all_gather(name, xs):
    n = len(xs)

    def body(*refs):
        x_refs, out_refs = refs[:n], refs[n:2 * n]
        send_sems, recv_sems, local_sems = refs[2 * n:]
        mx, my, mc = _mesh_pos()
        me, sibling = (mx, my, mc), (mx, my, 1 - mc)
        chips = [(1 - mx, my), (mx, 1 - my), (1 - mx, 1 - my)]

        def slot(a, px, py, pc):
            return out_refs[a].at[4 * px + 2 * py + pc]

        def copy(a, k, block, to, src=None):
            return pltpu.make_async_remote_copy(
                src_ref=slot(a, *block) if src is None else src, dst_ref=slot(a, *block),
                send_sem=send_sems.at[7 * a + k], recv_sem=recv_sems.at[7 * a + k], device_id=to, device_id_type=MESH)

        mine = [pltpu.make_async_copy(x_refs[a], slot(a, *me), local_sems.at[a]) for a in range(n)]
        first = []
        for a in range(n):
            mine[a].start()
            first.append(copy(a, 0, me, sibling, src=x_refs[a]))
            first += [copy(a, 1 + j, me, (*chip, mc), src=x_refs[a]) for j, chip in enumerate(chips)]
        for cp in first:
            cp.start()
        passed = []
        for j, chip in enumerate(chips):
            for a in range(n):
                copy(a, 1 + j, (*chip, mc), me).wait_recv()
                passed.append(copy(a, 4 + j, (*chip, mc), sibling))
                passed[-1].start()
        for a in range(n):
            copy(a, 0, sibling, me).wait_recv()
            for j, chip in enumerate(chips):
                copy(a, 4 + j, (*chip, 1 - mc), me).wait_recv()
        for cp in first + passed:
            cp.wait_send()
        for cp in mine:
            cp.wait()

    return pl.pallas_call(
        body, out_shape=[jax.ShapeDtypeStruct((N_DEV,) + x.shape, x.dtype) for x in xs], in_specs=[ANY] * n, out_specs=[ANY] * n,
        scratch_shapes=[pltpu.SemaphoreType.DMA((7 * n,)), pltpu.SemaphoreType.DMA((7 * n,)), pltpu.SemaphoreType.DMA((n,))],
        name=name)(*xs)


def _row_tile(R, cap):
    best = None
    for t in range(8, min(R, cap) + 1, 8):
        if R % t == 0:
            best = t
    return best or R


def _sum_slabs(name, a, out_dtype):
    n, R, C = a.shape
    tm = _row_tile(R, 512)

    def body(*refs):
        acc = refs[0][...].astype(F32)
        for r in refs[1:n]:
            acc = acc + r[...].astype(F32)
        refs[n][...] = acc.astype(out_dtype)

    return pl.pallas_call(
        body, grid=(R // tm,),
        in_specs=[pl.BlockSpec((None, tm, C), lambda i, q=q: (q, i, 0)) for q in range(n)],
        out_specs=pl.BlockSpec((tm, C), lambda i: (i, 0)), out_shape=jax.ShapeDtypeStruct((R, C), out_dtype),
        name=name, compiler_params=_params(1))(*([a] * n))


SMALL = ["ffn1_norm", "mix_norm", "mem_norm", "mem_q_gain", "mem_k_gain", "hgrn_o_gain", "fox_q_gain", "kv_norm",
         "fox_f_bias", "fox_k_gain", "ffn2_norm"]
COLS352 = ["ffn1_w_gate", "ffn1_w_up", "ffn2_w_gate", "ffn2_w_up"]
KV_WIDTH = 2 * MAIN_WIDTH + FOX_HEADS


def _pad_cols(w, width):
    return jnp.pad(w, [(0, 0)] * (w.ndim - 1) + [(0, width - w.shape[-1])])


def _pad128(a):
    flat = a.reshape(-1)
    return jnp.pad(flat, (0, -flat.shape[0] % LANES))


def _small_pack(parts):
    flat = jnp.concatenate([_pad128(p) for p in parts])
    rows = -(-flat.shape[0] // LANES)
    flat = jnp.pad(flat, (0, (-rows % 8) * LANES))
    return flat.reshape(-1, LANES)


def _small_unpack(buf, shapes):
    flat = buf.reshape(-1)
    out, off = [], 0
    for s in shapes:
        n = 1
        for d in s:
            n *= d
        out.append(flat[off:off + n].reshape(s))
        off += n + (-n % LANES)
    return out


def _lb_fn(l0, l1):
    m = lax.stop_gradient(jnp.maximum(l0, l1))
    e0, e1 = jnp.exp(l0 - m), jnp.exp(l1 - m)
    p0, p1 = e0 / (e0 + e1), e1 / (e0 + e1)
    return p0 - p0, (p0 + p1) - p0


def _lb_fwd(logits):
    return _rowwise("lb", _lb_fn, [logits[0:1], logits[1:2]], [], [(MAIN_WIDTH, F32)] * 2)


def _lb_bwd(logits, dlb0, dlb1):
    def fn(l0, l1, d0, d1):
        _, vjp = jax.vjp(_lb_fn, l0, l1)
        return vjp((d0, d1))
    return _rowwise("lb_bwd", fn, [logits[0:1], logits[1:2], dlb0, dlb1], [], [(MAIN_WIDTH, F32)] * 2)


def _adamw_fn(w, g, m, v):
    m = ADAM_B1 * m + (1.0 - ADAM_B1) * g
    v = ADAM_B2 * v + (1.0 - ADAM_B2) * jnp.square(g)
    m_hat = m / (1.0 - ADAM_B1 ** ADAM_STEP)
    v_hat = v / (1.0 - ADAM_B2 ** ADAM_STEP)
    return -ADAM_LR * (m_hat / (jnp.sqrt(v_hat) + ADAM_EPS) + ADAM_WD * w), m, v


def _sum_adamw(name, landed, w, m, v, comm=None):
    L, r, c = w.shape
    tm = _row_tile(r, 128)
    n_i = r // tm

    def body(*refs):
        land, (w_ref, m_ref, v_ref), outs = refs[:L], refs[L:L + 3], refs[L + 3:]
        for k in range(L):
            @pl.when(pl.program_id(0) == k)
            def _(k=k):
                g = land[k][0].astype(F32)
                for s in range(1, N_DEV):
                    g = g + land[k][s].astype(F32)
                for ref, val in zip(outs, (g,) + _adamw_fn(w_ref[...], g, m_ref[...], v_ref[...])):
                    ref[...] = val

    held = lambda k: (lambda l, i: (0, jnp.where(l < k, 0, jnp.where(l == k, i, n_i - 1)), 0))
    cur = pl.BlockSpec((None, tm, c), lambda l, i: (l, i, 0))
    return _call(body, [*landed, w, m, v], [jax.ShapeDtypeStruct((L, r, c), F32)] * 4, name=name, grid=(L, n_i),
                 in_specs=[pl.BlockSpec((N_DEV, tm, c), held(k)) for k in range(L)] + [cur] * 3, out_specs=[cur] * 4, comm=comm)


def _adamw(name, w, g, m, v):
    shape = w.shape
    C = shape[-1]
    two = lambda a: a.reshape(-1, C)
    R = two(w).shape[0]
    outs = _rowwise(name, _adamw_fn, [two(w), two(g), two(m), two(v)], [], [(C, F32)] * 3, tm=_row_tile(R, 512))
    return [o.reshape(shape) for o in outs]


def _whole_rows(g, r0, r1):
    return g[:, r0:r1].reshape(N_DEV * (r1 - r0), g.shape[2])


SHARD_ROWS = D_MODEL // N_DEV


def _w_out_of(Wl):
    n = Wl["d2"].shape[1]
    return _whole_rows(Wl["r1"], n, n + SHARD_ROWS)


def _w_mem_kv_of(Wl):
    n = Wl["d2"].shape[1]
    return _whole_rows(Wl["r1"], n + SHARD_ROWS, n + 2 * SHARD_ROWS)[:, :2 * MEM_WIDTH]


def _mixer_fwd(l, x1, mem, G, W, lbs, shared, local, units, relays):
    T = x1.shape[0]
    tag = f"l{l}"
    Wl = G[l]
    h = _rms_fwd(tag + "_mixrms", x1, W["mix_norm"][l:l + 1])
    mem_n = _rms_fwd(tag + "_memrms", mem, W["mem_norm"][l:l + 1])
    kv = _mm(tag + "_memkv", [(mem_n, _w_mem_kv_of(Wl), NN)], [F32], mem.shape[0], 2 * MEM_WIDTH)
    proj, moved = _proj_cols(tag + "_in", h, Wl["win"], 0, comm=_Comm(relay=[G[n][k] for n, k in relays]))
    for (n, k), m in zip(relays, moved):
        G[n][k] = m
    along = _Comm(gather=[local[n][k] for n, k in units])
    if l < 2:
        main, o, moved = _hgrn_fwd(tag + "_hgrn", proj, lbs[l], W["hgrn_o_gain"][l:l + 1], comm=along)
    else:
        main, o, moved = _fox_fwd(tag + "_fox", proj, shared["kvf"], shared["cr"], W["fox_q_gain"][l - 2:l - 1],
                                  W["fox_k_gain"], comm=along)
    for (n, k), m in zip(units, moved):
        G[n][k] = m
    mem_o = _mem_fwd(tag + "_mem", proj, kv, W["mem_q_gain"][l:l + 1], W["mem_k_gain"][l:l + 1])
    w_out = _w_out_of(Wl)
    x2 = _mm(tag + "_out", [(main, w_out[:MAIN_WIDTH], NN), (mem_o, w_out[MAIN_WIDTH:], NN)], [F32], T, D_MODEL,
             epi=lambda a, e: (e[0] + a[0] + a[1],), extras=[x1])
    return x2, dict(h=h, mem_n=mem_n, kv=kv, proj=proj, main=main, o=o, mem_o=mem_o)


def _mixer_bwd(l, x1, mem, Wl, W, lbs, shared, sv, dx2, acc, ready, landed):
    T = x1.shape[0]
    tag = f"l{l}b"
    w_out = _w_out_of(Wl)
    g = {}
    dmix = _mm(tag + "_dmix", [(dx2, w_out, NT)], [F32], T, D_MODEL)
    dw_out = jnp.concatenate([
        _mm(tag + "_dwout_a", [(sv["main"], dx2, TN)], [BF16], MAIN_WIDTH, D_MODEL),
        _mm(tag + "_dwout_b", [(sv["mem_o"], dx2, TN)], [BF16], MEM_WIDTH, D_MODEL)], axis=0).reshape(N_DEV, -1, D_MODEL)
    dqm, dkv, g["mem_q_gain"], g["mem_k_gain"] = _mem_bwd(tag + "_mem", sv["proj"], sv["kv"], W["mem_q_gain"][l:l + 1],
                                                           W["mem_k_gain"][l:l + 1], dmix)
    along = _Comm(scatter=[v for _, v in ready])
    if l < 2:
        dq, df, di, dg, g["lb"], g["hgrn_o_gain"], moved = _hgrn_bwd(tag + "_hgrn", sv["proj"], sv["o"], dmix, lbs[l],
                                                                      W["hgrn_o_gain"][l:l + 1], comm=along)
        dproj = jnp.concatenate([dq, df, di, dg, dqm], axis=1)
    else:
        dq, dgate, acc["dk"], acc["dv"], acc["dc"], g["fox_q_gain"], g["fox_k_gain"], moved = _fox_bwd(
            tag + "_fox", sv["proj"], shared["kvf"], shared["cr"], W["fox_q_gain"][l - 2:l - 1], W["fox_k_gain"],
            sv["o"], dmix, acc["dk"], acc["dv"], acc["dc"], comm=along)
        dproj = jnp.concatenate([dq, dgate, dqm], axis=1)
    landed.update({k: m for (k, _), m in zip(ready, moved)})
    dh, dw_in, moved = _proj_cols_bwd(tag + "_in", sv["h"], dproj, Wl["win"], 0, comm=_Comm(scatter=[dw_out]))
    landed[(l, "w_out")] = moved[0]
    dx1, g["mix_norm"] = _rms_bwd(tag + "_mixrms", x1, W["mix_norm"][l:l + 1], dh, dres=dx2)
    dw_mem_kv = _mm(tag + "_dwmemkv", [(sv["mem_n"], dkv, TN)], [BF16], D_MODEL, 2 * MEM_WIDTH)
    dmem_n = _mm(tag + "_dmemn", [(dkv, _w_mem_kv_of(Wl), NT)], [F32], mem.shape[0], D_MODEL)
    _, g["mem_norm"] = _rms_bwd(tag + "_memrms", mem, W["mem_norm"][l:l + 1], dmem_n)
    return dx1, g, [((l, "w_in"), dw_in), ((l, "w_mem_kv"), dw_mem_kv.reshape(N_DEV, -1, 2 * MEM_WIDTH))]


def _forget_cols(kvf):
    return kvf[:, 2 * MAIN_WIDTH:2 * MAIN_WIDTH + LANES]


def _log_forget(kvf, bias):
    return _rowwise("kv_logf", lambda f, b: jax.nn.log_sigmoid(f + b), [_forget_cols(kvf)], [bias], [(LANES, F32)])[0]


def _step(x, mem, target, W, lb_logits, G0, local):
    T = x.shape[0]
    W = dict(W, fox_k_gain=W["fox_k_gain"].reshape(1, -1))
    lbs = _lb_fwd(lb_logits)
    fox_bias = jnp.pad(W["fox_f_bias"], (0, LANES - FOX_HEADS)).reshape(1, LANES)
    w_kv = W["w_kv"]
    n_l = len(local)
    ffn1 = lambda l, Wl: (W["ffn1_norm"][l:l + 1], Wl["gu1"], Wl["r1"], 0, 1, 0)
    ffn2 = lambda l, Wl: (W["ffn2_norm"][l:l + 1], Wl["gu2"], Wl["d2"], 0, 1, 0)

    on_ffn1 = {0: [(1, "gu1")], 1: [(2, "gu1")], 2: [(3, "gu1")], 3: [(3, "gu2")]}
    on_mix = {0: [(1, "r1"), (1, "win"), (1, "gu2")], 1: [(2, "r1"), (2, "win"), (2, "gu2")], 2: [(3, "r1"), (3, "win")], 3: []}
    on_ffn2 = {0: [(1, "d2")], 1: [(2, "d2")], 2: [(3, "d2")], 3: []}
    saved, shared, G = [], {}, [G0] + [{} for _ in range(n_l - 1)]
    for l in range(n_l):
        Wl = G[l]
        relay = on_ffn2[l - 1] if l else []
        along = _Comm(gather=[local[n][k] for n, k in on_ffn1[l]], relay=[G[n][k] for n, k in relay])
        x1, moved = _ffn_fwd(f"l{l}_ffn1", x, *ffn1(l, Wl), comm=along)
        for (n, k), m in zip(on_ffn1[l] + relay, moved):
            G[n][k] = m
        x2, sv = _mixer_fwd(l, x1, mem, G, W, lbs, shared, local, on_mix[l], on_ffn1[l])
        along = _Comm(gather=[local[n][k] for n, k in on_ffn2[l]], relay=[G[n][k] for n, k in on_mix[l]])
        x3, moved = _ffn_fwd(f"l{l}_ffn2", x2, *ffn2(l, G[l]), comm=along)
        for (n, k), m in zip(on_ffn2[l] + on_mix[l], moved):
            G[n][k] = m
        sv.update(x=x, x1=x1, x2=x2)
        saved.append(sv)
        x = x3
        if l == 1:
            hk = _rms_fwd("kv_rms", x, W["kv_norm"].reshape(1, -1))
            kvf = _mm("kv_proj", [(hk, w_kv, NN)], [F32], T, w_kv.shape[1])
            cum = _cumsum_rows("kv_cum", _log_forget(kvf, fox_bias))[:, :FOX_HEADS].T
            shared = dict(kvf=kvf, cr=cum[:, None, :], hk=hk, x=x)

    def loss_fn(y, t):
        err = y - t
        return err * (1.0 / D_MODEL), jnp.sum(0.5 / D_MODEL * err * err, axis=0, keepdims=True)
    dx, loss = _rowwise("loss", loss_fn, [x, target], [], [(D_MODEL, F32)], [((1, D_MODEL), F32)])

    grads = [None] * n_l
    acc = dict(dk=jnp.zeros((T, MAIN_WIDTH), F32), dv=jnp.zeros((T, MAIN_WIDTH), F32), dc=jnp.zeros((FOX_HEADS, 1, T), F32))
    gkv = {}
    landed, late = {}, []
    for l in reversed(range(n_l)):
        sv, Wl = saved[l], G[l]
        ready, more, extra = [], [], []
        if l == 1:
            dcum = jnp.pad(acc["dc"][:, 0, :].T, ((0, 0), (0, LANES - FOX_HEADS)))
            dlf = _cumsum_rows("kv_dcum", dcum, reverse=True)
            def dlogf_fn(d, f, b):
                p = d * _sigmoid(-(f + b))
                return p, jnp.sum(p, axis=0, keepdims=True)
            dfl, gkv["fox_f_bias"] = _rowwise("kv_dlogf", dlogf_fn, [dlf, _forget_cols(shared["kvf"])], [fox_bias],
                                              [(LANES, BF16)], [((1, LANES), F32)])
            dkvf = _pad_cols(jnp.concatenate([acc["dk"].astype(BF16), acc["dv"].astype(BF16), dfl], axis=1), w_kv.shape[1])
            dw_kv = _mm("kv_dw", [(shared["hk"], dkvf, TN)], [BF16], D_MODEL, dkvf.shape[1])
            extra.append(((0, "w_kv"), dw_kv[:, :KV_WIDTH].reshape(N_DEV, -1, KV_WIDTH)))
            dhk = _mm("kv_dh", [(dkvf, w_kv, NT)], [F32], T, D_MODEL)
            dx, gkv["kv_norm"] = _rms_bwd("kv_rmsb", shared["x"], W["kv_norm"].reshape(1, -1), dhk, dres=dx)
        g = {}
        if l < 2:
            ready, late = ready + late[1:], late[:1]
        else:
            more, extra, late = late[1:2], extra + late[2:], late[:1]
        dx2, g["ffn2_norm"], dwg, dwu, dwd, moved_a, moved_w = _ffn_bwd(
            f"l{l}b_ffn2", sv["x2"], *ffn2(l, Wl), dx, comm_a=_Comm(scatter=[v for _, v in late[:2]]),
            comm_w=_Comm(scatter=[v for _, v in late[2:]]))
        landed.update({k: m for (k, _), m in zip(late, moved_a + moved_w)})
        ready += [((l, "ffn2_w_gate"), dwg), ((l, "ffn2_w_up"), dwu), ((l, "ffn2_w_down"), dwd)]
        dx1, gm, rest = _mixer_bwd(l, sv["x1"], mem, Wl, W, lbs, shared, sv, dx2, acc, ready, landed)
        g.update(gm)
        rest = rest + more
        dx, g["ffn1_norm"], dwg, dwu, dwd, moved_a, moved_w = _ffn_bwd(
            f"l{l}b_ffn1", sv["x"], *ffn1(l, Wl), dx1, comm_a=_Comm(scatter=[v for _, v in rest]),
            comm_w=_Comm(scatter=[v for _, v in extra]))
        landed.update({k: m for (k, _), m in zip(rest + extra, moved_a + moved_w)})
        late = [((l, "ffn1_w_gate"), dwg), ((l, "ffn1_w_up"), dwu), ((l, "ffn1_w_down"), dwd)]
        grads[l] = g

    out = {}
    for n in ["ffn1_norm", "mix_norm", "mem_norm", "mem_q_gain", "mem_k_gain", "ffn2_norm"]:
        out[n] = jnp.concatenate([grads[l][n] for l in range(4)], axis=0)
    out["hgrn_o_gain"] = jnp.concatenate([grads[l]["hgrn_o_gain"] for l in (0, 1)], axis=0)
    out["fox_q_gain"] = jnp.concatenate([grads[l]["fox_q_gain"] for l in (2, 3)], axis=0)
    out["fox_k_gain"] = (grads[2]["fox_k_gain"] + grads[3]["fox_k_gain"]).reshape(-1)
    out["kv_norm"] = gkv["kv_norm"].reshape(-1)
    out["fox_f_bias"] = gkv["fox_f_bias"][0, :FOX_HEADS]
    dl0, dl1 = _lb_bwd(lb_logits, grads[0]["lb"], grads[1]["lb"])
    out["hgrn_lb_logits"] = jnp.concatenate([dl0, dl1], axis=0)
    return loss, dx, out, landed, late


WEIGHTS = ["ffn1_norm", "ffn1_w_gate", "ffn1_w_up", "ffn1_w_down", "mix_norm", "mem_norm", "w_mem_kv", "mem_q_gain",
           "mem_k_gain", "w_in_a", "hgrn_lb_logits", "hgrn_o_gain", "w_in_b", "fox_q_gain", "kv_norm", "w_kv", "fox_f_bias",
           "fox_k_gain", "w_out", "ffn2_norm", "ffn2_w_gate", "ffn2_w_up", "ffn2_w_down"]
BIG = COLS352 + ["w_in_a", "w_in_b", "ffn1_w_down", "ffn2_w_down", "w_out", "w_mem_kv", "w_kv"]


def _train_step(a):
    bf = lambda w: w.astype(BF16)
    n_l = a["w_out"].shape[0]
    local = []
    for l in range(n_l):
        w_in = a["w_in_a"][l] if l < a["w_in_a"].shape[0] else a["w_in_b"][l - a["w_in_a"].shape[0]]
        local.append(dict(
            gu1=bf(jnp.concatenate([a["ffn1_w_gate"][l], a["ffn1_w_up"][l]], axis=0)),
            r1=bf(jnp.concatenate([a["ffn1_w_down"][l], a["w_out"][l], _pad_cols(a["w_mem_kv"][l], D_MODEL)], axis=0)),
            win=bf(w_in),
            gu2=bf(jnp.concatenate([a["ffn2_w_gate"][l], a["ffn2_w_up"][l]], axis=0)),
            d2=bf(a["ffn2_w_down"][l])))
    keys = ["gu1", "r1", "win", "gu2", "d2"]
    first = _all_gather("ag_first", [local[0][k] for k in keys] + [bf(a["w_kv"]), _small_pack([a["hgrn_lb_logits"]])])
    G0 = dict(zip(keys, first))
    W = {n: a[n] for n in SMALL}
    W["w_kv"] = _pad_cols(first[5].reshape(-1, a["w_kv"].shape[1]), 2 * D_MODEL)
    lb_shape = a["hgrn_lb_logits"].shape
    lb_all = first[6].reshape(N_DEV, -1)[:, :lb_shape[0] * lb_shape[1]]
    lb_logits = lb_all.reshape((N_DEV,) + lb_shape).transpose(1, 0, 2).reshape(lb_shape[0], -1)

    loss_part, dx, g, landed, tail = _step(a["x"][0], a["mem"][0], a["loss_target"][0], W, lb_logits, G0, local)

    n_a = a["w_in_a"].shape[0]
    grad, delta, new_m, new_v = {}, {}, {}, {}
    order = [n for n in BIG if n.startswith("ffn2")] + [n for n in BIG if not n.startswith("ffn")]
    for n in order + [n for n in BIG if n.startswith("ffn1")]:
        ls = range(n_a) if n == "w_in_a" else range(n_a, n_l) if n == "w_in_b" else range(1) if n == "w_kv" else range(n_l)
        key = "w_in" if n.startswith("w_in") else n
        lead = (lambda t: t[None]) if a[n].ndim == 2 else (lambda t: t)
        riding, tail = tail[:1], tail[1:]
        res, moved = _sum_adamw("adam_" + n, [landed[(l, key)] for l in ls], lead(a[n]), lead(a["m_" + n]), lead(a["v_" + n]),
                                comm=_Comm(scatter=[v for _, v in riding]))
        landed.update({k: m for (k, _), m in zip(riding, moved)})
        grad[n], delta[n], new_m[n], new_v[n] = [t.reshape(a[n].shape) for t in res]

    zeros = [jnp.zeros(lb_logits.shape, F32), jnp.zeros(loss_part.shape, F32)]
    small_shapes = [a[n].shape for n in SMALL] + [lb_logits.shape, loss_part.shape]
    small_part = _small_pack([g[n] for n in SMALL] + [g["hgrn_lb_logits"], loss_part])
    small_sum = _sum_slabs("small_sum", _all_gather("ag_small", [small_part])[0], F32)
    small = _small_unpack(small_sum, small_shapes)
    grad.update(dict(zip(SMALL, small)))
    loss = jnp.sum(small[-1])
    me = 4 * lax.axis_index("x") + 2 * lax.axis_index("y") + lax.axis_index("c")
    grad["hgrn_lb_logits"] = lax.dynamic_slice_in_dim(small[-2], me * lb_shape[1], lb_shape[1], axis=1)

    n = "hgrn_lb_logits"
    delta[n], new_m[n], new_v[n] = _adamw("adam_" + n, a[n], grad[n], a["m_" + n], a["v_" + n])
    packs = [_small_pack([a[p + n] for n in SMALL] + zeros) for p in ("", "m_", "v_")]
    upd = _rowwise("adam_small", _adamw_fn, [packs[0], small_sum, packs[1], packs[2]], [], [(LANES, F32)] * 3, tm=packs[0].shape[0])
    for d, u in zip((delta, new_m, new_v), upd):
        d.update(dict(zip(SMALL, _small_unpack(u, small_shapes))))
    return (loss, dx[None], *[grad[n] for n in WEIGHTS], *[delta[n] for n in WEIGHTS], *[new_m[n] for n in WEIGHTS],
            *[new_v[n] for n in WEIGHTS])


def kernel(x, mem, ffn1_norm, ffn1_w_gate, ffn1_w_up, ffn1_w_down, mix_norm, mem_norm, w_mem_kv, mem_q_gain, mem_k_gain, w_in_a, hgrn_lb_logits, hgrn_o_gain, w_in_b, fox_q_gain, kv_norm, w_kv, fox_f_bias, fox_k_gain, w_out, ffn2_norm, ffn2_w_gate, ffn2_w_up, ffn2_w_down, loss_target, m_ffn1_norm, m_ffn1_w_gate, m_ffn1_w_up, m_ffn1_w_down, m_mix_norm, m_mem_norm, m_w_mem_kv, m_mem_q_gain, m_mem_k_gain, m_w_in_a, m_hgrn_lb_logits, m_hgrn_o_gain, m_w_in_b, m_fox_q_gain, m_kv_norm, m_w_kv, m_fox_f_bias, m_fox_k_gain, m_w_out, m_ffn2_norm, m_ffn2_w_gate, m_ffn2_w_up, m_ffn2_w_down, v_ffn1_norm, v_ffn1_w_gate, v_ffn1_w_up, v_ffn1_w_down, v_mix_norm, v_mem_norm, v_w_mem_kv, v_mem_q_gain, v_mem_k_gain, v_w_in_a, v_hgrn_lb_logits, v_hgrn_o_gain, v_w_in_b, v_fox_q_gain, v_kv_norm, v_w_kv, v_fox_f_bias, v_fox_k_gain, v_w_out, v_ffn2_norm, v_ffn2_w_gate, v_ffn2_w_up, v_ffn2_w_down):
    return _train_step(dict(locals()))
```

```python
import functools

import jax
import jax.numpy as jnp
from jax import lax
from jax.experimental import pallas as pl
from jax.experimental.pallas import tpu as pltpu

F32, BF16 = jnp.float32, jnp.bfloat16
EPS = 1e-6
V7X_VMEM_LIMIT = 56 * 1024 * 1024
LANES = 128
N_DEV = 8

D_MODEL = 1024
MAIN_WIDTH = 768
MEM_WIDTH = 256
HG_HEAD_DIM = 128
HG_HEADS = 6
FOX_HEAD_DIM = 64
FOX_HEADS = 12
MEM_HEADS = 4
MEM_HEAD_DIM = 64
HG_BLOCK = 16

ADAM_LR, ADAM_B1, ADAM_B2, ADAM_EPS, ADAM_WD, ADAM_STEP = 0.001, 0.9, 0.999, 1e-08, 0.01, 10

NN = ((1,), (0,))
NT = ((1,), (1,))
TN = ((0,), (0,))


def _dot(a, b, dims, precision=None):
    return lax.dot_general(a, b, (dims, ((), ())), preferred_element_type=F32, precision=precision)


def _bdot(a, b, dims):
    return _dot(a.astype(BF16), b.astype(BF16), dims)


def _split(a):
    hi = a.astype(BF16)
    return hi, (a - hi.astype(F32)).astype(BF16)


def _fdot(a, b, dims):
    ah, al = _split(a)
    bh, bl = _split(b)
    return _dot(ah, bh, dims) + (_dot(ah, bl, dims) + _dot(al, bh, dims))


def _params(n_grid):
    return pltpu.CompilerParams(dimension_semantics=("arbitrary",) * n_grid, vmem_limit_bytes=V7X_VMEM_LIMIT)


def _rms(x, g):
    return x * lax.rsqrt(jnp.mean(x * x, axis=-1, keepdims=True) + EPS) * g


def _sigmoid(x):
    return jax.nn.sigmoid(x)


def _silu(x):
    return x * jax.nn.sigmoid(x)


MESH = pl.DeviceIdType.MESH
ANY = pl.BlockSpec(memory_space=pl.ANY)


def _mesh_pos():
    return lax.axis_index("x"), lax.axis_index("y"), lax.axis_index("c")


class _Comm:
    def __init__(self, gather=(), relay=(), scatter=()):
        self.gather, self.relay, self.scatter = list(gather), list(relay), list(scatter)
        self.arrays = self.gather + self.relay + self.scatter
        self.n_remote = 4 * len(self.gather) + 3 * len(self.relay) + 7 * len(self.scatter)
        self.n_local = len(self.gather) + len(self.scatter)

    def out_shapes(self):
        return ([jax.ShapeDtypeStruct((N_DEV,) + x.shape, x.dtype) for x in self.gather]
                + [jax.ShapeDtypeStruct(g.shape, g.dtype) for g in self.relay + self.scatter])

    def scratch(self):
        return [pltpu.SemaphoreType.DMA((self.n_remote,)), pltpu.SemaphoreType.DMA((self.n_remote,)),
                pltpu.SemaphoreType.DMA((max(self.n_local, 1),))]

    def _copies(self, ins, outs, send, recv, local, arrivals=True):
        mx, my, mc = _mesh_pos()
        flip = lambda v, f: 1 - v if f else v
        idx = lambda p: 4 * p[0] + 2 * p[1] + p[2]
        me = (mx, my, mc)
        count = [0, 0]
        loc, out, arrive = [], [], []

        def pair(src, dst, lands, to):
            k = count[0]
            count[0] += 1
            mk = lambda d: pltpu.make_async_remote_copy(src_ref=src, dst_ref=d, send_sem=send.at[k], recv_sem=recv.at[k],
                                                        device_id=to, device_id_type=MESH)
            out.append(mk(dst))
            if arrivals:
                arrive.append(mk(lands))

        def local_copy(src, dst):
            loc.append(pltpu.make_async_copy(src, dst, local.at[count[1]]))
            count[1] += 1

        refs = list(zip(ins, outs))
        near = [(0, 0, 1), (1, 0, 0), (0, 1, 0), (1, 1, 0)]
        for x, G in refs[:len(self.gather)]:
            local_copy(x, G.at[idx(me)])
            for f in near:
                peer = tuple(flip(v, b) for v, b in zip(me, f))
                pair(x, G.at[idx(me)], G.at[idx(peer)], peer)
        sibling = (mx, my, 1 - mc)
        for Gin, Gout in refs[len(self.gather):len(self.gather) + len(self.relay)]:
            for f in near[1:]:
                chip = (flip(mx, f[0]), flip(my, f[1]))
                pair(Gin.at[idx((*chip, mc))], Gout.at[idx((*chip, mc))], Gout.at[idx((*chip, 1 - mc))], sibling)
        every = near + [(1, 0, 1), (0, 1, 1), (1, 1, 1)]
        for g, R in refs[len(self.gather) + len(self.relay):]:
            local_copy(g.at[idx(me)], R.at[idx(me)])
            for f in every:
                peer = tuple(flip(v, b) for v, b in zip(me, f))
                pair(g.at[idx(peer)], R.at[idx(me)], R.at[idx(peer)], peer)
        return loc, out, arrive

    def start(self, ins, outs, send, recv, local):
        loc, out, _ = self._copies(ins, outs, send, recv, local, arrivals=False)
        for cp in loc + out:
            cp.start()

    def finish(self, ins, outs, send, recv, local):
        loc, out, arrive = self._copies(ins, outs, send, recv, local)
        for cp in arrive:
            cp.wait_recv()
        for cp in out:
            cp.wait_send()
        for cp in loc:
            cp.wait()


def _call(body, operands, out_shape, *, name, grid=(), in_specs=None, out_specs=None, scratch=(), comm=None):
    outs = list(out_shape) if isinstance(out_shape, (list, tuple)) else [out_shape]
    single = not isinstance(out_shape, (list, tuple))
    params = _params(len(grid))
    if comm is None or not comm.arrays:
        res = pl.pallas_call(body, grid=grid, in_specs=in_specs, out_specs=out_specs, out_shape=out_shape,
                             scratch_shapes=list(scratch), name=name, compiler_params=params)(*operands)
        return ([res] if single else list(res)), []
    n_in, n_out, n_s, n_c = len(operands), len(outs), len(scratch), len(comm.arrays)

    def wrapped(*refs):
        pos = [0]

        def take(n):
            pos[0] += n
            return refs[pos[0] - n:pos[0]]

        b_in, c_in, b_out, c_out, b_s, sems = take(n_in), take(n_c), take(n_out), take(n_c), take(n_s), take(3)
        ids = [pl.program_id(d) for d in range(len(grid))]
        first, last = True, True
        for d, i in enumerate(ids):
            first = (i == 0) & first
            last = (i == grid[d] - 1) & last
        if grid:
            pl.when(first)(lambda: comm.start(c_in, c_out, *sems))
        else:
            comm.start(c_in, c_out, *sems)
        body(*b_in, *b_out, *b_s)
        if grid:
            pl.when(last)(lambda: comm.finish(c_in, c_out, *sems))
        else:
            comm.finish(c_in, c_out, *sems)

    n_g = len(comm.gather)
    aliases = {n_in + n_g + r: n_out + n_g + r for r in range(len(comm.relay))}
    out_specs_l = list(out_specs) if isinstance(out_specs, (list, tuple)) else [out_specs]
    res = pl.pallas_call(
        wrapped, grid=grid, in_specs=list(in_specs) + [ANY] * n_c, out_specs=out_specs_l + [ANY] * n_c,
        out_shape=outs + comm.out_shapes(), scratch_shapes=list(scratch) + comm.scratch(), input_output_aliases=aliases,
        name=name, compiler_params=params)(*operands, *comm.arrays)
    return list(res[:n_out]), list(res[n_out:])


def _rowwise(name, fn, rows, consts, out_rows, out_reds=(), tm=512):
    R = rows[0].shape[0]
    tm = min(tm, R)
    assert R % tm == 0
    n_in, n_o = len(rows) + len(consts), len(out_rows)

    def body(*refs):
        outs = fn(*[r[...] for r in refs[:n_in]])
        if not isinstance(outs, (tuple, list)):
            outs = (outs,)
        for r, o in zip(refs[n_in:n_in + n_o], outs[:n_o]):
            r[...] = o.astype(r.dtype)
        red_refs = refs[n_in + n_o:]
        if red_refs:
            @pl.when(pl.program_id(0) == 0)
            def _():
                for r in red_refs:
                    r[...] = jnp.zeros(r.shape, r.dtype)
            for r, o in zip(red_refs, outs[n_o:]):
                r[...] += o

    zero = lambda n: (lambda i: (0,) * n)
    in_specs = [pl.BlockSpec((tm, a.shape[1]), lambda i: (i, 0)) for a in rows]
    in_specs += [pl.BlockSpec(c.shape, zero(c.ndim)) for c in consts]
    out_specs = [pl.BlockSpec((tm, c), lambda i: (i, 0)) for c, _ in out_rows]
    out_specs += [pl.BlockSpec(s, zero(len(s))) for s, _ in out_reds]
    out_shape = [jax.ShapeDtypeStruct((R, c), dt) for c, dt in out_rows]
    out_shape += [jax.ShapeDtypeStruct(s, dt) for s, dt in out_reds]
    return pl.pallas_call(body, grid=(R // tm,), in_specs=in_specs, out_specs=out_specs, out_shape=out_shape,
                          name=name, compiler_params=_params(1))(*rows, *consts)


def _tile(n, cap):
    best = None
    for t in range(LANES, min(n, cap) + 1, LANES):
        if n % t == 0:
            best = t
    return best or n


def _mm(name, pairs, out_dtypes, M, N, epi=None, extras=(), tm=512, tn=512):
    tm, tn = _tile(M, tm), _tile(N, tn)
    n_p, n_e = len(pairs), len(extras)
    modes = [m for _, _, m in pairs]

    def body(*refs):
        accs = [_bdot(refs[2 * k][...], refs[2 * k + 1][...], modes[k]) for k in range(n_p)]
        ex = [r[...] for r in refs[2 * n_p:2 * n_p + n_e]]
        outs = epi(accs, ex) if epi is not None else accs
        for r, o in zip(refs[2 * n_p + n_e:], outs):
            r[...] = o.astype(r.dtype)

    in_specs = []
    ops = []
    for a, b, mode in pairs:
        if mode == NN:
            K = a.shape[1]
            assert a.shape == (M, K) and b.shape == (K, N), (name, a.shape, b.shape)
            in_specs += [pl.BlockSpec((tm, K), lambda i, j: (i, 0)), pl.BlockSpec((K, tn), lambda i, j: (0, j))]
        elif mode == NT:
            K = a.shape[1]
            assert a.shape == (M, K) and b.shape == (N, K), (name, a.shape, b.shape)
            in_specs += [pl.BlockSpec((tm, K), lambda i, j: (i, 0)), pl.BlockSpec((tn, K), lambda i, j: (j, 0))]
        else:
            K = a.shape[0]
            assert a.shape == (K, M) and b.shape == (K, N), (name, a.shape, b.shape)
            in_specs += [pl.BlockSpec((K, tm), lambda i, j: (0, i)), pl.BlockSpec((K, tn), lambda i, j: (0, j))]
        ops += [a, b]
    in_specs += [pl.BlockSpec((tm, tn), lambda i, j: (i, j)) for _ in extras]
    out_specs = [pl.BlockSpec((tm, tn), lambda i, j: (i, j)) for _ in out_dtypes]
    out_shape = [jax.ShapeDtypeStruct((M, N), dt) for dt in out_dtypes]
    res = pl.pallas_call(body, grid=(M // tm, N // tn), in_specs=in_specs, out_specs=out_specs, out_shape=out_shape,
                         name=name, compiler_params=_params(2))(*ops, *extras)
    return res[0] if len(res) == 1 else res


def _rms_fwd(name, x, gain, dtype=BF16):
    return _rowwise(name, _rms, [x], [gain], [(x.shape[1], dtype)])[0]


def _rms_bwd(name, x, gain, dh, dres=None):
    def fn(x, dh, *rest):
        g = rest[-1]
        _, vjp = jax.vjp(_rms, x, g)
        dx, dg = vjp(dh)
        if dres is not None:
            dx = dx + rest[0]
        return dx, dg
    rows = [x, dh] + ([dres] if dres is not None else [])
    d = x.shape[1]
    return _rowwise(name, fn, rows, [gain], [(d, F32)], [((1, d), F32)])


def _ffn_specs(gcols, grows, ig, iu, idn):
    n = gcols.shape[2]
    D = grows.shape[2]
    wg = pl.BlockSpec((None, D, n), lambda i, j: (j, ig, 0))
    wu = pl.BlockSpec((None, D, n), lambda i, j: (j, iu, 0))
    wd = pl.BlockSpec((None, n, D), lambda i, j: (j, idn, 0))
    return n, wg, wu, wd


def _ffn_fwd(name, x, gain, gcols, grows, ig, iu, idn, tm=1024, comm=None):
    T, D = x.shape
    tm = min(T, tm)
    n, wg_s, wu_s, wd_s = _ffn_specs(gcols, grows, ig, iu, idn)
    last = N_DEV - 1

    def body(x_ref, g_ref, wg_ref, wu_ref, wd_ref, y_ref, h_s, acc):
        j = pl.program_id(1)

        @pl.when(j == 0)
        def _():
            h_s[...] = _rms(x_ref[...], g_ref[...]).astype(BF16)
            acc[...] = jnp.zeros(acc.shape, F32)
        h = h_s[...]
        z = _silu(_dot(h, wg_ref[...], NN)) * _dot(h, wu_ref[...], NN)
        acc[...] += _dot(z.astype(BF16), wd_ref[...], NN)

        @pl.when(j == last)
        def _():
            y_ref[...] = x_ref[...] + 0.5 * acc[...]

    row = pl.BlockSpec((tm, D), lambda i, j: (i, 0))
    (y,), moved = _call(
        body, [x, gain, gcols, gcols, grows], [jax.ShapeDtypeStruct((T, D), F32)], name=name, grid=(T // tm, N_DEV),
        in_specs=[row, pl.BlockSpec((1, D), lambda i, j: (0, 0)), wg_s, wu_s, wd_s], out_specs=[row],
        scratch=[pltpu.VMEM((tm, D), BF16), pltpu.VMEM((tm, D), F32)], comm=comm)
    return y, moved


def _ffn_bwd(tag, x, gain, gcols, grows, ig, iu, idn, dy, tm=512, comm_a=None, comm_w=None):
    T, D = x.shape
    tm = min(T, tm)
    n, wg_s, wu_s, wd_s = _ffn_specs(gcols, grows, ig, iu, idn)
    last = N_DEV - 1

    def body(x_ref, dy_ref, g_ref, wg_ref, wu_ref, wd_ref, dx_ref, dg_ref, h_ref, z_ref, da_ref, db_ref, dh_acc):
        i, j = pl.program_id(0), pl.program_id(1)

        @pl.when(j == 0)
        def _():
            h_ref[...] = _rms(x_ref[...], g_ref[...]).astype(BF16)
            dh_acc[...] = jnp.zeros(dh_acc.shape, F32)

        @pl.when((i == 0) & (j == 0))
        def _():
            dg_ref[...] = jnp.zeros(dg_ref.shape, F32)
        h = h_ref[...]
        a, b = _dot(h, wg_ref[...], NN), _dot(h, wu_ref[...], NN)
        dz = 0.5 * _dot(dy_ref[...].astype(BF16), wd_ref[...], NT)
        s = _sigmoid(a)
        si = a * s
        da = (dz * b * (s + si * (1.0 - s))).astype(BF16)
        db = (dz * si).astype(BF16)
        z_ref[...] = (si * b).astype(BF16)
        da_ref[...] = da
        db_ref[...] = db
        dh_acc[...] += _dot(da, wg_ref[...], NT) + _dot(db, wu_ref[...], NT)

        @pl.when(j == last)
        def _():
            _, vjp = jax.vjp(_rms, x_ref[...], g_ref[...])
            dx, dg = vjp(dh_acc[...])
            dx_ref[...] = dx + dy_ref[...]
            dg_ref[...] += dg

    row = pl.BlockSpec((tm, D), lambda i, j: (i, 0))
    vec = pl.BlockSpec((1, D), lambda i, j: (0, 0))
    hid = pl.BlockSpec((None, tm, n), lambda i, j: (j, i, 0))
    hidden = jax.ShapeDtypeStruct((N_DEV, T, n), BF16)
    (dx, dgain, h, z, da, db), moved_a = _call(
        body, [x, dy, gain, gcols, gcols, grows],
        [jax.ShapeDtypeStruct((T, D), F32), jax.ShapeDtypeStruct((1, D), F32), jax.ShapeDtypeStruct((T, D), BF16),
         hidden, hidden, hidden],
        name=tag + "_a", grid=(T // tm, N_DEV), in_specs=[row, row, vec, wg_s, wu_s, wd_s],
        out_specs=[row, vec, row, hid, hid, hid], scratch=[pltpu.VMEM((tm, D), F32)], comm=comm_a)

    def wbody(h_ref, dy_ref, z_ref, da_ref, db_ref, dwg_ref, dwu_ref, dwd_ref):
        h = h_ref[...]
        dwg_ref[...] = _dot(h, da_ref[...], TN).astype(BF16)
        dwu_ref[...] = _dot(h, db_ref[...], TN).astype(BF16)
        dwd_ref[...] = (0.5 * _dot(z_ref[...], dy_ref[...].astype(BF16), TN)).astype(BF16)

    full = pl.BlockSpec((T, D), lambda j: (0, 0))
    hid_all = pl.BlockSpec((None, T, n), lambda j: (j, 0, 0))
    (dwg, dwu, dwd), moved_w = _call(
        wbody, [h, dy, z, da, db],
        [jax.ShapeDtypeStruct((N_DEV, D, n), BF16)] * 2 + [jax.ShapeDtypeStruct((N_DEV, n, D), BF16)],
        name=tag + "_w", grid=(N_DEV,), in_specs=[full, full, hid_all, hid_all, hid_all],
        out_specs=[pl.BlockSpec((None, D, n), lambda j: (j, 0, 0))] * 2 + [pl.BlockSpec((None, n, D), lambda j: (j, 0, 0))],
        comm=comm_w)
    return dx, dgain, dwg, dwu, dwd, moved_a, moved_w


def _wcols_spec(gw, l, grid_rank):
    _, _, n = gw.shape
    K = D_MODEL
    zero = (lambda i: (0, l, 0)) if grid_rank == 1 else (lambda i, j: (0, l, 0))
    return n, K, pl.BlockSpec((N_DEV, K, n), zero)


def _proj_cols(name, h, gw, l, tm=512, comm=None):
    T = h.shape[0]
    tm = min(T, tm)
    n, K, wspec = _wcols_spec(gw, l, 1)

    def body(h_ref, w_ref, o_ref):
        h = h_ref[...]
        for j in range(N_DEV):
            o_ref[:, pl.ds(j * n, n)] = _dot(h, w_ref[j], NN)

    (proj,), moved = _call(
        body, [h, gw], [jax.ShapeDtypeStruct((T, N_DEV * n), F32)], name=name, grid=(T // tm,),
        in_specs=[pl.BlockSpec((tm, K), lambda i: (i, 0)), wspec], out_specs=[pl.BlockSpec((tm, N_DEV * n), lambda i: (i, 0))],
        comm=comm)
    return proj, moved


def _proj_cols_bwd(tag, h, dproj, gw, l, tm=512, tk=512, comm=None):
    T = h.shape[0]
    tm = min(T, tm)
    n, K, wspec = _wcols_spec(gw, l, 1)

    def dh_body(dp_ref, w_ref, o_ref):
        acc = jnp.zeros(o_ref.shape, F32)
        for j in range(N_DEV):
            acc = acc + _dot(dp_ref[:, pl.ds(j * n, n)], w_ref[j], NT)
        o_ref[...] = acc

    (dh,), moved = _call(
        dh_body, [dproj, gw], [jax.ShapeDtypeStruct((T, K), F32)], name=tag + "_dh", grid=(T // tm,),
        in_specs=[pl.BlockSpec((tm, N_DEV * n), lambda i: (i, 0)), wspec], out_specs=[pl.BlockSpec((tm, K), lambda i: (i, 0))],
        comm=comm)

    def dw_body(h_ref, dp_ref, o_ref):
        h = h_ref[...]
        for j in range(N_DEV):
            o_ref[j] = _dot(h, dp_ref[:, pl.ds(j * n, n)], TN).astype(BF16)

    dw = pl.pallas_call(
        dw_body, grid=(K // tk,), in_specs=[pl.BlockSpec((T, tk), lambda i: (0, i)), pl.BlockSpec((T, N_DEV * n), lambda i: (0, 0))],
        out_specs=pl.BlockSpec((N_DEV, tk, n), lambda i: (0, i, 0)), out_shape=jax.ShapeDtypeStruct((N_DEV, K, n), BF16),
        name=tag + "_dw", compiler_params=_params(1))(h, dproj)
    return dh, dw, moved


def _block_tri(n, reverse=False):
    r = lax.broadcasted_iota(jnp.int32, (n, n), 0)
    c = lax.broadcasted_iota(jnp.int32, (n, n), 1)
    same = (r // HG_BLOCK) == (c // HG_BLOCK)
    return (same & ((c >= r) if reverse else (c <= r))).astype(F32)


def _hgrn_prep(q_ref, f_ref, lbv, qs, ks, cs, T):
    pt = min(T, 256)
    tri = _block_tri(pt)
    for p in range(T // pt):
        rows = pl.ds(p * pt, pt)
        f = lbv + (1.0 - lbv) * _sigmoid(f_ref[rows, :])
        qs[rows, :] = _silu(q_ref[rows, :])
        ks[rows, :] = 1.0 - f
        cs[rows, :] = _dot(tri, jnp.log(f), NN, precision=lax.Precision.HIGHEST)


HG_GROUP = 128


def _groups_loop(nb, fn):
    gp = HG_GROUP if nb % HG_GROUP == 0 else nb

    def step(i, carry):
        base = pl.multiple_of(i * (gp * HG_BLOCK), gp * HG_BLOCK)
        fn(lambda t: pl.ds(base + t, gp, stride=HG_BLOCK))
        return carry

    lax.fori_loop(0, nb // gp, step, 0)


def _gate_out(o, og, g):
    return _rms(o, og) * _silu(g)


HG_UNROLL = 16


def _block_rows(n):
    return pl.ds(pl.multiple_of(n * HG_BLOCK, HG_BLOCK), HG_BLOCK)


def _blocks_loop(nb, fn):
    u = HG_UNROLL if nb % HG_UNROLL == 0 else 1

    def step(i, carry):
        for k in range(u):
            fn(i * u + k)
        return carry

    lax.fori_loop(0, nb // u, step, 0)


def _scan_states(buf, cs, nb, reverse=False):
    def step(m, st):
        n = nb - 1 - m if reverse else m
        own = buf[n]
        buf[n] = st
        rows = _block_rows(n)
        return jnp.exp(cs[rows, :][HG_BLOCK - 1:HG_BLOCK, :]) * st + own

    lax.fori_loop(0, nb, step, jnp.zeros(buf.shape[1:], F32))


def _hgrn_states(i_ref, ks, cs, states, nb):
    def own_step(n):
        rows = _block_rows(n)
        c = cs[rows, :]
        states[n] = _fdot(i_ref[rows, :], ks[rows, :] * jnp.exp(c[HG_BLOCK - 1:HG_BLOCK, :] - c), TN)

    _blocks_loop(nb, own_step)
    _scan_states(states, cs, nb)


def _hgrn_fwd(name, proj, lb, og, comm=None):
    T = proj.shape[0]
    nb = T // HG_BLOCK
    hd = HG_HEAD_DIM

    def body(q_ref, f_ref, i_ref, g_ref, lb_ref, og_ref, main_ref, o_ref, qs, ks, cs, states):
        _hgrn_prep(q_ref, f_ref, lb_ref[...], qs, ks, cs, T)
        def pairs(at):
            for t in range(HG_BLOCK):
                qt, ct = qs[at(t), :], cs[at(t), :]
                acc = jnp.zeros(qt.shape, F32)
                for s in range(t + 1):
                    w = qt * ks[at(s), :] * jnp.exp(ct - cs[at(s), :])
                    acc = acc + jnp.sum(w, axis=-1, keepdims=True) * i_ref[at(s), :]
                o_ref[at(t), :] = acc

        _groups_loop(nb, pairs)

        _hgrn_states(i_ref, ks, cs, states, nb)

        def out_step(n):
            rows = _block_rows(n)
            o_ref[rows, :] += _fdot(qs[rows, :] * jnp.exp(cs[rows, :]), states[n], NT)

        _blocks_loop(nb, out_step)
        pt = min(T, 256)
        for p in range(T // pt):
            rows = pl.ds(p * pt, pt)
            main_ref[rows, :] = _gate_out(o_ref[rows, :], og_ref[...], g_ref[rows, :])

    nh = HG_HEADS
    col = lambda off: pl.BlockSpec((T, hd), lambda h, off=off: (0, off + h))
    (main, o), moved = _call(
        body, [proj, proj, proj, proj, lb, og], [jax.ShapeDtypeStruct((T, MAIN_WIDTH), F32)] * 2, name=name, grid=(nh,),
        in_specs=[col(0), col(nh), col(2 * nh), col(3 * nh), pl.BlockSpec((1, hd), lambda h: (0, h)),
                  pl.BlockSpec((1, hd), lambda h: (0, 0))],
        out_specs=[col(0), col(0)], scratch=[pltpu.VMEM((T, hd), F32)] * 3 + [pltpu.VMEM((nb, hd, hd), F32)], comm=comm)
    return main, o, moved


def _hgrn_bwd(name, proj, o, dmix, lb, og, comm=None):
    T = proj.shape[0]
    nb = T // HG_BLOCK
    hd = HG_HEAD_DIM
    pt = min(T, 256)

    def body(q_ref, f_ref, i_ref, g_ref, o_ref, dm_ref, lb_ref, og_ref,
             dq_ref, df_ref, di_ref, dg_ref, dlb_ref, dog_ref, qs, ks, cs, dos, dqs, dks, dvs, states, behind):
        lbv = lb_ref[...]
        _hgrn_prep(q_ref, f_ref, lbv, qs, ks, cs, T)
        dog = jnp.zeros((1, hd), F32)
        for p in range(T // pt):
            rows = pl.ds(p * pt, pt)
            _, vjp = jax.vjp(_gate_out, o_ref[rows, :], og_ref[...], g_ref[rows, :])
            do, dog_p, dg = vjp(dm_ref[rows, :])
            dos[rows, :] = do
            dg_ref[rows, :] = dg.astype(dg_ref.dtype)
            dog = dog + dog_p

        @pl.when(pl.program_id(0) == 0)
        def _():
            dog_ref[...] = jnp.zeros(dog_ref.shape, F32)
        dog_ref[...] += dog

        def pairs(at):
            for t in range(HG_BLOCK):
                dqs[at(t), :] = jnp.zeros((HG_GROUP if nb % HG_GROUP == 0 else nb, hd), F32)
            for s in range(HG_BLOCK):
                k_s, c_s, v_s = ks[at(s), :], cs[at(s), :], i_ref[at(s), :]
                dk = jnp.zeros(k_s.shape, F32)
                dv = jnp.zeros(k_s.shape, F32)
                for t in range(s, HG_BLOCK):
                    q_t, do_t = qs[at(t), :], dos[at(t), :]
                    e = jnp.exp(cs[at(t), :] - c_s)
                    a = jnp.sum(q_t * k_s * e, axis=-1, keepdims=True)
                    g = jnp.sum(do_t * v_s, axis=-1, keepdims=True)
                    dqs[at(t), :] += g * k_s * e
                    dk = dk + g * q_t * e
                    dv = dv + a * do_t
                dks[at(s), :] = dk
                dvs[at(s), :] = dv

        _groups_loop(nb, pairs)

        _hgrn_states(i_ref, ks, cs, states, nb)

        def own_step(n):
            rows = _block_rows(n)
            behind[n] = _fdot(dos[rows, :], qs[rows, :] * jnp.exp(cs[rows, :]), TN)

        _blocks_loop(nb, own_step)
        _scan_states(behind, cs, nb, reverse=True)

        def grad_step(n):
            rows = _block_rows(n)
            c = cs[rows, :]
            ec, ek = jnp.exp(c), jnp.exp(c[HG_BLOCK - 1:HG_BLOCK, :] - c)
            dst = behind[n]
            dqs[rows, :] += _fdot(dos[rows, :], states[n], NN) * ec
            dks[rows, :] += _fdot(i_ref[rows, :], dst, NN) * ek
            dvs[rows, :] += _fdot(ks[rows, :] * ek, dst, NT)

        _blocks_loop(nb, grad_step)

        full = (lax.broadcasted_iota(jnp.int32, (pt, pt), 1) >= lax.broadcasted_iota(jnp.int32, (pt, pt), 0)).astype(F32)
        carry = jnp.zeros((1, hd), F32)
        dlb = jnp.zeros((1, hd), F32)
        for p in reversed(range(T // pt)):
            rows = pl.ds(p * pt, pt)
            q, k, dq, dk = qs[rows, :], ks[rows, :], dqs[rows, :], dks[rows, :]
            db = q * dq - k * dk
            dlf = _dot(full, db, NN, precision=lax.Precision.HIGHEST) + carry
            carry = carry + jnp.sum(db, axis=0, keepdims=True)
            sg = _sigmoid(f_ref[rows, :])
            df = dlf / (1.0 - k) - dk
            df_ref[rows, :] = (df * (1.0 - lbv) * sg * (1.0 - sg)).astype(df_ref.dtype)
            dlb = dlb + jnp.sum(df * (1.0 - sg), axis=0, keepdims=True)
            qr = q_ref[rows, :]
            sq = _sigmoid(qr)
            dq_ref[rows, :] = (dq * (sq + qr * sq * (1.0 - sq))).astype(dq_ref.dtype)
            di_ref[rows, :] = dvs[rows, :].astype(di_ref.dtype)
        dlb_ref[...] = dlb

    nh = HG_HEADS
    col = lambda off: pl.BlockSpec((T, hd), lambda h, off=off: (0, off + h))
    vec = pl.BlockSpec((1, hd), lambda h: (0, h))
    one = pl.BlockSpec((1, hd), lambda h: (0, 0))
    outs, moved = _call(
        body, [proj, proj, proj, proj, o, dmix, lb, og],
        [jax.ShapeDtypeStruct((T, MAIN_WIDTH), BF16)] * 4
        + [jax.ShapeDtypeStruct((1, MAIN_WIDTH), F32), jax.ShapeDtypeStruct((1, hd), F32)],
        name=name, grid=(nh,), in_specs=[col(0), col(nh), col(2 * nh), col(3 * nh), col(0), col(0), vec, one],
        out_specs=[col(0), col(0), col(0), col(0), vec, one],
        scratch=[pltpu.VMEM((T, hd), F32)] * 7 + [pltpu.VMEM((nb, hd, hd), F32)] * 2, comm=comm)
    return (*outs, moved)


def _softmax_rows(s):
    p = jnp.exp(s - jnp.max(s, axis=-1, keepdims=True))
    return p, jnp.sum(p, axis=-1, keepdims=True)


def _fox_probs(q, k, cr_ref, hh, qi, tq):
    q0 = qi * tq
    pieces = ([(0, q0)] if qi else []) + [(q0, q0 + tq)]
    ss = []
    for a, b in pieces:
        s = _dot(q, k[a:b], NT) - cr_ref[hh, :, pl.ds(a, b - a)]
        if a == q0:
            causal = lax.broadcasted_iota(jnp.int32, s.shape, 1) <= lax.broadcasted_iota(jnp.int32, s.shape, 0)
            s = jnp.where(causal, s, -jnp.inf)
        ss.append(s)
    m = functools.reduce(jnp.maximum, [jnp.max(s, axis=-1, keepdims=True) for s in ss])
    ps = [jnp.exp(s - m) for s in ss]
    l = functools.reduce(jnp.add, [jnp.sum(p, axis=-1, keepdims=True) for p in ps])
    return [(a, b, p) for (a, b), p in zip(pieces, ps)], l


def _fox_specs(T):
    w = 2 * FOX_HEAD_DIM
    n = MAIN_WIDTH // w
    col = lambda off: pl.BlockSpec((T, w), lambda p, off=off: (0, off + p))
    cr = pl.BlockSpec((2, 1, T), lambda p: (p, 0, 0))
    gain = pl.BlockSpec((1, FOX_HEAD_DIM), lambda p: (0, 0))
    return n, col, cr, gain


def _fox_fwd(name, proj, kvf, cr, gq, gk, comm=None):
    T = proj.shape[0]
    tq = min(T, 256)
    hd = FOX_HEAD_DIM
    scale = hd ** -0.5

    def body(q_ref, g_ref, k_ref, v_ref, cr_ref, gq_ref, gk_ref, main_ref, o_ref):
        for hh in range(2):
            lanes = pl.ds(hh * hd, hd)
            k = _rms(k_ref[:, lanes], gk_ref[...]).astype(BF16)
            v = v_ref[:, lanes].astype(BF16)
            for qi in range(T // tq):
                rows = pl.ds(qi * tq, tq)
                q = (_rms(q_ref[rows, lanes], gq_ref[...]) * scale).astype(BF16)
                ps, l = _fox_probs(q, k, cr_ref, hh, qi, tq)
                o = functools.reduce(jnp.add, [_dot(p.astype(BF16), v[a:b], NN) for a, b, p in ps]) / l
                o_ref[rows, lanes] = o
                main_ref[rows, lanes] = o * _sigmoid(g_ref[rows, lanes])

    n, col, crs, gain = _fox_specs(T)
    (main, o), moved = _call(
        body, [proj, proj, kvf, kvf, cr, gq, gk], [jax.ShapeDtypeStruct((T, MAIN_WIDTH), F32)] * 2, name=name, grid=(n,),
        in_specs=[col(0), col(n), col(0), col(n), crs, gain, gain], out_specs=[col(0), col(0)], comm=comm)
    return main, o, moved


def _fox_bwd(name, proj, kvf, cr, gq, gk, o, dmix, pdk, pdv, pdc, comm=None):
    T = proj.shape[0]
    tq = min(T, 256)
    hd = FOX_HEAD_DIM
    scale = hd ** -0.5

    def body(q_ref, g_ref, k_ref, v_ref, cr_ref, gq_ref, gk_ref, o_ref, dm_ref, pdk_ref, pdv_ref, pdc_ref,
             dq_ref, dg_ref, dk_ref, dv_ref, dc_ref, dgq_ref, dgk_ref, dka, dva, dca):
        dgq = jnp.zeros((1, hd), F32)
        dgk = jnp.zeros((1, hd), F32)
        for hh in range(2):
            lanes = pl.ds(hh * hd, hd)
            k32, vjp_k = jax.vjp(_rms, k_ref[:, lanes], gk_ref[...])
            k = k32.astype(BF16)
            v = v_ref[:, lanes].astype(BF16)
            dka[...] = jnp.zeros(dka.shape, F32)
            dva[...] = jnp.zeros(dva.shape, F32)
            dca[...] = jnp.zeros(dca.shape, F32)
            for qi in range(T // tq):
                rows = pl.ds(qi * tq, tq)
                q32, vjp_q = jax.vjp(_rms, q_ref[rows, lanes], gq_ref[...])
                q = (q32 * scale).astype(BF16)
                ps, l = _fox_probs(q, k, cr_ref, hh, qi, tq)
                ps = [(a, b, p / l) for a, b, p in ps]
                sg = _sigmoid(g_ref[rows, lanes])
                dm = dm_ref[rows, lanes]
                do = (dm * sg).astype(BF16)
                dg_ref[rows, lanes] = (dm * o_ref[rows, lanes] * sg * (1.0 - sg)).astype(dg_ref.dtype)
                dps = [_dot(do, v[a:b], NT) for a, b, _ in ps]
                delta = functools.reduce(jnp.add, [jnp.sum(p * dp, axis=-1, keepdims=True) for (_, _, p), dp in zip(ps, dps)])
                dq = jnp.zeros((tq, hd), F32)
                for (a, b, p), dp in zip(ps, dps):
                    ds = p * (dp - delta)
                    dsb = ds.astype(BF16)
                    dq = dq + _dot(dsb, k[a:b], NN)
                    dka[:, pl.ds(a, b - a)] += _dot(q, dsb, TN)
                    dva[:, pl.ds(a, b - a)] += _dot(do, p.astype(BF16), TN)
                    dca[:, pl.ds(a, b - a)] -= jnp.sum(ds, axis=0, keepdims=True)
                dqr, dgq_p = vjp_q(dq * scale)
                dq_ref[rows, lanes] = dqr.astype(dq_ref.dtype)
                dgq = dgq + dgq_p
            dkr, dgk_p = vjp_k(dka[...].T)
            dgk = dgk + dgk_p
            dk_ref[:, lanes] = dkr + pdk_ref[:, lanes]
            dv_ref[:, lanes] = dva[...].T + pdv_ref[:, lanes]
            dc_ref[hh] = dca[...] + pdc_ref[hh]

        @pl.when(pl.program_id(0) == 0)
        def _():
            dgq_ref[...] = jnp.zeros(dgq_ref.shape, F32)
            dgk_ref[...] = jnp.zeros(dgk_ref.shape, F32)
        dgq_ref[...] += dgq
        dgk_ref[...] += dgk

    n, col, crs, gain = _fox_specs(T)
    wide = jax.ShapeDtypeStruct((T, MAIN_WIDTH), F32)
    half = jax.ShapeDtypeStruct((T, MAIN_WIDTH), BF16)
    outs, moved = _call(
        body, [proj, proj, kvf, kvf, cr, gq, gk, o, dmix, pdk, pdv, pdc],
        [half, half, wide, wide, jax.ShapeDtypeStruct((FOX_HEADS, 1, T), F32),
         jax.ShapeDtypeStruct((1, hd), F32), jax.ShapeDtypeStruct((1, hd), F32)],
        name=name, grid=(n,),
        in_specs=[col(0), col(n), col(0), col(n), crs, gain, gain, col(0), col(0), col(0), col(0), crs],
        out_specs=[col(0), col(0), col(0), col(0), crs, gain, gain],
        scratch=[pltpu.VMEM((hd, T), F32), pltpu.VMEM((hd, T), F32), pltpu.VMEM((1, T), F32)], comm=comm)
    return (*outs, moved)


def _mem_specs(T, width):
    tq = min(T, 512)
    q = pl.BlockSpec((tq, MEM_WIDTH), lambda i, c=(width - MEM_WIDTH) // MEM_WIDTH: (i, c))
    gain = pl.BlockSpec((1, MEM_HEAD_DIM), lambda i: (0, 0))
    return tq, q, gain


def _mem_fwd(name, proj, kv, gq, gk):
    T, W = proj.shape
    hd = MEM_HEAD_DIM
    tq, qspec, gain = _mem_specs(T, W)

    def body(q_ref, kv_ref, gq_ref, gk_ref, o_ref):
        for h in range(MEM_HEADS):
            lanes = pl.ds(h * hd, hd)
            q = _rms(q_ref[:, lanes], gq_ref[...]).astype(BF16)
            k = _rms(kv_ref[:, lanes], gk_ref[...]).astype(BF16)
            v = kv_ref[:, pl.ds(MEM_WIDTH + h * hd, hd)].astype(BF16)
            p, l = _softmax_rows(_dot(q, k, NT) * (hd ** -0.5))
            o_ref[:, lanes] = _dot(p.astype(BF16), v, NN) / l

    return pl.pallas_call(
        body, grid=(T // tq,),
        in_specs=[qspec, pl.BlockSpec(kv.shape, lambda i: (0, 0)), gain, gain],
        out_specs=pl.BlockSpec((tq, MEM_WIDTH), lambda i: (i, 0)),
        out_shape=jax.ShapeDtypeStruct((T, MEM_WIDTH), F32),
        name=name, compiler_params=_params(1))(proj, kv, gq, gk)


def _mem_bwd(name, proj, kv, gq, gk, dmix):
    T, W = proj.shape
    hd = MEM_HEAD_DIM
    scale = hd ** -0.5
    tq, qspec, gain = _mem_specs(T, W)

    def body(q_ref, kv_ref, gq_ref, gk_ref, dm_ref, dq_ref, dkv_ref, dgq_ref, dgk_ref):
        @pl.when(pl.program_id(0) == 0)
        def _():
            dkv_ref[...] = jnp.zeros(dkv_ref.shape, F32)
            dgq_ref[...] = jnp.zeros(dgq_ref.shape, F32)
            dgk_ref[...] = jnp.zeros(dgk_ref.shape, F32)
        for h in range(MEM_HEADS):
            lanes = pl.ds(h * hd, hd)
            vl = pl.ds(MEM_WIDTH + h * hd, hd)
            q32, vjp_q = jax.vjp(_rms, q_ref[:, lanes], gq_ref[...])
            k32, vjp_k = jax.vjp(_rms, kv_ref[:, lanes], gk_ref[...])
            q, k, v = q32.astype(BF16), k32.astype(BF16), kv_ref[:, vl].astype(BF16)
            p, l = _softmax_rows(_dot(q, k, NT) * scale)
            p = p / l
            do = dm_ref[:, lanes].astype(BF16)
            dp = _dot(do, v, NT)
            dsb = (p * (dp - jnp.sum(p * dp, axis=-1, keepdims=True))).astype(BF16)
            dqr, dgq_p = vjp_q(_dot(dsb, k, NN) * scale)
            dkr, dgk_p = vjp_k(_dot(dsb, q, TN) * scale)
            dq_ref[:, lanes] = dqr.astype(dq_ref.dtype)
            dkv_ref[:, lanes] += dkr
            dkv_ref[:, vl] += _dot(p.astype(BF16), do, TN)
            dgq_ref[...] += dgq_p
            dgk_ref[...] += dgk_p

    return pl.pallas_call(
        body, grid=(T // tq,),
        in_specs=[qspec, pl.BlockSpec(kv.shape, lambda i: (0, 0)), gain, gain,
                  pl.BlockSpec((tq, MEM_WIDTH), lambda i: (i, MAIN_WIDTH // MEM_WIDTH))],
        out_specs=[pl.BlockSpec((tq, MEM_WIDTH), lambda i: (i, 0)), pl.BlockSpec(kv.shape, lambda i: (0, 0)), gain, gain],
        out_shape=[jax.ShapeDtypeStruct((T, MEM_WIDTH), BF16), jax.ShapeDtypeStruct(kv.shape, F32),
                   jax.ShapeDtypeStruct((1, hd), F32), jax.ShapeDtypeStruct((1, hd), F32)],
        name=name, compiler_params=_params(1))(proj, kv, gq, gk, dmix)


def _cumsum_rows(name, x, reverse=False):
    T, C = x.shape
    pt = min(T, 256)

    def body(x_ref, o_ref):
        r = lax.broadcasted_iota(jnp.int32, (pt, pt), 0)
        c = lax.broadcasted_iota(jnp.int32, (pt, pt), 1)
        tri = ((c >= r) if reverse else (c <= r)).astype(F32)
        carry = jnp.zeros((1, C), F32)
        order = range(T // pt)
        for p in (reversed(order) if reverse else order):
            rows = pl.ds(p * pt, pt)
            blk = x_ref[rows, :]
            o_ref[rows, :] = _dot(tri, blk, NN, precision=lax.Precision.HIGHEST) + carry
            carry = carry + jnp.sum(blk, axis=0, keepdims=True)

    return pl.pallas_call(body, out_shape=jax.ShapeDtypeStruct((T, C), F32), name=name,
                          compiler_params=pltpu.CompilerParams(vmem_limit_bytes=V7X_VMEM_LIMIT))(x)


def _all_gather(name, xs):
    n = len(xs)

    def body(*refs):
        x_refs, out_refs = refs[:n], refs[n:2 * n]
        send_sems, recv_sems, local_sems = refs[2 * n:]
        mx, my, mc = _mesh_pos()
        me, sibling = (mx, my, mc), (mx, my, 1 - mc)
        chips = [(1 - mx, my), (mx, 1 - my), (1 - mx, 1 - my)]

        def slot(a, px, py, pc):
            return out_refs[a].at[4 * px + 2 * py + pc]

        def copy(a, k, block, to, src=None):
            return pltpu.make_async_remote_copy(
                src_ref=slot(a, *block) if src is None else src, dst_ref=slot(a, *block),
                send_sem=send_sems.at[7 * a + k], recv_sem=recv_sems.at[7 * a + k], device_id=to, device_id_type=MESH)

        mine = [pltpu.make_async_copy(x_refs[a], slot(a, *me), local_sems.at[a]) for a in range(n)]
        first = []
        for a in range(n):
            mine[a].start()
            first.append(copy(a, 0, me, sibling, src=x_refs[a]))
            first += [copy(a, 1 + j, me, (*chip, mc), src=x_refs[a]) for j, chip in enumerate(chips)]
        for cp in first:
            cp.start()
        passed = []
        for j, chip in enumerate(chips):
            for a in range(n):
                copy(a, 1 + j, (*chip, mc), me).wait_recv()
                passed.append(copy(a, 4 + j, (*chip, mc), sibling))
                passed[-1].start()
        for a in range(n):
            copy(a, 0, sibling, me).wait_recv()
            for j, chip in enumerate(chips):
                copy(a, 4 + j, (*chip, 1 - mc), me).wait_recv()
        for cp in first + passed:
            cp.wait_send()
        for cp in mine:
            cp.wait()

    return pl.pallas_call(
        body, out_shape=[jax.ShapeDtypeStruct((N_DEV,) + x.shape, x.dtype) for x in xs], in_specs=[ANY] * n, out_specs=[ANY] * n,
        scratch_shapes=[pltpu.SemaphoreType.DMA((7 * n,)), pltpu.SemaphoreType.DMA((7 * n,)), pltpu.SemaphoreType.DMA((n,))],
        name=name)(*xs)


def _row_tile(R, cap):
    best = None
    for t in range(8, min(R, cap) + 1, 8):
        if R % t == 0:
            best = t
    return best or R


def _sum_slabs(name, a, out_dtype):
    n, R, C = a.shape
    tm = _row_tile(R, 512)

    def body(*refs):
        acc = refs[0][...].astype(F32)
        for r in refs[1:n]:
            acc = acc + r[...].astype(F32)
        refs[n][...] = acc.astype(out_dtype)

    return pl.pallas_call(
        body, grid=(R // tm,),
        in_specs=[pl.BlockSpec((None, tm, C), lambda i, q=q: (q, i, 0)) for q in range(n)],
        out_specs=pl.BlockSpec((tm, C), lambda i: (i, 0)), out_shape=jax.ShapeDtypeStruct((R, C), out_dtype),
        name=name, compiler_params=_params(1))(*([a] * n))


SMALL = ["ffn1_norm", "mix_norm", "mem_norm", "mem_q_gain", "mem_k_gain", "hgrn_o_gain", "fox_q_gain", "kv_norm",
         "fox_f_bias", "fox_k_gain", "ffn2_norm"]
COLS352 = ["ffn1_w_gate", "ffn1_w_up", "ffn2_w_gate", "ffn2_w_up"]
KV_WIDTH = 2 * MAIN_WIDTH + FOX_HEADS


def _pad_cols(w, width):
    return jnp.pad(w, [(0, 0)] * (w.ndim - 1) + [(0, width - w.shape[-1])])


def _pad128(a):
    flat = a.reshape(-1)
    return jnp.pad(flat, (0, -flat.shape[0] % LANES))


def _small_pack(parts):
    flat = jnp.concatenate([_pad128(p) for p in parts])
    rows = -(-flat.shape[0] // LANES)
    flat = jnp.pad(flat, (0, (-rows % 8) * LANES))
    return flat.reshape(-1, LANES)


def _small_unpack(buf, shapes):
    flat = buf.reshape(-1)
    out, off = [], 0
    for s in shapes:
        n = 1
        for d in s:
            n *= d
        out.append(flat[off:off + n].reshape(s))
        off += n + (-n % LANES)
    return out


def _lb_fn(l0, l1):
    m = lax.stop_gradient(jnp.maximum(l0, l1))
    e0, e1 = jnp.exp(l0 - m), jnp.exp(l1 - m)
    p0, p1 = e0 / (e0 + e1), e1 / (e0 + e1)
    return p0 - p0, (p0 + p1) - p0


def _lb_fwd(logits):
    return _rowwise("lb", _lb_fn, [logits[0:1], logits[1:2]], [], [(MAIN_WIDTH, F32)] * 2)


def _lb_bwd(logits, dlb0, dlb1):
    def fn(l0, l1, d0, d1):
        _, vjp = jax.vjp(_lb_fn, l0, l1)
        return vjp((d0, d1))
    return _rowwise("lb_bwd", fn, [logits[0:1], logits[1:2], dlb0, dlb1], [], [(MAIN_WIDTH, F32)] * 2)


def _adamw_fn(w, g, m, v):
    m = ADAM_B1 * m + (1.0 - ADAM_B1) * g
    v = ADAM_B2 * v + (1.0 - ADAM_B2) * jnp.square(g)
    m_hat = m / (1.0 - ADAM_B1 ** ADAM_STEP)
    v_hat = v / (1.0 - ADAM_B2 ** ADAM_STEP)
    return -ADAM_LR * (m_hat / (jnp.sqrt(v_hat) + ADAM_EPS) + ADAM_WD * w), m, v


def _sum_adamw(name, landed, w, m, v, comm=None):
    L, r, c = w.shape
    tm = _row_tile(r, 128)
    n_i = r // tm

    def body(*refs):
        land, (w_ref, m_ref, v_ref), outs = refs[:L], refs[L:L + 3], refs[L + 3:]
        for k in range(L):
            @pl.when(pl.program_id(0) == k)
            def _(k=k):
                g = land[k][0].astype(F32)
                for s in range(1, N_DEV):
                    g = g + land[k][s].astype(F32)
                for ref, val in zip(outs, (g,) + _adamw_fn(w_ref[...], g, m_ref[...], v_ref[...])):
                    ref[...] = val

    held = lambda k: (lambda l, i: (0, jnp.where(l < k, 0, jnp.where(l == k, i, n_i - 1)), 0))
    cur = pl.BlockSpec((None, tm, c), lambda l, i: (l, i, 0))
    return _call(body, [*landed, w, m, v], [jax.ShapeDtypeStruct((L, r, c), F32)] * 4, name=name, grid=(L, n_i),
                 in_specs=[pl.BlockSpec((N_DEV, tm, c), held(k)) for k in range(L)] + [cur] * 3, out_specs=[cur] * 4, comm=comm)


def _adamw(name, w, g, m, v):
    shape = w.shape
    C = shape[-1]
    two = lambda a: a.reshape(-1, C)
    R = two(w).shape[0]
    outs = _rowwise(name, _adamw_fn, [two(w), two(g), two(m), two(v)], [], [(C, F32)] * 3, tm=_row_tile(R, 512))
    return [o.reshape(shape) for o in outs]


def _whole_rows(g, r0, r1):
    return g[:, r0:r1].reshape(N_DEV * (r1 - r0), g.shape[2])


SHARD_ROWS = D_MODEL // N_DEV


def _w_out_of(Wl):
    n = Wl["d2"].shape[1]
    return _whole_rows(Wl["r1"], n, n + SHARD_ROWS)


def _w_mem_kv_of(Wl):
    n = Wl["d2"].shape[1]
    return _whole_rows(Wl["r1"], n + SHARD_ROWS, n + 2 * SHARD_ROWS)[:, :2 * MEM_WIDTH]


def _mixer_fwd(l, x1, mem, G, W, lbs, shared, local, units, relays):
    T = x1.shape[0]
    tag = f"l{l}"
    Wl = G[l]
    h = _rms_fwd(tag + "_mixrms", x1, W["mix_norm"][l:l + 1])
    mem_n = _rms_fwd(tag + "_memrms", mem, W["mem_norm"][l:l + 1])
    kv = _mm(tag + "_memkv", [(mem_n, _w_mem_kv_of(Wl), NN)], [F32], mem.shape[0], 2 * MEM_WIDTH)
    proj, moved = _proj_cols(tag + "_in", h, Wl["win"], 0, comm=_Comm(relay=[G[n][k] for n, k in relays]))
    for (n, k), m in zip(relays, moved):
        G[n][k] = m
    along = _Comm(gather=[local[n][k] for n, k in units])
    if l < 2:
        main, o, moved = _hgrn_fwd(tag + "_hgrn", proj, lbs[l], W["hgrn_o_gain"][l:l + 1], comm=along)
    else:
        main, o, moved = _fox_fwd(tag + "_fox", proj, shared["kvf"], shared["cr"], W["fox_q_gain"][l - 2:l - 1],
                                  W["fox_k_gain"], comm=along)
    for (n, k), m in zip(units, moved):
        G[n][k] = m
    mem_o = _mem_fwd(tag + "_mem", proj, kv, W["mem_q_gain"][l:l + 1], W["mem_k_gain"][l:l + 1])
    w_out = _w_out_of(Wl)
    x2 = _mm(tag + "_out", [(main, w_out[:MAIN_WIDTH], NN), (mem_o, w_out[MAIN_WIDTH:], NN)], [F32], T, D_MODEL,
             epi=lambda a, e: (e[0] + a[0] + a[1],), extras=[x1])
    return x2, dict(h=h, mem_n=mem_n, kv=kv, proj=proj, main=main, o=o, mem_o=mem_o)


def _mixer_bwd(l, x1, mem, Wl, W, lbs, shared, sv, dx2, acc, ready, landed):
    T = x1.shape[0]
    tag = f"l{l}b"
    w_out = _w_out_of(Wl)
    g = {}
    dmix = _mm(tag + "_dmix", [(dx2, w_out, NT)], [F32], T, D_MODEL)
    dw_out = jnp.concatenate([
        _mm(tag + "_dwout_a", [(sv["main"], dx2, TN)], [BF16], MAIN_WIDTH, D_MODEL),
        _mm(tag + "_dwout_b", [(sv["mem_o"], dx2, TN)], [BF16], MEM_WIDTH, D_MODEL)], axis=0).reshape(N_DEV, -1, D_MODEL)
    dqm, dkv, g["mem_q_gain"], g["mem_k_gain"] = _mem_bwd(tag + "_mem", sv["proj"], sv["kv"], W["mem_q_gain"][l:l + 1],
                                                           W["mem_k_gain"][l:l + 1], dmix)
    along = _Comm(scatter=[v for _, v in ready])
    if l < 2:
        dq, df, di, dg, g["lb"], g["hgrn_o_gain"], moved = _hgrn_bwd(tag + "_hgrn", sv["proj"], sv["o"], dmix, lbs[l],
                                                                      W["hgrn_o_gain"][l:l + 1], comm=along)
        dproj = jnp.concatenate([dq, df, di, dg, dqm], axis=1)
    else:
        dq, dgate, acc["dk"], acc["dv"], acc["dc"], g["fox_q_gain"], g["fox_k_gain"], moved = _fox_bwd(
            tag + "_fox", sv["proj"], shared["kvf"], shared["cr"], W["fox_q_gain"][l - 2:l - 1], W["fox_k_gain"],
            sv["o"], dmix, acc["dk"], acc["dv"], acc["dc"], comm=along)
        dproj = jnp.concatenate([dq, dgate, dqm], axis=1)
    landed.update({k: m for (k, _), m in zip(ready, moved)})
    dh, dw_in, moved = _proj_cols_bwd(tag + "_in", sv["h"], dproj, Wl["win"], 0, comm=_Comm(scatter=[dw_out]))
    landed[(l, "w_out")] = moved[0]
    dx1, g["mix_norm"] = _rms_bwd(tag + "_mixrms", x1, W["mix_norm"][l:l + 1], dh, dres=dx2)
    dw_mem_kv = _mm(tag + "_dwmemkv", [(sv["mem_n"], dkv, TN)], [BF16], D_MODEL, 2 * MEM_WIDTH)
    dmem_n = _mm(tag + "_dmemn", [(dkv, _w_mem_kv_of(Wl), NT)], [F32], mem.shape[0], D_MODEL)
    _, g["mem_norm"] = _rms_bwd(tag + "_memrms", mem, W["mem_norm"][l:l + 1], dmem_n)
    return dx1, g, [((l, "w_in"), dw_in), ((l, "w_mem_kv"), dw_mem_kv.reshape(N_DEV, -1, 2 * MEM_WIDTH))]


def _forget_cols(kvf):
    return kvf[:, 2 * MAIN_WIDTH:2 * MAIN_WIDTH + LANES]


def _log_forget(kvf, bias):
    return _rowwise("kv_logf", lambda f, b: jax.nn.log_sigmoid(f + b), [_forget_cols(kvf)], [bias], [(LANES, F32)])[0]


def _step(x, mem, target, W, lb_logits, G0, local):
    T = x.shape[0]
    W = dict(W, fox_k_gain=W["fox_k_gain"].reshape(1, -1))
    lbs = _lb_fwd(lb_logits)
    fox_bias = jnp.pad(W["fox_f_bias"], (0, LANES - FOX_HEADS)).reshape(1, LANES)
    w_kv = W["w_kv"]
    n_l = len(local)
    ffn1 = lambda l, Wl: (W["ffn1_norm"][l:l + 1], Wl["gu1"], Wl["r1"], 0, 1, 0)
    ffn2 = lambda l, Wl: (W["ffn2_norm"][l:l + 1], Wl["gu2"], Wl["d2"], 0, 1, 0)

    on_ffn1 = {0: [(1, "gu1")], 1: [(2, "gu1")], 2: [(3, "gu1")], 3: [(3, "gu2")]}
    on_mix = {0: [(1, "r1"), (1, "win"), (1, "gu2")], 1: [(2, "r1"), (2, "win"), (2, "gu2")], 2: [(3, "r1"), (3, "win")], 3: []}
    on_ffn2 = {0: [(1, "d2")], 1: [(2, "d2")], 2: [(3, "d2")], 3: []}
    saved, shared, G = [], {}, [G0] + [{} for _ in range(n_l - 1)]
    for l in range(n_l):
        Wl = G[l]
        relay = on_ffn2[l - 1] if l else []
        along = _Comm(gather=[local[n][k] for n, k in on_ffn1[l]], relay=[G[n][k] for n, k in relay])
        x1, moved = _ffn_fwd(f"l{l}_ffn1", x, *ffn1(l, Wl), comm=along)
        for (n, k), m in zip(on_ffn1[l] + relay, moved):
            G[n][k] = m
        x2, sv = _mixer_fwd(l, x1, mem, G, W, lbs, shared, local, on_mix[l], on_ffn1[l])
        along = _Comm(gather=[local[n][k] for n, k in on_ffn2[l]], relay=[G[n][k] for n, k in on_mix[l]])
        x3, moved = _ffn_fwd(f"l{l}_ffn2", x2, *ffn2(l, G[l]), comm=along)
        for (n, k), m in zip(on_ffn2[l] + on_mix[l], moved):
            G[n][k] = m
        sv.update(x=x, x1=x1, x2=x2)
        saved.append(sv)
        x = x3
        if l == 1:
            hk = _rms_fwd("kv_rms", x, W["kv_norm"].reshape(1, -1))
            kvf = _mm("kv_proj", [(hk, w_kv, NN)], [F32], T, w_kv.shape[1])
            cum = _cumsum_rows("kv_cum", _log_forget(kvf, fox_bias))[:, :FOX_HEADS].T
            shared = dict(kvf=kvf, cr=cum[:, None, :], hk=hk, x=x)

    def loss_fn(y, t):
        err = y - t
        return err * (1.0 / D_MODEL), jnp.sum(0.5 / D_MODEL * err * err, axis=0, keepdims=True)
    dx, loss = _rowwise("loss", loss_fn, [x, target], [], [(D_MODEL, F32)], [((1, D_MODEL), F32)])

    grads = [None] * n_l
    acc = dict(dk=jnp.zeros((T, MAIN_WIDTH), F32), dv=jnp.zeros((T, MAIN_WIDTH), F32), dc=jnp.zeros((FOX_HEADS, 1, T), F32))
    gkv = {}
    landed, late = {}, []
    for l in reversed(range(n_l)):
        sv, Wl = saved[l], G[l]
        ready, more, extra = [], [], []
        if l == 1:
            dcum = jnp.pad(acc["dc"][:, 0, :].T, ((0, 0), (0, LANES - FOX_HEADS)))
            dlf = _cumsum_rows("kv_dcum", dcum, reverse=True)
            def dlogf_fn(d, f, b):
                p = d * _sigmoid(-(f + b))
                return p, jnp.sum(p, axis=0, keepdims=True)
            dfl, gkv["fox_f_bias"] = _rowwise("kv_dlogf", dlogf_fn, [dlf, _forget_cols(shared["kvf"])], [fox_bias],
                                              [(LANES, BF16)], [((1, LANES), F32)])
            dkvf = _pad_cols(jnp.concatenate([acc["dk"].astype(BF16), acc["dv"].astype(BF16), dfl], axis=1), w_kv.shape[1])
            dw_kv = _mm("kv_dw", [(shared["hk"], dkvf, TN)], [BF16], D_MODEL, dkvf.shape[1])
            extra.append(((0, "w_kv"), dw_kv[:, :KV_WIDTH].reshape(N_DEV, -1, KV_WIDTH)))
            dhk = _mm("kv_dh", [(dkvf, w_kv, NT)], [F32], T, D_MODEL)
            dx, gkv["kv_norm"] = _rms_bwd("kv_rmsb", shared["x"], W["kv_norm"].reshape(1, -1), dhk, dres=dx)
        g = {}
        if l < 2:
            ready, late = ready + late[1:], late[:1]
        else:
            more, extra, late = late[1:2], extra + late[2:], late[:1]
        dx2, g["ffn2_norm"], dwg, dwu, dwd, moved_a, moved_w = _ffn_bwd(
            f"l{l}b_ffn2", sv["x2"], *ffn2(l, Wl), dx, comm_a=_Comm(scatter=[v for _, v in late[:2]]),
            comm_w=_Comm(scatter=[v for _, v in late[2:]]))
        landed.update({k: m for (k, _), m in zip(late, moved_a + moved_w)})
        ready += [((l, "ffn2_w_gate"), dwg), ((l, "ffn2_w_up"), dwu), ((l, "ffn2_w_down"), dwd)]
        dx1, gm, rest = _mixer_bwd(l, sv["x1"], mem, Wl, W, lbs, shared, sv, dx2, acc, ready, landed)
        g.update(gm)
        rest = rest + more
        dx, g["ffn1_norm"], dwg, dwu, dwd, moved_a, moved_w = _ffn_bwd(
            f"l{l}b_ffn1", sv["x"], *ffn1(l, Wl), dx1, comm_a=_Comm(scatter=[v for _, v in rest]),
            comm_w=_Comm(scatter=[v for _, v in extra]))
        landed.update({k: m for (k, _), m in zip(rest + extra, moved_a + moved_w)})
        late = [((l, "ffn1_w_gate"), dwg), ((l, "ffn1_w_up"), dwu), ((l, "ffn1_w_down"), dwd)]
        grads[l] = g

    out = {}
    for n in ["ffn1_norm", "mix_norm", "mem_norm", "mem_q_gain", "mem_k_gain", "ffn2_norm"]:
        out[n] = jnp.concatenate([grads[l][n] for l in range(4)], axis=0)
    out["hgrn_o_gain"] = jnp.concatenate([grads[l]["hgrn_o_gain"] for l in (0, 1)], axis=0)
    out["fox_q_gain"] = jnp.concatenate([grads[l]["fox_q_gain"] for l in (2, 3)], axis=0)
    out["fox_k_gain"] = (grads[2]["fox_k_gain"] + grads[3]["fox_k_gain"]).reshape(-1)
    out["kv_norm"] = gkv["kv_norm"].reshape(-1)
    out["fox_f_bias"] = gkv["fox_f_bias"][0, :FOX_HEADS]
    dl0, dl1 = _lb_bwd(lb_logits, grads[0]["lb"], grads[1]["lb"])
    out["hgrn_lb_logits"] = jnp.concatenate([dl0, dl1], axis=0)
    return loss, dx, out, landed, late


WEIGHTS = ["ffn1_norm", "ffn1_w_gate", "ffn1_w_up", "ffn1_w_down", "mix_norm", "mem_norm", "w_mem_kv", "mem_q_gain",
           "mem_k_gain", "w_in_a", "hgrn_lb_logits", "hgrn_o_gain", "w_in_b", "fox_q_gain", "kv_norm", "w_kv", "fox_f_bias",
           "fox_k_gain", "w_out", "ffn2_norm", "ffn2_w_gate", "ffn2_w_up", "ffn2_w_down"]
BIG = COLS352 + ["w_in_a", "w_in_b", "ffn1_w_down", "ffn2_w_down", "w_out", "w_mem_kv", "w_kv"]


def _train_step(a):
    bf = lambda w: w.astype(BF16)
    n_l = a["w_out"].shape[0]
    local = []
    for l in range(n_l):
        w_in = a["w_in_a"][l] if l < a["w_in_a"].shape[0] else a["w_in_b"][l - a["w_in_a"].shape[0]]
        local.append(dict(
            gu1=bf(jnp.concatenate([a["ffn1_w_gate"][l], a["ffn1_w_up"][l]], axis=0)),
            r1=bf(jnp.concatenate([a["ffn1_w_down"][l], a["w_out"][l], _pad_cols(a["w_mem_kv"][l], D_MODEL)], axis=0)),
            win=bf(w_in),
            gu2=bf(jnp.concatenate([a["ffn2_w_gate"][l], a["ffn2_w_up"][l]], axis=0)),
            d2=bf(a["ffn2_w_down"][l])))
    keys = ["gu1", "r1", "win", "gu2", "d2"]
    first = _all_gather("ag_first", [local[0][k] for k in keys] + [bf(a["w_kv"]), _small_pack([a["hgrn_lb_logits"]])])
    G0 = dict(zip(keys, first))
    W = {n: a[n] for n in SMALL}
    W["w_kv"] = _pad_cols(first[5].reshape(-1, a["w_kv"].shape[1]), 2 * D_MODEL)
    lb_shape = a["hgrn_lb_logits"].shape
    lb_all = first[6].reshape(N_DEV, -1)[:, :lb_shape[0] * lb_shape[1]]
    lb_logits = lb_all.reshape((N_DEV,) + lb_shape).transpose(1, 0, 2).reshape(lb_shape[0], -1)

    loss_part, dx, g, landed, tail = _step(a["x"][0], a["mem"][0], a["loss_target"][0], W, lb_logits, G0, local)

    n_a = a["w_in_a"].shape[0]
    grad, delta, new_m, new_v = {}, {}, {}, {}
    halves = lambda t: (t[:, :t.shape[1] // 2], t[:, t.shape[1] // 2:])
    tail_keys = [k for k, _ in tail]
    tail = [((k, i), h) for k, v in tail for i, h in enumerate(halves(v))]
    order = [n for n in BIG if n.startswith("ffn2")] + [n for n in BIG if not n.startswith("ffn")]
    for n in order + [n for n in BIG if n.startswith("ffn1")]:
        ls = range(n_a) if n == "w_in_a" else range(n_a, n_l) if n == "w_in_b" else range(1) if n == "w_kv" else range(n_l)
        key = "w_in" if n.startswith("w_in") else n
        lead = (lambda t: t[None]) if a[n].ndim == 2 else (lambda t: t)
        if n.startswith("ffn1") and tail_keys:
            assert not tail
            landed.update({k: jnp.concatenate([landed[(k, 0)], landed[(k, 1)]], axis=1) for k in tail_keys})
            tail_keys = []
        riding, tail = tail[:1], tail[1:]
        res, moved = _sum_adamw("adam_" + n, [landed[(l, key)] for l in ls], lead(a[n]), lead(a["m_" + n]), lead(a["v_" + n]),
                                comm=_Comm(scatter=[v for _, v in riding]))
        landed.update({k: m for (k, _), m in zip(riding, moved)})
        grad[n], delta[n], new_m[n], new_v[n] = [t.reshape(a[n].shape) for t in res]

    zeros = [jnp.zeros(lb_logits.shape, F32), jnp.zeros(loss_part.shape, F32)]
    small_shapes = [a[n].shape for n in SMALL] + [lb_logits.shape, loss_part.shape]
    small_part = _small_pack([g[n] for n in SMALL] + [g["hgrn_lb_logits"], loss_part])
    small_sum = _sum_slabs("small_sum", _all_gather("ag_small", [small_part])[0], F32)
    small = _small_unpack(small_sum, small_shapes)
    grad.update(dict(zip(SMALL, small)))
    loss = jnp.sum(small[-1])
    me = 4 * lax.axis_index("x") + 2 * lax.axis_index("y") + lax.axis_index("c")
    grad["hgrn_lb_logits"] = lax.dynamic_slice_in_dim(small[-2], me * lb_shape[1], lb_shape[1], axis=1)

    n = "hgrn_lb_logits"
    delta[n], new_m[n], new_v[n] = _adamw("adam_" + n, a[n], grad[n], a["m_" + n], a["v_" + n])
    packs = [_small_pack([a[p + n] for n in SMALL] + zeros) for p in ("", "m_", "v_")]
    upd = _rowwise("adam_small", _adamw_fn, [packs[0], small_sum, packs[1], packs[2]], [], [(LANES, F32)] * 3, tm=packs[0].shape[0])
    for d, u in zip((delta, new_m, new_v), upd):
        d.update(dict(zip(SMALL, _small_unpack(u, small_shapes))))
    return (loss, dx[None], *[grad[n] for n in WEIGHTS], *[delta[n] for n in WEIGHTS], *[new_m[n] for n in WEIGHTS],
            *[new_v[n] for n in WEIGHTS])


def kernel(x, mem, ffn1_norm, ffn1_w_gate, ffn1_w_up, ffn1_w_down, mix_norm, mem_norm, w_mem_kv, mem_q_gain, mem_k_gain, w_in_a, hgrn_lb_logits, hgrn_o_gain, w_in_b, fox_q_gain, kv_norm, w_kv, fox_f_bias, fox_k_gain, w_out, ffn2_norm, ffn2_w_gate, ffn2_w_up, ffn2_w_down, loss_target, m_ffn1_norm, m_ffn1_w_gate, m_ffn1_w_up, m_ffn1_w_down, m_mix_norm, m_mem_norm, m_w_mem_kv, m_mem_q_gain, m_mem_k_gain, m_w_in_a, m_hgrn_lb_logits, m_hgrn_o_gain, m_w_in_b, m_fox_q_gain, m_kv_norm, m_w_kv, m_fox_f_bias, m_fox_k_gain, m_w_out, m_ffn2_norm, m_ffn2_w_gate, m_ffn2_w_up, m_ffn2_w_down, v_ffn1_norm, v_ffn1_w_gate, v_ffn1_w_up, v_ffn1_w_down, v_mix_norm, v_mem_norm, v_w_mem_kv, v_mem_q_gain, v_mem_k_gain, v_w_in_a, v_hgrn_lb_logits, v_hgrn_o_gain, v_w_in_b, v_fox_q_gain, v_kv_norm, v_w_kv, v_fox_f_bias, v_fox_k_gain, v_w_out, v_ffn2_norm, v_ffn2_w_gate, v_ffn2_w_up, v_ffn2_w_down):
    return _train_step(dict(locals()))
```

```python
import functools

import jax
import jax.numpy as jnp
from jax import lax
from jax.experimental import pallas as pl
from jax.experimental.pallas import tpu as pltpu

F32, BF16 = jnp.float32, jnp.bfloat16
EPS = 1e-6
V7X_VMEM_LIMIT = 56 * 1024 * 1024
LANES = 128
N_DEV = 8

D_MODEL = 1024
MAIN_WIDTH = 768
MEM_WIDTH = 256
HG_HEAD_DIM = 128
HG_HEADS = 6
FOX_HEAD_DIM = 64
FOX_HEADS = 12
MEM_HEADS = 4
MEM_HEAD_DIM = 64
HG_BLOCK = 16

ADAM_LR, ADAM_B1, ADAM_B2, ADAM_EPS, ADAM_WD, ADAM_STEP = 0.001, 0.9, 0.999, 1e-08, 0.01, 10

NN = ((1,), (0,))
NT = ((1,), (1,))
TN = ((0,), (0,))


def _dot(a, b, dims, precision=None):
    return lax.dot_general(a, b, (dims, ((), ())), preferred_element_type=F32, precision=precision)


def _bdot(a, b, dims):
    return _dot(a.astype(BF16), b.astype(BF16), dims)


def _split(a):
    hi = a.astype(BF16)
    return hi, (a - hi.astype(F32)).astype(BF16)


def _fdot(a, b, dims):
    ah, al = _split(a)
    bh, bl = _split(b)
    return _dot(ah, bh, dims) + (_dot(ah, bl, dims) + _dot(al, bh, dims))


def _params(n_grid):
    return pltpu.CompilerParams(dimension_semantics=("arbitrary",) * n_grid, vmem_limit_bytes=V7X_VMEM_LIMIT)


def _rms(x, g):
    return x * lax.rsqrt(jnp.mean(x * x, axis=-1, keepdims=True) + EPS) * g


def _sigmoid(x):
    return jax.nn.sigmoid(x)


def _silu(x):
    return x * jax.nn.sigmoid(x)


MESH = pl.DeviceIdType.MESH
ANY = pl.BlockSpec(memory_space=pl.ANY)


def _mesh_pos():
    return lax.axis_index("x"), lax.axis_index("y"), lax.axis_index("c")


class _Comm:
    def __init__(self, gather=(), relay=(), scatter=()):
        self.gather, self.relay, self.scatter = list(gather), list(relay), list(scatter)
        self.arrays = self.gather + self.relay + self.scatter
        self.n_remote = 4 * len(self.gather) + 3 * len(self.relay) + 7 * len(self.scatter)
        self.n_local = len(self.gather) + len(self.scatter)

    def out_shapes(self):
        return ([jax.ShapeDtypeStruct((N_DEV,) + x.shape, x.dtype) for x in self.gather]
                + [jax.ShapeDtypeStruct(g.shape, g.dtype) for g in self.relay + self.scatter])

    def scratch(self):
        return [pltpu.SemaphoreType.DMA((self.n_remote,)), pltpu.SemaphoreType.DMA((self.n_remote,)),
                pltpu.SemaphoreType.DMA((max(self.n_local, 1),))]

    def _copies(self, ins, outs, send, recv, local, arrivals=True):
        mx, my, mc = _mesh_pos()
        flip = lambda v, f: 1 - v if f else v
        idx = lambda p: 4 * p[0] + 2 * p[1] + p[2]
        me = (mx, my, mc)
        count = [0, 0]
        loc, out, arrive = [], [], []

        def pair(src, dst, lands, to):
            k = count[0]
            count[0] += 1
            mk = lambda d: pltpu.make_async_remote_copy(src_ref=src, dst_ref=d, send_sem=send.at[k], recv_sem=recv.at[k],
                                                        device_id=to, device_id_type=MESH)
            out.append(mk(dst))
            if arrivals:
                arrive.append(mk(lands))

        def local_copy(src, dst):
            loc.append(pltpu.make_async_copy(src, dst, local.at[count[1]]))
            count[1] += 1

        refs = list(zip(ins, outs))
        near = [(0, 0, 1), (1, 0, 0), (0, 1, 0), (1, 1, 0)]
        for x, G in refs[:len(self.gather)]:
            local_copy(x, G.at[idx(me)])
            for f in near:
                peer = tuple(flip(v, b) for v, b in zip(me, f))
                pair(x, G.at[idx(me)], G.at[idx(peer)], peer)
        sibling = (mx, my, 1 - mc)
        for Gin, Gout in refs[len(self.gather):len(self.gather) + len(self.relay)]:
            for f in near[1:]:
                chip = (flip(mx, f[0]), flip(my, f[1]))
                pair(Gin.at[idx((*chip, mc))], Gout.at[idx((*chip, mc))], Gout.at[idx((*chip, 1 - mc))], sibling)
        every = near + [(1, 0, 1), (0, 1, 1), (1, 1, 1)]
        for g, R in refs[len(self.gather) + len(self.relay):]:
            local_copy(g.at[idx(me)], R.at[idx(me)])
            for f in every:
                peer = tuple(flip(v, b) for v, b in zip(me, f))
                pair(g.at[idx(peer)], R.at[idx(me)], R.at[idx(peer)], peer)
        return loc, out, arrive

    def start(self, ins, outs, send, recv, local):
        loc, out, _ = self._copies(ins, outs, send, recv, local, arrivals=False)
        for cp in loc + out:
            cp.start()

    def finish(self, ins, outs, send, recv, local):
        loc, out, arrive = self._copies(ins, outs, send, recv, local)
        for cp in arrive:
            cp.wait_recv()
        for cp in out:
            cp.wait_send()
        for cp in loc:
            cp.wait()


def _call(body, operands, out_shape, *, name, grid=(), in_specs=None, out_specs=None, scratch=(), comm=None):
    outs = list(out_shape) if isinstance(out_shape, (list, tuple)) else [out_shape]
    single = not isinstance(out_shape, (list, tuple))
    params = _params(len(grid))
    if comm is None or not comm.arrays:
        res = pl.pallas_call(body, grid=grid, in_specs=in_specs, out_specs=out_specs, out_shape=out_shape,
                             scratch_shapes=list(scratch), name=name, compiler_params=params)(*operands)
        return ([res] if single else list(res)), []
    n_in, n_out, n_s, n_c = len(operands), len(outs), len(scratch), len(comm.arrays)

    def wrapped(*refs):
        pos = [0]

        def take(n):
            pos[0] += n
            return refs[pos[0] - n:pos[0]]

        b_in, c_in, b_out, c_out, b_s, sems = take(n_in), take(n_c), take(n_out), take(n_c), take(n_s), take(3)
        ids = [pl.program_id(d) for d in range(len(grid))]
        first, last = True, True
        for d, i in enumerate(ids):
            first = (i == 0) & first
            last = (i == grid[d] - 1) & last
        if grid:
            pl.when(first)(lambda: comm.start(c_in, c_out, *sems))
        else:
            comm.start(c_in, c_out, *sems)
        body(*b_in, *b_out, *b_s)
        if grid:
            pl.when(last)(lambda: comm.finish(c_in, c_out, *sems))
        else:
            comm.finish(c_in, c_out, *sems)

    n_g = len(comm.gather)
    aliases = {n_in + n_g + r: n_out + n_g + r for r in range(len(comm.relay))}
    out_specs_l = list(out_specs) if isinstance(out_specs, (list, tuple)) else [out_specs]
    res = pl.pallas_call(
        wrapped, grid=grid, in_specs=list(in_specs) + [ANY] * n_c, out_specs=out_specs_l + [ANY] * n_c,
        out_shape=outs + comm.out_shapes(), scratch_shapes=list(scratch) + comm.scratch(), input_output_aliases=aliases,
        name=name, compiler_params=params)(*operands, *comm.arrays)
    return list(res[:n_out]), list(res[n_out:])


def _rowwise(name, fn, rows, consts, out_rows, out_reds=(), tm=512):
    R = rows[0].shape[0]
    tm = min(tm, R)
    assert R % tm == 0
    n_in, n_o = len(rows) + len(consts), len(out_rows)

    def body(*refs):
        outs = fn(*[r[...] for r in refs[:n_in]])
        if not isinstance(outs, (tuple, list)):
            outs = (outs,)
        for r, o in zip(refs[n_in:n_in + n_o], outs[:n_o]):
            r[...] = o.astype(r.dtype)
        red_refs = refs[n_in + n_o:]
        if red_refs:
            @pl.when(pl.program_id(0) == 0)
            def _():
                for r in red_refs:
                    r[...] = jnp.zeros(r.shape, r.dtype)
            for r, o in zip(red_refs, outs[n_o:]):
                r[...] += o

    zero = lambda n: (lambda i: (0,) * n)
    in_specs = [pl.BlockSpec((tm, a.shape[1]), lambda i: (i, 0)) for a in rows]
    in_specs += [pl.BlockSpec(c.shape, zero(c.ndim)) for c in consts]
    out_specs = [pl.BlockSpec((tm, c), lambda i: (i, 0)) for c, _ in out_rows]
    out_specs += [pl.BlockSpec(s, zero(len(s))) for s, _ in out_reds]
    out_shape = [jax.ShapeDtypeStruct((R, c), dt) for c, dt in out_rows]
    out_shape += [jax.ShapeDtypeStruct(s, dt) for s, dt in out_reds]
    return pl.pallas_call(body, grid=(R // tm,), in_specs=in_specs, out_specs=out_specs, out_shape=out_shape,
                          name=name, compiler_params=_params(1))(*rows, *consts)


def _tile(n, cap):
    best = None
    for t in range(LANES, min(n, cap) + 1, LANES):
        if n % t == 0:
            best = t
    return best or n


def _mm(name, pairs, out_dtypes, M, N, epi=None, extras=(), tm=512, tn=512):
    tm, tn = _tile(M, tm), _tile(N, tn)
    n_p, n_e = len(pairs), len(extras)
    modes = [m for _, _, m in pairs]

    def body(*refs):
        accs = [_bdot(refs[2 * k][...], refs[2 * k + 1][...], modes[k]) for k in range(n_p)]
        ex = [r[...] for r in refs[2 * n_p:2 * n_p + n_e]]
        outs = epi(accs, ex) if epi is not None else accs
        for r, o in zip(refs[2 * n_p + n_e:], outs):
            r[...] = o.astype(r.dtype)

    in_specs = []
    ops = []
    for a, b, mode in pairs:
        if mode == NN:
            K = a.shape[1]
            assert a.shape == (M, K) and b.shape == (K, N), (name, a.shape, b.shape)
            in_specs += [pl.BlockSpec((tm, K), lambda i, j: (i, 0)), pl.BlockSpec((K, tn), lambda i, j: (0, j))]
        elif mode == NT:
            K = a.shape[1]
            assert a.shape == (M, K) and b.shape == (N, K), (name, a.shape, b.shape)
            in_specs += [pl.BlockSpec((tm, K), lambda i, j: (i, 0)), pl.BlockSpec((tn, K), lambda i, j: (j, 0))]
        else:
            K = a.shape[0]
            assert a.shape == (K, M) and b.shape == (K, N), (name, a.shape, b.shape)
            in_specs += [pl.BlockSpec((K, tm), lambda i, j: (0, i)), pl.BlockSpec((K, tn), lambda i, j: (0, j))]
        ops += [a, b]
    in_specs += [pl.BlockSpec((tm, tn), lambda i, j: (i, j)) for _ in extras]
    out_specs = [pl.BlockSpec((tm, tn), lambda i, j: (i, j)) for _ in out_dtypes]
    out_shape = [jax.ShapeDtypeStruct((M, N), dt) for dt in out_dtypes]
    res = pl.pallas_call(body, grid=(M // tm, N // tn), in_specs=in_specs, out_specs=out_specs, out_shape=out_shape,
                         name=name, compiler_params=_params(2))(*ops, *extras)
    return res[0] if len(res) == 1 else res


def _rms_fwd(name, x, gain, dtype=BF16):
    return _rowwise(name, _rms, [x], [gain], [(x.shape[1], dtype)])[0]


def _rms_bwd(name, x, gain, dh, dres=None):
    def fn(x, dh, *rest):
        g = rest[-1]
        _, vjp = jax.vjp(_rms, x, g)
        dx, dg = vjp(dh)
        if dres is not None:
            dx = dx + rest[0]
        return dx, dg
    rows = [x, dh] + ([dres] if dres is not None else [])
    d = x.shape[1]
    return _rowwise(name, fn, rows, [gain], [(d, F32)], [((1, d), F32)])


def _ffn_specs(gcols, grows, ig, iu, idn):
    n = gcols.shape[2]
    D = grows.shape[2]
    wg = pl.BlockSpec((None, D, n), lambda i, j: (j, ig, 0))
    wu = pl.BlockSpec((None, D, n), lambda i, j: (j, iu, 0))
    wd = pl.BlockSpec((None, n, D), lambda i, j: (j, idn, 0))
    return n, wg, wu, wd


def _ffn_fwd(name, x, gain, gcols, grows, ig, iu, idn, tm=1024, comm=None):
    T, D = x.shape
    tm = min(T, tm)
    n, wg_s, wu_s, wd_s = _ffn_specs(gcols, grows, ig, iu, idn)
    last = N_DEV - 1

    def body(x_ref, g_ref, wg_ref, wu_ref, wd_ref, y_ref, h_s, acc):
        j = pl.program_id(1)

        @pl.when(j == 0)
        def _():
            h_s[...] = _rms(x_ref[...], g_ref[...]).astype(BF16)
            acc[...] = jnp.zeros(acc.shape, F32)
        h = h_s[...]
        z = _silu(_dot(h, wg_ref[...], NN)) * _dot(h, wu_ref[...], NN)
        acc[...] += _dot(z.astype(BF16), wd_ref[...], NN)

        @pl.when(j == last)
        def _():
            y_ref[...] = x_ref[...] + 0.5 * acc[...]

    row = pl.BlockSpec((tm, D), lambda i, j: (i, 0))
    (y,), moved = _call(
        body, [x, gain, gcols, gcols, grows], [jax.ShapeDtypeStruct((T, D), F32)], name=name, grid=(T // tm, N_DEV),
        in_specs=[row, pl.BlockSpec((1, D), lambda i, j: (0, 0)), wg_s, wu_s, wd_s], out_specs=[row],
        scratch=[pltpu.VMEM((tm, D), BF16), pltpu.VMEM((tm, D), F32)], comm=comm)
    return y, moved


def _ffn_bwd(tag, x, gain, gcols, grows, ig, iu, idn, dy, tm=512, comm_a=None, comm_w=None):
    T, D = x.shape
    tm = min(T, tm)
    n, wg_s, wu_s, wd_s = _ffn_specs(gcols, grows, ig, iu, idn)
    last = N_DEV - 1

    def body(x_ref, dy_ref, g_ref, wg_ref, wu_ref, wd_ref, dx_ref, dg_ref, h_ref, z_ref, da_ref, db_ref, dh_acc):
        i, j = pl.program_id(0), pl.program_id(1)

        @pl.when(j == 0)
        def _():
            h_ref[...] = _rms(x_ref[...], g_ref[...]).astype(BF16)
            dh_acc[...] = jnp.zeros(dh_acc.shape, F32)

        @pl.when((i == 0) & (j == 0))
        def _():
            dg_ref[...] = jnp.zeros(dg_ref.shape, F32)
        h = h_ref[...]
        a, b = _dot(h, wg_ref[...], NN), _dot(h, wu_ref[...], NN)
        dz = 0.5 * _dot(dy_ref[...].astype(BF16), wd_ref[...], NT)
        s = _sigmoid(a)
        si = a * s
        da = (dz * b * (s + si * (1.0 - s))).astype(BF16)
        db = (dz * si).astype(BF16)
        z_ref[...] = (si * b).astype(BF16)
        da_ref[...] = da
        db_ref[...] = db
        dh_acc[...] += _dot(da, wg_ref[...], NT) + _dot(db, wu_ref[...], NT)

        @pl.when(j == last)
        def _():
            _, vjp = jax.vjp(_rms, x_ref[...], g_ref[...])
            dx, dg = vjp(dh_acc[...])
            dx_ref[...] = dx + dy_ref[...]
            dg_ref[...] += dg

    row = pl.BlockSpec((tm, D), lambda i, j: (i, 0))
    vec = pl.BlockSpec((1, D), lambda i, j: (0, 0))
    hid = pl.BlockSpec((None, tm, n), lambda i, j: (j, i, 0))
    hidden = jax.ShapeDtypeStruct((N_DEV, T, n), BF16)
    (dx, dgain, h, z, da, db), moved_a = _call(
        body, [x, dy, gain, gcols, gcols, grows],
        [jax.ShapeDtypeStruct((T, D), F32), jax.ShapeDtypeStruct((1, D), F32), jax.ShapeDtypeStruct((T, D), BF16),
         hidden, hidden, hidden],
        name=tag + "_a", grid=(T // tm, N_DEV), in_specs=[row, row, vec, wg_s, wu_s, wd_s],
        out_specs=[row, vec, row, hid, hid, hid], scratch=[pltpu.VMEM((tm, D), F32)], comm=comm_a)

    def wbody(h_ref, dy_ref, z_ref, da_ref, db_ref, dwg_ref, dwu_ref, dwd_ref):
        h = h_ref[...]
        dwg_ref[...] = _dot(h, da_ref[...], TN).astype(BF16)
        dwu_ref[...] = _dot(h, db_ref[...], TN).astype(BF16)
        dwd_ref[...] = (0.5 * _dot(z_ref[...], dy_ref[...].astype(BF16), TN)).astype(BF16)

    full = pl.BlockSpec((T, D), lambda j: (0, 0))
    hid_all = pl.BlockSpec((None, T, n), lambda j: (j, 0, 0))
    (dwg, dwu, dwd), moved_w = _call(
        wbody, [h, dy, z, da, db],
        [jax.ShapeDtypeStruct((N_DEV, D, n), BF16)] * 2 + [jax.ShapeDtypeStruct((N_DEV, n, D), BF16)],
        name=tag + "_w", grid=(N_DEV,), in_specs=[full, full, hid_all, hid_all, hid_all],
        out_specs=[pl.BlockSpec((None, D, n), lambda j: (j, 0, 0))] * 2 + [pl.BlockSpec((None, n, D), lambda j: (j, 0, 0))],
        comm=comm_w)
    return dx, dgain, dwg, dwu, dwd, moved_a, moved_w


def _wcols_spec(gw, l, grid_rank):
    _, _, n = gw.shape
    K = D_MODEL
    zero = (lambda i: (0, l, 0)) if grid_rank == 1 else (lambda i, j: (0, l, 0))
    return n, K, pl.BlockSpec((N_DEV, K, n), zero)


def _proj_cols(name, h, gw, l, tm=512, comm=None):
    T = h.shape[0]
    tm = min(T, tm)
    n, K, wspec = _wcols_spec(gw, l, 1)

    def body(h_ref, w_ref, o_ref):
        h = h_ref[...]
        for j in range(N_DEV):
            o_ref[:, pl.ds(j * n, n)] = _dot(h, w_ref[j], NN)

    (proj,), moved = _call(
        body, [h, gw], [jax.ShapeDtypeStruct((T, N_DEV * n), F32)], name=name, grid=(T // tm,),
        in_specs=[pl.BlockSpec((tm, K), lambda i: (i, 0)), wspec], out_specs=[pl.BlockSpec((tm, N_DEV * n), lambda i: (i, 0))],
        comm=comm)
    return proj, moved


def _proj_cols_bwd(tag, h, dproj, gw, l, tm=512, tk=512, comm=None):
    T = h.shape[0]
    tm = min(T, tm)
    n, K, wspec = _wcols_spec(gw, l, 1)

    def dh_body(dp_ref, w_ref, o_ref):
        acc = jnp.zeros(o_ref.shape, F32)
        for j in range(N_DEV):
            acc = acc + _dot(dp_ref[:, pl.ds(j * n, n)], w_ref[j], NT)
        o_ref[...] = acc

    (dh,), moved = _call(
        dh_body, [dproj, gw], [jax.ShapeDtypeStruct((T, K), F32)], name=tag + "_dh", grid=(T // tm,),
        in_specs=[pl.BlockSpec((tm, N_DEV * n), lambda i: (i, 0)), wspec], out_specs=[pl.BlockSpec((tm, K), lambda i: (i, 0))],
        comm=comm)

    def dw_body(h_ref, dp_ref, o_ref):
        h = h_ref[...]
        for j in range(N_DEV):
            o_ref[j] = _dot(h, dp_ref[:, pl.ds(j * n, n)], TN).astype(BF16)

    dw = pl.pallas_call(
        dw_body, grid=(K // tk,), in_specs=[pl.BlockSpec((T, tk), lambda i: (0, i)), pl.BlockSpec((T, N_DEV * n), lambda i: (0, 0))],
        out_specs=pl.BlockSpec((N_DEV, tk, n), lambda i: (0, i, 0)), out_shape=jax.ShapeDtypeStruct((N_DEV, K, n), BF16),
        name=tag + "_dw", compiler_params=_params(1))(h, dproj)
    return dh, dw, moved


def _block_tri(n, reverse=False):
    r = lax.broadcasted_iota(jnp.int32, (n, n), 0)
    c = lax.broadcasted_iota(jnp.int32, (n, n), 1)
    same = (r // HG_BLOCK) == (c // HG_BLOCK)
    return (same & ((c >= r) if reverse else (c <= r))).astype(F32)


def _hgrn_prep(q_ref, f_ref, lbv, qs, ks, cs, T):
    pt = min(T, 256)
    tri = _block_tri(pt)
    for p in range(T // pt):
        rows = pl.ds(p * pt, pt)
        f = lbv + (1.0 - lbv) * _sigmoid(f_ref[rows, :])
        qs[rows, :] = _silu(q_ref[rows, :])
        ks[rows, :] = 1.0 - f
        cs[rows, :] = _dot(tri, jnp.log(f), NN, precision=lax.Precision.HIGHEST)


HG_GROUP = 128


def _groups_loop(nb, fn):
    gp = HG_GROUP if nb % HG_GROUP == 0 else nb

    def step(i, carry):
        base = pl.multiple_of(i * (gp * HG_BLOCK), gp * HG_BLOCK)
        fn(lambda t: pl.ds(base + t, gp, stride=HG_BLOCK))
        return carry

    lax.fori_loop(0, nb // gp, step, 0)


def _gate_out(o, og, g):
    return _rms(o, og) * _silu(g)


HG_UNROLL = 16


def _block_rows(n):
    return pl.ds(pl.multiple_of(n * HG_BLOCK, HG_BLOCK), HG_BLOCK)


def _blocks_loop(nb, fn):
    u = HG_UNROLL if nb % HG_UNROLL == 0 else 1

    def step(i, carry):
        for k in range(u):
            fn(i * u + k)
        return carry

    lax.fori_loop(0, nb // u, step, 0)


def _scan_states(buf, cs, nb, reverse=False):
    def step(m, st):
        n = nb - 1 - m if reverse else m
        own = buf[n]
        buf[n] = st
        rows = _block_rows(n)
        return jnp.exp(cs[rows, :][HG_BLOCK - 1:HG_BLOCK, :]) * st + own

    lax.fori_loop(0, nb, step, jnp.zeros(buf.shape[1:], F32))


def _hgrn_states(i_ref, ks, cs, states, nb):
    def own_step(n):
        rows = _block_rows(n)
        c = cs[rows, :]
        states[n] = _fdot(i_ref[rows, :], ks[rows, :] * jnp.exp(c[HG_BLOCK - 1:HG_BLOCK, :] - c), TN)

    _blocks_loop(nb, own_step)
    _scan_states(states, cs, nb)


def _hgrn_fwd(name, proj, lb, og, comm=None):
    T = proj.shape[0]
    nb = T // HG_BLOCK
    hd = HG_HEAD_DIM

    def body(q_ref, f_ref, i_ref, g_ref, lb_ref, og_ref, main_ref, o_ref, qs, ks, cs, states):
        _hgrn_prep(q_ref, f_ref, lb_ref[...], qs, ks, cs, T)
        def pairs(at):
            for t in range(HG_BLOCK):
                qt, ct = qs[at(t), :], cs[at(t), :]
                acc = jnp.zeros(qt.shape, F32)
                for s in range(t + 1):
                    w = qt * ks[at(s), :] * jnp.exp(ct - cs[at(s), :])
                    acc = acc + jnp.sum(w, axis=-1, keepdims=True) * i_ref[at(s), :]
                o_ref[at(t), :] = acc

        _groups_loop(nb, pairs)

        _hgrn_states(i_ref, ks, cs, states, nb)

        def out_step(n):
            rows = _block_rows(n)
            o_ref[rows, :] += _fdot(qs[rows, :] * jnp.exp(cs[rows, :]), states[n], NT)

        _blocks_loop(nb, out_step)
        pt = min(T, 256)
        for p in range(T // pt):
            rows = pl.ds(p * pt, pt)
            main_ref[rows, :] = _gate_out(o_ref[rows, :], og_ref[...], g_ref[rows, :])

    nh = HG_HEADS
    col = lambda off: pl.BlockSpec((T, hd), lambda h, off=off: (0, off + h))
    (main, o), moved = _call(
        body, [proj, proj, proj, proj, lb, og], [jax.ShapeDtypeStruct((T, MAIN_WIDTH), F32)] * 2, name=name, grid=(nh,),
        in_specs=[col(0), col(nh), col(2 * nh), col(3 * nh), pl.BlockSpec((1, hd), lambda h: (0, h)),
                  pl.BlockSpec((1, hd), lambda h: (0, 0))],
        out_specs=[col(0), col(0)], scratch=[pltpu.VMEM((T, hd), F32)] * 3 + [pltpu.VMEM((nb, hd, hd), F32)], comm=comm)
    return main, o, moved


def _hgrn_bwd(name, proj, o, dmix, lb, og, comm=None):
    T = proj.shape[0]
    nb = T // HG_BLOCK
    hd = HG_HEAD_DIM
    pt = min(T, 256)

    def body(q_ref, f_ref, i_ref, g_ref, o_ref, dm_ref, lb_ref, og_ref,
             dq_ref, df_ref, di_ref, dg_ref, dlb_ref, dog_ref, qs, ks, cs, dos, dqs, dks, dvs, states, behind):
        lbv = lb_ref[...]
        _hgrn_prep(q_ref, f_ref, lbv, qs, ks, cs, T)
        dog = jnp.zeros((1, hd), F32)
        for p in range(T // pt):
            rows = pl.ds(p * pt, pt)
            _, vjp = jax.vjp(_gate_out, o_ref[rows, :], og_ref[...], g_ref[rows, :])
            do, dog_p, dg = vjp(dm_ref[rows, :])
            dos[rows, :] = do
            dg_ref[rows, :] = dg.astype(dg_ref.dtype)
            dog = dog + dog_p

        @pl.when(pl.program_id(0) == 0)
        def _():
            dog_ref[...] = jnp.zeros(dog_ref.shape, F32)
        dog_ref[...] += dog

        def pairs(at):
            for t in range(HG_BLOCK):
                dqs[at(t), :] = jnp.zeros((HG_GROUP if nb % HG_GROUP == 0 else nb, hd), F32)
            for s in range(HG_BLOCK):
                k_s, c_s, v_s = ks[at(s), :], cs[at(s), :], i_ref[at(s), :]
                dk = jnp.zeros(k_s.shape, F32)
                dv = jnp.zeros(k_s.shape, F32)
                for t in range(s, HG_BLOCK):
                    q_t, do_t = qs[at(t), :], dos[at(t), :]
                    e = jnp.exp(cs[at(t), :] - c_s)
                    a = jnp.sum(q_t * k_s * e, axis=-1, keepdims=True)
                    g = jnp.sum(do_t * v_s, axis=-1, keepdims=True)
                    dqs[at(t), :] += g * k_s * e
                    dk = dk + g * q_t * e
                    dv = dv + a * do_t
                dks[at(s), :] = dk
                dvs[at(s), :] = dv

        _groups_loop(nb, pairs)

        _hgrn_states(i_ref, ks, cs, states, nb)

        def own_step(n):
            rows = _block_rows(n)
            behind[n] = _fdot(dos[rows, :], qs[rows, :] * jnp.exp(cs[rows, :]), TN)

        _blocks_loop(nb, own_step)
        _scan_states(behind, cs, nb, reverse=True)

        def grad_step(n):
            rows = _block_rows(n)
            c = cs[rows, :]
            ec, ek = jnp.exp(c), jnp.exp(c[HG_BLOCK - 1:HG_BLOCK, :] - c)
            dst = behind[n]
            dqs[rows, :] += _fdot(dos[rows, :], states[n], NN) * ec
            dks[rows, :] += _fdot(i_ref[rows, :], dst, NN) * ek
            dvs[rows, :] += _fdot(ks[rows, :] * ek, dst, NT)

        _blocks_loop(nb, grad_step)

        full = (lax.broadcasted_iota(jnp.int32, (pt, pt), 1) >= lax.broadcasted_iota(jnp.int32, (pt, pt), 0)).astype(F32)
        carry = jnp.zeros((1, hd), F32)
        dlb = jnp.zeros((1, hd), F32)
        for p in reversed(range(T // pt)):
            rows = pl.ds(p * pt, pt)
            q, k, dq, dk = qs[rows, :], ks[rows, :], dqs[rows, :], dks[rows, :]
            db = q * dq - k * dk
            dlf = _dot(full, db, NN, precision=lax.Precision.HIGHEST) + carry
            carry = carry + jnp.sum(db, axis=0, keepdims=True)
            sg = _sigmoid(f_ref[rows, :])
            df = dlf / (1.0 - k) - dk
            df_ref[rows, :] = (df * (1.0 - lbv) * sg * (1.0 - sg)).astype(df_ref.dtype)
            dlb = dlb + jnp.sum(df * (1.0 - sg), axis=0, keepdims=True)
            qr = q_ref[rows, :]
            sq = _sigmoid(qr)
            dq_ref[rows, :] = (dq * (sq + qr * sq * (1.0 - sq))).astype(dq_ref.dtype)
            di_ref[rows, :] = dvs[rows, :].astype(di_ref.dtype)
        dlb_ref[...] = dlb

    nh = HG_HEADS
    col = lambda off: pl.BlockSpec((T, hd), lambda h, off=off: (0, off + h))
    vec = pl.BlockSpec((1, hd), lambda h: (0, h))
    one = pl.BlockSpec((1, hd), lambda h: (0, 0))
    outs, moved = _call(
        body, [proj, proj, proj, proj, o, dmix, lb, og],
        [jax.ShapeDtypeStruct((T, MAIN_WIDTH), BF16)] * 4
        + [jax.ShapeDtypeStruct((1, MAIN_WIDTH), F32), jax.ShapeDtypeStruct((1, hd), F32)],
        name=name, grid=(nh,), in_specs=[col(0), col(nh), col(2 * nh), col(3 * nh), col(0), col(0), vec, one],
        out_specs=[col(0), col(0), col(0), col(0), vec, one],
        scratch=[pltpu.VMEM((T, hd), F32)] * 7 + [pltpu.VMEM((nb, hd, hd), F32)] * 2, comm=comm)
    return (*outs, moved)


def _softmax_rows(s):
    p = jnp.exp(s - jnp.max(s, axis=-1, keepdims=True))
    return p, jnp.sum(p, axis=-1, keepdims=True)


def _fox_probs(q, k, cr_ref, hh, qi, tq):
    q0 = qi * tq
    pieces = ([(0, q0)] if qi else []) + [(q0, q0 + tq)]
    ss = []
    for a, b in pieces:
        s = _dot(q, k[a:b], NT) - cr_ref[hh, :, pl.ds(a, b - a)]
        if a == q0:
            causal = lax.broadcasted_iota(jnp.int32, s.shape, 1) <= lax.broadcasted_iota(jnp.int32, s.shape, 0)
            s = jnp.where(causal, s, -jnp.inf)
        ss.append(s)
    m = functools.reduce(jnp.maximum, [jnp.max(s, axis=-1, keepdims=True) for s in ss])
    ps = [jnp.exp(s - m) for s in ss]
    l = functools.reduce(jnp.add, [jnp.sum(p, axis=-1, keepdims=True) for p in ps])
    return [(a, b, p) for (a, b), p in zip(pieces, ps)], l


def _fox_specs(T):
    w = 2 * FOX_HEAD_DIM
    n = MAIN_WIDTH // w
    col = lambda off: pl.BlockSpec((T, w), lambda p, off=off: (0, off + p))
    cr = pl.BlockSpec((2, 1, T), lambda p: (p, 0, 0))
    gain = pl.BlockSpec((1, FOX_HEAD_DIM), lambda p: (0, 0))
    return n, col, cr, gain


def _fox_fwd(name, proj, kvf, cr, gq, gk, comm=None):
    T = proj.shape[0]
    tq = min(T, 256)
    hd = FOX_HEAD_DIM
    scale = hd ** -0.5

    def body(q_ref, g_ref, k_ref, v_ref, cr_ref, gq_ref, gk_ref, main_ref, o_ref):
        for hh in range(2):
            lanes = pl.ds(hh * hd, hd)
            k = _rms(k_ref[:, lanes], gk_ref[...]).astype(BF16)
            v = v_ref[:, lanes].astype(BF16)
            for qi in range(T // tq):
                rows = pl.ds(qi * tq, tq)
                q = (_rms(q_ref[rows, lanes], gq_ref[...]) * scale).astype(BF16)
                ps, l = _fox_probs(q, k, cr_ref, hh, qi, tq)
                o = functools.reduce(jnp.add, [_dot(p.astype(BF16), v[a:b], NN) for a, b, p in ps]) / l
                o_ref[rows, lanes] = o
                main_ref[rows, lanes] = o * _sigmoid(g_ref[rows, lanes])

    n, col, crs, gain = _fox_specs(T)
    (main, o), moved = _call(
        body, [proj, proj, kvf, kvf, cr, gq, gk], [jax.ShapeDtypeStruct((T, MAIN_WIDTH), F32)] * 2, name=name, grid=(n,),
        in_specs=[col(0), col(n), col(0), col(n), crs, gain, gain], out_specs=[col(0), col(0)], comm=comm)
    return main, o, moved


def _fox_bwd(name, proj, kvf, cr, gq, gk, o, dmix, pdk, pdv, pdc, comm=None):
    T = proj.shape[0]
    tq = min(T, 256)
    hd = FOX_HEAD_DIM
    scale = hd ** -0.5

    def body(q_ref, g_ref, k_ref, v_ref, cr_ref, gq_ref, gk_ref, o_ref, dm_ref, pdk_ref, pdv_ref, pdc_ref,
             dq_ref, dg_ref, dk_ref, dv_ref, dc_ref, dgq_ref, dgk_ref, dka, dva, dca):
        dgq = jnp.zeros((1, hd), F32)
        dgk = jnp.zeros((1, hd), F32)
        for hh in range(2):
            lanes = pl.ds(hh * hd, hd)
            k32, vjp_k = jax.vjp(_rms, k_ref[:, lanes], gk_ref[...])
            k = k32.astype(BF16)
            v = v_ref[:, lanes].astype(BF16)
            dka[...] = jnp.zeros(dka.shape, F32)
            dva[...] = jnp.zeros(dva.shape, F32)
            dca[...] = jnp.zeros(dca.shape, F32)
            for qi in range(T // tq):
                rows = pl.ds(qi * tq, tq)
                q32, vjp_q = jax.vjp(_rms, q_ref[rows, lanes], gq_ref[...])
                q = (q32 * scale).astype(BF16)
                ps, l = _fox_probs(q, k, cr_ref, hh, qi, tq)
                ps = [(a, b, p / l) for a, b, p in ps]
                sg = _sigmoid(g_ref[rows, lanes])
                dm = dm_ref[rows, lanes]
                do = (dm * sg).astype(BF16)
                dg_ref[rows, lanes] = (dm * o_ref[rows, lanes] * sg * (1.0 - sg)).astype(dg_ref.dtype)
                dps = [_dot(do, v[a:b], NT) for a, b, _ in ps]
                delta = functools.reduce(jnp.add, [jnp.sum(p * dp, axis=-1, keepdims=True) for (_, _, p), dp in zip(ps, dps)])
                dq = jnp.zeros((tq, hd), F32)
                for (a, b, p), dp in zip(ps, dps):
                    ds = p * (dp - delta)
                    dsb = ds.astype(BF16)
                    dq = dq + _dot(dsb, k[a:b], NN)
                    dka[:, pl.ds(a, b - a)] += _dot(q, dsb, TN)
                    dva[:, pl.ds(a, b - a)] += _dot(do, p.astype(BF16), TN)
                    dca[:, pl.ds(a, b - a)] -= jnp.sum(ds, axis=0, keepdims=True)
                dqr, dgq_p = vjp_q(dq * scale)
                dq_ref[rows, lanes] = dqr.astype(dq_ref.dtype)
                dgq = dgq + dgq_p
            dkr, dgk_p = vjp_k(dka[...].T)
            dgk = dgk + dgk_p
            dk_ref[:, lanes] = dkr + pdk_ref[:, lanes]
            dv_ref[:, lanes] = dva[...].T + pdv_ref[:, lanes]
            dc_ref[hh] = dca[...] + pdc_ref[hh]

        @pl.when(pl.program_id(0) == 0)
        def _():
            dgq_ref[...] = jnp.zeros(dgq_ref.shape, F32)
            dgk_ref[...] = jnp.zeros(dgk_ref.shape, F32)
        dgq_ref[...] += dgq
        dgk_ref[...] += dgk

    n, col, crs, gain = _fox_specs(T)
    wide = jax.ShapeDtypeStruct((T, MAIN_WIDTH), F32)
    half = jax.ShapeDtypeStruct((T, MAIN_WIDTH), BF16)
    outs, moved = _call(
        body, [proj, proj, kvf, kvf, cr, gq, gk, o, dmix, pdk, pdv, pdc],
        [half, half, wide, wide, jax.ShapeDtypeStruct((FOX_HEADS, 1, T), F32),
         jax.ShapeDtypeStruct((1, hd), F32), jax.ShapeDtypeStruct((1, hd), F32)],
        name=name, grid=(n,),
        in_specs=[col(0), col(n), col(0), col(n), crs, gain, gain, col(0), col(0), col(0), col(0), crs],
        out_specs=[col(0), col(0), col(0), col(0), crs, gain, gain],
        scratch=[pltpu.VMEM((hd, T), F32), pltpu.VMEM((hd, T), F32), pltpu.VMEM((1, T), F32)], comm=comm)
    return (*outs, moved)


def _mem_specs(T, width):
    tq = min(T, 512)
    q = pl.BlockSpec((tq, MEM_WIDTH), lambda i, c=(width - MEM_WIDTH) // MEM_WIDTH: (i, c))
    gain = pl.BlockSpec((1, MEM_HEAD_DIM), lambda i: (0, 0))
    return tq, q, gain


def _mem_fwd(name, proj, kv, gq, gk):
    T, W = proj.shape
    hd = MEM_HEAD_DIM
    tq, qspec, gain = _mem_specs(T, W)

    def body(q_ref, kv_ref, gq_ref, gk_ref, o_ref):
        for h in range(MEM_HEADS):
            lanes = pl.ds(h * hd, hd)
            q = _rms(q_ref[:, lanes], gq_ref[...]).astype(BF16)
            k = _rms(kv_ref[:, lanes], gk_ref[...]).astype(BF16)
            v = kv_ref[:, pl.ds(MEM_WIDTH + h * hd, hd)].astype(BF16)
            p, l = _softmax_rows(_dot(q, k, NT) * (hd ** -0.5))
            o_ref[:, lanes] = _dot(p.astype(BF16), v, NN) / l

    return pl.pallas_call(
        body, grid=(T // tq,),
        in_specs=[qspec, pl.BlockSpec(kv.shape, lambda i: (0, 0)), gain, gain],
        out_specs=pl.BlockSpec((tq, MEM_WIDTH), lambda i: (i, 0)),
        out_shape=jax.ShapeDtypeStruct((T, MEM_WIDTH), F32),
        name=name, compiler_params=_params(1))(proj, kv, gq, gk)


def _mem_bwd(name, proj, kv, gq, gk, dmix):
    T, W = proj.shape
    hd = MEM_HEAD_DIM
    scale = hd ** -0.5
    tq, qspec, gain = _mem_specs(T, W)

    def body(q_ref, kv_ref, gq_ref, gk_ref, dm_ref, dq_ref, dkv_ref, dgq_ref, dgk_ref):
        @pl.when(pl.program_id(0) == 0)
        def _():
            dkv_ref[...] = jnp.zeros(dkv_ref.shape, F32)
            dgq_ref[...] = jnp.zeros(dgq_ref.shape, F32)
            dgk_ref[...] = jnp.zeros(dgk_ref.shape, F32)
        for h in range(MEM_HEADS):
            lanes = pl.ds(h * hd, hd)
            vl = pl.ds(MEM_WIDTH + h * hd, hd)
            q32, vjp_q = jax.vjp(_rms, q_ref[:, lanes], gq_ref[...])
            k32, vjp_k = jax.vjp(_rms, kv_ref[:, lanes], gk_ref[...])
            q, k, v = q32.astype(BF16), k32.astype(BF16), kv_ref[:, vl].astype(BF16)
            p, l = _softmax_rows(_dot(q, k, NT) * scale)
            p = p / l
            do = dm_ref[:, lanes].astype(BF16)
            dp = _dot(do, v, NT)
            dsb = (p * (dp - jnp.sum(p * dp, axis=-1, keepdims=True))).astype(BF16)
            dqr, dgq_p = vjp_q(_dot(dsb, k, NN) * scale)
            dkr, dgk_p = vjp_k(_dot(dsb, q, TN) * scale)
            dq_ref[:, lanes] = dqr.astype(dq_ref.dtype)
            dkv_ref[:, lanes] += dkr
            dkv_ref[:, vl] += _dot(p.astype(BF16), do, TN)
            dgq_ref[...] += dgq_p
            dgk_ref[...] += dgk_p

    return pl.pallas_call(
        body, grid=(T // tq,),
        in_specs=[qspec, pl.BlockSpec(kv.shape, lambda i: (0, 0)), gain, gain,
                  pl.BlockSpec((tq, MEM_WIDTH), lambda i: (i, MAIN_WIDTH // MEM_WIDTH))],
        out_specs=[pl.BlockSpec((tq, MEM_WIDTH), lambda i: (i, 0)), pl.BlockSpec(kv.shape, lambda i: (0, 0)), gain, gain],
        out_shape=[jax.ShapeDtypeStruct((T, MEM_WIDTH), BF16), jax.ShapeDtypeStruct(kv.shape, F32),
                   jax.ShapeDtypeStruct((1, hd), F32), jax.ShapeDtypeStruct((1, hd), F32)],
        name=name, compiler_params=_params(1))(proj, kv, gq, gk, dmix)


def _cumsum_rows(name, x, reverse=False):
    T, C = x.shape
    pt = min(T, 256)

    def body(x_ref, o_ref):
        r = lax.broadcasted_iota(jnp.int32, (pt, pt), 0)
        c = lax.broadcasted_iota(jnp.int32, (pt, pt), 1)
        tri = ((c >= r) if reverse else (c <= r)).astype(F32)
        carry = jnp.zeros((1, C), F32)
        order = range(T // pt)
        for p in (reversed(order) if reverse else order):
            rows = pl.ds(p * pt, pt)
            blk = x_ref[rows, :]
            o_ref[rows, :] = _dot(tri, blk, NN, precision=lax.Precision.HIGHEST) + carry
            carry = carry + jnp.sum(blk, axis=0, keepdims=True)

    return pl.pallas_call(body, out_shape=jax.ShapeDtypeStruct((T, C), F32), name=name,
                          compiler_params=pltpu.CompilerParams(vmem_limit_bytes=V7X_VMEM_LIMIT))(x)


def _all_gather(name, xs):
    n = len(xs)

    def body(*refs):
        x_refs, out_refs = refs[:n], refs[n:2 * n]
        send_sems, recv_sems, local_sems = refs[2 * n:]
        mx, my, mc = _mesh_pos()
        me, sibling = (mx, my, mc), (mx, my, 1 - mc)
        chips = [(1 - mx, my), (mx, 1 - my), (1 - mx, 1 - my)]

        def slot(a, px, py, pc):
            return out_refs[a].at[4 * px + 2 * py + pc]

        def copy(a, k, block, to, src=None):
            return pltpu.make_async_remote_copy(
                src_ref=slot(a, *block) if src is None else src, dst_ref=slot(a, *block),
                send_sem=send_sems.at[7 * a + k], recv_sem=recv_sems.at[7 * a + k], device_id=to, device_id_type=MESH)

        mine = [pltpu.make_async_copy(x_refs[a], slot(a, *me), local_sems.at[a]) for a in range(n)]
        first = []
        for a in range(n):
            mine[a].start()
            first.append(copy(a, 0, me, sibling, src=x_refs[a]))
            first += [copy(a, 1 + j, me, (*chip, mc), src=x_refs[a]) for j, chip in enumerate(chips)]
        for cp in first:
            cp.start()
        passed = []
        for j, chip in enumerate(chips):
            for a in range(n):
                copy(a, 1 + j, (*chip, mc), me).wait_recv()
                passed.append(copy(a, 4 + j, (*chip, mc), sibling))
                passed[-1].start()
        for a in range(n):
            copy(a, 0, sibling, me).wait_recv()
            for j, chip in enumerate(chips):
                copy(a, 4 + j, (*chip, 1 - mc), me).wait_recv()
        for cp in first + passed:
            cp.wait_send()
        for cp in mine:
            cp.wait()

    return pl.pallas_call(
        body, out_shape=[jax.ShapeDtypeStruct((N_DEV,) + x.shape, x.dtype) for x in xs], in_specs=[ANY] * n, out_specs=[ANY] * n,
        scratch_shapes=[pltpu.SemaphoreType.DMA((7 * n,)), pltpu.SemaphoreType.DMA((7 * n,)), pltpu.SemaphoreType.DMA((n,))],
        name=name)(*xs)


def _row_tile(R, cap):
    best = None
    for t in range(8, min(R, cap) + 1, 8):
        if R % t == 0:
            best = t
    return best or R


def _sum_slabs(name, a, out_dtype):
    n, R, C = a.shape
    tm = _row_tile(R, 512)

    def body(*refs):
        acc = refs[0][...].astype(F32)
        for r in refs[1:n]:
            acc = acc + r[...].astype(F32)
        refs[n][...] = acc.astype(out_dtype)

    return pl.pallas_call(
        body, grid=(R // tm,),
        in_specs=[pl.BlockSpec((None, tm, C), lambda i, q=q: (q, i, 0)) for q in range(n)],
        out_specs=pl.BlockSpec((tm, C), lambda i: (i, 0)), out_shape=jax.ShapeDtypeStruct((R, C), out_dtype),
        name=name, compiler_params=_params(1))(*([a] * n))


SMALL = ["ffn1_norm", "mix_norm", "mem_norm", "mem_q_gain", "mem_k_gain", "hgrn_o_gain", "fox_q_gain", "kv_norm",
         "fox_f_bias", "fox_k_gain", "ffn2_norm"]
COLS352 = ["ffn1_w_gate", "ffn1_w_up", "ffn2_w_gate", "ffn2_w_up"]
KV_WIDTH = 2 * MAIN_WIDTH + FOX_HEADS


def _pad_cols(w, width):
    return jnp.pad(w, [(0, 0)] * (w.ndim - 1) + [(0, width - w.shape[-1])])


def _pad128(a):
    flat = a.reshape(-1)
    return jnp.pad(flat, (0, -flat.shape[0] % LANES))


def _small_pack(parts):
    flat = jnp.concatenate([_pad128(p) for p in parts])
    rows = -(-flat.shape[0] // LANES)
    flat = jnp.pad(flat, (0, (-rows % 8) * LANES))
    return flat.reshape(-1, LANES)


def _small_unpack(buf, shapes):
    flat = buf.reshape(-1)
    out, off = [], 0
    for s in shapes:
        n = 1
        for d in s:
            n *= d
        out.append(flat[off:off + n].reshape(s))
        off += n + (-n % LANES)
    return out


def _lb_fn(l0, l1):
    m = lax.stop_gradient(jnp.maximum(l0, l1))
    e0, e1 = jnp.exp(l0 - m), jnp.exp(l1 - m)
    p0, p1 = e0 / (e0 + e1), e1 / (e0 + e1)
    return p0 - p0, (p0 + p1) - p0


def _lb_fwd(logits):
    return _rowwise("lb", _lb_fn, [logits[0:1], logits[1:2]], [], [(MAIN_WIDTH, F32)] * 2)


def _lb_bwd(logits, dlb0, dlb1):
    def fn(l0, l1, d0, d1):
        _, vjp = jax.vjp(_lb_fn, l0, l1)
        return vjp((d0, d1))
    return _rowwise("lb_bwd", fn, [logits[0:1], logits[1:2], dlb0, dlb1], [], [(MAIN_WIDTH, F32)] * 2)


def _adamw_fn(w, g, m, v):
    m = ADAM_B1 * m + (1.0 - ADAM_B1) * g
    v = ADAM_B2 * v + (1.0 - ADAM_B2) * jnp.square(g)
    m_hat = m / (1.0 - ADAM_B1 ** ADAM_STEP)
    v_hat = v / (1.0 - ADAM_B2 ** ADAM_STEP)
    return -ADAM_LR * (m_hat / (jnp.sqrt(v_hat) + ADAM_EPS) + ADAM_WD * w), m, v


def _sum_adamw(name, landed, w, m, v, comm=None):
    L, r, c = w.shape
    tm = _row_tile(r, 128)
    n_i = r // tm

    def body(*refs):
        land, (w_ref, m_ref, v_ref), outs = refs[:L], refs[L:L + 3], refs[L + 3:]
        for k in range(L):
            @pl.when(pl.program_id(0) == k)
            def _(k=k):
                g = land[k][0].astype(F32)
                for s in range(1, N_DEV):
                    g = g + land[k][s].astype(F32)
                for ref, val in zip(outs, (g,) + _adamw_fn(w_ref[...], g, m_ref[...], v_ref[...])):
                    ref[...] = val

    held = lambda k: (lambda l, i: (0, jnp.where(l < k, 0, jnp.where(l == k, i, n_i - 1)), 0))
    cur = pl.BlockSpec((None, tm, c), lambda l, i: (l, i, 0))
    return _call(body, [*landed, w, m, v], [jax.ShapeDtypeStruct((L, r, c), F32)] * 4, name=name, grid=(L, n_i),
                 in_specs=[pl.BlockSpec((N_DEV, tm, c), held(k)) for k in range(L)] + [cur] * 3, out_specs=[cur] * 4, comm=comm)


def _adamw(name, w, g, m, v):
    shape = w.shape
    C = shape[-1]
    two = lambda a: a.reshape(-1, C)
    R = two(w).shape[0]
    outs = _rowwise(name, _adamw_fn, [two(w), two(g), two(m), two(v)], [], [(C, F32)] * 3, tm=_row_tile(R, 512))
    return [o.reshape(shape) for o in outs]


def _whole_rows(g, r0, r1):
    return g[:, r0:r1].reshape(N_DEV * (r1 - r0), g.shape[2])


SHARD_ROWS = D_MODEL // N_DEV


def _w_out_of(Wl):
    n = Wl["d2"].shape[1]
    return _whole_rows(Wl["r1"], n, n + SHARD_ROWS)


def _w_mem_kv_of(Wl):
    n = Wl["d2"].shape[1]
    return _whole_rows(Wl["r1"], n + SHARD_ROWS, n + 2 * SHARD_ROWS)[:, :2 * MEM_WIDTH]


def _mixer_fwd(l, x1, mem, G, W, lbs, shared, local, units, relays):
    T = x1.shape[0]
    tag = f"l{l}"
    Wl = G[l]
    h = _rms_fwd(tag + "_mixrms", x1, W["mix_norm"][l:l + 1])
    mem_n = _rms_fwd(tag + "_memrms", mem, W["mem_norm"][l:l + 1])
    kv = _mm(tag + "_memkv", [(mem_n, _w_mem_kv_of(Wl), NN)], [F32], mem.shape[0], 2 * MEM_WIDTH)
    proj, moved = _proj_cols(tag + "_in", h, Wl["win"], 0, comm=_Comm(relay=[G[n][k] for n, k in relays]))
    for (n, k), m in zip(relays, moved):
        G[n][k] = m
    along = _Comm(gather=[local[n][k] for n, k in units])
    if l < 2:
        main, o, moved = _hgrn_fwd(tag + "_hgrn", proj, lbs[l], W["hgrn_o_gain"][l:l + 1], comm=along)
    else:
        main, o, moved = _fox_fwd(tag + "_fox", proj, shared["kvf"], shared["cr"], W["fox_q_gain"][l - 2:l - 1],
                                  W["fox_k_gain"], comm=along)
    for (n, k), m in zip(units, moved):
        G[n][k] = m
    mem_o = _mem_fwd(tag + "_mem", proj, kv, W["mem_q_gain"][l:l + 1], W["mem_k_gain"][l:l + 1])
    w_out = _w_out_of(Wl)
    x2 = _mm(tag + "_out", [(main, w_out[:MAIN_WIDTH], NN), (mem_o, w_out[MAIN_WIDTH:], NN)], [F32], T, D_MODEL,
             epi=lambda a, e: (e[0] + a[0] + a[1],), extras=[x1])
    return x2, dict(h=h, mem_n=mem_n, kv=kv, proj=proj, main=main, o=o, mem_o=mem_o)


def _mixer_bwd(l, x1, mem, Wl, W, lbs, shared, sv, dx2, acc, ready, landed):
    T = x1.shape[0]
    tag = f"l{l}b"
    w_out = _w_out_of(Wl)
    g = {}
    dmix = _mm(tag + "_dmix", [(dx2, w_out, NT)], [F32], T, D_MODEL)
    dw_out = jnp.concatenate([
        _mm(tag + "_dwout_a", [(sv["main"], dx2, TN)], [BF16], MAIN_WIDTH, D_MODEL),
        _mm(tag + "_dwout_b", [(sv["mem_o"], dx2, TN)], [BF16], MEM_WIDTH, D_MODEL)], axis=0).reshape(N_DEV, -1, D_MODEL)
    dqm, dkv, g["mem_q_gain"], g["mem_k_gain"] = _mem_bwd(tag + "_mem", sv["proj"], sv["kv"], W["mem_q_gain"][l:l + 1],
                                                           W["mem_k_gain"][l:l + 1], dmix)
    along = _Comm(scatter=[v for _, v in ready])
    if l < 2:
        dq, df, di, dg, g["lb"], g["hgrn_o_gain"], moved = _hgrn_bwd(tag + "_hgrn", sv["proj"], sv["o"], dmix, lbs[l],
                                                                      W["hgrn_o_gain"][l:l + 1], comm=along)
        dproj = jnp.concatenate([dq, df, di, dg, dqm], axis=1)
    else:
        dq, dgate, acc["dk"], acc["dv"], acc["dc"], g["fox_q_gain"], g["fox_k_gain"], moved = _fox_bwd(
            tag + "_fox", sv["proj"], shared["kvf"], shared["cr"], W["fox_q_gain"][l - 2:l - 1], W["fox_k_gain"],
            sv["o"], dmix, acc["dk"], acc["dv"], acc["dc"], comm=along)
        dproj = jnp.concatenate([dq, dgate, dqm], axis=1)
    landed.update({k: m for (k, _), m in zip(ready, moved)})
    dh, dw_in, moved = _proj_cols_bwd(tag + "_in", sv["h"], dproj, Wl["win"], 0, comm=_Comm(scatter=[dw_out]))
    landed[(l, "w_out")] = moved[0]
    dx1, g["mix_norm"] = _rms_bwd(tag + "_mixrms", x1, W["mix_norm"][l:l + 1], dh, dres=dx2)
    dw_mem_kv = _mm(tag + "_dwmemkv", [(sv["mem_n"], dkv, TN)], [BF16], D_MODEL, 2 * MEM_WIDTH)
    dmem_n = _mm(tag + "_dmemn", [(dkv, _w_mem_kv_of(Wl), NT)], [F32], mem.shape[0], D_MODEL)
    _, g["mem_norm"] = _rms_bwd(tag + "_memrms", mem, W["mem_norm"][l:l + 1], dmem_n)
    return dx1, g, [((l, "w_in"), dw_in), ((l, "w_mem_kv"), dw_mem_kv.reshape(N_DEV, -1, 2 * MEM_WIDTH))]


def _forget_cols(kvf):
    return kvf[:, 2 * MAIN_WIDTH:2 * MAIN_WIDTH + LANES]


def _log_forget(kvf, bias):
    return _rowwise("kv_logf", lambda f, b: jax.nn.log_sigmoid(f + b), [_forget_cols(kvf)], [bias], [(LANES, F32)])[0]


def _step(x, mem, target, W, lb_logits, G0, local):
    T = x.shape[0]
    W = dict(W, fox_k_gain=W["fox_k_gain"].reshape(1, -1))
    lbs = _lb_fwd(lb_logits)
    fox_bias = jnp.pad(W["fox_f_bias"], (0, LANES - FOX_HEADS)).reshape(1, LANES)
    w_kv = W["w_kv"]
    n_l = len(local)
    ffn1 = lambda l, Wl: (W["ffn1_norm"][l:l + 1], Wl["gu1"], Wl["r1"], 0, 1, 0)
    ffn2 = lambda l, Wl: (W["ffn2_norm"][l:l + 1], Wl["gu2"], Wl["d2"], 0, 1, 0)

    on_ffn1 = {0: [(1, "gu1")], 1: [(2, "gu1")], 2: [(3, "gu1")], 3: [(3, "gu2")]}
    on_mix = {0: [(1, "r1"), (1, "gu2")], 1: [(2, "r1"), (2, "win"), (2, "gu2")], 2: [(3, "r1")], 3: []}
    on_ffn2 = {0: [(1, "d2"), (1, "win")], 1: [(2, "d2")], 2: [(3, "d2"), (3, "win")], 3: []}
    saved, shared, G = [], {}, [G0] + [{} for _ in range(n_l - 1)]
    for l in range(n_l):
        Wl = G[l]
        relay = on_ffn2[l - 1] if l else []
        along = _Comm(gather=[local[n][k] for n, k in on_ffn1[l]], relay=[G[n][k] for n, k in relay])
        x1, moved = _ffn_fwd(f"l{l}_ffn1", x, *ffn1(l, Wl), comm=along)
        for (n, k), m in zip(on_ffn1[l] + relay, moved):
            G[n][k] = m
        x2, sv = _mixer_fwd(l, x1, mem, G, W, lbs, shared, local, on_mix[l], on_ffn1[l])
        along = _Comm(gather=[local[n][k] for n, k in on_ffn2[l]], relay=[G[n][k] for n, k in on_mix[l]])
        x3, moved = _ffn_fwd(f"l{l}_ffn2", x2, *ffn2(l, G[l]), comm=along)
        for (n, k), m in zip(on_ffn2[l] + on_mix[l], moved):
            G[n][k] = m
        sv.update(x=x, x1=x1, x2=x2)
        saved.append(sv)
        x = x3
        if l == 1:
            hk = _rms_fwd("kv_rms", x, W["kv_norm"].reshape(1, -1))
            kvf = _mm("kv_proj", [(hk, w_kv, NN)], [F32], T, w_kv.shape[1])
            cum = _cumsum_rows("kv_cum", _log_forget(kvf, fox_bias))[:, :FOX_HEADS].T
            shared = dict(kvf=kvf, cr=cum[:, None, :], hk=hk, x=x)

    def loss_fn(y, t):
        err = y - t
        return err * (1.0 / D_MODEL), jnp.sum(0.5 / D_MODEL * err * err, axis=0, keepdims=True)
    dx, loss = _rowwise("loss", loss_fn, [x, target], [], [(D_MODEL, F32)], [((1, D_MODEL), F32)])

    grads = [None] * n_l
    acc = dict(dk=jnp.zeros((T, MAIN_WIDTH), F32), dv=jnp.zeros((T, MAIN_WIDTH), F32), dc=jnp.zeros((FOX_HEADS, 1, T), F32))
    gkv = {}
    landed, late = {}, []
    for l in reversed(range(n_l)):
        sv, Wl = saved[l], G[l]
        ready, more, extra = [], [], []
        if l == 1:
            dcum = jnp.pad(acc["dc"][:, 0, :].T, ((0, 0), (0, LANES - FOX_HEADS)))
            dlf = _cumsum_rows("kv_dcum", dcum, reverse=True)
            def dlogf_fn(d, f, b):
                p = d * _sigmoid(-(f + b))
                return p, jnp.sum(p, axis=0, keepdims=True)
            dfl, gkv["fox_f_bias"] = _rowwise("kv_dlogf", dlogf_fn, [dlf, _forget_cols(shared["kvf"])], [fox_bias],
                                              [(LANES, BF16)], [((1, LANES), F32)])
            dkvf = _pad_cols(jnp.concatenate([acc["dk"].astype(BF16), acc["dv"].astype(BF16), dfl], axis=1), w_kv.shape[1])
            dw_kv = _mm("kv_dw", [(shared["hk"], dkvf, TN)], [BF16], D_MODEL, dkvf.shape[1])
            extra.append(((0, "w_kv"), dw_kv[:, :KV_WIDTH].reshape(N_DEV, -1, KV_WIDTH)))
            dhk = _mm("kv_dh", [(dkvf, w_kv, NT)], [F32], T, D_MODEL)
            dx, gkv["kv_norm"] = _rms_bwd("kv_rmsb", shared["x"], W["kv_norm"].reshape(1, -1), dhk, dres=dx)
        g = {}
        if l < 2:
            ready, late = ready + late[1:], late[:1]
        else:
            more, extra, late = late[1:2], extra + late[2:], late[:1]
        dx2, g["ffn2_norm"], dwg, dwu, dwd, moved_a, moved_w = _ffn_bwd(
            f"l{l}b_ffn2", sv["x2"], *ffn2(l, Wl), dx, comm_a=_Comm(scatter=[v for _, v in late[:2]]),
            comm_w=_Comm(scatter=[v for _, v in late[2:]]))
        landed.update({k: m for (k, _), m in zip(late, moved_a + moved_w)})
        ready += [((l, "ffn2_w_gate"), dwg), ((l, "ffn2_w_up"), dwu), ((l, "ffn2_w_down"), dwd)]
        dx1, gm, rest = _mixer_bwd(l, sv["x1"], mem, Wl, W, lbs, shared, sv, dx2, acc, ready, landed)
        g.update(gm)
        rest = rest + more
        dx, g["ffn1_norm"], dwg, dwu, dwd, moved_a, moved_w = _ffn_bwd(
            f"l{l}b_ffn1", sv["x"], *ffn1(l, Wl), dx1, comm_a=_Comm(scatter=[v for _, v in rest]),
            comm_w=_Comm(scatter=[v for _, v in extra]))
        landed.update({k: m for (k, _), m in zip(rest + extra, moved_a + moved_w)})
        late = [((l, "ffn1_w_gate"), dwg), ((l, "ffn1_w_up"), dwu), ((l, "ffn1_w_down"), dwd)]
        grads[l] = g

    out = {}
    for n in ["ffn1_norm", "mix_norm", "mem_norm", "mem_q_gain", "mem_k_gain", "ffn2_norm"]:
        out[n] = jnp.concatenate([grads[l][n] for l in range(4)], axis=0)
    out["hgrn_o_gain"] = jnp.concatenate([grads[l]["hgrn_o_gain"] for l in (0, 1)], axis=0)
    out["fox_q_gain"] = jnp.concatenate([grads[l]["fox_q_gain"] for l in (2, 3)], axis=0)
    out["fox_k_gain"] = (grads[2]["fox_k_gain"] + grads[3]["fox_k_gain"]).reshape(-1)
    out["kv_norm"] = gkv["kv_norm"].reshape(-1)
    out["fox_f_bias"] = gkv["fox_f_bias"][0, :FOX_HEADS]
    dl0, dl1 = _lb_bwd(lb_logits, grads[0]["lb"], grads[1]["lb"])
    out["hgrn_lb_logits"] = jnp.concatenate([dl0, dl1], axis=0)
    return loss, dx, out, landed, late


WEIGHTS = ["ffn1_norm", "ffn1_w_gate", "ffn1_w_up", "ffn1_w_down", "mix_norm", "mem_norm", "w_mem_kv", "mem_q_gain",
           "mem_k_gain", "w_in_a", "hgrn_lb_logits", "hgrn_o_gain", "w_in_b", "fox_q_gain", "kv_norm", "w_kv", "fox_f_bias",
           "fox_k_gain", "w_out", "ffn2_norm", "ffn2_w_gate", "ffn2_w_up", "ffn2_w_down"]
BIG = COLS352 + ["w_in_a", "w_in_b", "ffn1_w_down", "ffn2_w_down", "w_out", "w_mem_kv", "w_kv"]


def _train_step(a):
    bf = lambda w: w.astype(BF16)
    n_l = a["w_out"].shape[0]
    local = []
    for l in range(n_l):
        w_in = a["w_in_a"][l] if l < a["w_in_a"].shape[0] else a["w_in_b"][l - a["w_in_a"].shape[0]]
        local.append(dict(
            gu1=bf(jnp.concatenate([a["ffn1_w_gate"][l], a["ffn1_w_up"][l]], axis=0)),
            r1=bf(jnp.concatenate([a["ffn1_w_down"][l], a["w_out"][l], _pad_cols(a["w_mem_kv"][l], D_MODEL)], axis=0)),
            win=bf(w_in),
            gu2=bf(jnp.concatenate([a["ffn2_w_gate"][l], a["ffn2_w_up"][l]], axis=0)),
            d2=bf(a["ffn2_w_down"][l])))
    keys = ["gu1", "r1", "win", "gu2", "d2"]
    first = _all_gather("ag_first", [local[0][k] for k in keys] + [bf(a["w_kv"]), _small_pack([a["hgrn_lb_logits"]])])
    G0 = dict(zip(keys, first))
    W = {n: a[n] for n in SMALL}
    W["w_kv"] = _pad_cols(first[5].reshape(-1, a["w_kv"].shape[1]), 2 * D_MODEL)
    lb_shape = a["hgrn_lb_logits"].shape
    lb_all = first[6].reshape(N_DEV, -1)[:, :lb_shape[0] * lb_shape[1]]
    lb_logits = lb_all.reshape((N_DEV,) + lb_shape).transpose(1, 0, 2).reshape(lb_shape[0], -1)

    loss_part, dx, g, landed, tail = _step(a["x"][0], a["mem"][0], a["loss_target"][0], W, lb_logits, G0, local)

    n_a = a["w_in_a"].shape[0]
    grad, delta, new_m, new_v = {}, {}, {}, {}
    order = [n for n in BIG if n.startswith("ffn2")] + [n for n in BIG if not n.startswith("ffn")]
    for n in order + [n for n in BIG if n.startswith("ffn1")]:
        ls = range(n_a) if n == "w_in_a" else range(n_a, n_l) if n == "w_in_b" else range(1) if n == "w_kv" else range(n_l)
        key = "w_in" if n.startswith("w_in") else n
        lead = (lambda t: t[None]) if a[n].ndim == 2 else (lambda t: t)
        riding, tail = tail[:1], tail[1:]
        res, moved = _sum_adamw("adam_" + n, [landed[(l, key)] for l in ls], lead(a[n]), lead(a["m_" + n]), lead(a["v_" + n]),
                                comm=_Comm(scatter=[v for _, v in riding]))
        landed.update({k: m for (k, _), m in zip(riding, moved)})
        grad[n], delta[n], new_m[n], new_v[n] = [t.reshape(a[n].shape) for t in res]

    zeros = [jnp.zeros(lb_logits.shape, F32), jnp.zeros(loss_part.shape, F32)]
    small_shapes = [a[n].shape for n in SMALL] + [lb_logits.shape, loss_part.shape]
    small_part = _small_pack([g[n] for n in SMALL] + [g["hgrn_lb_logits"], loss_part])
    small_sum = _sum_slabs("small_sum", _all_gather("ag_small", [small_part])[0], F32)
    small = _small_unpack(small_sum, small_shapes)
    grad.update(dict(zip(SMALL, small)))
    loss = jnp.sum(small[-1])
    me = 4 * lax.axis_index("x") + 2 * lax.axis_index("y") + lax.axis_index("c")
    grad["hgrn_lb_logits"] = lax.dynamic_slice_in_dim(small[-2], me * lb_shape[1], lb_shape[1], axis=1)

    n = "hgrn_lb_logits"
    delta[n], new_m[n], new_v[n] = _adamw("adam_" + n, a[n], grad[n], a["m_" + n], a["v_" + n])
    packs = [_small_pack([a[p + n] for n in SMALL] + zeros) for p in ("", "m_", "v_")]
    upd = _rowwise("adam_small", _adamw_fn, [packs[0], small_sum, packs[1], packs[2]], [], [(LANES, F32)] * 3, tm=packs[0].shape[0])
    for d, u in zip((delta, new_m, new_v), upd):
        d.update(dict(zip(SMALL, _small_unpack(u, small_shapes))))
    return (loss, dx[None], *[grad[n] for n in WEIGHTS], *[delta[n] for n in WEIGHTS], *[new_m[n] for n in WEIGHTS],
            *[new_v[n] for n in WEIGHTS])


def kernel(x, mem, ffn1_norm, ffn1_w_gate, ffn1_w_up, ffn1_w_down, mix_norm, mem_norm, w_mem_kv, mem_q_gain, mem_k_gain, w_in_a, hgrn_lb_logits, hgrn_o_gain, w_in_b, fox_q_gain, kv_norm, w_kv, fox_f_bias, fox_k_gain, w_out, ffn2_norm, ffn2_w_gate, ffn2_w_up, ffn2_w_down, loss_target, m_ffn1_norm, m_ffn1_w_gate, m_ffn1_w_up, m_ffn1_w_down, m_mix_norm, m_mem_norm, m_w_mem_kv, m_mem_q_gain, m_mem_k_gain, m_w_in_a, m_hgrn_lb_logits, m_hgrn_o_gain, m_w_in_b, m_fox_q_gain, m_kv_norm, m_w_kv, m_fox_f_bias, m_fox_k_gain, m_w_out, m_ffn2_norm, m_ffn2_w_gate, m_ffn2_w_up, m_ffn2_w_down, v_ffn1_norm, v_ffn1_w_gate, v_ffn1_w_up, v_ffn1_w_down, v_mix_norm, v_mem_norm, v_w_mem_kv, v_mem_q_gain, v_mem_k_gain, v_w_in_a, v_hgrn_lb_logits, v_hgrn_o_gain, v_w_in_b, v_fox_q_gain, v_kv_norm, v_w_kv, v_fox_f_bias, v_fox_k_gain, v_w_out, v_ffn2_norm, v_ffn2_w_gate, v_ffn2_w_up, v_ffn2_w_down):
    return _train_step(dict(locals()))
```
